```python
import jax, jax.numpy as jnp
from jax import lax
import numpy as np

D_MODEL = 1024
BATCH = 8
SEQ = 4096
DEPTH = 1

CHUNK = 64
Q_BLOCK = 128
LN_EPS = 1e-5
RMS_EPS = 1e-6

MLA_HEADS = 8
MLA_NOPE = 64
MLA_ROPE = 32
MLA_V = 64
MLA_QK = MLA_NOPE + MLA_ROPE
MLA_Q_RANK = 256
MLA_KV_RANK = 128
MLA_WIDTH = MLA_HEADS * MLA_V
ROPE_THETA = 10000.0

RWKV_HEADS = 8
RWKV_HEAD = 64
RWKV_WIDTH = RWKV_HEADS * RWKV_HEAD
DECAY_LORA = 64
ICLR_LORA = 64
RWKV_SHIFT_WIDTH = 3 * RWKV_WIDTH + DECAY_LORA + ICLR_LORA
GN_EPS = 64e-5

ALPHA = (2.0 * DEPTH) ** 0.25
BETA = (8.0 * DEPTH) ** -0.25

IN_SPLITS = (MLA_Q_RANK, MLA_KV_RANK, MLA_ROPE, MLA_WIDTH,
             RWKV_SHIFT_WIDTH, RWKV_WIDTH,
             D_MODEL, D_MODEL)
IN_WIDTH = sum(IN_SPLITS)

kernel_name = 'hybrid_mla_rwkv7_gated_deepnorm_block'


def _split(t, sizes):
    return jnp.split(t, [int(i) for i in np.cumsum(sizes)[:-1]], axis=-1)


def layer_norm(x):
    x = x.astype(jnp.float32)
    xc = x - jnp.mean(x, -1, keepdims=True)
    return xc * lax.rsqrt(jnp.mean(xc * xc, -1, keepdims=True) + LN_EPS)


def rms_norm(x, g):
    x32 = x.astype(jnp.float32)
    y = x32 * lax.rsqrt(jnp.mean(x32 * x32, -1, keepdims=True) + RMS_EPS)
    return (y * g).astype(x.dtype)


def rope_tables(positions):
    inv = ROPE_THETA ** (-jnp.arange(0, MLA_ROPE, 2, dtype=jnp.float32) / MLA_ROPE)
    ang = positions.astype(jnp.float32)[..., None] * inv
    return jnp.cos(ang)[:, :, None, :], jnp.sin(ang)[:, :, None, :]


def apply_rope(t, cos, sin):
    t1, t2 = jnp.split(t.astype(jnp.float32), 2, axis=-1)
    return jnp.concatenate([t1 * cos - t2 * sin, t1 * sin + t2 * cos], -1).astype(t.dtype)


def token_shift(u, mu):
    u_prev = jnp.pad(u, ((0, 0), (1, 0), (0, 0)))[:, :-1]
    return u + (u_prev - u) * mu


def chunk_causal_attention(q, k, v):
    B, S, H, Dk = q.shape
    nb = S // Q_BLOCK
    qb = q.reshape(B, nb, Q_BLOCK, H, Dk).transpose(1, 0, 2, 3, 4)
    key_chunk = jnp.arange(S) // CHUNK
    scale = Dk ** -0.5

    def block(args):
        qi, bi = args
        s = jnp.einsum('bqhd,bkhd->bhqk', qi, k).astype(jnp.float32) * scale
        q_chunk = (bi * Q_BLOCK + jnp.arange(Q_BLOCK)) // CHUNK
        mask = key_chunk[None, :] <= q_chunk[:, None]
        p = jax.nn.softmax(jnp.where(mask, s, -jnp.inf), axis=-1).astype(v.dtype)
        return jnp.einsum('bhqk,bkhd->bqhd', p, v)

    out = lax.map(block, (qb, jnp.arange(nb)))
    return out.transpose(1, 0, 2, 3, 4).reshape(B, S, H, v.shape[-1])


def wkv7(r, w, k, v, a, b):
    B, S, H, N = r.shape

    def step(state, inp):
        r_t, w_t, k_t, v_t, a_t, b_t = inp
        sa = jnp.einsum('bhij,bhj->bhi', state, a_t)
        state = (state * w_t[:, :, None, :] + sa[..., None] * b_t[:, :, None, :]
                 + v_t[..., None] * k_t[:, :, None, :])
        return state, jnp.einsum('bhij,bhj->bhi', state, r_t)

    xs = tuple(jnp.moveaxis(t.astype(jnp.float32), 1, 0) for t in (r, w, k, v, a, b))
    _, y = lax.scan(step, jnp.zeros((B, H, N, N), jnp.float32), xs)
    return jnp.moveaxis(y, 0, 1)


def mla_branch(q_c, kv_c, k_rope, cos, sin, q_norm_g, w_uq, kv_norm_g, w_ukv):
    B, S, _ = q_c.shape
    q = (rms_norm(q_c, q_norm_g) @ w_uq).reshape(B, S, MLA_HEADS, MLA_QK)
    kv = (rms_norm(kv_c, kv_norm_g) @ w_ukv).reshape(B, S, MLA_HEADS, MLA_NOPE + MLA_V)
    q_nope, q_pe = jnp.split(q, [MLA_NOPE], axis=-1)
    k_nope, v = jnp.split(kv, [MLA_NOPE], axis=-1)
    q_pe = apply_rope(q_pe, cos, sin)
    k_pe = apply_rope(k_rope[:, :, None, :], cos, sin)
    q = jnp.concatenate([q_nope, q_pe], -1)
    k = jnp.concatenate([k_nope, jnp.broadcast_to(k_pe, (B, S, MLA_HEADS, MLA_ROPE))], -1)
    return chunk_causal_attention(q, k, v).reshape(B, S, MLA_WIDTH)


def rwkv7_branch(u, w0, w_decay_up, a0, w_iclr_up, k_k, k_a, r_k, gn_g, gn_b):
    B, S, _ = u.shape
    r, k, v, wd, ad = _split(u, (RWKV_WIDTH, RWKV_WIDTH, RWKV_WIDTH, DECAY_LORA, ICLR_LORA))
    w_log = -jax.nn.softplus(-(w0 + jnp.tanh(wd) @ w_decay_up)) - 0.5
    decay = jnp.exp(-jnp.exp(w_log.astype(jnp.float32)))
    a = jax.nn.sigmoid(a0 + ad @ w_iclr_up)
    hs = lambda t: t.reshape(B, S, RWKV_HEADS, RWKV_HEAD)
    kk = hs(k * k_k).astype(jnp.float32)
    kk = kk / jnp.maximum(jnp.sqrt(jnp.sum(kk * kk, -1, keepdims=True)), 1e-12)
    k = k * (1 + (a - 1) * k_a)
    r_h, k_h, v_h, a_h = hs(r), hs(k), hs(v), hs(a)
    y = wkv7(r_h, hs(decay), k_h, v_h, -kk, kk * a_h)
    y = y - jnp.mean(y, -1, keepdims=True)
    y = y * lax.rsqrt(jnp.mean(y * y, -1, keepdims=True) + GN_EPS)
    y = y.reshape(B, S, RWKV_WIDTH) * gn_g + gn_b
    bonus = jnp.sum((r_h * k_h * r_k).astype(jnp.float32), -1, keepdims=True) * v_h
    return (y + bonus.reshape(B, S, RWKV_WIDTH)).astype(u.dtype)


def hybrid_layer(x, c, cos, sin, w_ada, b_ada, w_in, q_norm_g, w_uq, kv_norm_g, w_ukv,
                 mu_rwkv, w0, w_decay_up, a0, w_iclr_up, k_k, k_a, r_k, gn_g, gn_b,
                 w_proj_a, w_proj_b, w_out, post_g, post_b):
    dt = x.dtype
    shift, scale, gate = jnp.split(jax.nn.silu(c) @ w_ada + b_ada, 3, axis=-1)
    h = (layer_norm(x) * (1 + scale[:, None]) + shift[:, None]).astype(dt)
    proj = h @ w_in
    q_c, kv_c, k_rope, gpath_a, rwkv_in, gpath_b, merge_a, merge_b = _split(proj, IN_SPLITS)
    y_a = mla_branch(q_c, kv_c, k_rope, cos, sin, q_norm_g, w_uq, kv_norm_g, w_ukv)
    y_b = rwkv7_branch(token_shift(rwkv_in, mu_rwkv), w0, w_decay_up, a0, w_iclr_up,
                       k_k, k_a, r_k, gn_g, gn_b)
    y_a = (y_a * jax.nn.silu(gpath_a)) @ w_proj_a
    y_b = (y_b * jax.nn.silu(gpath_b)) @ w_proj_b
    merged = jax.nn.sigmoid(merge_a) * y_a + jax.nn.sigmoid(merge_b) * y_b
    sub = merged @ w_out
    out = layer_norm(ALPHA * x + (1 + gate[:, None]) * sub) * post_g + post_b
    return out.astype(dt)


def _fwd_setup_inputs(seed: int = 0) -> dict:
    key = jax.random.key(seed)
    k = jax.random.split(key, 32)

    def nrm(i, shape, fan_in, gain=1.0):
        return jax.random.normal(k[i], (DEPTH,) + shape, jnp.float32) * (gain * fan_in ** -0.5)

    def near(i, shape, center, spread=0.02):
        return center + spread * jax.random.normal(k[i], (DEPTH,) + shape, jnp.float32)

    x = jax.random.normal(k[0], (BATCH, SEQ, D_MODEL), jnp.float32)
    c = jax.random.normal(k[1], (BATCH, D_MODEL), jnp.float32)
    positions = (jax.random.randint(k[2], (BATCH, 1), 0, 8192, dtype=jnp.int32)
                 + jnp.arange(SEQ, dtype=jnp.int32)[None, :])
    decay_base = -6.0 + 5.0 * jnp.linspace(0.0, 1.0, RWKV_WIDTH, dtype=jnp.float32) ** 0.9
    return {
        'x': x,
        'c': c,
        'positions': positions,
        'w_ada': nrm(3, (D_MODEL, 3 * D_MODEL), D_MODEL, 0.2),
        'b_ada': near(4, (3 * D_MODEL,), 0.0),
        'w_in': nrm(5, (D_MODEL, IN_WIDTH), D_MODEL),
        'q_norm_g': near(6, (MLA_Q_RANK,), 1.0),
        'w_uq': nrm(7, (MLA_Q_RANK, MLA_HEADS * MLA_QK), MLA_Q_RANK),
        'kv_norm_g': near(8, (MLA_KV_RANK,), 1.0),
        'w_ukv': nrm(9, (MLA_KV_RANK, MLA_HEADS * (MLA_NOPE + MLA_V)), MLA_KV_RANK),
        'mu_rwkv': jax.random.uniform(k[10], (DEPTH, RWKV_SHIFT_WIDTH), jnp.float32),
        'w0': decay_base + near(11, (RWKV_WIDTH,), 0.0, 0.1),
        'w_decay_up': nrm(12, (DECAY_LORA, RWKV_WIDTH), DECAY_LORA),
        'a0': near(13, (RWKV_WIDTH,), 0.0, 0.1),
        'w_iclr_up': nrm(14, (ICLR_LORA, RWKV_WIDTH), ICLR_LORA),
        'k_k': near(15, (RWKV_WIDTH,), 0.85),
        'k_a': near(16, (RWKV_WIDTH,), 1.0),
        'r_k': near(17, (RWKV_HEADS, RWKV_HEAD), 0.0, 0.1),
        'gn_g': near(18, (RWKV_WIDTH,), 1.0),
        'gn_b': near(19, (RWKV_WIDTH,), 0.0),
        'w_proj_a': nrm(20, (MLA_WIDTH, D_MODEL), MLA_WIDTH, BETA),
        'w_proj_b': nrm(21, (RWKV_WIDTH, D_MODEL), RWKV_WIDTH, BETA),
        'w_out': nrm(22, (D_MODEL, D_MODEL), D_MODEL, BETA),
        'post_g': near(23, (D_MODEL,), 1.0),
        'post_b': near(24, (D_MODEL,), 0.0),
    }


def _fwd_reference(x, c, positions, w_ada, b_ada, w_in, q_norm_g, w_uq, kv_norm_g, w_ukv,
              mu_rwkv, w0, w_decay_up, a0, w_iclr_up, k_k, k_a, r_k, gn_g, gn_b,
              w_proj_a, w_proj_b, w_out, post_g, post_b):
    cos, sin = rope_tables(positions)
    for l in range(DEPTH):
        x = hybrid_layer(x, c, cos, sin, w_ada[l], b_ada[l], w_in[l], q_norm_g[l], w_uq[l],
                         kv_norm_g[l], w_ukv[l], mu_rwkv[l], w0[l], w_decay_up[l], a0[l],
                         w_iclr_up[l], k_k[l], k_a[l], r_k[l], gn_g[l], gn_b[l],
                         w_proj_a[l], w_proj_b[l], w_out[l], post_g[l], post_b[l])
    return x


import jax as _jax
import jax.numpy as _jnp

TWIN_FORMAT = 'train_step'
FWD_PARAMS = ['x', 'c', 'positions', 'w_ada', 'b_ada', 'w_in', 'q_norm_g', 'w_uq', 'kv_norm_g', 'w_ukv', 'mu_rwkv', 'w0', 'w_decay_up', 'a0', 'w_iclr_up', 'k_k', 'k_a', 'r_k', 'gn_g', 'gn_b', 'w_proj_a', 'w_proj_b', 'w_out', 'post_g', 'post_b']
TWIN_WEIGHTS = ['w_ada', 'b_ada', 'w_in', 'q_norm_g', 'w_uq', 'kv_norm_g', 'w_ukv', 'mu_rwkv', 'w0', 'w_decay_up', 'a0', 'w_iclr_up', 'k_k', 'k_a', 'r_k', 'gn_g', 'gn_b', 'w_proj_a', 'w_proj_b', 'w_out', 'post_g', 'post_b']
TWIN_DIFF_INPUT = 'x'
TWIN_INPUTS = ['x', 'c', 'positions', 'w_ada', 'b_ada', 'w_in', 'q_norm_g', 'w_uq', 'kv_norm_g', 'w_ukv', 'mu_rwkv', 'w0', 'w_decay_up', 'a0', 'w_iclr_up', 'k_k', 'k_a', 'r_k', 'gn_g', 'gn_b', 'w_proj_a', 'w_proj_b', 'w_out', 'post_g', 'post_b', 'loss_target', 'm_w_ada', 'm_b_ada', 'm_w_in', 'm_q_norm_g', 'm_w_uq', 'm_kv_norm_g', 'm_w_ukv', 'm_mu_rwkv', 'm_w0', 'm_w_decay_up', 'm_a0', 'm_w_iclr_up', 'm_k_k', 'm_k_a', 'm_r_k', 'm_gn_g', 'm_gn_b', 'm_w_proj_a', 'm_w_proj_b', 'm_w_out', 'm_post_g', 'm_post_b', 'v_w_ada', 'v_b_ada', 'v_w_in', 'v_q_norm_g', 'v_w_uq', 'v_kv_norm_g', 'v_w_ukv', 'v_mu_rwkv', 'v_w0', 'v_w_decay_up', 'v_a0', 'v_w_iclr_up', 'v_k_k', 'v_k_a', 'v_r_k', 'v_gn_g', 'v_gn_b', 'v_w_proj_a', 'v_w_proj_b', 'v_w_out', 'v_post_g', 'v_post_b']
TWIN_OUTPUTS = ['loss', 'grad_x', 'grad_w_ada', 'grad_b_ada', 'grad_w_in', 'grad_q_norm_g', 'grad_w_uq', 'grad_kv_norm_g', 'grad_w_ukv', 'grad_mu_rwkv', 'grad_w0', 'grad_w_decay_up', 'grad_a0', 'grad_w_iclr_up', 'grad_k_k', 'grad_k_a', 'grad_r_k', 'grad_gn_g', 'grad_gn_b', 'grad_w_proj_a', 'grad_w_proj_b', 'grad_w_out', 'grad_post_g', 'grad_post_b', 'delta_w_ada', 'delta_b_ada', 'delta_w_in', 'delta_q_norm_g', 'delta_w_uq', 'delta_kv_norm_g', 'delta_w_ukv', 'delta_mu_rwkv', 'delta_w0', 'delta_w_decay_up', 'delta_a0', 'delta_w_iclr_up', 'delta_k_k', 'delta_k_a', 'delta_r_k', 'delta_gn_g', 'delta_gn_b', 'delta_w_proj_a', 'delta_w_proj_b', 'delta_w_out', 'delta_post_g', 'delta_post_b', 'new_m_w_ada', 'new_m_b_ada', 'new_m_w_in', 'new_m_q_norm_g', 'new_m_w_uq', 'new_m_kv_norm_g', 'new_m_w_ukv', 'new_m_mu_rwkv', 'new_m_w0', 'new_m_w_decay_up', 'new_m_a0', 'new_m_w_iclr_up', 'new_m_k_k', 'new_m_k_a', 'new_m_r_k', 'new_m_gn_g', 'new_m_gn_b', 'new_m_w_proj_a', 'new_m_w_proj_b', 'new_m_w_out', 'new_m_post_g', 'new_m_post_b', 'new_v_w_ada', 'new_v_b_ada', 'new_v_w_in', 'new_v_q_norm_g', 'new_v_w_uq', 'new_v_kv_norm_g', 'new_v_w_ukv', 'new_v_mu_rwkv', 'new_v_w0', 'new_v_w_decay_up', 'new_v_a0', 'new_v_w_iclr_up', 'new_v_k_k', 'new_v_k_a', 'new_v_r_k', 'new_v_gn_g', 'new_v_gn_b', 'new_v_w_proj_a', 'new_v_w_proj_b', 'new_v_w_out', 'new_v_post_g', 'new_v_post_b']
TWIN_LEAF_KINDS = {'loss': 'loss', 'grad_x': 'grad_x', 'grad_w_ada': 'grad_w', 'grad_b_ada': 'grad_w', 'grad_w_in': 'grad_w', 'grad_q_norm_g': 'grad_w', 'grad_w_uq': 'grad_w', 'grad_kv_norm_g': 'grad_w', 'grad_w_ukv': 'grad_w', 'grad_mu_rwkv': 'grad_w', 'grad_w0': 'grad_w', 'grad_w_decay_up': 'grad_w', 'grad_a0': 'grad_w', 'grad_w_iclr_up': 'grad_w', 'grad_k_k': 'grad_w', 'grad_k_a': 'grad_w', 'grad_r_k': 'grad_w', 'grad_gn_g': 'grad_w', 'grad_gn_b': 'grad_w', 'grad_w_proj_a': 'grad_w', 'grad_w_proj_b': 'grad_w', 'grad_w_out': 'grad_w', 'grad_post_g': 'grad_w', 'grad_post_b': 'grad_w', 'delta_w_ada': 'delta_w', 'delta_b_ada': 'delta_w', 'delta_w_in': 'delta_w', 'delta_q_norm_g': 'delta_w', 'delta_w_uq': 'delta_w', 'delta_kv_norm_g': 'delta_w', 'delta_w_ukv': 'delta_w', 'delta_mu_rwkv': 'delta_w', 'delta_w0': 'delta_w', 'delta_w_decay_up': 'delta_w', 'delta_a0': 'delta_w', 'delta_w_iclr_up': 'delta_w', 'delta_k_k': 'delta_w', 'delta_k_a': 'delta_w', 'delta_r_k': 'delta_w', 'delta_gn_g': 'delta_w', 'delta_gn_b': 'delta_w', 'delta_w_proj_a': 'delta_w', 'delta_w_proj_b': 'delta_w', 'delta_w_out': 'delta_w', 'delta_post_g': 'delta_w', 'delta_post_b': 'delta_w', 'new_m_w_ada': 'new_m', 'new_m_b_ada': 'new_m', 'new_m_w_in': 'new_m', 'new_m_q_norm_g': 'new_m', 'new_m_w_uq': 'new_m', 'new_m_kv_norm_g': 'new_m', 'new_m_w_ukv': 'new_m', 'new_m_mu_rwkv': 'new_m', 'new_m_w0': 'new_m', 'new_m_w_decay_up': 'new_m', 'new_m_a0': 'new_m', 'new_m_w_iclr_up': 'new_m', 'new_m_k_k': 'new_m', 'new_m_k_a': 'new_m', 'new_m_r_k': 'new_m', 'new_m_gn_g': 'new_m', 'new_m_gn_b': 'new_m', 'new_m_w_proj_a': 'new_m', 'new_m_w_proj_b': 'new_m', 'new_m_w_out': 'new_m', 'new_m_post_g': 'new_m', 'new_m_post_b': 'new_m', 'new_v_w_ada': 'new_v', 'new_v_b_ada': 'new_v', 'new_v_w_in': 'new_v', 'new_v_q_norm_g': 'new_v', 'new_v_w_uq': 'new_v', 'new_v_kv_norm_g': 'new_v', 'new_v_w_ukv': 'new_v', 'new_v_mu_rwkv': 'new_v', 'new_v_w0': 'new_v', 'new_v_w_decay_up': 'new_v', 'new_v_a0': 'new_v', 'new_v_w_iclr_up': 'new_v', 'new_v_k_k': 'new_v', 'new_v_k_a': 'new_v', 'new_v_r_k': 'new_v', 'new_v_gn_g': 'new_v', 'new_v_gn_b': 'new_v', 'new_v_w_proj_a': 'new_v', 'new_v_w_proj_b': 'new_v', 'new_v_w_out': 'new_v', 'new_v_post_g': 'new_v', 'new_v_post_b': 'new_v'}


def _forward(args):
    return _fwd_reference(*[args[k] for k in FWD_PARAMS])


def _output_shape():
    out = _jax.eval_shape(lambda: _forward(_fwd_setup_inputs(0)))
    return out.shape, out.dtype

N_MICROBATCH = 1
ADAM_LR = 0.001
ADAM_B1 = 0.9
ADAM_B2 = 0.999
ADAM_EPS = 1e-08
ADAM_WD = 0.01
ADAM_STEP = 10
PER_EXAMPLE_BATCH_AXIS = {'x': 0, 'c': 0, 'positions': 0, 'loss_target': 0}
SHARED_INPUTS = []
_WEIGHT_DTYPES = {'w_ada': _jnp.float32, 'b_ada': _jnp.float32, 'w_in': _jnp.float32, 'q_norm_g': _jnp.float32, 'w_uq': _jnp.float32, 'kv_norm_g': _jnp.float32, 'w_ukv': _jnp.float32, 'mu_rwkv': _jnp.float32, 'w0': _jnp.float32, 'w_decay_up': _jnp.float32, 'a0': _jnp.float32, 'w_iclr_up': _jnp.float32, 'k_k': _jnp.float32, 'k_a': _jnp.float32, 'r_k': _jnp.float32, 'gn_g': _jnp.float32, 'gn_b': _jnp.float32, 'w_proj_a': _jnp.float32, 'w_proj_b': _jnp.float32, 'w_out': _jnp.float32, 'post_g': _jnp.float32, 'post_b': _jnp.float32}
MOMENT_SCALE = {'w_ada': 2.390567e-02, 'b_ada': 4.087651e-02, 'w_in': 1.772786e-02, 'q_norm_g': 5.249208e-03, 'w_uq': 3.156483e-03, 'kv_norm_g': 1.243998e-02, 'w_ukv': 4.200936e-03, 'mu_rwkv': 4.415131e-02, 'w0': 1.075199e-02, 'w_decay_up': 2.308710e-03, 'a0': 1.066162e-02, 'w_iclr_up': 9.162571e-03, 'k_k': 2.064347e-02, 'k_a': 2.952903e-02, 'r_k': 5.956229e-02, 'gn_g': 2.607188e-02, 'gn_b': 2.832800e-02, 'w_proj_a': 5.848641e-03, 'w_proj_b': 3.018656e-02, 'w_out': 3.067423e-02, 'post_g': 3.194954e+01, 'post_b': 4.892174e-01}


def _to_microbatches(a, axis):
    t = _jnp.moveaxis(a, axis, 0)
    t = t.reshape((N_MICROBATCH, t.shape[0] // N_MICROBATCH) + t.shape[1:])
    return _jnp.moveaxis(t, 1, axis + 1)


def setup_inputs(seed: int = 0) -> dict:
    inp = _fwd_setup_inputs(seed)
    key = _jax.random.fold_in(_jax.random.key(seed), 7919)
    shape, _ = _output_shape()
    out = dict(inp)
    out["loss_target"] = _jax.random.normal(_jax.random.fold_in(key, 0), shape, _jnp.float32)
    for i, name in enumerate(TWIN_WEIGHTS):
        w = inp[name].astype(_jnp.float32)
        if MOMENT_SCALE is None:
            s = _jnp.sqrt(_jnp.mean(_jnp.square(w)) + 1e-30)
        else:
            s = MOMENT_SCALE[name]
        km, kv = _jax.random.split(_jax.random.fold_in(key, i + 1))
        out[name] = w
        out["m_" + name] = s * _jax.random.normal(km, w.shape, _jnp.float32)
        out["v_" + name] = (s * s) * _jax.random.uniform(kv, w.shape, _jnp.float32, 0.5, 1.5)
    if N_MICROBATCH > 1:
        for name, axis in PER_EXAMPLE_BATCH_AXIS.items():
            out[name] = _to_microbatches(out[name], axis)
    return {'x': out['x'], 'c': out['c'], 'positions': out['positions'], 'w_ada': out['w_ada'], 'b_ada': out['b_ada'], 'w_in': out['w_in'], 'q_norm_g': out['q_norm_g'], 'w_uq': out['w_uq'], 'kv_norm_g': out['kv_norm_g'], 'w_ukv': out['w_ukv'], 'mu_rwkv': out['mu_rwkv'], 'w0': out['w0'], 'w_decay_up': out['w_decay_up'], 'a0': out['a0'], 'w_iclr_up': out['w_iclr_up'], 'k_k': out['k_k'], 'k_a': out['k_a'], 'r_k': out['r_k'], 'gn_g': out['gn_g'], 'gn_b': out['gn_b'], 'w_proj_a': out['w_proj_a'], 'w_proj_b': out['w_proj_b'], 'w_out': out['w_out'], 'post_g': out['post_g'], 'post_b': out['post_b'], 'loss_target': out['loss_target'], 'm_w_ada': out['m_w_ada'], 'm_b_ada': out['m_b_ada'], 'm_w_in': out['m_w_in'], 'm_q_norm_g': out['m_q_norm_g'], 'm_w_uq': out['m_w_uq'], 'm_kv_norm_g': out['m_kv_norm_g'], 'm_w_ukv': out['m_w_ukv'], 'm_mu_rwkv': out['m_mu_rwkv'], 'm_w0': out['m_w0'], 'm_w_decay_up': out['m_w_decay_up'], 'm_a0': out['m_a0'], 'm_w_iclr_up': out['m_w_iclr_up'], 'm_k_k': out['m_k_k'], 'm_k_a': out['m_k_a'], 'm_r_k': out['m_r_k'], 'm_gn_g': out['m_gn_g'], 'm_gn_b': out['m_gn_b'], 'm_w_proj_a': out['m_w_proj_a'], 'm_w_proj_b': out['m_w_proj_b'], 'm_w_out': out['m_w_out'], 'm_post_g': out['m_post_g'], 'm_post_b': out['m_post_b'], 'v_w_ada': out['v_w_ada'], 'v_b_ada': out['v_b_ada'], 'v_w_in': out['v_w_in'], 'v_q_norm_g': out['v_q_norm_g'], 'v_w_uq': out['v_w_uq'], 'v_kv_norm_g': out['v_kv_norm_g'], 'v_w_ukv': out['v_w_ukv'], 'v_mu_rwkv': out['v_mu_rwkv'], 'v_w0': out['v_w0'], 'v_w_decay_up': out['v_w_decay_up'], 'v_a0': out['v_a0'], 'v_w_iclr_up': out['v_w_iclr_up'], 'v_k_k': out['v_k_k'], 'v_k_a': out['v_k_a'], 'v_r_k': out['v_r_k'], 'v_gn_g': out['v_gn_g'], 'v_gn_b': out['v_gn_b'], 'v_w_proj_a': out['v_w_proj_a'], 'v_w_proj_b': out['v_w_proj_b'], 'v_w_out': out['v_w_out'], 'v_post_g': out['v_post_g'], 'v_post_b': out['v_post_b']}


def _loss(weights, diff, rest, loss_target):
    with _jax.named_scope("forward"):
        args = {**rest, TWIN_DIFF_INPUT: diff, **{k: w.astype(_WEIGHT_DTYPES[k]) for k, w in weights.items()}}
        y = _forward(args)
    with _jax.named_scope("loss_head"):
        err = _jnp.square(y.astype(_jnp.float32) - loss_target)
        return 0.5 * _jnp.sum(_jnp.mean(err, axis=-1)) if err.ndim else 0.5 * err


def _adamw(w, g, m, v):
    m = ADAM_B1 * m + (1.0 - ADAM_B1) * g
    v = ADAM_B2 * v + (1.0 - ADAM_B2) * _jnp.square(g)
    m_hat = m / (1.0 - ADAM_B1 ** ADAM_STEP)
    v_hat = v / (1.0 - ADAM_B2 ** ADAM_STEP)
    delta = -ADAM_LR * (m_hat / (_jnp.sqrt(v_hat) + ADAM_EPS) + ADAM_WD * w)
    return delta, m, v


def reference(x, c, positions, w_ada, b_ada, w_in, q_norm_g, w_uq, kv_norm_g, w_ukv, mu_rwkv, w0, w_decay_up, a0, w_iclr_up, k_k, k_a, r_k, gn_g, gn_b, w_proj_a, w_proj_b, w_out, post_g, post_b, loss_target, m_w_ada, m_b_ada, m_w_in, m_q_norm_g, m_w_uq, m_kv_norm_g, m_w_ukv, m_mu_rwkv, m_w0, m_w_decay_up, m_a0, m_w_iclr_up, m_k_k, m_k_a, m_r_k, m_gn_g, m_gn_b, m_w_proj_a, m_w_proj_b, m_w_out, m_post_g, m_post_b, v_w_ada, v_b_ada, v_w_in, v_q_norm_g, v_w_uq, v_kv_norm_g, v_w_ukv, v_mu_rwkv, v_w0, v_w_decay_up, v_a0, v_w_iclr_up, v_k_k, v_k_a, v_r_k, v_gn_g, v_gn_b, v_w_proj_a, v_w_proj_b, v_w_out, v_post_g, v_post_b):
    given = dict(x=x, c=c, positions=positions, w_ada=w_ada, b_ada=b_ada, w_in=w_in, q_norm_g=q_norm_g, w_uq=w_uq, kv_norm_g=kv_norm_g, w_ukv=w_ukv, mu_rwkv=mu_rwkv, w0=w0, w_decay_up=w_decay_up, a0=a0, w_iclr_up=w_iclr_up, k_k=k_k, k_a=k_a, r_k=r_k, gn_g=gn_g, gn_b=gn_b, w_proj_a=w_proj_a, w_proj_b=w_proj_b, w_out=w_out, post_g=post_g, post_b=post_b, loss_target=loss_target, m_w_ada=m_w_ada, m_b_ada=m_b_ada, m_w_in=m_w_in, m_q_norm_g=m_q_norm_g, m_w_uq=m_w_uq, m_kv_norm_g=m_kv_norm_g, m_w_ukv=m_w_ukv, m_mu_rwkv=m_mu_rwkv, m_w0=m_w0, m_w_decay_up=m_w_decay_up, m_a0=m_a0, m_w_iclr_up=m_w_iclr_up, m_k_k=m_k_k, m_k_a=m_k_a, m_r_k=m_r_k, m_gn_g=m_gn_g, m_gn_b=m_gn_b, m_w_proj_a=m_w_proj_a, m_w_proj_b=m_w_proj_b, m_w_out=m_w_out, m_post_g=m_post_g, m_post_b=m_post_b, v_w_ada=v_w_ada, v_b_ada=v_b_ada, v_w_in=v_w_in, v_q_norm_g=v_q_norm_g, v_w_uq=v_w_uq, v_kv_norm_g=v_kv_norm_g, v_w_ukv=v_w_ukv, v_mu_rwkv=v_mu_rwkv, v_w0=v_w0, v_w_decay_up=v_w_decay_up, v_a0=v_a0, v_w_iclr_up=v_w_iclr_up, v_k_k=v_k_k, v_k_a=v_k_a, v_r_k=v_r_k, v_gn_g=v_gn_g, v_gn_b=v_gn_b, v_w_proj_a=v_w_proj_a, v_w_proj_b=v_w_proj_b, v_w_out=v_w_out, v_post_g=v_post_g, v_post_b=v_post_b)
    weights = {n: given[n] for n in TWIN_WEIGHTS}
    shared = {n: given[n] for n in SHARED_INPUTS}
    per_example = {n: given[n] for n in ['x', 'c', 'positions']}
    grad_fn = _jax.value_and_grad(_loss, argnums=(0, 1))

    def one_microbatch(ex, loss_target):
        ex = dict(ex)
        diff = ex.pop(TWIN_DIFF_INPUT)
        return grad_fn(weights, diff, {**shared, **ex}, loss_target)

    if N_MICROBATCH == 1:
        loss, (grad_w, grad_x) = one_microbatch(per_example, given["loss_target"])
    else:
        def body(carry, xs):
            loss_sum, grad_sum = carry
            l_k, (gw_k, gx_k) = one_microbatch(xs[0], xs[1])
            with _jax.named_scope("update"):
                return (loss_sum + l_k, _jax.tree.map(_jnp.add, grad_sum, gw_k)), gx_k

        init = (_jnp.zeros((), _jnp.float32), _jax.tree.map(_jnp.zeros_like, weights))
        (loss, grad_w), grad_x = _jax.lax.scan(body, init, (per_example, given["loss_target"]))
    with _jax.named_scope("update"):
        delta_w, new_m, new_v = {}, {}, {}
        for n in TWIN_WEIGHTS:
            delta_w[n], new_m[n], new_v[n] = _adamw(weights[n], grad_w[n], given["m_" + n], given["v_" + n])
    return (loss, grad_x, *[grad_w[n] for n in TWIN_WEIGHTS], *[delta_w[n] for n in TWIN_WEIGHTS],
            *[new_m[n] for n in TWIN_WEIGHTS], *[new_v[n] for n in TWIN_WEIGHTS])
```

```python
import numpy as np
import jax
import jax.numpy as jnp
from jax import lax
from jax.experimental import pallas as pl
from jax.experimental.pallas import tpu as pltpu

F32 = jnp.float32
BF16 = jnp.bfloat16

D_MODEL = 1024
LN_EPS = 1e-5
RMS_EPS = 1e-6
GN_EPS = 64e-5
HEADS = 8
HEAD = 64
MLA_ROPE = 32
MLA_QK = HEAD + MLA_ROPE
ROPE_THETA = 10000.0
WIDTH = HEADS * HEAD
LORA = 64
SHIFT_W = 3 * WIDTH + 2 * LORA
CHUNK = 64
ALPHA = 2.0 ** 0.25

ADAM_LR, ADAM_B1, ADAM_B2, ADAM_EPS, ADAM_WD, ADAM_STEP = 0.001, 0.9, 0.999, 1e-08, 0.01, 10

LANES = 128
SUBLANES = 8
VMEM_LIMIT = 56 * 1024 * 1024
N_DEV = 8
MESH = pl.DeviceIdType.MESH
NEG = -1e30

_WEIGHTS = ['w_ada', 'b_ada', 'w_in', 'q_norm_g', 'w_uq', 'kv_norm_g', 'w_ukv', 'mu_rwkv', 'w0',
            'w_decay_up', 'a0', 'w_iclr_up', 'k_k', 'k_a', 'r_k', 'gn_g', 'gn_b', 'w_proj_a',
            'w_proj_b', 'w_out', 'post_g', 'post_b']
_SHARDED = [('w_ada', (1024, 3072), 1), ('w_in', (1024, 5152), 1), ('w_uq', (256, 768), 1),
            ('w_ukv', (128, 1024), 1), ('w_decay_up', (64, 512), 1), ('w_iclr_up', (64, 512), 1),
            ('w_proj_a', (512, 1024), 1), ('w_proj_b', (512, 1024), 1), ('w_out', (1024, 1024), 0)]
_SMALL = [('b_ada', 3072), ('q_norm_g', 256), ('kv_norm_g', 128), ('mu_rwkv', 1664), ('w0', 512),
          ('a0', 512), ('k_k', 512), ('k_a', 512), ('r_k', 512), ('gn_g', 512), ('gn_b', 512),
          ('post_g', 1024), ('post_b', 1024)]
PACK_ROWS = 256
PACK_COLS = 11264
HALF_COLS = PACK_COLS // 2
ADA_COLS = 4 * 768
GRAD_HALF = (PACK_COLS - ADA_COLS) // 2
SMALL_USED = 84
SMALL_ROWS = 96


def _bf(x):
    return x.astype(BF16)


def _dot(a, b, ca, cb):
    return lax.dot_general(a, b, (((ca,), (cb,)), ((), ())), preferred_element_type=F32)


@jax.custom_vjp
def mm(a, w):
    return _dot(_bf(a), _bf(w), 1, 0)


def _mm_fwd(a, w):
    return mm(a, w), (a, w)


def _mm_bwd(res, g):
    a, w = res
    gb = _bf(g)
    return _dot(gb, _bf(w), 1, 1), _dot(_bf(a), gb, 0, 0).astype(w.dtype)


mm.defvjp(_mm_fwd, _mm_bwd)


def _split3(x):
    hi = _bf(x)
    r1 = x - hi.astype(F32)
    mid = _bf(r1)
    lo = _bf(r1 - mid.astype(F32))
    return hi, mid, lo


def _exact_dot(x, m, cm):
    hi, mid, lo = _split3(x)
    return _dot(hi, m, 1, cm) + _dot(mid, m, 1, cm) + _dot(lo, m, 1, cm)


@jax.custom_vjp
def segsum(x, ones_blocks):
    return _exact_dot(x, ones_blocks, 0)


def _segsum_fwd(x, ones_blocks):
    return segsum(x, ones_blocks), ones_blocks


def _segsum_bwd(ones_blocks, g):
    return _exact_dot(g, ones_blocks, 0), jnp.zeros_like(ones_blocks)


segsum.defvjp(_segsum_fwd, _segsum_bwd)


@jax.custom_vjp
def lane_perm(x, perm):
    return _exact_dot(x, perm, 0)


def _lane_perm_fwd(x, perm):
    return lane_perm(x, perm), perm


def _lane_perm_bwd(perm, g):
    return _exact_dot(g, perm, 1), jnp.zeros_like(perm)


lane_perm.defvjp(_lane_perm_fwd, _lane_perm_bwd)


def _silu(z):
    return z * jax.nn.sigmoid(z)


def _softplus(z):
    return jnp.maximum(z, 0.0) + jnp.log(1.0 + jnp.exp(-jnp.abs(z)))


def _layer_norm(x):
    xc = x - jnp.mean(x, -1, keepdims=True)
    return xc * lax.rsqrt(jnp.mean(xc * xc, -1, keepdims=True) + LN_EPS)


def _rope(t, cos_t, sin_t, perm):
    outs = []
    for h in range(t.shape[1] // LANES):
        th = t[:, h * LANES:(h + 1) * LANES]
        outs.append(th * cos_t + lane_perm(th, perm) * sin_t)
    return outs[0] if len(outs) == 1 else jnp.concatenate(outs, axis=1)


def f_ada(c8, b_ada, w_ada):
    return (mm(_silu(c8), w_ada) + b_ada,)


def _make_f_in(splits):
    def f_in(x, shift, scale, w):
        h = _layer_norm(x) * (1.0 + scale) + shift
        p = mm(h, w)
        outs, o = [], 0
        for s in splits:
            outs.append(p[:, o:o + s])
            o += s
        return tuple(outs)
    return f_in


def f_mla_pre(p, cs, qg, kvg, w_uq, w_ukv, perm):
    q_c, kv_c, k_r = p[:, :256], p[:, 256:384], p[:, 384:512]
    cos_t, sin_t = cs[:, :LANES], cs[:, LANES:]
    qn = q_c * lax.rsqrt(jnp.mean(q_c * q_c, -1, keepdims=True) + RMS_EPS) * qg
    kvn = kv_c * lax.rsqrt(jnp.mean(kv_c * kv_c, -1, keepdims=True) + RMS_EPS) * kvg
    q = _rope(mm(qn, w_uq), cos_t, sin_t, perm)
    kv = mm(kvn, w_ukv)
    return q, kv, _rope(k_r, cos_t, sin_t, perm)


def f_rwkv_pre(u, w0, a0, k_k, k_a, w_lora, ones_blocks):
    r, k, v, lo = u[:, :WIDTH], u[:, WIDTH:2 * WIDTH], u[:, 2 * WIDTH:3 * WIDTH], u[:, 3 * WIDTH:]
    lane = lax.broadcasted_iota(jnp.int32, lo.shape, 1)
    dl = mm(jnp.where(lane < LORA, jnp.tanh(lo), lo), w_lora)
    w_log = -_softplus(-(w0 + dl[:, :WIDTH])) - 0.5
    decay = jnp.exp(-jnp.exp(w_log))
    a = jax.nn.sigmoid(a0 + dl[:, WIDTH:])
    kk = k * k_k
    kk = kk / jnp.maximum(jnp.sqrt(segsum(kk * kk, ones_blocks)), 1e-12)
    k2 = k * (1.0 + (a - 1.0) * k_a)
    return r, decay, k2, v, -kk, kk * a


def f_rwkv_post(y, r, k2, v, r_k, gn_g, gn_b, ones_blocks):
    yc = y - segsum(y, ones_blocks) * (1.0 / HEAD)
    yn = yc * lax.rsqrt(segsum(yc * yc, ones_blocks) * (1.0 / HEAD) + GN_EPS)
    return (yn * gn_g + gn_b + segsum(r * k2 * r_k, ones_blocks) * v,)


def f_post(ya, gpa, yb, gpb, ma, mb, x, tgt, gate, post_g, post_b, w_pa, w_pb, w_out):
    pa = mm(ya * _silu(gpa), w_pa)
    pb = mm(yb * _silu(gpb), w_pb)
    merged = jax.nn.sigmoid(ma) * pa + jax.nn.sigmoid(mb) * pb
    z = ALPHA * x + (1.0 + gate) * mm(merged, w_out)
    err = _layer_norm(z) * post_g + post_b - tgt
    lrow = 0.5 * jnp.mean(err * err, -1, keepdims=True)
    return (jnp.broadcast_to(lrow, (lrow.shape[0], LANES)),)


def f_adamw(w, g, m, v):
    m2 = ADAM_B1 * m + (1.0 - ADAM_B1) * g
    v2 = ADAM_B2 * v + (1.0 - ADAM_B2) * jnp.square(g)
    m_hat = m2 / (1.0 - ADAM_B1 ** ADAM_STEP)
    v_hat = v2 / (1.0 - ADAM_B2 ** ADAM_STEP)
    return -ADAM_LR * (m_hat / (jnp.sqrt(v_hat) + ADAM_EPS) + ADAM_WD * w), m2, v2


def _params(sem=("arbitrary",)):
    return pltpu.CompilerParams(dimension_semantics=sem, vmem_limit_bytes=VMEM_LIMIT)


def _row_spec(tr, a):
    return pl.BlockSpec((tr, a.shape[1]), lambda i: (i, 0))


def _full_spec(a):
    return pl.BlockSpec(a.shape, lambda i: (0,) * a.ndim)


def row_fwd(name, f, rows, params, consts, out_widths, tr, out_dtype=F32):
    n_rows = rows[0].shape[0]
    nr, npar, ncon = len(rows), len(params), len(consts)

    def body(*refs):
        rv = [r[...] for r in refs[:nr]]
        pv = [r[...].astype(F32) for r in refs[nr:nr + npar]]
        cv = [r[...] for r in refs[nr + npar:nr + npar + ncon]]
        outs = f(*rv, *pv, *cv)
        for o_ref, o in zip(refs[nr + npar + ncon:], outs):
            o_ref[...] = o.astype(o_ref.dtype)

    return pl.pallas_call(
        body, name=name, grid=(n_rows // tr,),
        in_specs=[_row_spec(tr, a) for a in rows] + [_full_spec(a) for a in list(params) + list(consts)],
        out_specs=[pl.BlockSpec((tr, w), lambda i: (i, 0)) for w in out_widths],
        out_shape=[jax.ShapeDtypeStruct((n_rows, w), out_dtype) for w in out_widths],
        compiler_params=_params(),
    )(*rows, *params, *consts)


def row_bwd(name, f, rows, n_diff, params, consts, douts, tr, add_rows=None):
    n_rows = rows[0].shape[0]
    douts = [d if isinstance(d, (tuple, list)) else (d,) for d in douts]
    counts = [len(d) for d in douts]
    flat_d = [a for d in douts for a in d]
    add_rows = add_rows or [None] * n_diff
    adds = [a for a in add_rows if a is not None]
    nr, npar, ncon, nd, na = len(rows), len(params), len(consts), len(flat_d), len(adds)

    def body(*refs):
        o = 0
        rv = [r[...] for r in refs[o:o + nr]]; o += nr
        pv = [r[...].astype(F32) for r in refs[o:o + npar]]; o += npar
        cv = [r[...] for r in refs[o:o + ncon]]; o += ncon
        dv = []
        for cnt in counts:
            s = refs[o][...]
            for e in range(1, cnt):
                s = s + refs[o + e][...]
            dv.append(s)
            o += cnt
        add_v = [r[...] for r in refs[o:o + na]]; o += na
        drow_refs = refs[o:o + n_diff]; o += n_diff
        dpar_refs = refs[o:o + npar]

        def g(*args):
            return tuple(f(*args[:n_diff], *rv[n_diff:], *args[n_diff:], *cv))

        _, vjp = jax.vjp(g, *rv[:n_diff], *pv)
        grads = vjp(tuple(dv))
        ai = 0
        for j, (r, gr) in enumerate(zip(drow_refs, grads[:n_diff])):
            if add_rows[j] is not None:
                gr = gr + add_v[ai]
                ai += 1
            r[...] = gr

        @pl.when(pl.program_id(0) == 0)
        def _():
            for r in dpar_refs:
                r[...] = jnp.zeros_like(r)

        for r, gr in zip(dpar_refs, grads[n_diff:]):
            r[...] += gr

    outs = pl.pallas_call(
        body, name=name, grid=(n_rows // tr,),
        in_specs=([_row_spec(tr, a) for a in rows] + [_full_spec(a) for a in list(params) + list(consts)]
                  + [_row_spec(tr, a) for a in flat_d + adds]),
        out_specs=[_row_spec(tr, a) for a in rows[:n_diff]] + [_full_spec(a) for a in params],
        out_shape=([jax.ShapeDtypeStruct(a.shape, F32) for a in rows[:n_diff]]
                   + [jax.ShapeDtypeStruct(a.shape, F32) for a in params]),
        compiler_params=_params(),
    )(*rows, *params, *consts, *flat_d, *adds)
    return outs[:n_diff], outs[n_diff:]


def shift_fwd(p, mu, tr):
    n_rows, w = p.shape

    def body(p_ref, mu_ref, u_ref, carry):
        @pl.when(pl.program_id(0) == 0)
        def _():
            carry[...] = jnp.zeros_like(carry)

        x = p_ref[...]
        rolled = pltpu.roll(x, 1, 0)
        head = pltpu.roll(carry[...], 1, 0)
        fixed = jnp.concatenate([head, rolled[SUBLANES:]], axis=0)
        row = lax.broadcasted_iota(jnp.int32, x.shape, 0)
        prev = jnp.where(row == 0, fixed, rolled)
        u_ref[...] = x + (prev - x) * mu_ref[...]
        carry[...] = x[tr - SUBLANES:]

    return pl.pallas_call(
        body, name="shift_fwd", grid=(n_rows // tr,),
        in_specs=[_row_spec(tr, p), _full_spec(mu)],
        out_specs=_row_spec(tr, p),
        out_shape=jax.ShapeDtypeStruct(p.shape, F32),
        scratch_shapes=[pltpu.VMEM((SUBLANES, w), F32)],
        compiler_params=_params(),
    )(p, mu)


def shift_bwd(du, p, mu, tr):
    n_rows, w = p.shape
    nb = n_rows // tr

    def body(du_ref, p_ref, mu_ref, dp_ref, dmu_ref, carry):
        @pl.when(pl.program_id(0) == 0)
        def _():
            carry[...] = jnp.zeros_like(carry)
            dmu_ref[...] = jnp.zeros_like(dmu_ref)

        d = du_ref[...]
        rolled = pltpu.roll(d, tr - 1, 0)
        tail = pltpu.roll(carry[...], SUBLANES - 1, 0)
        fixed = jnp.concatenate([rolled[:tr - SUBLANES], tail], axis=0)
        row = lax.broadcasted_iota(jnp.int32, d.shape, 0)
        nxt = jnp.where(row == tr - 1, fixed, rolled)
        mu_v = mu_ref[...]
        dp_ref[...] = d * (1.0 - mu_v) + nxt * mu_v
        dmu_ref[...] += jnp.sum(p_ref[...] * (nxt - d), axis=0, keepdims=True)
        carry[...] = d[:SUBLANES]

    rev = lambda i: (nb - 1 - i, 0)
    return pl.pallas_call(
        body, name="shift_bwd", grid=(nb,),
        in_specs=[pl.BlockSpec((tr, w), rev), pl.BlockSpec((tr, w), rev), _full_spec(mu)],
        out_specs=[pl.BlockSpec((tr, w), rev), _full_spec(mu)],
        out_shape=[jax.ShapeDtypeStruct(p.shape, F32), jax.ShapeDtypeStruct(mu.shape, F32)],
        scratch_shapes=[pltpu.VMEM((SUBLANES, w), F32)],
        compiler_params=_params(),
    )(du, p, mu)


ATT_T = 256


def _att_rows(j):
    return pl.ds(pl.multiple_of(j * ATT_T, ATT_T), ATT_T)


def _att_prep(kv_ref, kpe_ref, kf_scr, vf_scr, n_blocks):
    lane = lax.broadcasted_iota(jnp.int32, (ATT_T, LANES), 1)

    def prep(j, _):
        rows = _att_rows(j)
        kv = kv_ref[rows, :]
        kf_scr[rows, :] = _bf(jnp.where(lane < HEAD, kv, kpe_ref[rows, :]))
        vf_scr[rows, :] = _bf(jnp.where(lane >= HEAD, kv, 0.0))
        return 0

    lax.fori_loop(0, n_blocks, prep, 0)


def _att_diag_mask():
    shift = CHUNK.bit_length() - 1
    qc = jnp.right_shift(lax.broadcasted_iota(jnp.int32, (ATT_T, ATT_T), 0), shift)
    kc = jnp.right_shift(lax.broadcasted_iota(jnp.int32, (ATT_T, ATT_T), 1), shift)
    return kc <= qc


def _wide(x):
    return jnp.concatenate([x] * (ATT_T // LANES), axis=1)


def attn_fwd(q, kv, kpe):
    seq = q.shape[0]
    nb = seq // ATT_T
    assert seq % (2 * ATT_T) == 0, "blocks are taken two per trip"
    scale = MLA_QK ** -0.5

    def body(q_ref, kv_ref, kpe_ref, o_ref, lse_ref, kf_scr, vf_scr):
        _att_prep(kv_ref, kpe_ref, kf_scr, vf_scr, nb)
        mask = _att_diag_mask()

        def scores(qb, kj):
            return _dot(qb, kf_scr[_att_rows(kj), :], 1, 1) * scale

        def update(s, kj, carry, masked):
            m, l, acc = carry
            if masked:
                s = jnp.where(mask, s, NEG)
            m_new = jnp.maximum(m, jnp.broadcast_to(jnp.max(s, -1, keepdims=True), m.shape))
            alpha = jnp.exp(m - m_new)
            p = jnp.exp(s - _wide(m_new))
            l = alpha * l + jnp.broadcast_to(jnp.sum(p, -1, keepdims=True), l.shape)
            acc = alpha * acc + _dot(_bf(p), vf_scr[_att_rows(kj), :], 1, 0)
            return m_new, l, acc

        def finish(rows, carry):
            m, l, acc = carry
            o_ref[rows, :] = acc / l
            lse_ref[rows, :] = m + jnp.log(l)

        def q_pair(qp, _):
            rows_a, rows_b = _att_rows(2 * qp), _att_rows(2 * qp + 1)
            qa, qb = _bf(q_ref[rows_a, :]), _bf(q_ref[rows_b, :])
            init = (jnp.full((ATT_T, LANES), NEG, F32), jnp.zeros((ATT_T, LANES), F32),
                    jnp.zeros((ATT_T, LANES), F32))

            def trip(kj, c):
                ca, cb, sa, sb = c
                sa_next, sb_next = scores(qa, kj + 1), scores(qb, kj + 1)
                return update(sa, kj, ca, False), update(sb, kj, cb, False), sa_next, sb_next

            ca, cb, sa, sb = lax.fori_loop(0, 2 * qp, trip, (init, init, scores(qa, 0), scores(qb, 0)))
            sb_last = scores(qb, 2 * qp + 1)
            ca = update(sa, 2 * qp, ca, True)
            cb = update(sb_last, 2 * qp + 1, update(sb, 2 * qp, cb, False), True)
            finish(rows_a, ca)
            finish(rows_b, cb)
            return 0

        lax.fori_loop(0, nb // 2, q_pair, 0)

    head = pl.BlockSpec((seq, LANES), lambda h: (0, h))
    return pl.pallas_call(
        body, name="attn_fwd", grid=(HEADS,),
        in_specs=[head, head, pl.BlockSpec((seq, LANES), lambda h: (0, 0))],
        out_specs=[head, head],
        out_shape=[jax.ShapeDtypeStruct((seq, HEADS * LANES), F32)] * 2,
        scratch_shapes=[pltpu.VMEM((seq, LANES), BF16)] * 2,
        compiler_params=_params(),
    )(q, kv, kpe)


def attn_bwd(q, kv, kpe, o, lse, do):
    seq = q.shape[0]
    nb = seq // ATT_T
    assert seq % (2 * ATT_T) == 0, "blocks are taken two per trip"
    scale = MLA_QK ** -0.5

    def body(q_ref, kv_ref, kpe_ref, o_ref, lse_ref, do_ref, dq_ref, dkv_ref, dkpe_ref,
             kf_scr, vf_scr, qb_scr, dob_scr, dsum):
        lane = lax.broadcasted_iota(jnp.int32, (ATT_T, LANES), 1)

        @pl.when(pl.program_id(0) == 0)
        def _():
            dkpe_ref[...] = jnp.zeros_like(dkpe_ref)

        dq_ref[...] = jnp.zeros_like(dq_ref)
        _att_prep(kv_ref, kpe_ref, kf_scr, vf_scr, nb)

        def pre(j, _):
            rows = _att_rows(j)
            d = do_ref[rows, :]
            qb_scr[rows, :] = _bf(q_ref[rows, :])
            dob_scr[rows, :] = _bf(d)
            dsum[rows, :] = jnp.broadcast_to(jnp.sum(d * o_ref[rows, :], -1, keepdims=True), (ATT_T, LANES))
            return 0

        lax.fori_loop(0, nb, pre, 0)
        mask = _att_diag_mask()

        def front(kf, vf, qi):
            rows = _att_rows(qi)
            return _dot(qb_scr[rows, :], kf, 1, 1), _dot(dob_scr[rows, :], vf, 1, 1)

        def back(kf, qi, fr, carry, masked):
            s, dp = fr
            dk, dv = carry
            rows = _att_rows(qi)
            qb, dob = qb_scr[rows, :], dob_scr[rows, :]
            p = jnp.exp(s * scale - _wide(lse_ref[rows, :]))
            if masked:
                p = jnp.where(mask, p, 0.0)
            ds = _bf(p * (dp - _wide(dsum[rows, :])) * scale)
            return (dk + _dot(ds, qb, 0, 0), dv + _dot(_bf(p), dob, 0, 0)), _dot(ds, kf, 1, 0)

        def store(krows, carry):
            dk, dv = carry
            dkv_ref[krows, :] = jnp.where(lane < HEAD, dk, dv)
            dkpe_ref[krows, :] += jnp.where((lane >= HEAD) & (lane < MLA_QK), dk, 0.0)

        def k_pair(kp, _):
            ka, kb = 2 * kp, 2 * kp + 1
            rows_a, rows_b = _att_rows(ka), _att_rows(kb)
            kfa, vfa, kfb, vfb = kf_scr[rows_a, :], vf_scr[rows_a, :], kf_scr[rows_b, :], vf_scr[rows_b, :]
            zero = jnp.zeros((ATT_T, LANES), F32)
            ca, dq_a = back(kfa, ka, front(kfa, vfa, ka), (zero, zero), True)
            dq_ref[rows_a, :] += dq_a
            ca, dq_a = back(kfa, kb, front(kfa, vfa, kb), ca, False)
            cb, dq_b = back(kfb, kb, front(kfb, vfb, kb), (zero, zero), True)
            dq_ref[rows_b, :] += dq_a + dq_b

            def both(qi, c):
                ca, cb, fa, fb = c
                nxt = jnp.minimum(qi + 1, nb - 1)
                fa_next, fb_next = front(kfa, vfa, nxt), front(kfb, vfb, nxt)
                ca, dq_a = back(kfa, qi, fa, ca, False)
                cb, dq_b = back(kfb, qi, fb, cb, False)
                dq_ref[_att_rows(qi), :] += dq_a + dq_b
                return ca, cb, fa_next, fb_next

            first = jnp.minimum(kb + 1, nb - 1)
            ca, cb, _, _ = lax.fori_loop(kb + 1, nb, both, (ca, cb, front(kfa, vfa, first), front(kfb, vfb, first)))
            store(rows_a, ca)
            store(rows_b, cb)
            return 0

        lax.fori_loop(0, nb // 2, k_pair, 0)

    head = pl.BlockSpec((seq, LANES), lambda h: (0, h))
    shared = pl.BlockSpec((seq, LANES), lambda h: (0, 0))
    return pl.pallas_call(
        body, name="attn_bwd", grid=(HEADS,),
        in_specs=[head, head, shared, head, head, head],
        out_specs=[head, head, shared],
        out_shape=[jax.ShapeDtypeStruct((seq, HEADS * LANES), F32)] * 2
        + [jax.ShapeDtypeStruct((seq, LANES), F32)],
        scratch_shapes=[pltpu.VMEM((seq, LANES), BF16)] * 4 + [pltpu.VMEM((seq, LANES), F32)],
        compiler_params=_params(),
    )(q, kv, kpe, o, lse, do)


WKV_TB = 64
WKV_GROUP = SUBLANES
WKV_HALF = WIDTH // 2


def _wkv_consts():
    row = lax.broadcasted_iota(jnp.int32, (HEAD, WKV_HALF), 0)
    lane = lax.broadcasted_iota(jnp.int32, (HEAD, WKV_HALF), 1)
    diag = row == jnp.bitwise_and(lane, HEAD - 1)
    sub = lax.broadcasted_iota(jnp.int32, (WKV_GROUP, WKV_HALF), 0)
    return diag, sub


def _halves(x):
    return [x[:, :WKV_HALF], x[:, WKV_HALF:]]


def _diag_rows(row, diag):
    return _bf(jnp.where(diag, jnp.broadcast_to(row, diag.shape), 0.0))


def _put_row(tile, row, i, sub):
    return jnp.where(sub == i, jnp.broadcast_to(row, tile.shape), tile)


def _col_sum(x):
    return jnp.sum(x, axis=0, keepdims=True)


def _step(x, i):
    return x[i * HEAD:(i + 1) * HEAD]


def _expand_group(rows8, diag, ones_b):
    lhs = jnp.concatenate([_diag_rows(rows8[i:i + 1], diag) for i in range(WKV_GROUP)], axis=0)
    return _dot(lhs, ones_b, 1, 0)


def _head_dots(prods, ones_b, sub):
    tile = jnp.zeros((WKV_GROUP, WKV_HALF), F32)
    for i, p in enumerate(prods):
        tile = _put_row(tile, p, i, sub)
    res = _exact_dot(tile, ones_b, 0)
    return [res[i:i + 1] for i in range(len(prods))]


def _diag_group(x, diag, sub):
    out = jnp.zeros((WKV_GROUP, WKV_HALF), F32)
    for i in range(WKV_GROUP):
        out = _put_row(out, _col_sum(jnp.where(diag, _step(x, i), 0.0)), i, sub)
    return out


def wkv_fwd(r, w, k, v, a, b, ones_half):
    seq = r.shape[0]

    def body(r_ref, w_ref, k_ref, v_ref, a_ref, b_ref, ones_ref, y_ref, st_ref, s_scr):
        @pl.when(pl.program_id(0) == 0)
        def _():
            s_scr[...] = jnp.zeros_like(s_scr)

        ones_b = ones_ref[...]
        diag, sub = _wkv_consts()

        def group(g, state):
            base = pl.multiple_of(g * WKV_GROUP, WKV_GROUP)
            rows = pl.ds(base, WKV_GROUP)
            r8, w8, k8, v8, a8, b8 = (_halves(ref[rows, :]) for ref in (r_ref, w_ref, k_ref, v_ref, a_ref, b_ref))
            state = list(state)
            v_e = [_expand_group(v8[hf], diag, ones_b) for hf in range(2)]
            evens = range(0, WKV_GROUP, 2)
            dots = [_head_dots([b8[hf][t:t + 1] * a8[hf][t + 1:t + 2] for t in evens]
                               + [k8[hf][t:t + 1] * a8[hf][t + 1:t + 2] for t in evens], ones_b, sub) for hf in range(2)]
            read = [[], []]
            for t in range(0, WKV_GROUP, 2):
                s0, s1 = slice(t, t + 1), slice(t + 1, t + 2)
                for hf in range(2):
                    s_in = state[hf]
                    res = _dot(jnp.concatenate([_bf(s_in * a8[hf][s0]), _bf(s_in * (w8[hf][s0] * a8[hf][s1]))], axis=0),
                               ones_b, 1, 0)
                    sa0, v0 = res[:HEAD], _step(v_e[hf], t)
                    st0 = s_in * w8[hf][s0] + sa0 * b8[hf][s0] + v0 * k8[hf][s0]
                    sa1 = res[HEAD:] + sa0 * dots[hf][t // 2] + v0 * dots[hf][WKV_GROUP // 2 + t // 2]
                    st1 = st0 * w8[hf][s1] + sa1 * b8[hf][s1] + _step(v_e[hf], t + 1) * k8[hf][s1]
                    state[hf] = (st0, st1)
                    read[hf] += [_bf(st0 * r8[hf][s0]), _bf(st1 * r8[hf][s1])]
                for j in range(2):
                    st_ref[base + t + j] = jnp.concatenate([state[0][j], state[1][j]], axis=1)
                state = [state[0][1], state[1][1]]
            y8 = []
            for hf in range(2):
                yexp = _dot(jnp.concatenate(read[hf], axis=0), ones_b, 1, 0)
                y8.append(_diag_group(yexp, diag, sub))
            y_ref[rows, :] = jnp.concatenate(y8, axis=1)
            return tuple(state)

        fin = lax.fori_loop(0, WKV_TB // WKV_GROUP, group, tuple(_halves(s_scr[...])))
        s_scr[...] = jnp.concatenate(fin, axis=1)

    vec = pl.BlockSpec((WKV_TB, WIDTH), lambda i: (i, 0))
    return pl.pallas_call(
        body, name="wkv_fwd", grid=(seq // WKV_TB,),
        in_specs=[vec] * 6 + [_full_spec(ones_half)],
        out_specs=[vec, pl.BlockSpec((WKV_TB, HEAD, WIDTH), lambda i: (i, 0, 0))],
        out_shape=[jax.ShapeDtypeStruct((seq, WIDTH), F32), jax.ShapeDtypeStruct((seq, HEAD, WIDTH), F32)],
        scratch_shapes=[pltpu.VMEM((HEAD, WIDTH), F32)],
        compiler_params=_params(),
    )(r, w, k, v, a, b, ones_half)


def wkv_bwd(r, w, k, v, a, b, dy, states, ones_half):
    seq = r.shape[0]
    nb = seq // WKV_TB
    ng = WKV_TB // WKV_GROUP

    def body(r_ref, w_ref, k_ref, v_ref, a_ref, b_ref, dy_ref, st_ref, halo_ref, ones_ref,
             dr_ref, dw_ref, dk_ref, dv_ref, da_ref, db_ref, ds_scr):
        blk = nb - 1 - pl.program_id(0)

        @pl.when(pl.program_id(0) == 0)
        def _():
            ds_scr[...] = jnp.zeros_like(ds_scr)

        ones_b = ones_ref[...]
        diag, sub = _wkv_consts()
        before_block = jnp.where(blk == 0, 0.0, halo_ref[0])

        def group(gg, dstate):
            g = ng - 1 - gg
            base = pl.multiple_of(g * WKV_GROUP, WKV_GROUP)
            rows = pl.ds(base, WKV_GROUP)
            r8, w8, k8, v8, a8, b8, dy8 = (
                _halves(ref[rows, :]) for ref in (r_ref, w_ref, k_ref, v_ref, a_ref, b_ref, dy_ref))
            dstate = list(dstate)
            zero8 = jnp.zeros((WKV_GROUP, WKV_HALF), F32)
            out = {n: [zero8, zero8] for n in ("dr", "dw", "dk", "da", "db")}
            before_group = jnp.where(g == 0, before_block, st_ref[jnp.maximum(base - 1, 0)])
            states = [_halves(before_group)] + [_halves(st_ref[base + i]) for i in range(WKV_GROUP)]
            dy_e = [_expand_group(dy8[hf], diag, ones_b) for hf in range(2)]
            v_e = [_expand_group(v8[hf], diag, ones_b) for hf in range(2)]
            sa_e = [_dot(jnp.concatenate([_bf(states[i][hf] * a8[hf][i:i + 1]) for i in range(WKV_GROUP)], axis=0),
                         ones_b, 1, 0) for hf in range(2)]
            odds = range(1, WKV_GROUP, 2)
            dots = [_head_dots([a8[hf][t:t + 1] * b8[hf][t - 1:t] for t in odds]
                               + [r8[hf][t - 1:t] * b8[hf][t - 1:t] for t in odds], ones_b, sub) for hf in range(2)]
            dv_in = [[None] * WKV_GROUP, [None] * WKV_GROUP]

            def emit(hf, i, d_i, dsa_i):
                s_p, s_t = states[i][hf], states[i + 1][hf]
                dv_in[hf][i] = _bf(d_i * k8[hf][i:i + 1])
                for n, val in (("dr", _col_sum(s_t * _step(dy_e[hf], i))), ("dw", _col_sum(d_i * s_p)),
                               ("db", _col_sum(d_i * _step(sa_e[hf], i))), ("da", _col_sum(s_p * dsa_i)),
                               ("dk", _col_sum(d_i * _step(v_e[hf], i)))):
                    out[n][hf] = _put_row(out[n][hf], val, i, sub)

            for t in reversed(odds):
                s1, s0 = slice(t, t + 1), slice(t - 1, t)
                for hf in range(2):
                    d1 = dstate[hf] + _step(dy_e[hf], t) * r8[hf][s1]
                    res = _dot(jnp.concatenate([_bf(d1 * b8[hf][s1]), _bf(d1 * (w8[hf][s1] * b8[hf][s0]))], axis=0),
                               ones_b, 1, 0)
                    dsa1, dy0 = res[:HEAD], _step(dy_e[hf], t - 1)
                    d0 = d1 * w8[hf][s1] + dsa1 * a8[hf][s1] + dy0 * r8[hf][s0]
                    dsa0 = res[HEAD:] + dsa1 * dots[hf][t // 2] + dy0 * dots[hf][WKV_GROUP // 2 + t // 2]
                    dstate[hf] = d0 * w8[hf][s0] + dsa0 * a8[hf][s0]
                    emit(hf, t, d1, dsa1)
                    emit(hf, t - 1, d0, dsa0)
            out["dv"] = [_diag_group(_dot(jnp.concatenate(dv_in[hf], axis=0), ones_b, 1, 0), diag, sub)
                         for hf in range(2)]
            for ref, n in ((dr_ref, "dr"), (dw_ref, "dw"), (dk_ref, "dk"), (dv_ref, "dv"), (da_ref, "da"), (db_ref, "db")):
                ref[rows, :] = jnp.concatenate(out[n], axis=1)
            return tuple(dstate)

        fin = lax.fori_loop(0, ng, group, tuple(_halves(ds_scr[...])))
        ds_scr[...] = jnp.concatenate(fin, axis=1)

    vec = pl.BlockSpec((WKV_TB, WIDTH), lambda i: (nb - 1 - i, 0))
    return pl.pallas_call(
        body, name="wkv_bwd", grid=(nb,),
        in_specs=[vec] * 7 + [
            pl.BlockSpec((WKV_TB, HEAD, WIDTH), lambda i: (nb - 1 - i, 0, 0)),
            pl.BlockSpec((1, HEAD, WIDTH), lambda i: (jnp.maximum((nb - 1 - i) * WKV_TB - 1, 0), 0, 0)),
            _full_spec(ones_half)],
        out_specs=[vec] * 6,
        out_shape=[jax.ShapeDtypeStruct((seq, WIDTH), F32)] * 6,
        scratch_shapes=[pltpu.VMEM((HEAD, WIDTH), F32)],
        compiler_params=_params(),
    )(r, w, k, v, a, b, dy, states, states, ones_half)


def ada_grad_shard(sc_cols, dada_rows):
    n = len(sc_cols)

    def body(*refs):
        d_ref, o_ref = refs[n], refs[n + 1]
        acc = refs[0][...] * d_ref[0:1, :]
        for b in range(1, n):
            acc = acc + refs[b][...] * d_ref[b:b + 1, :]
        o_ref[...] = acc

    return pl.pallas_call(
        body, name="ada_grad_shard",
        out_shape=jax.ShapeDtypeStruct((sc_cols[0].shape[0], dada_rows.shape[1]), F32),
        compiler_params=pltpu.CompilerParams(vmem_limit_bytes=VMEM_LIMIT),
    )(*sc_cols, dada_rows)


def sum_slots(buf, tr):
    n, rows, cols = buf.shape

    def body(b_ref, o_ref):
        acc = b_ref[0].astype(F32)
        for s in range(1, n):
            acc = acc + b_ref[s].astype(F32)
        o_ref[...] = acc

    return pl.pallas_call(
        body, name="sum_slots", grid=(rows // tr,),
        in_specs=[pl.BlockSpec((n, tr, cols), lambda i: (0, i, 0))],
        out_specs=pl.BlockSpec((tr, cols), lambda i: (i, 0)),
        out_shape=jax.ShapeDtypeStruct((rows, cols), F32),
        compiler_params=_params(),
    )(buf)


def adamw_small(gathered, w, m, v):
    n = gathered.shape[0]

    def body(g_ref, w_ref, m_ref, v_ref, go_ref, d_ref, mo_ref, vo_ref):
        g = g_ref[0]
        for s in range(1, n):
            g = g + g_ref[s]
        go_ref[...] = g
        d_ref[...], mo_ref[...], vo_ref[...] = f_adamw(w_ref[...], g, m_ref[...], v_ref[...])

    return pl.pallas_call(
        body, name="adamw_small",
        out_shape=[jax.ShapeDtypeStruct(w.shape, F32)] * 4,
        compiler_params=pltpu.CompilerParams(vmem_limit_bytes=VMEM_LIMIT),
    )(gathered, w, m, v)


def _coords():
    return lax.axis_index("x"), lax.axis_index("y"), lax.axis_index("c")


def _flip(v, bit):
    return 1 - v if bit else v


def _hbm_call(body, name, out_shape, n_sems, *args):
    any_spec = pl.BlockSpec(memory_space=pl.ANY)
    return pl.pallas_call(
        body, name=name, out_shape=out_shape,
        in_specs=[any_spec] * len(args), out_specs=any_spec,
        scratch_shapes=[pltpu.SemaphoreType.DMA((n_sems,)), pltpu.SemaphoreType.DMA((n_sems,)),
                        pltpu.SemaphoreType.DMA],
    )(*args)


def all_gather8(name, block):
    def body(x_ref, out_ref, send_sems, recv_sems, local_sem):
        x, y, c = _coords()
        me, sibling = (x, y, c), (x, y, 1 - c)
        chips = [(1 - x, y), (x, 1 - y), (1 - x, 1 - y)]

        def slot(px, py, pc):
            return out_ref.at[4 * px + 2 * py + pc]

        def copy(k, blk, to, src=None):
            return pltpu.make_async_remote_copy(
                src_ref=slot(*blk) if src is None else src, dst_ref=slot(*blk),
                send_sem=send_sems.at[k], recv_sem=recv_sems.at[k], device_id=to, device_id_type=MESH)

        mine = pltpu.make_async_copy(x_ref, slot(*me), local_sem)
        mine.start()
        first = [copy(0, me, sibling, src=x_ref)]
        first += [copy(1 + j, me, (*chip, c), src=x_ref) for j, chip in enumerate(chips)]
        for cp in first:
            cp.start()
        passed = [copy(4 + j, (*chip, c), sibling) for j, chip in enumerate(chips)]
        for j, chip in enumerate(chips):
            copy(1 + j, (*chip, c), me).wait_recv()
            passed[j].start()
        copy(0, sibling, me).wait_recv()
        for j, chip in enumerate(chips):
            copy(4 + j, (*chip, 1 - c), me).wait_recv()
        for cp in first + passed:
            cp.wait_send()
        mine.wait()

    return _hbm_call(body, name, jax.ShapeDtypeStruct((N_DEV,) + block.shape, block.dtype), 7, block)


def pair_swap(name, block):
    def body(x_ref, out_ref, send_sems, recv_sems, local_sem):
        x, y, c = _coords()
        cp = pltpu.make_async_remote_copy(
            src_ref=x_ref, dst_ref=out_ref, send_sem=send_sems.at[0], recv_sem=recv_sems.at[0],
            device_id=(x, y, 1 - c), device_id_type=MESH)
        cp.start()
        cp.wait_recv()
        cp.wait_send()

    return _hbm_call(body, name, jax.ShapeDtypeStruct(block.shape, block.dtype), 1, block)


def chip_all_to_all(name, buf):
    def body(x_ref, out_ref, send_sems, recv_sems, local_sem):
        x, y, c = _coords()
        me = 2 * x + y
        mine = pltpu.make_async_copy(x_ref.at[me], out_ref.at[me], local_sem)
        mine.start()
        copies = []
        for k in range(1, 4):
            px, py = _flip(x, k & 2), _flip(y, k & 1)
            copies.append(pltpu.make_async_remote_copy(
                src_ref=x_ref.at[2 * px + py], dst_ref=out_ref.at[me],
                send_sem=send_sems.at[k - 1], recv_sem=recv_sems.at[k - 1],
                device_id=(px, py, c), device_id_type=MESH))
        for cp in copies:
            cp.start()
        for cp in copies:
            cp.wait_recv()
        for cp in copies:
            cp.wait_send()
        mine.wait()

    return _hbm_call(body, name, jax.ShapeDtypeStruct(buf.shape, buf.dtype), 3, buf)


def sibling_gather(name, block):
    def body(x_ref, out_ref, send_sems, recv_sems, local_sem):
        x, y, c = _coords()
        mine = pltpu.make_async_copy(x_ref, out_ref.at[c], local_sem)
        mine.start()
        cp = pltpu.make_async_remote_copy(
            src_ref=x_ref, dst_ref=out_ref.at[c], send_sem=send_sems.at[0], recv_sem=recv_sems.at[0],
            device_id=(x, y, 1 - c), device_id_type=MESH)
        cp.start()
        cp.wait_recv()
        cp.wait_send()
        mine.wait()

    return _hbm_call(body, name, jax.ShapeDtypeStruct((2,) + block.shape, block.dtype), 1, block)


def _col_blocks(a, cols):
    a = jnp.pad(a, ((0, 0), (0, cols - a.shape[1])))
    return [a[i * PACK_ROWS:(i + 1) * PACK_ROWS] for i in range(a.shape[0] // PACK_ROWS)]


def _pack_shard(sh, dtype, with_ada=True):
    lora = jnp.concatenate([sh['w_decay_up'], sh['w_iclr_up']], axis=1)
    misc = jnp.concatenate([sh['w_ukv'], lora, jnp.zeros((LORA, 2 * LANES), lora.dtype)], axis=0)
    blocks = ((_col_blocks(sh['w_ada'], 768) if with_ada else [])
              + _col_blocks(sh['w_in'], 1408) + _col_blocks(sh['w_proj_a'], 256)
              + _col_blocks(sh['w_proj_b'], 256) + [sh['w_out']] + _col_blocks(sh['w_uq'], 256) + [misc])
    return jnp.concatenate([b.astype(dtype) for b in blocks], axis=1)


def _unpack_shard(p, with_ada=True):
    o = [0]

    def take(n_blocks, cols, used):
        blocks = [p[:, o[0] + i * cols:o[0] + (i + 1) * cols] for i in range(n_blocks)]
        o[0] += n_blocks * cols
        return jnp.concatenate(blocks, axis=0)[:, :used]

    out = {'w_ada': take(4, 768, 768)} if with_ada else {}
    out.update({'w_in': take(4, 1408, 1288), 'w_proj_a': take(2, 256, 256),
                'w_proj_b': take(2, 256, 256), 'w_out': take(1, 1024, 1024), 'w_uq': take(1, 256, 192)})
    misc = take(1, 256, 256)
    out['w_ukv'] = misc[:2 * LORA]
    out['w_decay_up'] = misc[2 * LORA:3 * LORA, :LANES]
    out['w_iclr_up'] = misc[2 * LORA:3 * LORA, LANES:]
    return out


def _pack_small(parts):
    flat = jnp.concatenate([p.reshape(-1) for p in parts])
    return jnp.pad(flat, (0, SMALL_ROWS * LANES - flat.shape[0])).reshape(SMALL_ROWS, LANES)


def _unpack_small(packed):
    flat, out, o = packed.reshape(-1), {}, 0
    for name, n in _SMALL:
        out[name] = flat[o:o + n]
        o += n
    return out


def _pad_heads_cols(w, used, left):
    k = w.shape[0]
    return jnp.pad(w.reshape(k, HEADS, used), ((0, 0), (0, 0), (left, LANES - used - left))).reshape(k, HEADS * LANES)


def _unpad_heads_cols(w, used, left):
    k = w.shape[0]
    return w.reshape(k, HEADS, LANES)[:, :, left:left + used].reshape(k, HEADS * used)


def kernel(x, c, positions, w_ada, b_ada, w_in, q_norm_g, w_uq, kv_norm_g, w_ukv, mu_rwkv, w0, w_decay_up, a0, w_iclr_up, k_k, k_a, r_k, gn_g, gn_b, w_proj_a, w_proj_b, w_out, post_g, post_b, loss_target, m_w_ada, m_b_ada, m_w_in, m_q_norm_g, m_w_uq, m_kv_norm_g, m_w_ukv, m_mu_rwkv, m_w0, m_w_decay_up, m_a0, m_w_iclr_up, m_k_k, m_k_a, m_r_k, m_gn_g, m_gn_b, m_w_proj_a, m_w_proj_b, m_w_out, m_post_g, m_post_b, v_w_ada, v_b_ada, v_w_in, v_q_norm_g, v_w_uq, v_kv_norm_g, v_w_ukv, v_mu_rwkv, v_w0, v_w_decay_up, v_a0, v_w_iclr_up, v_k_k, v_k_a, v_r_k, v_gn_g, v_gn_b, v_w_proj_a, v_w_proj_b, v_w_out, v_post_g, v_post_b):
    given = dict(locals())
    seq = x.shape[1]
    my_c = lax.axis_index("c")

    shard_names = [n for n, _, _ in _SHARDED]
    w_pack = _pack_shard({n: given[n][0] for n in shard_names}, BF16)
    my_half = lax.dynamic_slice_in_dim(w_pack, my_c * HALF_COLS, HALF_COLS, 1)
    gathered = all_gather8("gather_weights", my_half)
    shards = [_unpack_shard(jnp.concatenate([gathered[2 * s], gathered[2 * s + 1]], axis=1)) for s in range(4)]
    full = {n: jnp.concatenate([sh[n] for sh in shards], axis=ax) for n, _, ax in _SHARDED}

    wi = full['w_in']
    zcol = lambda n: jnp.zeros((D_MODEL, n), BF16)
    w_g1 = jnp.concatenate([wi[:, :384], zcol(HEAD), wi[:, 384:416], zcol(LANES - MLA_QK),
                            _pad_heads_cols(wi[:, 416:928], HEAD, HEAD)], axis=1)
    w_g2 = wi[:, 928:3104]
    w_g3 = wi[:, 3104:5152]
    w_uq_p = _pad_heads_cols(full['w_uq'], MLA_QK, 0)
    w_pa_p = jnp.pad(full['w_proj_a'].reshape(HEADS, HEAD, D_MODEL), ((0, 0), (HEAD, 0), (0, 0))).reshape(HEADS * LANES, D_MODEL)
    zl = jnp.zeros((LORA, WIDTH), BF16)
    w_lora = jnp.concatenate([jnp.concatenate([full['w_decay_up'], zl], 1),
                              jnp.concatenate([zl, full['w_iclr_up']], 1)], 0)

    hd = np.arange(WIDTH) // HEAD
    ones_blocks = jnp.asarray(hd[:, None] == hd[None, :], BF16)
    ones_half = ones_blocks[:WKV_HALF, :WKV_HALF]
    perm_np = np.zeros((LANES, LANES), np.float32)
    for d in range(MLA_ROPE // 2):
        perm_np[HEAD + 16 + d, HEAD + d] = -1.0
        perm_np[HEAD + d, HEAD + 16 + d] = 1.0
    perm = jnp.asarray(perm_np, BF16)
    inv = ROPE_THETA ** (-jnp.arange(0, MLA_ROPE, 2, dtype=F32) / MLA_ROPE)
    ang = positions[0].astype(F32)[:, None] * inv
    cos_a, sin_a = jnp.cos(ang), jnp.sin(ang)
    cs = jnp.concatenate([jnp.ones((seq, HEAD), F32), cos_a, cos_a, jnp.zeros((seq, LANES - MLA_QK), F32),
                          jnp.zeros((seq, HEAD), F32), sin_a, sin_a, jnp.zeros((seq, LANES - MLA_QK), F32)], axis=1)

    x2, tgt = x[0], loss_target[0]
    r_k2 = r_k.reshape(1, WIDTH)

    c8 = jnp.broadcast_to(c, (SUBLANES, D_MODEL))
    ada = row_fwd("ada_fwd", f_ada, [c8], [b_ada, full['w_ada']], [], [3 * D_MODEL], SUBLANES)[0][:1]
    shift, scale, gate = ada[:, :D_MODEL], ada[:, D_MODEL:2 * D_MODEL], ada[:, 2 * D_MODEL:]

    f_in1, f_in2, f_in3 = _make_f_in((512, 1024)), _make_f_in((SHIFT_W, WIDTH)), _make_f_in((1024, 1024))
    tr = min(256, seq)
    p_mla, gpa = row_fwd("in1_fwd", f_in1, [x2], [shift, scale, w_g1], [], [512, 1024], tr)
    p_rwkv, gpb = row_fwd("in2_fwd", f_in2, [x2], [shift, scale, w_g2], [], [SHIFT_W, WIDTH], tr)
    ma, mb = row_fwd("in3_fwd", f_in3, [x2], [shift, scale, w_g3], [], [1024, 1024], tr)

    mla_par = [q_norm_g, kv_norm_g, w_uq_p, full['w_ukv']]
    q_f, kv_f, kpe = row_fwd("mla_pre_fwd", f_mla_pre, [p_mla, cs], mla_par, [perm], [1024, 1024, LANES], tr)
    ya, lse = attn_fwd(q_f, kv_f, kpe)

    u = shift_fwd(p_rwkv, mu_rwkv, tr)
    pre_par = [w0, a0, k_k, k_a, w_lora]
    rr, wd, k2, vv, an, bb = row_fwd("rwkv_pre_fwd", f_rwkv_pre, [u], pre_par, [ones_blocks], [WIDTH] * 6, tr)
    y_wkv, states = wkv_fwd(rr, wd, k2, vv, an, bb, ones_half)
    post_b_par = [r_k2, gn_g, gn_b]
    yb, = row_fwd("rwkv_post_fwd", f_rwkv_post, [y_wkv, rr, k2, vv], post_b_par, [ones_blocks], [WIDTH], tr)

    out_rows = [ya, gpa, yb, gpb, ma, mb, x2, tgt]
    out_par = [gate, post_g, post_b, w_pa_p, full['w_proj_b'], full['w_out']]
    lrows, = row_fwd("post_fwd", f_post, out_rows, out_par, [], [LANES], tr)
    loss = lax.psum(jnp.sum(lrows[:, 0]), ("x", "y", "c"))

    trb = min(128, seq)
    dl = jnp.broadcast_to((jnp.arange(LANES) == 0).astype(F32), (seq, LANES))
    (dya, dgpa, dyb, dgpb, dma, dmb, dx_res), (dgate, dpost_g, dpost_b, dw_pa_p, dw_pb, dw_out) = row_bwd(
        "post_bwd", f_post, out_rows, 7, out_par, [], [dl], trb)

    (dy_wkv, dr1, dk1, dv1), (dr_k, dgn_g, dgn_b) = row_bwd(
        "rwkv_post_bwd", f_rwkv_post, [y_wkv, rr, k2, vv], 4, post_b_par, [ones_blocks], [dyb], tr)
    dr2, dwd, dk2, dv2, dan, dbb = wkv_bwd(rr, wd, k2, vv, an, bb, dy_wkv, states, ones_half)
    (du,), (dw0, da0, dk_k, dk_a, dw_lora) = row_bwd(
        "rwkv_pre_bwd", f_rwkv_pre, [u], 1, pre_par, [ones_blocks],
        [(dr1, dr2), dwd, (dk1, dk2), (dv1, dv2), dan, dbb], tr)
    dp_rwkv, dmu = shift_bwd(du, p_rwkv, mu_rwkv, tr)

    dq_f, dkv_f, dkpe = attn_bwd(q_f, kv_f, kpe, ya, lse, dya)
    (dp_mla,), (dqg, dkvg, dw_uq_p, dw_ukv) = row_bwd(
        "mla_pre_bwd", f_mla_pre, [p_mla, cs], 1, mla_par, [perm], [dq_f, dkv_f, dkpe], tr)

    (dx1,), (dsh1, dsc1, dw_g1) = row_bwd("in1_bwd", f_in1, [x2], 1, [shift, scale, w_g1], [], [dp_mla, dgpa], trb, [dx_res])
    (dx2,), (dsh2, dsc2, dw_g2) = row_bwd("in2_bwd", f_in2, [x2], 1, [shift, scale, w_g2], [], [dp_rwkv, dgpb], trb, [dx1])
    (dx3,), (dsh3, dsc3, dw_g3) = row_bwd("in3_bwd", f_in3, [x2], 1, [shift, scale, w_g3], [], [dma, dmb], trb, [dx2])
    grad_x = dx3[None]

    dada = jnp.concatenate([dsh1 + dsh2 + dsh3, dsc1 + dsc2 + dsc3, dgate], axis=1)
    local = {
        'w_in': jnp.concatenate([dw_g1[:, :384], dw_g1[:, 448:480], _unpad_heads_cols(dw_g1[:, 512:], HEAD, HEAD),
                                 dw_g2, dw_g3], axis=1),
        'w_uq': _unpad_heads_cols(dw_uq_p, MLA_QK, 0),
        'w_ukv': dw_ukv,
        'w_decay_up': dw_lora[:LORA, :WIDTH],
        'w_iclr_up': dw_lora[LORA:, WIDTH:],
        'w_proj_a': dw_pa_p.reshape(HEADS, LANES, D_MODEL)[:, HEAD:].reshape(WIDTH, D_MODEL),
        'w_proj_b': dw_pb,
        'w_out': dw_out,
    }
    small_local = {'b_ada': dada, 'q_norm_g': dqg, 'kv_norm_g': dkvg, 'mu_rwkv': dmu, 'w0': dw0, 'a0': da0,
                   'k_k': dk_k, 'k_a': dk_a, 'r_k': dr_k, 'gn_g': dgn_g, 'gn_b': dgn_b,
                   'post_g': dpost_g, 'post_b': dpost_b}

    def shard_of(g, axis, s):
        n = g.shape[axis] // 4
        return lax.slice_in_dim(g, s * n, (s + 1) * n, axis=axis)

    packed = jnp.stack([_pack_shard({n: shard_of(local[n], ax, s) for n, _, ax in _SHARDED if n != 'w_ada'}, F32, False)
                        for s in range(4)])
    keep = lax.dynamic_slice_in_dim(packed, my_c * GRAD_HALF, GRAD_HALF, 2).reshape(4 * PACK_ROWS, GRAD_HALF)
    give = lax.dynamic_slice_in_dim(packed, (1 - my_c) * GRAD_HALF, GRAD_HALF, 2).reshape(4 * PACK_ROWS, GRAD_HALF)
    pair_sum, = row_fwd("pair_sum", lambda p, q: (p + q,), [keep, pair_swap("swap_halves", give)], [], [],
                        [GRAD_HALF], PACK_ROWS // 2, BF16)
    received = chip_all_to_all("exchange_grads", pair_sum.reshape(4, PACK_ROWS, GRAD_HALF))
    my_sum = sum_slots(received, PACK_ROWS // 2)
    halves = sibling_gather("gather_halves", my_sum)
    g_shard = _unpack_shard(jnp.concatenate([halves[0], halves[1]], axis=1), False)

    small_pack = lambda d, extra=(): _pack_small([d[n] for n, _ in _SMALL] + list(extra))
    small_all = all_gather8("gather_small", small_pack(small_local, [c * jax.nn.sigmoid(c)]))
    sc_all = small_all[:, SMALL_USED:SMALL_USED + D_MODEL // LANES].reshape(N_DEV, D_MODEL)
    dada_all = small_all[:, :3 * D_MODEL // LANES].reshape(N_DEV, 3 * D_MODEL)
    my_cols = lax.dynamic_slice_in_dim(dada_all, (2 * lax.axis_index("x") + lax.axis_index("y")) * 768, 768, 1)
    g_shard['w_ada'] = ada_grad_shard([sc_all[b].reshape(D_MODEL, 1) for b in range(N_DEV)], my_cols)

    big = [{}, {}, {}, {}]
    for n in shard_names:
        w2, m2, v2 = given[n][0], given['m_' + n][0], given['v_' + n][0]
        cols = w2.shape[1]
        outs = row_fwd("adamw_" + n, f_adamw, [w2, g_shard[n], m2, v2], [], [], [cols] * 3, min(256, w2.shape[0]))
        for dst, val in zip(big, (g_shard[n], *outs)):
            dst[n] = val

    small_out = adamw_small(small_all, small_pack({n: given[n] for n, _ in _SMALL}),
                            small_pack({n: given['m_' + n] for n, _ in _SMALL}),
                            small_pack({n: given['v_' + n] for n, _ in _SMALL}))

    results = []
    for big_k, packed_small in zip(big, small_out):
        small = _unpack_small(packed_small)
        results.append([(big_k[n] if n in big_k else small[n]).reshape(given[n].shape) for n in _WEIGHTS])
    return (loss, grad_x, *results[0], *results[1], *results[2], *results[3])
```

```python
import numpy as np
import jax
import jax.numpy as jnp
from jax import lax
from jax.experimental import pallas as pl
from jax.experimental.pallas import tpu as pltpu

F32 = jnp.float32
BF16 = jnp.bfloat16

D_MODEL = 1024
LN_EPS = 1e-5
RMS_EPS = 1e-6
GN_EPS = 64e-5
HEADS = 8
HEAD = 64
MLA_ROPE = 32
MLA_QK = HEAD + MLA_ROPE
ROPE_THETA = 10000.0
WIDTH = HEADS * HEAD
LORA = 64
SHIFT_W = 3 * WIDTH + 2 * LORA
CHUNK = 64
ALPHA = 2.0 ** 0.25

ADAM_LR, ADAM_B1, ADAM_B2, ADAM_EPS, ADAM_WD, ADAM_STEP = 0.001, 0.9, 0.999, 1e-08, 0.01, 10

LANES = 128
SUBLANES = 8
VMEM_LIMIT = 56 * 1024 * 1024
N_DEV = 8
MESH = pl.DeviceIdType.MESH
NEG = -1e30

_WEIGHTS = ['w_ada', 'b_ada', 'w_in', 'q_norm_g', 'w_uq', 'kv_norm_g', 'w_ukv', 'mu_rwkv', 'w0',
            'w_decay_up', 'a0', 'w_iclr_up', 'k_k', 'k_a', 'r_k', 'gn_g', 'gn_b', 'w_proj_a',
            'w_proj_b', 'w_out', 'post_g', 'post_b']
_SHARDED = [('w_ada', (1024, 3072), 1), ('w_in', (1024, 5152), 1), ('w_uq', (256, 768), 1),
            ('w_ukv', (128, 1024), 1), ('w_decay_up', (64, 512), 1), ('w_iclr_up', (64, 512), 1),
            ('w_proj_a', (512, 1024), 1), ('w_proj_b', (512, 1024), 1), ('w_out', (1024, 1024), 0)]
_SMALL = [('b_ada', 3072), ('q_norm_g', 256), ('kv_norm_g', 128), ('mu_rwkv', 1664), ('w0', 512),
          ('a0', 512), ('k_k', 512), ('k_a', 512), ('r_k', 512), ('gn_g', 512), ('gn_b', 512),
          ('post_g', 1024), ('post_b', 1024)]
PACK_ROWS = 256
PACK_COLS = 11264
HALF_COLS = PACK_COLS // 2
ADA_COLS = 4 * 768
GRAD_HALF = (PACK_COLS - ADA_COLS) // 2
SMALL_USED = 84
SMALL_ROWS = 96


def _bf(x):
    return x.astype(BF16)


def _dot(a, b, ca, cb):
    return lax.dot_general(a, b, (((ca,), (cb,)), ((), ())), preferred_element_type=F32)


@jax.custom_vjp
def mm(a, w):
    return _dot(_bf(a), _bf(w), 1, 0)


def _mm_fwd(a, w):
    return mm(a, w), (a, w)


def _mm_bwd(res, g):
    a, w = res
    gb = _bf(g)
    return _dot(gb, _bf(w), 1, 1), _dot(_bf(a), gb, 0, 0).astype(w.dtype)


mm.defvjp(_mm_fwd, _mm_bwd)


def _split3(x):
    hi = _bf(x)
    r1 = x - hi.astype(F32)
    mid = _bf(r1)
    lo = _bf(r1 - mid.astype(F32))
    return hi, mid, lo


def _exact_dot(x, m, cm):
    hi, mid, lo = _split3(x)
    return _dot(hi, m, 1, cm) + _dot(mid, m, 1, cm) + _dot(lo, m, 1, cm)


@jax.custom_vjp
def segsum(x, ones_blocks):
    return _exact_dot(x, ones_blocks, 0)


def _segsum_fwd(x, ones_blocks):
    return segsum(x, ones_blocks), ones_blocks


def _segsum_bwd(ones_blocks, g):
    return _exact_dot(g, ones_blocks, 0), jnp.zeros_like(ones_blocks)


segsum.defvjp(_segsum_fwd, _segsum_bwd)


@jax.custom_vjp
def lane_perm(x, perm):
    return _exact_dot(x, perm, 0)


def _lane_perm_fwd(x, perm):
    return lane_perm(x, perm), perm


def _lane_perm_bwd(perm, g):
    return _exact_dot(g, perm, 1), jnp.zeros_like(perm)


lane_perm.defvjp(_lane_perm_fwd, _lane_perm_bwd)


def _silu(z):
    return z * jax.nn.sigmoid(z)


def _softplus(z):
    return jnp.maximum(z, 0.0) + jnp.log(1.0 + jnp.exp(-jnp.abs(z)))


def _layer_norm(x):
    xc = x - jnp.mean(x, -1, keepdims=True)
    return xc * lax.rsqrt(jnp.mean(xc * xc, -1, keepdims=True) + LN_EPS)


def _rope(t, cos_t, sin_t, perm):
    outs = []
    for h in range(t.shape[1] // LANES):
        th = t[:, h * LANES:(h + 1) * LANES]
        outs.append(th * cos_t + lane_perm(th, perm) * sin_t)
    return outs[0] if len(outs) == 1 else jnp.concatenate(outs, axis=1)


def _make_f_in(splits):
    def f_in(x, shift, scale, w):
        h = _layer_norm(x) * (1.0 + scale) + shift
        p = mm(h, w)
        outs, o = [], 0
        for s in splits:
            outs.append(p[:, o:o + s])
            o += s
        return tuple(outs)
    return f_in


def f_mla_pre(p, cs, qg, kvg, w_uq, w_ukv, perm):
    q_c, kv_c, k_r = p[:, :256], p[:, 256:384], p[:, 384:512]
    cos_t, sin_t = cs[:, :LANES], cs[:, LANES:]
    qn = q_c * lax.rsqrt(jnp.mean(q_c * q_c, -1, keepdims=True) + RMS_EPS) * qg
    kvn = kv_c * lax.rsqrt(jnp.mean(kv_c * kv_c, -1, keepdims=True) + RMS_EPS) * kvg
    q = _rope(mm(qn, w_uq), cos_t, sin_t, perm)
    kv = mm(kvn, w_ukv)
    return q, kv, _rope(k_r, cos_t, sin_t, perm)


def f_rwkv_pre(u, w0, a0, k_k, k_a, w_lora, ones_blocks):
    r, k, v, lo = u[:, :WIDTH], u[:, WIDTH:2 * WIDTH], u[:, 2 * WIDTH:3 * WIDTH], u[:, 3 * WIDTH:]
    lane = lax.broadcasted_iota(jnp.int32, lo.shape, 1)
    dl = mm(jnp.where(lane < LORA, jnp.tanh(lo), lo), w_lora)
    w_log = -_softplus(-(w0 + dl[:, :WIDTH])) - 0.5
    decay = jnp.exp(-jnp.exp(w_log))
    a = jax.nn.sigmoid(a0 + dl[:, WIDTH:])
    kk = k * k_k
    kk = kk / jnp.maximum(jnp.sqrt(segsum(kk * kk, ones_blocks)), 1e-12)
    k2 = k * (1.0 + (a - 1.0) * k_a)
    return r, decay, k2, v, -kk, kk * a


def f_rwkv_post(y, r, k2, v, r_k, gn_g, gn_b, ones_blocks):
    yc = y - segsum(y, ones_blocks) * (1.0 / HEAD)
    yn = yc * lax.rsqrt(segsum(yc * yc, ones_blocks) * (1.0 / HEAD) + GN_EPS)
    return (yn * gn_g + gn_b + segsum(r * k2 * r_k, ones_blocks) * v,)


def f_post(ya, gpa, yb, gpb, ma, mb, x, tgt, gate, post_g, post_b, w_pa, w_pb, w_out):
    pa = mm(ya * _silu(gpa), w_pa)
    pb = mm(yb * _silu(gpb), w_pb)
    merged = jax.nn.sigmoid(ma) * pa + jax.nn.sigmoid(mb) * pb
    z = ALPHA * x + (1.0 + gate) * mm(merged, w_out)
    err = _layer_norm(z) * post_g + post_b - tgt
    lrow = 0.5 * jnp.mean(err * err, -1, keepdims=True)
    return (jnp.broadcast_to(lrow, (lrow.shape[0], LANES)),)


def f_adamw(w, g, m, v):
    m2 = ADAM_B1 * m + (1.0 - ADAM_B1) * g
    v2 = ADAM_B2 * v + (1.0 - ADAM_B2) * jnp.square(g)
    m_hat = m2 / (1.0 - ADAM_B1 ** ADAM_STEP)
    v_hat = v2 / (1.0 - ADAM_B2 ** ADAM_STEP)
    return -ADAM_LR * (m_hat / (jnp.sqrt(v_hat) + ADAM_EPS) + ADAM_WD * w), m2, v2


def _params(sem=("arbitrary",)):
    return pltpu.CompilerParams(dimension_semantics=sem, vmem_limit_bytes=VMEM_LIMIT)


def _row_spec(tr, a):
    return pl.BlockSpec((tr, a.shape[1]), lambda i: (i, 0))


def _full_spec(a):
    return pl.BlockSpec(a.shape, lambda i: (0,) * a.ndim)


def row_fwd(name, f, rows, params, consts, out_widths, tr, out_dtype=F32):
    n_rows = rows[0].shape[0]
    nr, npar, ncon = len(rows), len(params), len(consts)

    def body(*refs):
        rv = [r[...] for r in refs[:nr]]
        pv = [r[...].astype(F32) for r in refs[nr:nr + npar]]
        cv = [r[...] for r in refs[nr + npar:nr + npar + ncon]]
        outs = f(*rv, *pv, *cv)
        for o_ref, o in zip(refs[nr + npar + ncon:], outs):
            o_ref[...] = o.astype(o_ref.dtype)

    return pl.pallas_call(
        body, name=name, grid=(n_rows // tr,),
        in_specs=[_row_spec(tr, a) for a in rows] + [_full_spec(a) for a in list(params) + list(consts)],
        out_specs=[pl.BlockSpec((tr, w), lambda i: (i, 0)) for w in out_widths],
        out_shape=[jax.ShapeDtypeStruct((n_rows, w), out_dtype) for w in out_widths],
        compiler_params=_params(),
    )(*rows, *params, *consts)


def row_bwd(name, f, rows, n_diff, params, consts, douts, tr, add_rows=None):
    n_rows = rows[0].shape[0]
    douts = [d if isinstance(d, (tuple, list)) else (d,) for d in douts]
    counts = [len(d) for d in douts]
    flat_d = [a for d in douts for a in d]
    add_rows = add_rows or [None] * n_diff
    adds = [a for a in add_rows if a is not None]
    nr, npar, ncon, nd, na = len(rows), len(params), len(consts), len(flat_d), len(adds)

    def body(*refs):
        o = 0
        rv = [r[...] for r in refs[o:o + nr]]; o += nr
        pv = [r[...].astype(F32) for r in refs[o:o + npar]]; o += npar
        cv = [r[...] for r in refs[o:o + ncon]]; o += ncon
        dv = []
        for cnt in counts:
            s = refs[o][...]
            for e in range(1, cnt):
                s = s + refs[o + e][...]
            dv.append(s)
            o += cnt
        add_v = [r[...] for r in refs[o:o + na]]; o += na
        drow_refs = refs[o:o + n_diff]; o += n_diff
        dpar_refs = refs[o:o + npar]

        def g(*args):
            return tuple(f(*args[:n_diff], *rv[n_diff:], *args[n_diff:], *cv))

        _, vjp = jax.vjp(g, *rv[:n_diff], *pv)
        grads = vjp(tuple(dv))
        ai = 0
        for j, (r, gr) in enumerate(zip(drow_refs, grads[:n_diff])):
            if add_rows[j] is not None:
                gr = gr + add_v[ai]
                ai += 1
            r[...] = gr

        @pl.when(pl.program_id(0) == 0)
        def _():
            for r in dpar_refs:
                r[...] = jnp.zeros_like(r)

        for r, gr in zip(dpar_refs, grads[n_diff:]):
            r[...] += gr

    outs = pl.pallas_call(
        body, name=name, grid=(n_rows // tr,),
        in_specs=([_row_spec(tr, a) for a in rows] + [_full_spec(a) for a in list(params) + list(consts)]
                  + [_row_spec(tr, a) for a in flat_d + adds]),
        out_specs=[_row_spec(tr, a) for a in rows[:n_diff]] + [_full_spec(a) for a in params],
        out_shape=([jax.ShapeDtypeStruct(a.shape, F32) for a in rows[:n_diff]]
                   + [jax.ShapeDtypeStruct(a.shape, F32) for a in params]),
        compiler_params=_params(),
    )(*rows, *params, *consts, *flat_d, *adds)
    return outs[:n_diff], outs[n_diff:]


def shift_fwd(p, mu, tr):
    n_rows, w = p.shape

    def body(p_ref, mu_ref, u_ref, carry):
        @pl.when(pl.program_id(0) == 0)
        def _():
            carry[...] = jnp.zeros_like(carry)

        x = p_ref[...]
        rolled = pltpu.roll(x, 1, 0)
        head = pltpu.roll(carry[...], 1, 0)
        fixed = jnp.concatenate([head, rolled[SUBLANES:]], axis=0)
        row = lax.broadcasted_iota(jnp.int32, x.shape, 0)
        prev = jnp.where(row == 0, fixed, rolled)
        u_ref[...] = x + (prev - x) * mu_ref[...]
        carry[...] = x[tr - SUBLANES:]

    return pl.pallas_call(
        body, name="shift_fwd", grid=(n_rows // tr,),
        in_specs=[_row_spec(tr, p), _full_spec(mu)],
        out_specs=_row_spec(tr, p),
        out_shape=jax.ShapeDtypeStruct(p.shape, F32),
        scratch_shapes=[pltpu.VMEM((SUBLANES, w), F32)],
        compiler_params=_params(),
    )(p, mu)


def shift_bwd(du, p, mu, tr):
    n_rows, w = p.shape
    nb = n_rows // tr

    def body(du_ref, p_ref, mu_ref, dp_ref, dmu_ref, carry):
        @pl.when(pl.program_id(0) == 0)
        def _():
            carry[...] = jnp.zeros_like(carry)
            dmu_ref[...] = jnp.zeros_like(dmu_ref)

        d = du_ref[...]
        rolled = pltpu.roll(d, tr - 1, 0)
        tail = pltpu.roll(carry[...], SUBLANES - 1, 0)
        fixed = jnp.concatenate([rolled[:tr - SUBLANES], tail], axis=0)
        row = lax.broadcasted_iota(jnp.int32, d.shape, 0)
        nxt = jnp.where(row == tr - 1, fixed, rolled)
        mu_v = mu_ref[...]
        dp_ref[...] = d * (1.0 - mu_v) + nxt * mu_v
        dmu_ref[...] += jnp.sum(p_ref[...] * (nxt - d), axis=0, keepdims=True)
        carry[...] = d[:SUBLANES]

    rev = lambda i: (nb - 1 - i, 0)
    return pl.pallas_call(
        body, name="shift_bwd", grid=(nb,),
        in_specs=[pl.BlockSpec((tr, w), rev), pl.BlockSpec((tr, w), rev), _full_spec(mu)],
        out_specs=[pl.BlockSpec((tr, w), rev), _full_spec(mu)],
        out_shape=[jax.ShapeDtypeStruct(p.shape, F32), jax.ShapeDtypeStruct(mu.shape, F32)],
        scratch_shapes=[pltpu.VMEM((SUBLANES, w), F32)],
        compiler_params=_params(),
    )(du, p, mu)


ATT_T = 256


def _att_rows(j):
    return pl.ds(pl.multiple_of(j * ATT_T, ATT_T), ATT_T)


def _att_prep(kv_ref, kpe_ref, kf_scr, vf_scr, n_blocks):
    lane = lax.broadcasted_iota(jnp.int32, (ATT_T, LANES), 1)

    def prep(j, _):
        rows = _att_rows(j)
        kv = kv_ref[rows, :]
        kf_scr[rows, :] = _bf(jnp.where(lane < HEAD, kv, kpe_ref[rows, :]))
        vf_scr[rows, :] = _bf(jnp.where(lane >= HEAD, kv, 0.0))
        return 0

    lax.fori_loop(0, n_blocks, prep, 0)


def _att_diag_mask():
    shift = CHUNK.bit_length() - 1
    qc = jnp.right_shift(lax.broadcasted_iota(jnp.int32, (ATT_T, ATT_T), 0), shift)
    kc = jnp.right_shift(lax.broadcasted_iota(jnp.int32, (ATT_T, ATT_T), 1), shift)
    return kc <= qc


def _wide(x):
    return jnp.concatenate([x] * (ATT_T // LANES), axis=1)


def attn_fwd(q, kv, kpe):
    seq = q.shape[0]
    nb = seq // ATT_T
    assert seq % (2 * ATT_T) == 0, "blocks are taken two per trip"
    scale = MLA_QK ** -0.5

    def body(q_ref, kv_ref, kpe_ref, o_ref, lse_ref, kf_scr, vf_scr):
        _att_prep(kv_ref, kpe_ref, kf_scr, vf_scr, nb)
        mask = _att_diag_mask()

        def scores(qb, kj):
            return _dot(qb, kf_scr[_att_rows(kj), :], 1, 1) * scale

        def update(s, kj, carry, masked):
            m, l, acc = carry
            if masked:
                s = jnp.where(mask, s, NEG)
            m_new = jnp.maximum(m, jnp.broadcast_to(jnp.max(s, -1, keepdims=True), m.shape))
            alpha = jnp.exp(m - m_new)
            p = jnp.exp(s - _wide(m_new))
            l = alpha * l + jnp.broadcast_to(jnp.sum(p, -1, keepdims=True), l.shape)
            acc = alpha * acc + _dot(_bf(p), vf_scr[_att_rows(kj), :], 1, 0)
            return m_new, l, acc

        def finish(rows, carry):
            m, l, acc = carry
            o_ref[rows, :] = acc / l
            lse_ref[rows, :] = m + jnp.log(l)

        def q_pair(qp, _):
            rows_a, rows_b = _att_rows(2 * qp), _att_rows(2 * qp + 1)
            qa, qb = _bf(q_ref[rows_a, :]), _bf(q_ref[rows_b, :])
            init = (jnp.full((ATT_T, LANES), NEG, F32), jnp.zeros((ATT_T, LANES), F32),
                    jnp.zeros((ATT_T, LANES), F32))

            def trip(kj, c):
                ca, cb, sa, sb = c
                sa_next, sb_next = scores(qa, kj + 1), scores(qb, kj + 1)
                return update(sa, kj, ca, False), update(sb, kj, cb, False), sa_next, sb_next

            ca, cb, sa, sb = lax.fori_loop(0, 2 * qp, trip, (init, init, scores(qa, 0), scores(qb, 0)))
            sb_last = scores(qb, 2 * qp + 1)
            ca = update(sa, 2 * qp, ca, True)
            cb = update(sb_last, 2 * qp + 1, update(sb, 2 * qp, cb, False), True)
            finish(rows_a, ca)
            finish(rows_b, cb)
            return 0

        lax.fori_loop(0, nb // 2, q_pair, 0)

    head = pl.BlockSpec((seq, LANES), lambda h: (0, h))
    return pl.pallas_call(
        body, name="attn_fwd", grid=(HEADS,),
        in_specs=[head, head, pl.BlockSpec((seq, LANES), lambda h: (0, 0))],
        out_specs=[head, head],
        out_shape=[jax.ShapeDtypeStruct((seq, HEADS * LANES), F32)] * 2,
        scratch_shapes=[pltpu.VMEM((seq, LANES), BF16)] * 2,
        compiler_params=_params(),
    )(q, kv, kpe)


def attn_bwd(q, kv, kpe, o, lse, do):
    seq = q.shape[0]
    nb = seq // ATT_T
    assert seq % (2 * ATT_T) == 0, "blocks are taken two per trip"
    scale = MLA_QK ** -0.5

    def body(q_ref, kv_ref, kpe_ref, o_ref, lse_ref, do_ref, dq_ref, dkv_ref, dkpe_ref,
             kf_scr, vf_scr, qb_scr, dob_scr, dsum):
        lane = lax.broadcasted_iota(jnp.int32, (ATT_T, LANES), 1)

        @pl.when(pl.program_id(0) == 0)
        def _():
            dkpe_ref[...] = jnp.zeros_like(dkpe_ref)

        dq_ref[...] = jnp.zeros_like(dq_ref)
        _att_prep(kv_ref, kpe_ref, kf_scr, vf_scr, nb)

        def pre(j, _):
            rows = _att_rows(j)
            d = do_ref[rows, :]
            qb_scr[rows, :] = _bf(q_ref[rows, :])
            dob_scr[rows, :] = _bf(d)
            dsum[rows, :] = jnp.broadcast_to(jnp.sum(d * o_ref[rows, :], -1, keepdims=True), (ATT_T, LANES))
            return 0

        lax.fori_loop(0, nb, pre, 0)
        mask = _att_diag_mask()

        def front(kf, vf, qi):
            rows = _att_rows(qi)
            return _dot(qb_scr[rows, :], kf, 1, 1), _dot(dob_scr[rows, :], vf, 1, 1)

        def back(kf, qi, fr, carry, masked):
            s, dp = fr
            dk, dv = carry
            rows = _att_rows(qi)
            qb, dob = qb_scr[rows, :], dob_scr[rows, :]
            p = jnp.exp(s * scale - _wide(lse_ref[rows, :]))
            if masked:
                p = jnp.where(mask, p, 0.0)
            ds = _bf(p * (dp - _wide(dsum[rows, :])) * scale)
            return (dk + _dot(ds, qb, 0, 0), dv + _dot(_bf(p), dob, 0, 0)), _dot(ds, kf, 1, 0)

        def store(krows, carry):
            dk, dv = carry
            dkv_ref[krows, :] = jnp.where(lane < HEAD, dk, dv)
            dkpe_ref[krows, :] += jnp.where((lane >= HEAD) & (lane < MLA_QK), dk, 0.0)

        def k_pair(kp, _):
            ka, kb = 2 * kp, 2 * kp + 1
            rows_a, rows_b = _att_rows(ka), _att_rows(kb)
            kfa, vfa, kfb, vfb = kf_scr[rows_a, :], vf_scr[rows_a, :], kf_scr[rows_b, :], vf_scr[rows_b, :]
            zero = jnp.zeros((ATT_T, LANES), F32)
            ca, dq_a = back(kfa, ka, front(kfa, vfa, ka), (zero, zero), True)
            dq_ref[rows_a, :] += dq_a
            ca, dq_a = back(kfa, kb, front(kfa, vfa, kb), ca, False)
            cb, dq_b = back(kfb, kb, front(kfb, vfb, kb), (zero, zero), True)
            dq_ref[rows_b, :] += dq_a + dq_b

            def both(qi, c):
                ca, cb, fa, fb = c
                nxt = jnp.minimum(qi + 1, nb - 1)
                fa_next, fb_next = front(kfa, vfa, nxt), front(kfb, vfb, nxt)
                ca, dq_a = back(kfa, qi, fa, ca, False)
                cb, dq_b = back(kfb, qi, fb, cb, False)
                dq_ref[_att_rows(qi), :] += dq_a + dq_b
                return ca, cb, fa_next, fb_next

            first = jnp.minimum(kb + 1, nb - 1)
            ca, cb, _, _ = lax.fori_loop(kb + 1, nb, both, (ca, cb, front(kfa, vfa, first), front(kfb, vfb, first)))
            store(rows_a, ca)
            store(rows_b, cb)
            return 0

        lax.fori_loop(0, nb // 2, k_pair, 0)

    head = pl.BlockSpec((seq, LANES), lambda h: (0, h))
    shared = pl.BlockSpec((seq, LANES), lambda h: (0, 0))
    return pl.pallas_call(
        body, name="attn_bwd", grid=(HEADS,),
        in_specs=[head, head, shared, head, head, head],
        out_specs=[head, head, shared],
        out_shape=[jax.ShapeDtypeStruct((seq, HEADS * LANES), F32)] * 2
        + [jax.ShapeDtypeStruct((seq, LANES), F32)],
        scratch_shapes=[pltpu.VMEM((seq, LANES), BF16)] * 4 + [pltpu.VMEM((seq, LANES), F32)],
        compiler_params=_params(),
    )(q, kv, kpe, o, lse, do)


WKV_TB = 128
WKV_GROUP = SUBLANES
WKV_HALF = WIDTH // 2


def _wkv_consts():
    row = lax.broadcasted_iota(jnp.int32, (HEAD, WKV_HALF), 0)
    lane = lax.broadcasted_iota(jnp.int32, (HEAD, WKV_HALF), 1)
    diag = row == jnp.bitwise_and(lane, HEAD - 1)
    sub = lax.broadcasted_iota(jnp.int32, (WKV_GROUP, WKV_HALF), 0)
    return diag, sub


def _halves(x):
    return [x[:, :WKV_HALF], x[:, WKV_HALF:]]


def _diag_rows(row, diag):
    return _bf(jnp.where(diag, jnp.broadcast_to(row, diag.shape), 0.0))


def _put_row(tile, row, i, sub):
    return jnp.where(sub == i, jnp.broadcast_to(row, tile.shape), tile)


def _col_sum(x):
    return jnp.sum(x, axis=0, keepdims=True)


def _step(x, i):
    return x[i * HEAD:(i + 1) * HEAD]


def _expand_group(rows8, diag, ones_b):
    lhs = jnp.concatenate([_diag_rows(rows8[i:i + 1], diag) for i in range(WKV_GROUP)], axis=0)
    return _dot(lhs, ones_b, 1, 0)


def _head_dots(prods, ones_b, sub):
    tile = jnp.zeros((WKV_GROUP, WKV_HALF), F32)
    for i, p in enumerate(prods):
        tile = _put_row(tile, p, i, sub)
    res = _exact_dot(tile, ones_b, 0)
    return [res[i:i + 1] for i in range(len(prods))]


def _diag_group(x, diag, sub):
    out = jnp.zeros((WKV_GROUP, WKV_HALF), F32)
    for i in range(WKV_GROUP):
        out = _put_row(out, _col_sum(jnp.where(diag, _step(x, i), 0.0)), i, sub)
    return out


def wkv_fwd(r, w, k, v, a, b, ones_half):
    seq = r.shape[0]

    def body(r_ref, w_ref, k_ref, v_ref, a_ref, b_ref, ones_ref, y_ref, st_ref, s_scr):
        @pl.when(pl.program_id(0) == 0)
        def _():
            s_scr[...] = jnp.zeros_like(s_scr)

        ones_b = ones_ref[...]
        diag, sub = _wkv_consts()

        ng = WKV_TB // WKV_GROUP
        last = WKV_GROUP - 2

        def rows_of(g):
            return pl.ds(pl.multiple_of(g * WKV_GROUP, WKV_GROUP), WKV_GROUP)

        def pair_rows(x8, t):
            return jnp.concatenate([_diag_rows(x8[t:t + 1], diag), _diag_rows(x8[t + 1:t + 2], diag)], axis=0)

        def put_y(g, pairs_y):
            tile = _halves(y_ref[rows_of(g), :])
            for hf in range(2):
                tile[hf] = _put_row(_put_row(tile[hf], pairs_y[hf][0], last, sub), pairs_y[hf][1], last + 1, sub)
            y_ref[rows_of(g), :] = jnp.concatenate(tile, axis=1)

        def read_out(yexp):
            return _col_sum(jnp.where(diag, yexp[:HEAD], 0.0)), _col_sum(jnp.where(diag, yexp[HEAD:], 0.0))

        def group(g, carry):
            state, v_cur, read = (list(c) for c in carry)
            base = pl.multiple_of(g * WKV_GROUP, WKV_GROUP)
            rows = rows_of(g)
            r8, w8, k8, v8, a8, b8 = (_halves(ref[rows, :]) for ref in (r_ref, w_ref, k_ref, v_ref, a_ref, b_ref))
            v_after = _halves(v_ref[rows_of(jnp.minimum(g + 1, ng - 1)), :])
            evens = range(0, WKV_GROUP, 2)
            dots = [_head_dots([b8[hf][t:t + 1] * a8[hf][t + 1:t + 2] for t in evens]
                               + [k8[hf][t:t + 1] * a8[hf][t + 1:t + 2] for t in evens], ones_b, sub) for hf in range(2)]
            y8 = [jnp.zeros((WKV_GROUP, WKV_HALF), F32)] * 2
            y_before = [None, None]
            for t in evens:
                s0, s1 = slice(t, t + 1), slice(t + 1, t + 2)
                both = []
                for hf in range(2):
                    s_in = state[hf]
                    v_next = pair_rows(v8[hf], t + 2) if t < last else pair_rows(v_after[hf], 0)
                    res = _dot(jnp.concatenate([_bf(s_in * a8[hf][s0]), _bf(s_in * (w8[hf][s0] * a8[hf][s1])),
                                                v_next, read[hf]], axis=0), ones_b, 1, 0)
                    sa0, v0, v1 = res[:HEAD], v_cur[hf][:HEAD], v_cur[hf][HEAD:]
                    st0 = s_in * w8[hf][s0] + sa0 * b8[hf][s0] + v0 * k8[hf][s0]
                    sa1 = res[HEAD:2 * HEAD] + sa0 * dots[hf][t // 2] + v0 * dots[hf][WKV_GROUP // 2 + t // 2]
                    st1 = st0 * w8[hf][s1] + sa1 * b8[hf][s1] + v1 * k8[hf][s1]
                    both.append((st0, st1))
                    state[hf], v_cur[hf] = st1, res[2 * HEAD:4 * HEAD]
                    read[hf] = jnp.concatenate([_bf(st0 * r8[hf][s0]), _bf(st1 * r8[hf][s1])], axis=0)
                    ya, yb = read_out(res[4 * HEAD:])
                    if t == 0:
                        y_before[hf] = (ya, yb)
                    else:
                        y8[hf] = _put_row(_put_row(y8[hf], ya, t - 2, sub), yb, t - 1, sub)
                for j in range(2):
                    st_ref[base + t + j] = jnp.concatenate([both[0][j], both[1][j]], axis=1)
            y_ref[rows, :] = jnp.concatenate(y8, axis=1)
            put_y(jnp.maximum(g - 1, 0), y_before)
            return tuple(state), tuple(v_cur), tuple(read)

        v_first = _halves(v_ref[rows_of(0), :])
        init = (tuple(_halves(s_scr[...])),
                tuple(_dot(pair_rows(v_first[hf], 0), ones_b, 1, 0) for hf in range(2)),
                tuple(jnp.zeros((2 * HEAD, WKV_HALF), BF16) for _ in range(2)))
        fin, _, read = lax.fori_loop(0, ng, group, init)
        put_y(ng - 1, [read_out(_dot(read[hf], ones_b, 1, 0)) for hf in range(2)])
        s_scr[...] = jnp.concatenate(fin, axis=1)

    vec = pl.BlockSpec((WKV_TB, WIDTH), lambda i: (i, 0))
    return pl.pallas_call(
        body, name="wkv_fwd", grid=(seq // WKV_TB,),
        in_specs=[vec] * 6 + [_full_spec(ones_half)],
        out_specs=[vec, pl.BlockSpec((WKV_TB, HEAD, WIDTH), lambda i: (i, 0, 0))],
        out_shape=[jax.ShapeDtypeStruct((seq, WIDTH), F32), jax.ShapeDtypeStruct((seq, HEAD, WIDTH), F32)],
        scratch_shapes=[pltpu.VMEM((HEAD, WIDTH), F32)],
        compiler_params=_params(),
    )(r, w, k, v, a, b, ones_half)


def wkv_bwd(r, w, k, v, a, b, dy, states, ones_half):
    seq = r.shape[0]
    nb = seq // WKV_TB
    ng = WKV_TB // WKV_GROUP

    def body(r_ref, w_ref, k_ref, v_ref, a_ref, b_ref, dy_ref, st_ref, halo_ref, ones_ref,
             dr_ref, dw_ref, dk_ref, dv_ref, da_ref, db_ref, ds_scr):
        blk = nb - 1 - pl.program_id(0)

        @pl.when(pl.program_id(0) == 0)
        def _():
            ds_scr[...] = jnp.zeros_like(ds_scr)

        ones_b = ones_ref[...]
        diag, sub = _wkv_consts()
        before_block = jnp.where(blk == 0, 0.0, halo_ref[0])

        def rows_of(g):
            return pl.ds(pl.multiple_of(g * WKV_GROUP, WKV_GROUP), WKV_GROUP)

        def expand_rows(hf, dy8, v8, a8, t, s_t, s_u):
            s1, s0 = slice(t, t + 1), slice(t - 1, t)
            return jnp.concatenate([_diag_rows(dy8[hf][s1], diag), _diag_rows(dy8[hf][s0], diag),
                                    _diag_rows(v8[hf][s1], diag), _diag_rows(v8[hf][s0], diag),
                                    _bf(s_t[hf] * a8[hf][s1]), _bf(s_u[hf] * a8[hf][s0])], axis=0)

        def read_out(x):
            return _col_sum(jnp.where(diag, x[:HEAD], 0.0)), _col_sum(jnp.where(diag, x[HEAD:], 0.0))

        def put_dv(g, pair_dv):
            tile = _halves(dv_ref[rows_of(g), :])
            for hf in range(2):
                tile[hf] = _put_row(_put_row(tile[hf], pair_dv[hf][0], 1, sub), pair_dv[hf][1], 0, sub)
            dv_ref[rows_of(g), :] = jnp.concatenate(tile, axis=1)

        def group(gg, carry):
            dstate, e_cur, dv_pend = (list(c) for c in carry)
            g = ng - 1 - gg
            base = pl.multiple_of(g * WKV_GROUP, WKV_GROUP)
            rows = rows_of(g)
            r8, w8, k8, v8, a8, b8, dy8 = (
                _halves(ref[rows, :]) for ref in (r_ref, w_ref, k_ref, v_ref, a_ref, b_ref, dy_ref))
            g_next = jnp.maximum(g - 1, 0)
            base_next = pl.multiple_of(g_next * WKV_GROUP, WKV_GROUP)
            dy8n, v8n, a8n = (_halves(ref[rows_of(g_next), :]) for ref in (dy_ref, v_ref, a_ref))
            zero8 = jnp.zeros((WKV_GROUP, WKV_HALF), F32)
            out = {n: [zero8, zero8] for n in ("dr", "dw", "dk", "dv", "da", "db")}
            before_group = jnp.where(g == 0, before_block, st_ref[jnp.maximum(base - 1, 0)])
            states = [_halves(before_group)] + [_halves(st_ref[base + i]) for i in range(WKV_GROUP)]
            odds = range(1, WKV_GROUP, 2)
            dots = [_head_dots([a8[hf][t:t + 1] * b8[hf][t - 1:t] for t in odds]
                               + [r8[hf][t - 1:t] * b8[hf][t - 1:t] for t in odds], ones_b, sub) for hf in range(2)]
            dv_after = [None, None]

            def emit(hf, i, d_i, dsa_i, dy_i, v_i, sa_i):
                s_p, s_t = states[i][hf], states[i + 1][hf]
                for n, val in (("dr", _col_sum(s_t * dy_i)), ("dw", _col_sum(d_i * s_p)), ("db", _col_sum(d_i * sa_i)),
                               ("da", _col_sum(s_p * dsa_i)), ("dk", _col_sum(d_i * v_i))):
                    out[n][hf] = _put_row(out[n][hf], val, i, sub)

            for t in reversed(odds):
                s1, s0 = slice(t, t + 1), slice(t - 1, t)
                for hf in range(2):
                    dy1, dy0, v1, v0, sa1, sa0 = (_step(e_cur[hf], j) for j in range(6))
                    d1 = dstate[hf] + dy1 * r8[hf][s1]
                    if t > 1:
                        nxt = expand_rows(hf, dy8, v8, a8, t - 2, states[t - 2], states[t - 3])
                    else:
                        nxt = expand_rows(hf, dy8n, v8n, a8n, WKV_GROUP - 1, _halves(st_ref[base_next + WKV_GROUP - 2]),
                                          _halves(st_ref[base_next + WKV_GROUP - 3]))
                    res = _dot(jnp.concatenate([_bf(d1 * b8[hf][s1]), _bf(d1 * (w8[hf][s1] * b8[hf][s0])),
                                                nxt, dv_pend[hf]], axis=0), ones_b, 1, 0)
                    dsa1 = res[:HEAD]
                    d0 = d1 * w8[hf][s1] + dsa1 * a8[hf][s1] + dy0 * r8[hf][s0]
                    dsa0 = res[HEAD:2 * HEAD] + dsa1 * dots[hf][t // 2] + dy0 * dots[hf][WKV_GROUP // 2 + t // 2]
                    dstate[hf] = d0 * w8[hf][s0] + dsa0 * a8[hf][s0]
                    e_cur[hf] = res[2 * HEAD:8 * HEAD]
                    dv_pend[hf] = jnp.concatenate([_bf(d1 * k8[hf][s1]), _bf(d0 * k8[hf][s0])], axis=0)
                    emit(hf, t, d1, dsa1, dy1, v1, sa1)
                    emit(hf, t - 1, d0, dsa0, dy0, v0, sa0)
                    dv_a, dv_b = read_out(res[8 * HEAD:])
                    if t == WKV_GROUP - 1:
                        dv_after[hf] = (dv_a, dv_b)
                    else:
                        out["dv"][hf] = _put_row(_put_row(out["dv"][hf], dv_a, t + 2, sub), dv_b, t + 1, sub)
            for ref, n in ((dr_ref, "dr"), (dw_ref, "dw"), (dk_ref, "dk"), (dv_ref, "dv"), (da_ref, "da"), (db_ref, "db")):
                ref[rows, :] = jnp.concatenate(out[n], axis=1)
            put_dv(jnp.minimum(g + 1, ng - 1), dv_after)
            return tuple(dstate), tuple(e_cur), tuple(dv_pend)

        top = rows_of(ng - 1)
        dy8t, v8t, a8t = (_halves(ref[top, :]) for ref in (dy_ref, v_ref, a_ref))
        s_t, s_u = _halves(st_ref[WKV_TB - 2]), _halves(st_ref[WKV_TB - 3])
        init = (tuple(_halves(ds_scr[...])),
                tuple(_dot(expand_rows(hf, dy8t, v8t, a8t, WKV_GROUP - 1, s_t, s_u), ones_b, 1, 0) for hf in range(2)),
                tuple(jnp.zeros((2 * HEAD, WKV_HALF), BF16) for _ in range(2)))
        fin, _, dv_pend = lax.fori_loop(0, ng, group, init)
        put_dv(0, [read_out(_dot(dv_pend[hf], ones_b, 1, 0)) for hf in range(2)])
        ds_scr[...] = jnp.concatenate(fin, axis=1)

    vec = pl.BlockSpec((WKV_TB, WIDTH), lambda i: (nb - 1 - i, 0))
    return pl.pallas_call(
        body, name="wkv_bwd", grid=(nb,),
        in_specs=[vec] * 7 + [
            pl.BlockSpec((WKV_TB, HEAD, WIDTH), lambda i: (nb - 1 - i, 0, 0)),
            pl.BlockSpec((1, HEAD, WIDTH), lambda i: (jnp.maximum((nb - 1 - i) * WKV_TB - 1, 0), 0, 0)),
            _full_spec(ones_half)],
        out_specs=[vec] * 6,
        out_shape=[jax.ShapeDtypeStruct((seq, WIDTH), F32)] * 6,
        scratch_shapes=[pltpu.VMEM((HEAD, WIDTH), F32)],
        compiler_params=_params(),
    )(r, w, k, v, a, b, dy, states, states, ones_half)


def ada_fwd(c8, b_ada, gathered):
    cols = 3 * D_MODEL // 4

    def body(c_ref, b_ref, w_ref, o_ref):
        @pl.when(pl.program_id(1) == 0)
        def _():
            o_ref[...] = jnp.broadcast_to(b_ref[...], o_ref.shape)

        o_ref[...] += mm(_silu(c_ref[...]), w_ref[0])

    return pl.pallas_call(
        body, name="ada_fwd", grid=(4, D_MODEL // PACK_ROWS),
        in_specs=[pl.BlockSpec((SUBLANES, PACK_ROWS), lambda s, i: (0, i)),
                  pl.BlockSpec((1, cols), lambda s, i: (0, s)),
                  pl.BlockSpec((1, PACK_ROWS, cols), lambda s, i: (2 * s, 0, i))],
        out_specs=pl.BlockSpec((SUBLANES, cols), lambda s, i: (0, s)),
        out_shape=jax.ShapeDtypeStruct((SUBLANES, 3 * D_MODEL), F32),
        compiler_params=_params(("arbitrary", "arbitrary")),
    )(c8, b_ada, gathered)


def ada_grad_shard(sc_cols, dada_rows):
    n = len(sc_cols)

    def body(*refs):
        d_ref, o_ref = refs[n], refs[n + 1]
        acc = refs[0][...] * d_ref[0:1, :]
        for b in range(1, n):
            acc = acc + refs[b][...] * d_ref[b:b + 1, :]
        o_ref[...] = acc

    return pl.pallas_call(
        body, name="ada_grad_shard",
        out_shape=jax.ShapeDtypeStruct((sc_cols[0].shape[0], dada_rows.shape[1]), F32),
        compiler_params=pltpu.CompilerParams(vmem_limit_bytes=VMEM_LIMIT),
    )(*sc_cols, dada_rows)


def sum_slots(buf, tr):
    n, rows, cols = buf.shape

    def body(b_ref, o_ref):
        acc = b_ref[0].astype(F32)
        for s in range(1, n):
            acc = acc + b_ref[s].astype(F32)
        o_ref[...] = acc

    return pl.pallas_call(
        body, name="sum_slots", grid=(rows // tr,),
        in_specs=[pl.BlockSpec((n, tr, cols), lambda i: (0, i, 0))],
        out_specs=pl.BlockSpec((tr, cols), lambda i: (i, 0)),
        out_shape=jax.ShapeDtypeStruct((rows, cols), F32),
        compiler_params=_params(),
    )(buf)


def adamw_small(gathered, w, m, v):
    n = gathered.shape[0]

    def body(g_ref, w_ref, m_ref, v_ref, go_ref, d_ref, mo_ref, vo_ref):
        g = g_ref[0]
        for s in range(1, n):
            g = g + g_ref[s]
        go_ref[...] = g
        d_ref[...], mo_ref[...], vo_ref[...] = f_adamw(w_ref[...], g, m_ref[...], v_ref[...])

    return pl.pallas_call(
        body, name="adamw_small",
        out_shape=[jax.ShapeDtypeStruct(w.shape, F32)] * 4,
        compiler_params=pltpu.CompilerParams(vmem_limit_bytes=VMEM_LIMIT),
    )(gathered, w, m, v)


def _coords():
    return lax.axis_index("x"), lax.axis_index("y"), lax.axis_index("c")


def _flip(v, bit):
    return 1 - v if bit else v


def _hbm_call(body, name, out_shape, n_sems, *args):
    any_spec = pl.BlockSpec(memory_space=pl.ANY)
    return pl.pallas_call(
        body, name=name, out_shape=out_shape,
        in_specs=[any_spec] * len(args), out_specs=any_spec,
        scratch_shapes=[pltpu.SemaphoreType.DMA((n_sems,)), pltpu.SemaphoreType.DMA((n_sems,)),
                        pltpu.SemaphoreType.DMA],
    )(*args)


def all_gather8(name, block):
    def body(x_ref, out_ref, send_sems, recv_sems, local_sem):
        x, y, c = _coords()
        me, sibling = (x, y, c), (x, y, 1 - c)
        chips = [(1 - x, y), (x, 1 - y), (1 - x, 1 - y)]

        def slot(px, py, pc):
            return out_ref.at[4 * px + 2 * py + pc]

        def copy(k, blk, to, src=None):
            return pltpu.make_async_remote_copy(
                src_ref=slot(*blk) if src is None else src, dst_ref=slot(*blk),
                send_sem=send_sems.at[k], recv_sem=recv_sems.at[k], device_id=to, device_id_type=MESH)

        mine = pltpu.make_async_copy(x_ref, slot(*me), local_sem)
        mine.start()
        first = [copy(0, me, sibling, src=x_ref)]
        first += [copy(1 + j, me, (*chip, c), src=x_ref) for j, chip in enumerate(chips)]
        for cp in first:
            cp.start()
        passed = [copy(4 + j, (*chip, c), sibling) for j, chip in enumerate(chips)]
        for j, chip in enumerate(chips):
            copy(1 + j, (*chip, c), me).wait_recv()
            passed[j].start()
        copy(0, sibling, me).wait_recv()
        for j, chip in enumerate(chips):
            copy(4 + j, (*chip, 1 - c), me).wait_recv()
        for cp in first + passed:
            cp.wait_send()
        mine.wait()

    return _hbm_call(body, name, jax.ShapeDtypeStruct((N_DEV,) + block.shape, block.dtype), 7, block)


def pair_swap(name, block):
    def body(x_ref, out_ref, send_sems, recv_sems, local_sem):
        x, y, c = _coords()
        cp = pltpu.make_async_remote_copy(
            src_ref=x_ref, dst_ref=out_ref, send_sem=send_sems.at[0], recv_sem=recv_sems.at[0],
            device_id=(x, y, 1 - c), device_id_type=MESH)
        cp.start()
        cp.wait_recv()
        cp.wait_send()

    return _hbm_call(body, name, jax.ShapeDtypeStruct(block.shape, block.dtype), 1, block)


def chip_all_to_all(name, buf):
    def body(x_ref, out_ref, send_sems, recv_sems, local_sem):
        x, y, c = _coords()
        me = 2 * x + y
        mine = pltpu.make_async_copy(x_ref.at[me], out_ref.at[me], local_sem)
        mine.start()
        copies = []
        for k in range(1, 4):
            px, py = _flip(x, k & 2), _flip(y, k & 1)
            copies.append(pltpu.make_async_remote_copy(
                src_ref=x_ref.at[2 * px + py], dst_ref=out_ref.at[me],
                send_sem=send_sems.at[k - 1], recv_sem=recv_sems.at[k - 1],
                device_id=(px, py, c), device_id_type=MESH))
        for cp in copies:
            cp.start()
        for cp in copies:
            cp.wait_recv()
        for cp in copies:
            cp.wait_send()
        mine.wait()

    return _hbm_call(body, name, jax.ShapeDtypeStruct(buf.shape, buf.dtype), 3, buf)


def sibling_gather(name, block):
    def body(x_ref, out_ref, send_sems, recv_sems, local_sem):
        x, y, c = _coords()
        mine = pltpu.make_async_copy(x_ref, out_ref.at[c], local_sem)
        mine.start()
        cp = pltpu.make_async_remote_copy(
            src_ref=x_ref, dst_ref=out_ref.at[c], send_sem=send_sems.at[0], recv_sem=recv_sems.at[0],
            device_id=(x, y, 1 - c), device_id_type=MESH)
        cp.start()
        cp.wait_recv()
        cp.wait_send()
        mine.wait()

    return _hbm_call(body, name, jax.ShapeDtypeStruct((2,) + block.shape, block.dtype), 1, block)


def _col_blocks(a, cols):
    a = jnp.pad(a, ((0, 0), (0, cols - a.shape[1])))
    return [a[i * PACK_ROWS:(i + 1) * PACK_ROWS] for i in range(a.shape[0] // PACK_ROWS)]


def _pack_shard(sh, dtype, with_ada=True):
    lora = jnp.concatenate([sh['w_decay_up'], sh['w_iclr_up']], axis=1)
    misc = jnp.concatenate([sh['w_ukv'], lora, jnp.zeros((LORA, 2 * LANES), lora.dtype)], axis=0)
    blocks = ((_col_blocks(sh['w_ada'], 768) if with_ada else [])
              + _col_blocks(sh['w_in'], 1408) + _col_blocks(sh['w_proj_a'], 256)
              + _col_blocks(sh['w_proj_b'], 256) + [sh['w_out']] + _col_blocks(sh['w_uq'], 256) + [misc])
    return jnp.concatenate([b.astype(dtype) for b in blocks], axis=1)


def _unpack_shard(p, with_ada=True):
    o = [0]

    def take(n_blocks, cols, used):
        blocks = [p[:, o[0] + i * cols:o[0] + (i + 1) * cols] for i in range(n_blocks)]
        o[0] += n_blocks * cols
        return jnp.concatenate(blocks, axis=0)[:, :used]

    out = {'w_ada': take(4, 768, 768)} if with_ada else {}
    out.update({'w_in': take(4, 1408, 1288), 'w_proj_a': take(2, 256, 256),
                'w_proj_b': take(2, 256, 256), 'w_out': take(1, 1024, 1024), 'w_uq': take(1, 256, 192)})
    misc = take(1, 256, 256)
    out['w_ukv'] = misc[:2 * LORA]
    out['w_decay_up'] = misc[2 * LORA:3 * LORA, :LANES]
    out['w_iclr_up'] = misc[2 * LORA:3 * LORA, LANES:]
    return out


def _pack_small(parts):
    flat = jnp.concatenate([p.reshape(-1) for p in parts])
    return jnp.pad(flat, (0, SMALL_ROWS * LANES - flat.shape[0])).reshape(SMALL_ROWS, LANES)


def _unpack_small(packed):
    flat, out, o = packed.reshape(-1), {}, 0
    for name, n in _SMALL:
        out[name] = flat[o:o + n]
        o += n
    return out


def _pad_heads_cols(w, used, left):
    k = w.shape[0]
    return jnp.pad(w.reshape(k, HEADS, used), ((0, 0), (0, 0), (left, LANES - used - left))).reshape(k, HEADS * LANES)


def _unpad_heads_cols(w, used, left):
    k = w.shape[0]
    return w.reshape(k, HEADS, LANES)[:, :, left:left + used].reshape(k, HEADS * used)


def kernel(x, c, positions, w_ada, b_ada, w_in, q_norm_g, w_uq, kv_norm_g, w_ukv, mu_rwkv, w0, w_decay_up, a0, w_iclr_up, k_k, k_a, r_k, gn_g, gn_b, w_proj_a, w_proj_b, w_out, post_g, post_b, loss_target, m_w_ada, m_b_ada, m_w_in, m_q_norm_g, m_w_uq, m_kv_norm_g, m_w_ukv, m_mu_rwkv, m_w0, m_w_decay_up, m_a0, m_w_iclr_up, m_k_k, m_k_a, m_r_k, m_gn_g, m_gn_b, m_w_proj_a, m_w_proj_b, m_w_out, m_post_g, m_post_b, v_w_ada, v_b_ada, v_w_in, v_q_norm_g, v_w_uq, v_kv_norm_g, v_w_ukv, v_mu_rwkv, v_w0, v_w_decay_up, v_a0, v_w_iclr_up, v_k_k, v_k_a, v_r_k, v_gn_g, v_gn_b, v_w_proj_a, v_w_proj_b, v_w_out, v_post_g, v_post_b):
    given = dict(locals())
    seq = x.shape[1]
    my_c = lax.axis_index("c")

    shard_names = [n for n, _, _ in _SHARDED]
    w_pack = _pack_shard({n: given[n][0] for n in shard_names}, BF16)
    my_half = lax.dynamic_slice_in_dim(w_pack, my_c * HALF_COLS, HALF_COLS, 1)
    gathered = all_gather8("gather_weights", my_half)
    shards = [_unpack_shard(jnp.concatenate([gathered[2 * s], gathered[2 * s + 1]], axis=1)) for s in range(4)]
    full = {n: jnp.concatenate([sh[n] for sh in shards], axis=ax) for n, _, ax in _SHARDED}

    wi = full['w_in']
    zcol = lambda n: jnp.zeros((D_MODEL, n), BF16)
    w_g1 = jnp.concatenate([wi[:, :384], zcol(HEAD), wi[:, 384:416], zcol(LANES - MLA_QK),
                            _pad_heads_cols(wi[:, 416:928], HEAD, HEAD)], axis=1)
    w_g2 = wi[:, 928:3104]
    w_g3 = wi[:, 3104:5152]
    w_uq_p = _pad_heads_cols(full['w_uq'], MLA_QK, 0)
    w_pa_p = jnp.pad(full['w_proj_a'].reshape(HEADS, HEAD, D_MODEL), ((0, 0), (HEAD, 0), (0, 0))).reshape(HEADS * LANES, D_MODEL)
    zl = jnp.zeros((LORA, WIDTH), BF16)
    w_lora = jnp.concatenate([jnp.concatenate([full['w_decay_up'], zl], 1),
                              jnp.concatenate([zl, full['w_iclr_up']], 1)], 0)

    hd = np.arange(WIDTH) // HEAD
    ones_blocks = jnp.asarray(hd[:, None] == hd[None, :], BF16)
    ones_half = ones_blocks[:WKV_HALF, :WKV_HALF]
    perm_np = np.zeros((LANES, LANES), np.float32)
    for d in range(MLA_ROPE // 2):
        perm_np[HEAD + 16 + d, HEAD + d] = -1.0
        perm_np[HEAD + d, HEAD + 16 + d] = 1.0
    perm = jnp.asarray(perm_np, BF16)
    inv = ROPE_THETA ** (-jnp.arange(0, MLA_ROPE, 2, dtype=F32) / MLA_ROPE)
    ang = positions[0].astype(F32)[:, None] * inv
    cos_a, sin_a = jnp.cos(ang), jnp.sin(ang)
    cs = jnp.concatenate([jnp.ones((seq, HEAD), F32), cos_a, cos_a, jnp.zeros((seq, LANES - MLA_QK), F32),
                          jnp.zeros((seq, HEAD), F32), sin_a, sin_a, jnp.zeros((seq, LANES - MLA_QK), F32)], axis=1)

    x2, tgt = x[0], loss_target[0]
    r_k2 = r_k.reshape(1, WIDTH)

    c8 = jnp.broadcast_to(c, (SUBLANES, D_MODEL))
    ada = ada_fwd(c8, b_ada, gathered)[:1]
    shift, scale, gate = ada[:, :D_MODEL], ada[:, D_MODEL:2 * D_MODEL], ada[:, 2 * D_MODEL:]

    f_in1, f_in2, f_in3 = _make_f_in((512, 1024)), _make_f_in((SHIFT_W, WIDTH)), _make_f_in((1024, 1024))
    tr = min(256, seq)
    p_mla, gpa = row_fwd("in1_fwd", f_in1, [x2], [shift, scale, w_g1], [], [512, 1024], tr)
    p_rwkv, gpb = row_fwd("in2_fwd", f_in2, [x2], [shift, scale, w_g2], [], [SHIFT_W, WIDTH], tr)
    ma, mb = row_fwd("in3_fwd", f_in3, [x2], [shift, scale, w_g3], [], [1024, 1024], tr)

    mla_par = [q_norm_g, kv_norm_g, w_uq_p, full['w_ukv']]
    q_f, kv_f, kpe = row_fwd("mla_pre_fwd", f_mla_pre, [p_mla, cs], mla_par, [perm], [1024, 1024, LANES], tr)
    ya, lse = attn_fwd(q_f, kv_f, kpe)

    u = shift_fwd(p_rwkv, mu_rwkv, tr)
    pre_par = [w0, a0, k_k, k_a, w_lora]
    rr, wd, k2, vv, an, bb = row_fwd("rwkv_pre_fwd", f_rwkv_pre, [u], pre_par, [ones_blocks], [WIDTH] * 6, tr)
    y_wkv, states = wkv_fwd(rr, wd, k2, vv, an, bb, ones_half)
    post_b_par = [r_k2, gn_g, gn_b]
    yb, = row_fwd("rwkv_post_fwd", f_rwkv_post, [y_wkv, rr, k2, vv], post_b_par, [ones_blocks], [WIDTH], tr)

    out_rows = [ya, gpa, yb, gpb, ma, mb, x2, tgt]
    out_par = [gate, post_g, post_b, w_pa_p, full['w_proj_b'], full['w_out']]
    lrows, = row_fwd("post_fwd", f_post, out_rows, out_par, [], [LANES], tr)
    loss = lax.psum(jnp.sum(lrows[:, 0]), ("x", "y", "c"))

    trb = min(128, seq)
    dl = jnp.broadcast_to((jnp.arange(LANES) == 0).astype(F32), (seq, LANES))
    (dya, dgpa, dyb, dgpb, dma, dmb, dx_res), (dgate, dpost_g, dpost_b, dw_pa_p, dw_pb, dw_out) = row_bwd(
        "post_bwd", f_post, out_rows, 7, out_par, [], [dl], trb)

    (dy_wkv, dr1, dk1, dv1), (dr_k, dgn_g, dgn_b) = row_bwd(
        "rwkv_post_bwd", f_rwkv_post, [y_wkv, rr, k2, vv], 4, post_b_par, [ones_blocks], [dyb], tr)
    dr2, dwd, dk2, dv2, dan, dbb = wkv_bwd(rr, wd, k2, vv, an, bb, dy_wkv, states, ones_half)
    (du,), (dw0, da0, dk_k, dk_a, dw_lora) = row_bwd(
        "rwkv_pre_bwd", f_rwkv_pre, [u], 1, pre_par, [ones_blocks],
        [(dr1, dr2), dwd, (dk1, dk2), (dv1, dv2), dan, dbb], tr)
    dp_rwkv, dmu = shift_bwd(du, p_rwkv, mu_rwkv, tr)

    dq_f, dkv_f, dkpe = attn_bwd(q_f, kv_f, kpe, ya, lse, dya)
    (dp_mla,), (dqg, dkvg, dw_uq_p, dw_ukv) = row_bwd(
        "mla_pre_bwd", f_mla_pre, [p_mla, cs], 1, mla_par, [perm], [dq_f, dkv_f, dkpe], tr)

    (dx1,), (dsh1, dsc1, dw_g1) = row_bwd("in1_bwd", f_in1, [x2], 1, [shift, scale, w_g1], [], [dp_mla, dgpa], trb, [dx_res])
    (dx2,), (dsh2, dsc2, dw_g2) = row_bwd("in2_bwd", f_in2, [x2], 1, [shift, scale, w_g2], [], [dp_rwkv, dgpb], trb, [dx1])
    (dx3,), (dsh3, dsc3, dw_g3) = row_bwd("in3_bwd", f_in3, [x2], 1, [shift, scale, w_g3], [], [dma, dmb], trb, [dx2])
    grad_x = dx3[None]

    dada = jnp.concatenate([dsh1 + dsh2 + dsh3, dsc1 + dsc2 + dsc3, dgate], axis=1)
    local = {
        'w_in': jnp.concatenate([dw_g1[:, :384], dw_g1[:, 448:480], _unpad_heads_cols(dw_g1[:, 512:], HEAD, HEAD),
                                 dw_g2, dw_g3], axis=1),
        'w_uq': _unpad_heads_cols(dw_uq_p, MLA_QK, 0),
        'w_ukv': dw_ukv,
        'w_decay_up': dw_lora[:LORA, :WIDTH],
        'w_iclr_up': dw_lora[LORA:, WIDTH:],
        'w_proj_a': dw_pa_p.reshape(HEADS, LANES, D_MODEL)[:, HEAD:].reshape(WIDTH, D_MODEL),
        'w_proj_b': dw_pb,
        'w_out': dw_out,
    }
    small_local = {'b_ada': dada, 'q_norm_g': dqg, 'kv_norm_g': dkvg, 'mu_rwkv': dmu, 'w0': dw0, 'a0': da0,
                   'k_k': dk_k, 'k_a': dk_a, 'r_k': dr_k, 'gn_g': dgn_g, 'gn_b': dgn_b,
                   'post_g': dpost_g, 'post_b': dpost_b}

    def shard_of(g, axis, s):
        n = g.shape[axis] // 4
        return lax.slice_in_dim(g, s * n, (s + 1) * n, axis=axis)

    packed = jnp.stack([_pack_shard({n: shard_of(local[n], ax, s) for n, _, ax in _SHARDED if n != 'w_ada'}, F32, False)
                        for s in range(4)])
    keep = lax.dynamic_slice_in_dim(packed, my_c * GRAD_HALF, GRAD_HALF, 2).reshape(4 * PACK_ROWS, GRAD_HALF)
    give = lax.dynamic_slice_in_dim(packed, (1 - my_c) * GRAD_HALF, GRAD_HALF, 2).reshape(4 * PACK_ROWS, GRAD_HALF)
    pair_sum, = row_fwd("pair_sum", lambda p, q: (p + q,), [keep, pair_swap("swap_halves", give)], [], [],
                        [GRAD_HALF], PACK_ROWS // 2, BF16)
    received = chip_all_to_all("exchange_grads", pair_sum.reshape(4, PACK_ROWS, GRAD_HALF))
    my_sum = sum_slots(received, PACK_ROWS // 2)
    halves = sibling_gather("gather_halves", my_sum)
    g_shard = _unpack_shard(jnp.concatenate([halves[0], halves[1]], axis=1), False)

    small_pack = lambda d, extra=(): _pack_small([d[n] for n, _ in _SMALL] + list(extra))
    small_all = all_gather8("gather_small", small_pack(small_local, [c * jax.nn.sigmoid(c)]))
    sc_all = small_all[:, SMALL_USED:SMALL_USED + D_MODEL // LANES].reshape(N_DEV, D_MODEL)
    dada_all = small_all[:, :3 * D_MODEL // LANES].reshape(N_DEV, 3 * D_MODEL)
    my_cols = lax.dynamic_slice_in_dim(dada_all, (2 * lax.axis_index("x") + lax.axis_index("y")) * 768, 768, 1)
    g_shard['w_ada'] = ada_grad_shard([sc_all[b].reshape(D_MODEL, 1) for b in range(N_DEV)], my_cols)

    big = [{}, {}, {}, {}]
    for n in shard_names:
        w2, m2, v2 = given[n][0], given['m_' + n][0], given['v_' + n][0]
        cols = w2.shape[1]
        outs = row_fwd("adamw_" + n, f_adamw, [w2, g_shard[n], m2, v2], [], [], [cols] * 3, min(256, w2.shape[0]))
        for dst, val in zip(big, (g_shard[n], *outs)):
            dst[n] = val

    small_out = adamw_small(small_all, small_pack({n: given[n] for n, _ in _SMALL}),
                            small_pack({n: given['m_' + n] for n, _ in _SMALL}),
                            small_pack({n: given['v_' + n] for n, _ in _SMALL}))

    results = []
    for big_k, packed_small in zip(big, small_out):
        small = _unpack_small(packed_small)
        results.append([(big_k[n] if n in big_k else small[n]).reshape(given[n].shape) for n in _WEIGHTS])
    return (loss, grad_x, *results[0], *results[1], *results[2], *results[3])
```

```python
from typing import NamedTuple

import numpy as np
import jax
import jax.numpy as jnp
from jax import lax
from jax.experimental import pallas as pl
from jax.experimental.pallas import tpu as pltpu

F32 = jnp.float32
BF16 = jnp.bfloat16

D_MODEL = 1024
LN_EPS = 1e-5
RMS_EPS = 1e-6
GN_EPS = 64e-5
HEADS = 8
HEAD = 64
MLA_ROPE = 32
MLA_QK = HEAD + MLA_ROPE
ROPE_THETA = 10000.0
WIDTH = HEADS * HEAD
LORA = 64
SHIFT_W = 3 * WIDTH + 2 * LORA
CHUNK = 64
ALPHA = 2.0 ** 0.25

ADAM_LR, ADAM_B1, ADAM_B2, ADAM_EPS, ADAM_WD, ADAM_STEP = 0.001, 0.9, 0.999, 1e-08, 0.01, 10

LANES = 128
SUBLANES = 8
VMEM_LIMIT = 56 * 1024 * 1024
N_DEV = 8
MESH = pl.DeviceIdType.MESH
NEG = -1e30

_WEIGHTS = ['w_ada', 'b_ada', 'w_in', 'q_norm_g', 'w_uq', 'kv_norm_g', 'w_ukv', 'mu_rwkv', 'w0',
            'w_decay_up', 'a0', 'w_iclr_up', 'k_k', 'k_a', 'r_k', 'gn_g', 'gn_b', 'w_proj_a',
            'w_proj_b', 'w_out', 'post_g', 'post_b']
_SHARDED = [('w_ada', (1024, 3072), 1), ('w_in', (1024, 5152), 1), ('w_uq', (256, 768), 1),
            ('w_ukv', (128, 1024), 1), ('w_decay_up', (64, 512), 1), ('w_iclr_up', (64, 512), 1),
            ('w_proj_a', (512, 1024), 1), ('w_proj_b', (512, 1024), 1), ('w_out', (1024, 1024), 0)]
_SMALL = [('b_ada', 3072), ('q_norm_g', 256), ('kv_norm_g', 128), ('mu_rwkv', 1664), ('w0', 512),
          ('a0', 512), ('k_k', 512), ('k_a', 512), ('r_k', 512), ('gn_g', 512), ('gn_b', 512),
          ('post_g', 1024), ('post_b', 1024)]
PACK_ROWS = 256
PACK_COLS = 11264
HALF_COLS = PACK_COLS // 2
ADA_COLS = 4 * 768
GRAD_HALF = (PACK_COLS - ADA_COLS) // 2
SMALL_USED = 84
SMALL_ROWS = 96


def _bf(x):
    return x.astype(BF16)


def _dot(a, b, ca, cb):
    return lax.dot_general(a, b, (((ca,), (cb,)), ((), ())), preferred_element_type=F32)


class Weight(NamedTuple):
    value: jax.Array
    grad: jax.Array


@jax.custom_vjp
def _mm(a, w, w_grad):
    return _dot(_bf(a), _bf(w), 1, 0)


def _mm_fwd(a, w, w_grad):
    return _mm(a, w, w_grad), (a, w)


def _mm_bwd(res, g):
    a, w = res
    gb = _bf(g)
    return _dot(gb, _bf(w), 1, 1), jnp.zeros_like(w), _dot(_bf(a), gb, 0, 0)


_mm.defvjp(_mm_fwd, _mm_bwd)


def mm(a, w):
    if isinstance(w, Weight):
        return _mm(a, w.value, w.grad)
    return _dot(_bf(a), _bf(w), 1, 0)


def _split3(x):
    hi = _bf(x)
    r1 = x - hi.astype(F32)
    mid = _bf(r1)
    lo = _bf(r1 - mid.astype(F32))
    return hi, mid, lo


def _exact_dot(x, m, cm):
    hi, mid, lo = _split3(x)
    return _dot(hi, m, 1, cm) + _dot(mid, m, 1, cm) + _dot(lo, m, 1, cm)


@jax.custom_vjp
def segsum(x, ones_blocks):
    return _exact_dot(x, ones_blocks, 0)


def _segsum_fwd(x, ones_blocks):
    return segsum(x, ones_blocks), ones_blocks


def _segsum_bwd(ones_blocks, g):
    return _exact_dot(g, ones_blocks, 0), jnp.zeros_like(ones_blocks)


segsum.defvjp(_segsum_fwd, _segsum_bwd)


@jax.custom_vjp
def lane_perm(x, perm):
    return _exact_dot(x, perm, 0)


def _lane_perm_fwd(x, perm):
    return lane_perm(x, perm), perm


def _lane_perm_bwd(perm, g):
    return _exact_dot(g, perm, 1), jnp.zeros_like(perm)


lane_perm.defvjp(_lane_perm_fwd, _lane_perm_bwd)


def _silu(z):
    return z * jax.nn.sigmoid(z)


def _softplus(z):
    return jnp.maximum(z, 0.0) + jnp.log(1.0 + jnp.exp(-jnp.abs(z)))


def _layer_norm(x):
    xc = x - jnp.mean(x, -1, keepdims=True)
    return xc * lax.rsqrt(jnp.mean(xc * xc, -1, keepdims=True) + LN_EPS)


def _rope(t, cos_t, sin_t, perm):
    outs = []
    for h in range(t.shape[1] // LANES):
        th = t[:, h * LANES:(h + 1) * LANES]
        outs.append(th * cos_t + lane_perm(th, perm) * sin_t)
    return outs[0] if len(outs) == 1 else jnp.concatenate(outs, axis=1)


def _make_f_in(splits):
    def f_in(x, shift, scale, w):
        h = _layer_norm(x) * (1.0 + scale) + shift
        p = mm(h, w)
        outs, o = [], 0
        for s in splits:
            outs.append(p[:, o:o + s])
            o += s
        return tuple(outs)
    return f_in


def f_mla_pre(p, cs, qg, kvg, w_uq, w_ukv, perm):
    q_c, kv_c, k_r = p[:, :256], p[:, 256:384], p[:, 384:512]
    cos_t, sin_t = cs[:, :LANES], cs[:, LANES:]
    qn = q_c * lax.rsqrt(jnp.mean(q_c * q_c, -1, keepdims=True) + RMS_EPS) * qg
    kvn = kv_c * lax.rsqrt(jnp.mean(kv_c * kv_c, -1, keepdims=True) + RMS_EPS) * kvg
    q = _rope(mm(qn, w_uq), cos_t, sin_t, perm)
    kv = mm(kvn, w_ukv)
    return q, kv, _rope(k_r, cos_t, sin_t, perm)


def f_rwkv_pre(u, w0, a0, k_k, k_a, w_lora, ones_blocks):
    r, k, v, lo = u[:, :WIDTH], u[:, WIDTH:2 * WIDTH], u[:, 2 * WIDTH:3 * WIDTH], u[:, 3 * WIDTH:]
    lane = lax.broadcasted_iota(jnp.int32, lo.shape, 1)
    dl = mm(jnp.where(lane < LORA, jnp.tanh(lo), lo), w_lora)
    w_log = -_softplus(-(w0 + dl[:, :WIDTH])) - 0.5
    decay = jnp.exp(-jnp.exp(w_log))
    a = jax.nn.sigmoid(a0 + dl[:, WIDTH:])
    kk = k * k_k
    kk = kk / jnp.maximum(jnp.sqrt(segsum(kk * kk, ones_blocks)), 1e-12)
    k2 = k * (1.0 + (a - 1.0) * k_a)
    return r, decay, k2, v, -kk, kk * a


def f_rwkv_post(y, r, k2, v, r_k, gn_g, gn_b, ones_blocks):
    yc = y - segsum(y, ones_blocks) * (1.0 / HEAD)
    yn = yc * lax.rsqrt(segsum(yc * yc, ones_blocks) * (1.0 / HEAD) + GN_EPS)
    return (yn * gn_g + gn_b + segsum(r * k2 * r_k, ones_blocks) * v,)


def f_post(ya, gpa, yb, gpb, ma, mb, x, tgt, gate, post_g, post_b, w_pa, w_pb, w_out):
    pa = mm(ya * _silu(gpa), w_pa)
    pb = mm(yb * _silu(gpb), w_pb)
    merged = jax.nn.sigmoid(ma) * pa + jax.nn.sigmoid(mb) * pb
    z = ALPHA * x + (1.0 + gate) * mm(merged, w_out)
    err = _layer_norm(z) * post_g + post_b - tgt
    lrow = 0.5 * jnp.mean(err * err, -1, keepdims=True)
    return (jnp.broadcast_to(lrow, (lrow.shape[0], LANES)),)


def f_adamw(w, g, m, v):
    m2 = ADAM_B1 * m + (1.0 - ADAM_B1) * g
    v2 = ADAM_B2 * v + (1.0 - ADAM_B2) * jnp.square(g)
    m_hat = m2 / (1.0 - ADAM_B1 ** ADAM_STEP)
    v_hat = v2 / (1.0 - ADAM_B2 ** ADAM_STEP)
    return -ADAM_LR * (m_hat / (jnp.sqrt(v_hat) + ADAM_EPS) + ADAM_WD * w), m2, v2


def _params(sem=("arbitrary",)):
    return pltpu.CompilerParams(dimension_semantics=sem, vmem_limit_bytes=VMEM_LIMIT)


def _row_spec(tr, a):
    return pl.BlockSpec((tr, a.shape[1]), lambda i: (i, 0))


def _full_spec(a):
    return pl.BlockSpec(a.shape, lambda i: (0,) * a.ndim)


def row_fwd(name, f, rows, params, consts, out_widths, tr, out_dtype=F32):
    n_rows = rows[0].shape[0]
    nr, npar, ncon = len(rows), len(params), len(consts)

    def body(*refs):
        rv = [r[...] for r in refs[:nr]]
        pv = [r[...] for r in refs[nr:nr + npar]]
        cv = [r[...] for r in refs[nr + npar:nr + npar + ncon]]
        outs = f(*rv, *pv, *cv)
        for o_ref, o in zip(refs[nr + npar + ncon:], outs):
            o_ref[...] = o.astype(o_ref.dtype)

    return pl.pallas_call(
        body, name=name, grid=(n_rows // tr,),
        in_specs=[_row_spec(tr, a) for a in rows] + [_full_spec(a) for a in list(params) + list(consts)],
        out_specs=[pl.BlockSpec((tr, w), lambda i: (i, 0)) for w in out_widths],
        out_shape=[jax.ShapeDtypeStruct((n_rows, w), out_dtype) for w in out_widths],
        compiler_params=_params(),
    )(*rows, *params, *consts)


def row_bwd(name, f, rows, n_diff, params, consts, douts, tr, add_rows=None):
    n_rows = rows[0].shape[0]
    douts = [d if isinstance(d, (tuple, list)) else (d,) for d in douts]
    counts = [len(d) for d in douts]
    flat_d = [a for d in douts for a in d]
    add_rows = add_rows or [None] * n_diff
    adds = [a for a in add_rows if a is not None]
    nr, npar, ncon, nd, na = len(rows), len(params), len(consts), len(flat_d), len(adds)

    def body(*refs):
        o = 0
        rv = [r[...] for r in refs[o:o + nr]]; o += nr
        pv = [Weight(r[...], jnp.zeros(r.shape, F32)) if r.dtype == BF16 else r[...] for r in refs[o:o + npar]]
        o += npar
        cv = [r[...] for r in refs[o:o + ncon]]; o += ncon
        dv = []
        for cnt in counts:
            s = refs[o][...]
            for e in range(1, cnt):
                s = s + refs[o + e][...]
            dv.append(s)
            o += cnt
        add_v = [r[...] for r in refs[o:o + na]]; o += na
        drow_refs = refs[o:o + n_diff]; o += n_diff
        dpar_refs = refs[o:o + npar]

        def g(*args):
            return tuple(f(*args[:n_diff], *rv[n_diff:], *args[n_diff:], *cv))

        _, vjp = jax.vjp(g, *rv[:n_diff], *pv)
        grads = vjp(tuple(dv))
        ai = 0
        for j, (r, gr) in enumerate(zip(drow_refs, grads[:n_diff])):
            if add_rows[j] is not None:
                gr = gr + add_v[ai]
                ai += 1
            r[...] = gr

        @pl.when(pl.program_id(0) == 0)
        def _():
            for r in dpar_refs:
                r[...] = jnp.zeros_like(r)

        for r, gr in zip(dpar_refs, grads[n_diff:]):
            r[...] += gr.grad if isinstance(gr, Weight) else gr

    outs = pl.pallas_call(
        body, name=name, grid=(n_rows // tr,),
        in_specs=([_row_spec(tr, a) for a in rows] + [_full_spec(a) for a in list(params) + list(consts)]
                  + [_row_spec(tr, a) for a in flat_d + adds]),
        out_specs=[_row_spec(tr, a) for a in rows[:n_diff]] + [_full_spec(a) for a in params],
        out_shape=([jax.ShapeDtypeStruct(a.shape, F32) for a in rows[:n_diff]]
                   + [jax.ShapeDtypeStruct(a.shape, F32) for a in params]),
        compiler_params=_params(),
    )(*rows, *params, *consts, *flat_d, *adds)
    return outs[:n_diff], outs[n_diff:]


def shift_fwd(p, mu, tr):
    n_rows, w = p.shape

    def body(p_ref, mu_ref, u_ref, carry):
        @pl.when(pl.program_id(0) == 0)
        def _():
            carry[...] = jnp.zeros_like(carry)

        x = p_ref[...]
        rolled = pltpu.roll(x, 1, 0)
        head = pltpu.roll(carry[...], 1, 0)
        fixed = jnp.concatenate([head, rolled[SUBLANES:]], axis=0)
        row = lax.broadcasted_iota(jnp.int32, x.shape, 0)
        prev = jnp.where(row == 0, fixed, rolled)
        u_ref[...] = x + (prev - x) * mu_ref[...]
        carry[...] = x[tr - SUBLANES:]

    return pl.pallas_call(
        body, name="shift_fwd", grid=(n_rows // tr,),
        in_specs=[_row_spec(tr, p), _full_spec(mu)],
        out_specs=_row_spec(tr, p),
        out_shape=jax.ShapeDtypeStruct(p.shape, F32),
        scratch_shapes=[pltpu.VMEM((SUBLANES, w), F32)],
        compiler_params=_params(),
    )(p, mu)


def shift_bwd(du, p, mu, tr):
    n_rows, w = p.shape
    nb = n_rows // tr

    def body(du_ref, p_ref, mu_ref, dp_ref, dmu_ref, carry):
        @pl.when(pl.program_id(0) == 0)
        def _():
            carry[...] = jnp.zeros_like(carry)
            dmu_ref[...] = jnp.zeros_like(dmu_ref)

        d = du_ref[...]
        rolled = pltpu.roll(d, tr - 1, 0)
        tail = pltpu.roll(carry[...], SUBLANES - 1, 0)
        fixed = jnp.concatenate([rolled[:tr - SUBLANES], tail], axis=0)
        row = lax.broadcasted_iota(jnp.int32, d.shape, 0)
        nxt = jnp.where(row == tr - 1, fixed, rolled)
        mu_v = mu_ref[...]
        dp_ref[...] = d * (1.0 - mu_v) + nxt * mu_v
        dmu_ref[...] += jnp.sum(p_ref[...] * (nxt - d), axis=0, keepdims=True)
        carry[...] = d[:SUBLANES]

    rev = lambda i: (nb - 1 - i, 0)
    return pl.pallas_call(
        body, name="shift_bwd", grid=(nb,),
        in_specs=[pl.BlockSpec((tr, w), rev), pl.BlockSpec((tr, w), rev), _full_spec(mu)],
        out_specs=[pl.BlockSpec((tr, w), rev), _full_spec(mu)],
        out_shape=[jax.ShapeDtypeStruct(p.shape, F32), jax.ShapeDtypeStruct(mu.shape, F32)],
        scratch_shapes=[pltpu.VMEM((SUBLANES, w), F32)],
        compiler_params=_params(),
    )(du, p, mu)


ATT_T = 256


def _att_rows(j):
    return pl.ds(pl.multiple_of(j * ATT_T, ATT_T), ATT_T)


def _att_prep(kv_ref, kpe_ref, kf_scr, vf_scr, n_blocks):
    lane = lax.broadcasted_iota(jnp.int32, (ATT_T, LANES), 1)

    def prep(j, _):
        rows = _att_rows(j)
        kv = kv_ref[rows, :]
        kf_scr[rows, :] = _bf(jnp.where(lane < HEAD, kv, kpe_ref[rows, :]))
        vf_scr[rows, :] = _bf(jnp.where(lane >= HEAD, kv, 0.0))
        return 0

    lax.fori_loop(0, n_blocks, prep, 0)


def _att_diag_mask():
    shift = CHUNK.bit_length() - 1
    qc = jnp.right_shift(lax.broadcasted_iota(jnp.int32, (ATT_T, ATT_T), 0), shift)
    kc = jnp.right_shift(lax.broadcasted_iota(jnp.int32, (ATT_T, ATT_T), 1), shift)
    return kc <= qc


def _wide(x):
    return jnp.concatenate([x] * (ATT_T // LANES), axis=1)


def attn_fwd(q, kv, kpe):
    seq = q.shape[0]
    nb = seq // ATT_T
    assert seq % (2 * ATT_T) == 0, "blocks are taken two per trip"
    scale = MLA_QK ** -0.5

    def body(q_ref, kv_ref, kpe_ref, o_ref, lse_ref, kf_scr, vf_scr):
        _att_prep(kv_ref, kpe_ref, kf_scr, vf_scr, nb)
        mask = _att_diag_mask()

        def scores(qb, kj):
            return _dot(qb, kf_scr[_att_rows(kj), :], 1, 1) * scale

        def update(s, kj, carry, masked):
            m, l, acc = carry
            if masked:
                s = jnp.where(mask, s, NEG)
            m_new = jnp.maximum(m, jnp.broadcast_to(jnp.max(s, -1, keepdims=True), m.shape))
            alpha = jnp.exp(m - m_new)
            p = jnp.exp(s - _wide(m_new))
            l = alpha * l + jnp.broadcast_to(jnp.sum(p, -1, keepdims=True), l.shape)
            acc = alpha * acc + _dot(_bf(p), vf_scr[_att_rows(kj), :], 1, 0)
            return m_new, l, acc

        def finish(rows, carry):
            m, l, acc = carry
            o_ref[rows, :] = acc / l
            lse_ref[rows, :] = m + jnp.log(l)

        def q_pair(qp, _):
            rows_a, rows_b = _att_rows(2 * qp), _att_rows(2 * qp + 1)
            qa, qb = _bf(q_ref[rows_a, :]), _bf(q_ref[rows_b, :])
            init = (jnp.full((ATT_T, LANES), NEG, F32), jnp.zeros((ATT_T, LANES), F32),
                    jnp.zeros((ATT_T, LANES), F32))

            def trip(kj, c):
                ca, cb, sa, sb = c
                sa_next, sb_next = scores(qa, kj + 1), scores(qb, kj + 1)
                return update(sa, kj, ca, False), update(sb, kj, cb, False), sa_next, sb_next

            ca, cb, sa, sb = lax.fori_loop(0, 2 * qp, trip, (init, init, scores(qa, 0), scores(qb, 0)))
            sb_last = scores(qb, 2 * qp + 1)
            ca = update(sa, 2 * qp, ca, True)
            cb = update(sb_last, 2 * qp + 1, update(sb, 2 * qp, cb, False), True)
            finish(rows_a, ca)
            finish(rows_b, cb)
            return 0

        lax.fori_loop(0, nb // 2, q_pair, 0)

    head = pl.BlockSpec((seq, LANES), lambda h: (0, h))
    return pl.pallas_call(
        body, name="attn_fwd", grid=(HEADS,),
        in_specs=[head, head, pl.BlockSpec((seq, LANES), lambda h: (0, 0))],
        out_specs=[head, head],
        out_shape=[jax.ShapeDtypeStruct((seq, HEADS * LANES), F32)] * 2,
        scratch_shapes=[pltpu.VMEM((seq, LANES), BF16)] * 2,
        compiler_params=_params(),
    )(q, kv, kpe)


def attn_bwd(q, kv, kpe, o, lse, do):
    seq = q.shape[0]
    nb = seq // ATT_T
    assert seq % (2 * ATT_T) == 0, "blocks are taken two per trip"
    scale = MLA_QK ** -0.5

    def body(q_ref, kv_ref, kpe_ref, o_ref, lse_ref, do_ref, dq_ref, dkv_ref, dkpe_ref,
             kf_scr, vf_scr, qb_scr, dob_scr, dsum):
        lane = lax.broadcasted_iota(jnp.int32, (ATT_T, LANES), 1)

        @pl.when(pl.program_id(0) == 0)
        def _():
            dkpe_ref[...] = jnp.zeros_like(dkpe_ref)

        dq_ref[...] = jnp.zeros_like(dq_ref)
        _att_prep(kv_ref, kpe_ref, kf_scr, vf_scr, nb)

        def pre(j, _):
            rows = _att_rows(j)
            d = do_ref[rows, :]
            qb_scr[rows, :] = _bf(q_ref[rows, :])
            dob_scr[rows, :] = _bf(d)
            dsum[rows, :] = jnp.broadcast_to(jnp.sum(d * o_ref[rows, :], -1, keepdims=True), (ATT_T, LANES))
            return 0

        lax.fori_loop(0, nb, pre, 0)
        mask = _att_diag_mask()

        def front(kf, vf, qi):
            rows = _att_rows(qi)
            return _dot(qb_scr[rows, :], kf, 1, 1), _dot(dob_scr[rows, :], vf, 1, 1)

        def back(kf, qi, fr, carry, masked):
            s, dp = fr
            dk, dv = carry
            rows = _att_rows(qi)
            qb, dob = qb_scr[rows, :], dob_scr[rows, :]
            p = jnp.exp(s * scale - _wide(lse_ref[rows, :]))
            if masked:
                p = jnp.where(mask, p, 0.0)
            ds = _bf(p * (dp - _wide(dsum[rows, :])) * scale)
            return (dk + _dot(ds, qb, 0, 0), dv + _dot(_bf(p), dob, 0, 0)), _dot(ds, kf, 1, 0)

        def store(krows, carry):
            dk, dv = carry
            dkv_ref[krows, :] = jnp.where(lane < HEAD, dk, dv)
            dkpe_ref[krows, :] += jnp.where((lane >= HEAD) & (lane < MLA_QK), dk, 0.0)

        def k_pair(kp, _):
            ka, kb = 2 * kp, 2 * kp + 1
            rows_a, rows_b = _att_rows(ka), _att_rows(kb)
            kfa, vfa, kfb, vfb = kf_scr[rows_a, :], vf_scr[rows_a, :], kf_scr[rows_b, :], vf_scr[rows_b, :]
            zero = jnp.zeros((ATT_T, LANES), F32)
            ca, dq_a = back(kfa, ka, front(kfa, vfa, ka), (zero, zero), True)
            dq_ref[rows_a, :] += dq_a
            ca, dq_a = back(kfa, kb, front(kfa, vfa, kb), ca, False)
            cb, dq_b = back(kfb, kb, front(kfb, vfb, kb), (zero, zero), True)
            dq_ref[rows_b, :] += dq_a + dq_b

            def both(qi, c):
                ca, cb, fa, fb = c
                nxt = jnp.minimum(qi + 1, nb - 1)
                fa_next, fb_next = front(kfa, vfa, nxt), front(kfb, vfb, nxt)
                ca, dq_a = back(kfa, qi, fa, ca, False)
                cb, dq_b = back(kfb, qi, fb, cb, False)
                dq_ref[_att_rows(qi), :] += dq_a + dq_b
                return ca, cb, fa_next, fb_next

            first = jnp.minimum(kb + 1, nb - 1)
            ca, cb, _, _ = lax.fori_loop(kb + 1, nb, both, (ca, cb, front(kfa, vfa, first), front(kfb, vfb, first)))
            store(rows_a, ca)
            store(rows_b, cb)
            return 0

        lax.fori_loop(0, nb // 2, k_pair, 0)

    head = pl.BlockSpec((seq, LANES), lambda h: (0, h))
    shared = pl.BlockSpec((seq, LANES), lambda h: (0, 0))
    return pl.pallas_call(
        body, name="attn_bwd", grid=(HEADS,),
        in_specs=[head, head, shared, head, head, head],
        out_specs=[head, head, shared],
        out_shape=[jax.ShapeDtypeStruct((seq, HEADS * LANES), F32)] * 2
        + [jax.ShapeDtypeStruct((seq, LANES), F32)],
        scratch_shapes=[pltpu.VMEM((seq, LANES), BF16)] * 4 + [pltpu.VMEM((seq, LANES), F32)],
        compiler_params=_params(),
    )(q, kv, kpe, o, lse, do)


WKV_TB = 128
WKV_GROUP = SUBLANES
WKV_HALF = WIDTH // 2


def _wkv_consts():
    row = lax.broadcasted_iota(jnp.int32, (HEAD, WKV_HALF), 0)
    lane = lax.broadcasted_iota(jnp.int32, (HEAD, WKV_HALF), 1)
    diag = row == jnp.bitwise_and(lane, HEAD - 1)
    sub = lax.broadcasted_iota(jnp.int32, (WKV_GROUP, WKV_HALF), 0)
    return diag, sub


def _halves(x):
    return [x[:, :WKV_HALF], x[:, WKV_HALF:]]


def _diag_rows(row, diag):
    return _bf(jnp.where(diag, jnp.broadcast_to(row, diag.shape), 0.0))


def _put_row(tile, row, i, sub):
    return jnp.where(sub == i, jnp.broadcast_to(row, tile.shape), tile)


def _col_sum(x):
    return jnp.sum(x, axis=0, keepdims=True)


def _step(x, i):
    return x[i * HEAD:(i + 1) * HEAD]


def _expand_group(rows8, diag, ones_b):
    lhs = jnp.concatenate([_diag_rows(rows8[i:i + 1], diag) for i in range(WKV_GROUP)], axis=0)
    return _dot(lhs, ones_b, 1, 0)


def _head_dots(prods, ones_b, sub):
    tile = jnp.zeros((WKV_GROUP, WKV_HALF), F32)
    for i, p in enumerate(prods):
        tile = _put_row(tile, p, i, sub)
    res = _exact_dot(tile, ones_b, 0)
    return [res[i:i + 1] for i in range(len(prods))]


def _diag_group(x, diag, sub):
    out = jnp.zeros((WKV_GROUP, WKV_HALF), F32)
    for i in range(WKV_GROUP):
        out = _put_row(out, _col_sum(jnp.where(diag, _step(x, i), 0.0)), i, sub)
    return out


def wkv_fwd(r, w, k, v, a, b, ones_half):
    seq = r.shape[0]

    def body(r_ref, w_ref, k_ref, v_ref, a_ref, b_ref, ones_ref, y_ref, st_ref, s_scr):
        @pl.when(pl.program_id(0) == 0)
        def _():
            s_scr[...] = jnp.zeros_like(s_scr)

        ones_b = ones_ref[...]
        diag, sub = _wkv_consts()

        ng = WKV_TB // WKV_GROUP
        last = WKV_GROUP - 2

        def rows_of(g):
            return pl.ds(pl.multiple_of(g * WKV_GROUP, WKV_GROUP), WKV_GROUP)

        def pair_rows(x8, t):
            return jnp.concatenate([_diag_rows(x8[t:t + 1], diag), _diag_rows(x8[t + 1:t + 2], diag)], axis=0)

        def put_y(g, pairs_y):
            tile = _halves(y_ref[rows_of(g), :])
            for hf in range(2):
                tile[hf] = _put_row(_put_row(tile[hf], pairs_y[hf][0], last, sub), pairs_y[hf][1], last + 1, sub)
            y_ref[rows_of(g), :] = jnp.concatenate(tile, axis=1)

        def read_out(yexp):
            return _col_sum(jnp.where(diag, yexp[:HEAD], 0.0)), _col_sum(jnp.where(diag, yexp[HEAD:], 0.0))

        def group(g, carry):
            state, v_cur, read = (list(c) for c in carry)
            base = pl.multiple_of(g * WKV_GROUP, WKV_GROUP)
            rows = rows_of(g)
            r8, w8, k8, v8, a8, b8 = (_halves(ref[rows, :]) for ref in (r_ref, w_ref, k_ref, v_ref, a_ref, b_ref))
            v_after = _halves(v_ref[rows_of(jnp.minimum(g + 1, ng - 1)), :])
            evens = range(0, WKV_GROUP, 2)
            dots = [_head_dots([b8[hf][t:t + 1] * a8[hf][t + 1:t + 2] for t in evens]
                               + [k8[hf][t:t + 1] * a8[hf][t + 1:t + 2] for t in evens], ones_b, sub) for hf in range(2)]
            y8 = [jnp.zeros((WKV_GROUP, WKV_HALF), F32)] * 2
            y_before = [None, None]
            for t in evens:
                s0, s1 = slice(t, t + 1), slice(t + 1, t + 2)
                both = []
                for hf in range(2):
                    s_in = state[hf]
                    v_next = pair_rows(v8[hf], t + 2) if t < last else pair_rows(v_after[hf], 0)
                    res = _dot(jnp.concatenate([_bf(s_in * a8[hf][s0]), _bf(s_in * (w8[hf][s0] * a8[hf][s1])),
                                                v_next, read[hf]], axis=0), ones_b, 1, 0)
                    sa0, v0, v1 = res[:HEAD], v_cur[hf][:HEAD], v_cur[hf][HEAD:]
                    st0 = s_in * w8[hf][s0] + sa0 * b8[hf][s0] + v0 * k8[hf][s0]
                    sa1 = res[HEAD:2 * HEAD] + sa0 * dots[hf][t // 2] + v0 * dots[hf][WKV_GROUP // 2 + t // 2]
                    st1 = st0 * w8[hf][s1] + sa1 * b8[hf][s1] + v1 * k8[hf][s1]
                    both.append((st0, st1))
                    state[hf], v_cur[hf] = st1, res[2 * HEAD:4 * HEAD]
                    read[hf] = jnp.concatenate([_bf(st0 * r8[hf][s0]), _bf(st1 * r8[hf][s1])], axis=0)
                    ya, yb = read_out(res[4 * HEAD:])
                    if t == 0:
                        y_before[hf] = (ya, yb)
                    else:
                        y8[hf] = _put_row(_put_row(y8[hf], ya, t - 2, sub), yb, t - 1, sub)
                for j in range(2):
                    st_ref[base + t + j] = jnp.concatenate([both[0][j], both[1][j]], axis=1)
            y_ref[rows, :] = jnp.concatenate(y8, axis=1)
            put_y(jnp.maximum(g - 1, 0), y_before)
            return tuple(state), tuple(v_cur), tuple(read)

        v_first = _halves(v_ref[rows_of(0), :])
        init = (tuple(_halves(s_scr[...])),
                tuple(_dot(pair_rows(v_first[hf], 0), ones_b, 1, 0) for hf in range(2)),
                tuple(jnp.zeros((2 * HEAD, WKV_HALF), BF16) for _ in range(2)))
        fin, _, read = lax.fori_loop(0, ng, group, init)
        put_y(ng - 1, [read_out(_dot(read[hf], ones_b, 1, 0)) for hf in range(2)])
        s_scr[...] = jnp.concatenate(fin, axis=1)

    vec = pl.BlockSpec((WKV_TB, WIDTH), lambda i: (i, 0))
    return pl.pallas_call(
        body, name="wkv_fwd", grid=(seq // WKV_TB,),
        in_specs=[vec] * 6 + [_full_spec(ones_half)],
        out_specs=[vec, pl.BlockSpec((WKV_TB, HEAD, WIDTH), lambda i: (i, 0, 0))],
        out_shape=[jax.ShapeDtypeStruct((seq, WIDTH), F32), jax.ShapeDtypeStruct((seq, HEAD, WIDTH), F32)],
        scratch_shapes=[pltpu.VMEM((HEAD, WIDTH), F32)],
        compiler_params=_params(),
    )(r, w, k, v, a, b, ones_half)


def wkv_bwd(r, w, k, v, a, b, dy, states, ones_half):
    seq = r.shape[0]
    nb = seq // WKV_TB
    ng = WKV_TB // WKV_GROUP

    def body(r_ref, w_ref, k_ref, v_ref, a_ref, b_ref, dy_ref, st_ref, halo_ref, ones_ref,
             dr_ref, dw_ref, dk_ref, dv_ref, da_ref, db_ref, ds_scr):
        blk = nb - 1 - pl.program_id(0)

        @pl.when(pl.program_id(0) == 0)
        def _():
            ds_scr[...] = jnp.zeros_like(ds_scr)

        ones_b = ones_ref[...]
        diag, sub = _wkv_consts()
        before_block = jnp.where(blk == 0, 0.0, halo_ref[0])

        def rows_of(g):
            return pl.ds(pl.multiple_of(g * WKV_GROUP, WKV_GROUP), WKV_GROUP)

        def expand_rows(hf, dy8, v8, a8, t, s_t, s_u):
            s1, s0 = slice(t, t + 1), slice(t - 1, t)
            return jnp.concatenate([_diag_rows(dy8[hf][s1], diag), _diag_rows(dy8[hf][s0], diag),
                                    _diag_rows(v8[hf][s1], diag), _diag_rows(v8[hf][s0], diag),
                                    _bf(s_t[hf] * a8[hf][s1]), _bf(s_u[hf] * a8[hf][s0])], axis=0)

        def read_out(x):
            return _col_sum(jnp.where(diag, x[:HEAD], 0.0)), _col_sum(jnp.where(diag, x[HEAD:], 0.0))

        def put_dv(g, pair_dv):
            tile = _halves(dv_ref[rows_of(g), :])
            for hf in range(2):
                tile[hf] = _put_row(_put_row(tile[hf], pair_dv[hf][0], 1, sub), pair_dv[hf][1], 0, sub)
            dv_ref[rows_of(g), :] = jnp.concatenate(tile, axis=1)

        def group(gg, carry):
            dstate, e_cur, dv_pend = (list(c) for c in carry)
            g = ng - 1 - gg
            base = pl.multiple_of(g * WKV_GROUP, WKV_GROUP)
            rows = rows_of(g)
            r8, w8, k8, v8, a8, b8, dy8 = (
                _halves(ref[rows, :]) for ref in (r_ref, w_ref, k_ref, v_ref, a_ref, b_ref, dy_ref))
            g_next = jnp.maximum(g - 1, 0)
            base_next = pl.multiple_of(g_next * WKV_GROUP, WKV_GROUP)
            dy8n, v8n, a8n = (_halves(ref[rows_of(g_next), :]) for ref in (dy_ref, v_ref, a_ref))
            zero8 = jnp.zeros((WKV_GROUP, WKV_HALF), F32)
            out = {n: [zero8, zero8] for n in ("dr", "dw", "dk", "dv", "da", "db")}
            before_group = jnp.where(g == 0, before_block, st_ref[jnp.maximum(base - 1, 0)])
            states = [_halves(before_group)] + [_halves(st_ref[base + i]) for i in range(WKV_GROUP)]
            odds = range(1, WKV_GROUP, 2)
            dots = [_head_dots([a8[hf][t:t + 1] * b8[hf][t - 1:t] for t in odds]
                               + [r8[hf][t - 1:t] * b8[hf][t - 1:t] for t in odds], ones_b, sub) for hf in range(2)]
            dv_after = [None, None]

            def emit(hf, i, d_i, dsa_i, dy_i, v_i, sa_i):
                s_p, s_t = states[i][hf], states[i + 1][hf]
                for n, val in (("dr", _col_sum(s_t * dy_i)), ("dw", _col_sum(d_i * s_p)), ("db", _col_sum(d_i * sa_i)),
                               ("da", _col_sum(s_p * dsa_i)), ("dk", _col_sum(d_i * v_i))):
                    out[n][hf] = _put_row(out[n][hf], val, i, sub)

            for t in reversed(odds):
                s1, s0 = slice(t, t + 1), slice(t - 1, t)
                for hf in range(2):
                    dy1, dy0, v1, v0, sa1, sa0 = (_step(e_cur[hf], j) for j in range(6))
                    d1 = dstate[hf] + dy1 * r8[hf][s1]
                    if t > 1:
                        nxt = expand_rows(hf, dy8, v8, a8, t - 2, states[t - 2], states[t - 3])
                    else:
                        nxt = expand_rows(hf, dy8n, v8n, a8n, WKV_GROUP - 1, _halves(st_ref[base_next + WKV_GROUP - 2]),
                                          _halves(st_ref[base_next + WKV_GROUP - 3]))
                    res = _dot(jnp.concatenate([_bf(d1 * b8[hf][s1]), _bf(d1 * (w8[hf][s1] * b8[hf][s0])),
                                                nxt, dv_pend[hf]], axis=0), ones_b, 1, 0)
                    dsa1 = res[:HEAD]
                    d0 = d1 * w8[hf][s1] + dsa1 * a8[hf][s1] + dy0 * r8[hf][s0]
                    dsa0 = res[HEAD:2 * HEAD] + dsa1 * dots[hf][t // 2] + dy0 * dots[hf][WKV_GROUP // 2 + t // 2]
                    dstate[hf] = d0 * w8[hf][s0] + dsa0 * a8[hf][s0]
                    e_cur[hf] = res[2 * HEAD:8 * HEAD]
                    dv_pend[hf] = jnp.concatenate([_bf(d1 * k8[hf][s1]), _bf(d0 * k8[hf][s0])], axis=0)
                    emit(hf, t, d1, dsa1, dy1, v1, sa1)
                    emit(hf, t - 1, d0, dsa0, dy0, v0, sa0)
                    dv_a, dv_b = read_out(res[8 * HEAD:])
                    if t == WKV_GROUP - 1:
                        dv_after[hf] = (dv_a, dv_b)
                    else:
                        out["dv"][hf] = _put_row(_put_row(out["dv"][hf], dv_a, t + 2, sub), dv_b, t + 1, sub)
            for ref, n in ((dr_ref, "dr"), (dw_ref, "dw"), (dk_ref, "dk"), (dv_ref, "dv"), (da_ref, "da"), (db_ref, "db")):
                ref[rows, :] = jnp.concatenate(out[n], axis=1)
            put_dv(jnp.minimum(g + 1, ng - 1), dv_after)
            return tuple(dstate), tuple(e_cur), tuple(dv_pend)

        top = rows_of(ng - 1)
        dy8t, v8t, a8t = (_halves(ref[top, :]) for ref in (dy_ref, v_ref, a_ref))
        s_t, s_u = _halves(st_ref[WKV_TB - 2]), _halves(st_ref[WKV_TB - 3])
        init = (tuple(_halves(ds_scr[...])),
                tuple(_dot(expand_rows(hf, dy8t, v8t, a8t, WKV_GROUP - 1, s_t, s_u), ones_b, 1, 0) for hf in range(2)),
                tuple(jnp.zeros((2 * HEAD, WKV_HALF), BF16) for _ in range(2)))
        fin, _, dv_pend = lax.fori_loop(0, ng, group, init)
        put_dv(0, [read_out(_dot(dv_pend[hf], ones_b, 1, 0)) for hf in range(2)])
        ds_scr[...] = jnp.concatenate(fin, axis=1)

    vec = pl.BlockSpec((WKV_TB, WIDTH), lambda i: (nb - 1 - i, 0))
    return pl.pallas_call(
        body, name="wkv_bwd", grid=(nb,),
        in_specs=[vec] * 7 + [
            pl.BlockSpec((WKV_TB, HEAD, WIDTH), lambda i: (nb - 1 - i, 0, 0)),
            pl.BlockSpec((1, HEAD, WIDTH), lambda i: (jnp.maximum((nb - 1 - i) * WKV_TB - 1, 0), 0, 0)),
            _full_spec(ones_half)],
        out_specs=[vec] * 6,
        out_shape=[jax.ShapeDtypeStruct((seq, WIDTH), F32)] * 6,
        scratch_shapes=[pltpu.VMEM((HEAD, WIDTH), F32)],
        compiler_params=_params(),
    )(r, w, k, v, a, b, dy, states, states, ones_half)


def ada_fwd(c8, b_ada, gathered):
    cols = 3 * D_MODEL // 4

    def body(c_ref, b_ref, w_ref, o_ref):
        @pl.when(pl.program_id(1) == 0)
        def _():
            o_ref[...] = jnp.broadcast_to(b_ref[...], o_ref.shape)

        o_ref[...] += mm(_silu(c_ref[...]), w_ref[0])

    return pl.pallas_call(
        body, name="ada_fwd", grid=(4, D_MODEL // PACK_ROWS),
        in_specs=[pl.BlockSpec((SUBLANES, PACK_ROWS), lambda s, i: (0, i)),
                  pl.BlockSpec((1, cols), lambda s, i: (0, s)),
                  pl.BlockSpec((1, PACK_ROWS, cols), lambda s, i: (2 * s, 0, i))],
        out_specs=pl.BlockSpec((SUBLANES, cols), lambda s, i: (0, s)),
        out_shape=jax.ShapeDtypeStruct((SUBLANES, 3 * D_MODEL), F32),
        compiler_params=_params(("arbitrary", "arbitrary")),
    )(c8, b_ada, gathered)


def ada_grad_shard(sc_cols, dada_rows):
    n = len(sc_cols)

    def body(*refs):
        d_ref, o_ref = refs[n], refs[n + 1]
        acc = refs[0][...] * d_ref[0:1, :]
        for b in range(1, n):
            acc = acc + refs[b][...] * d_ref[b:b + 1, :]
        o_ref[...] = acc

    return pl.pallas_call(
        body, name="ada_grad_shard",
        out_shape=jax.ShapeDtypeStruct((sc_cols[0].shape[0], dada_rows.shape[1]), F32),
        compiler_params=pltpu.CompilerParams(vmem_limit_bytes=VMEM_LIMIT),
    )(*sc_cols, dada_rows)


def sum_slots(buf, tr):
    n, rows, cols = buf.shape

    def body(b_ref, o_ref):
        acc = b_ref[0].astype(F32)
        for s in range(1, n):
            acc = acc + b_ref[s].astype(F32)
        o_ref[...] = acc

    return pl.pallas_call(
        body, name="sum_slots", grid=(rows // tr,),
        in_specs=[pl.BlockSpec((n, tr, cols), lambda i: (0, i, 0))],
        out_specs=pl.BlockSpec((tr, cols), lambda i: (i, 0)),
        out_shape=jax.ShapeDtypeStruct((rows, cols), F32),
        compiler_params=_params(),
    )(buf)


def adamw_small(gathered, w, m, v):
    n = gathered.shape[0]

    def body(g_ref, w_ref, m_ref, v_ref, go_ref, d_ref, mo_ref, vo_ref):
        g = g_ref[0]
        for s in range(1, n):
            g = g + g_ref[s]
        go_ref[...] = g
        d_ref[...], mo_ref[...], vo_ref[...] = f_adamw(w_ref[...], g, m_ref[...], v_ref[...])

    return pl.pallas_call(
        body, name="adamw_small",
        out_shape=[jax.ShapeDtypeStruct(w.shape, F32)] * 4,
        compiler_params=pltpu.CompilerParams(vmem_limit_bytes=VMEM_LIMIT),
    )(gathered, w, m, v)


def _coords():
    return lax.axis_index("x"), lax.axis_index("y"), lax.axis_index("c")


def _flip(v, bit):
    return 1 - v if bit else v


def _hbm_call(body, name, out_shape, n_sems, *args):
    any_spec = pl.BlockSpec(memory_space=pl.ANY)
    return pl.pallas_call(
        body, name=name, out_shape=out_shape,
        in_specs=[any_spec] * len(args), out_specs=any_spec,
        scratch_shapes=[pltpu.SemaphoreType.DMA((n_sems,)), pltpu.SemaphoreType.DMA((n_sems,)),
                        pltpu.SemaphoreType.DMA],
    )(*args)


def all_gather8(name, block):
    def body(x_ref, out_ref, send_sems, recv_sems, local_sem):
        x, y, c = _coords()
        me, sibling = (x, y, c), (x, y, 1 - c)
        chips = [(1 - x, y), (x, 1 - y), (1 - x, 1 - y)]

        def slot(px, py, pc):
            return out_ref.at[4 * px + 2 * py + pc]

        def copy(k, blk, to, src=None):
            return pltpu.make_async_remote_copy(
                src_ref=slot(*blk) if src is None else src, dst_ref=slot(*blk),
                send_sem=send_sems.at[k], recv_sem=recv_sems.at[k], device_id=to, device_id_type=MESH)

        mine = pltpu.make_async_copy(x_ref, slot(*me), local_sem)
        mine.start()
        first = [copy(0, me, sibling, src=x_ref)]
        first += [copy(1 + j, me, (*chip, c), src=x_ref) for j, chip in enumerate(chips)]
        for cp in first:
            cp.start()
        passed = [copy(4 + j, (*chip, c), sibling) for j, chip in enumerate(chips)]
        for j, chip in enumerate(chips):
            copy(1 + j, (*chip, c), me).wait_recv()
            passed[j].start()
        copy(0, sibling, me).wait_recv()
        for j, chip in enumerate(chips):
            copy(4 + j, (*chip, 1 - c), me).wait_recv()
        for cp in first + passed:
            cp.wait_send()
        mine.wait()

    return _hbm_call(body, name, jax.ShapeDtypeStruct((N_DEV,) + block.shape, block.dtype), 7, block)


def pair_swap(name, block):
    def body(x_ref, out_ref, send_sems, recv_sems, local_sem):
        x, y, c = _coords()
        cp = pltpu.make_async_remote_copy(
            src_ref=x_ref, dst_ref=out_ref, send_sem=send_sems.at[0], recv_sem=recv_sems.at[0],
            device_id=(x, y, 1 - c), device_id_type=MESH)
        cp.start()
        cp.wait_recv()
        cp.wait_send()

    return _hbm_call(body, name, jax.ShapeDtypeStruct(block.shape, block.dtype), 1, block)


def chip_all_to_all(name, buf):
    def body(x_ref, out_ref, send_sems, recv_sems, local_sem):
        x, y, c = _coords()
        me = 2 * x + y
        mine = pltpu.make_async_copy(x_ref.at[me], out_ref.at[me], local_sem)
        mine.start()
        copies = []
        for k in range(1, 4):
            px, py = _flip(x, k & 2), _flip(y, k & 1)
            copies.append(pltpu.make_async_remote_copy(
                src_ref=x_ref.at[2 * px + py], dst_ref=out_ref.at[me],
                send_sem=send_sems.at[k - 1], recv_sem=recv_sems.at[k - 1],
                device_id=(px, py, c), device_id_type=MESH))
        for cp in copies:
            cp.start()
        for cp in copies:
            cp.wait_recv()
        for cp in copies:
            cp.wait_send()
        mine.wait()

    return _hbm_call(body, name, jax.ShapeDtypeStruct(buf.shape, buf.dtype), 3, buf)


def sibling_gather(name, block):
    def body(x_ref, out_ref, send_sems, recv_sems, local_sem):
        x, y, c = _coords()
        mine = pltpu.make_async_copy(x_ref, out_ref.at[c], local_sem)
        mine.start()
        cp = pltpu.make_async_remote_copy(
            src_ref=x_ref, dst_ref=out_ref.at[c], send_sem=send_sems.at[0], recv_sem=recv_sems.at[0],
            device_id=(x, y, 1 - c), device_id_type=MESH)
        cp.start()
        cp.wait_recv()
        cp.wait_send()
        mine.wait()

    return _hbm_call(body, name, jax.ShapeDtypeStruct((2,) + block.shape, block.dtype), 1, block)


def _col_blocks(a, cols):
    a = jnp.pad(a, ((0, 0), (0, cols - a.shape[1])))
    return [a[i * PACK_ROWS:(i + 1) * PACK_ROWS] for i in range(a.shape[0] // PACK_ROWS)]


def _pack_shard(sh, dtype, with_ada=True):
    lora = jnp.concatenate([sh['w_decay_up'], sh['w_iclr_up']], axis=1)
    misc = jnp.concatenate([sh['w_ukv'], lora, jnp.zeros((LORA, 2 * LANES), lora.dtype)], axis=0)
    blocks = ((_col_blocks(sh['w_ada'], 768) if with_ada else [])
              + _col_blocks(sh['w_in'], 1408) + _col_blocks(sh['w_proj_a'], 256)
              + _col_blocks(sh['w_proj_b'], 256) + [sh['w_out']] + _col_blocks(sh['w_uq'], 256) + [misc])
    return jnp.concatenate([b.astype(dtype) for b in blocks], axis=1)


def _unpack_shard(p, with_ada=True):
    o = [0]

    def take(n_blocks, cols, used):
        blocks = [p[:, o[0] + i * cols:o[0] + (i + 1) * cols] for i in range(n_blocks)]
        o[0] += n_blocks * cols
        return jnp.concatenate(blocks, axis=0)[:, :used]

    out = {'w_ada': take(4, 768, 768)} if with_ada else {}
    out.update({'w_in': take(4, 1408, 1288), 'w_proj_a': take(2, 256, 256),
                'w_proj_b': take(2, 256, 256), 'w_out': take(1, 1024, 1024), 'w_uq': take(1, 256, 192)})
    misc = take(1, 256, 256)
    out['w_ukv'] = misc[:2 * LORA]
    out['w_decay_up'] = misc[2 * LORA:3 * LORA, :LANES]
    out['w_iclr_up'] = misc[2 * LORA:3 * LORA, LANES:]
    return out


def _pack_small(parts):
    flat = jnp.concatenate([p.reshape(-1) for p in parts])
    return jnp.pad(flat, (0, SMALL_ROWS * LANES - flat.shape[0])).reshape(SMALL_ROWS, LANES)


def _unpack_small(packed):
    flat, out, o = packed.reshape(-1), {}, 0
    for name, n in _SMALL:
        out[name] = flat[o:o + n]
        o += n
    return out


def _pad_heads_cols(w, used, left):
    k = w.shape[0]
    return jnp.pad(w.reshape(k, HEADS, used), ((0, 0), (0, 0), (left, LANES - used - left))).reshape(k, HEADS * LANES)


def _unpad_heads_cols(w, used, left):
    k = w.shape[0]
    return w.reshape(k, HEADS, LANES)[:, :, left:left + used].reshape(k, HEADS * used)


def kernel(x, c, positions, w_ada, b_ada, w_in, q_norm_g, w_uq, kv_norm_g, w_ukv, mu_rwkv, w0, w_decay_up, a0, w_iclr_up, k_k, k_a, r_k, gn_g, gn_b, w_proj_a, w_proj_b, w_out, post_g, post_b, loss_target, m_w_ada, m_b_ada, m_w_in, m_q_norm_g, m_w_uq, m_kv_norm_g, m_w_ukv, m_mu_rwkv, m_w0, m_w_decay_up, m_a0, m_w_iclr_up, m_k_k, m_k_a, m_r_k, m_gn_g, m_gn_b, m_w_proj_a, m_w_proj_b, m_w_out, m_post_g, m_post_b, v_w_ada, v_b_ada, v_w_in, v_q_norm_g, v_w_uq, v_kv_norm_g, v_w_ukv, v_mu_rwkv, v_w0, v_w_decay_up, v_a0, v_w_iclr_up, v_k_k, v_k_a, v_r_k, v_gn_g, v_gn_b, v_w_proj_a, v_w_proj_b, v_w_out, v_post_g, v_post_b):
    given = dict(locals())
    seq = x.shape[1]
    my_c = lax.axis_index("c")

    shard_names = [n for n, _, _ in _SHARDED]
    w_pack = _pack_shard({n: given[n][0] for n in shard_names}, BF16)
    my_half = lax.dynamic_slice_in_dim(w_pack, my_c * HALF_COLS, HALF_COLS, 1)
    gathered = all_gather8("gather_weights", my_half)
    shards = [_unpack_shard(jnp.concatenate([gathered[2 * s], gathered[2 * s + 1]], axis=1)) for s in range(4)]
    full = {n: jnp.concatenate([sh[n] for sh in shards], axis=ax) for n, _, ax in _SHARDED}

    wi = full['w_in']
    zcol = lambda n: jnp.zeros((D_MODEL, n), BF16)
    w_g1 = jnp.concatenate([wi[:, :384], zcol(HEAD), wi[:, 384:416], zcol(LANES - MLA_QK),
                            _pad_heads_cols(wi[:, 416:928], HEAD, HEAD)], axis=1)
    w_g2 = wi[:, 928:3104]
    w_g3 = wi[:, 3104:5152]
    w_uq_p = _pad_heads_cols(full['w_uq'], MLA_QK, 0)
    w_pa_p = jnp.pad(full['w_proj_a'].reshape(HEADS, HEAD, D_MODEL), ((0, 0), (HEAD, 0), (0, 0))).reshape(HEADS * LANES, D_MODEL)
    zl = jnp.zeros((LORA, WIDTH), BF16)
    w_lora = jnp.concatenate([jnp.concatenate([full['w_decay_up'], zl], 1),
                              jnp.concatenate([zl, full['w_iclr_up']], 1)], 0)

    hd = np.arange(WIDTH) // HEAD
    ones_blocks = jnp.asarray(hd[:, None] == hd[None, :], BF16)
    ones_half = ones_blocks[:WKV_HALF, :WKV_HALF]
    perm_np = np.zeros((LANES, LANES), np.float32)
    for d in range(MLA_ROPE // 2):
        perm_np[HEAD + 16 + d, HEAD + d] = -1.0
        perm_np[HEAD + d, HEAD + 16 + d] = 1.0
    perm = jnp.asarray(perm_np, BF16)
    inv = ROPE_THETA ** (-jnp.arange(0, MLA_ROPE, 2, dtype=F32) / MLA_ROPE)
    ang = positions[0].astype(F32)[:, None] * inv
    cos_a, sin_a = jnp.cos(ang), jnp.sin(ang)
    cs = jnp.concatenate([jnp.ones((seq, HEAD), F32), cos_a, cos_a, jnp.zeros((seq, LANES - MLA_QK), F32),
                          jnp.zeros((seq, HEAD), F32), sin_a, sin_a, jnp.zeros((seq, LANES - MLA_QK), F32)], axis=1)

    x2, tgt = x[0], loss_target[0]
    r_k2 = r_k.reshape(1, WIDTH)

    c8 = jnp.broadcast_to(c, (SUBLANES, D_MODEL))
    ada = ada_fwd(c8, b_ada, gathered)[:1]
    shift, scale, gate = ada[:, :D_MODEL], ada[:, D_MODEL:2 * D_MODEL], ada[:, 2 * D_MODEL:]

    f_in1, f_in2, f_in3 = _make_f_in((512, 1024)), _make_f_in((SHIFT_W, WIDTH)), _make_f_in((1024, 1024))
    tr = min(256, seq)
    p_mla, gpa = row_fwd("in1_fwd", f_in1, [x2], [shift, scale, w_g1], [], [512, 1024], tr)
    p_rwkv, gpb = row_fwd("in2_fwd", f_in2, [x2], [shift, scale, w_g2], [], [SHIFT_W, WIDTH], tr)
    ma, mb = row_fwd("in3_fwd", f_in3, [x2], [shift, scale, w_g3], [], [1024, 1024], tr)

    mla_par = [q_norm_g, kv_norm_g, w_uq_p, full['w_ukv']]
    q_f, kv_f, kpe = row_fwd("mla_pre_fwd", f_mla_pre, [p_mla, cs], mla_par, [perm], [1024, 1024, LANES], tr)
    ya, lse = attn_fwd(q_f, kv_f, kpe)

    u = shift_fwd(p_rwkv, mu_rwkv, tr)
    pre_par = [w0, a0, k_k, k_a, w_lora]
    rr, wd, k2, vv, an, bb = row_fwd("rwkv_pre_fwd", f_rwkv_pre, [u], pre_par, [ones_blocks], [WIDTH] * 6, tr)
    y_wkv, states = wkv_fwd(rr, wd, k2, vv, an, bb, ones_half)
    post_b_par = [r_k2, gn_g, gn_b]
    yb, = row_fwd("rwkv_post_fwd", f_rwkv_post, [y_wkv, rr, k2, vv], post_b_par, [ones_blocks], [WIDTH], tr)

    out_rows = [ya, gpa, yb, gpb, ma, mb, x2, tgt]
    out_par = [gate, post_g, post_b, w_pa_p, full['w_proj_b'], full['w_out']]
    lrows, = row_fwd("post_fwd", f_post, out_rows, out_par, [], [LANES], tr)
    loss = lax.psum(jnp.sum(lrows[:, 0]), ("x", "y", "c"))

    trb = min(128, seq)
    dl = jnp.broadcast_to((jnp.arange(LANES) == 0).astype(F32), (seq, LANES))
    (dya, dgpa, dyb, dgpb, dma, dmb, dx_res), (dgate, dpost_g, dpost_b, dw_pa_p, dw_pb, dw_out) = row_bwd(
        "post_bwd", f_post, out_rows, 7, out_par, [], [dl], trb)

    (dy_wkv, dr1, dk1, dv1), (dr_k, dgn_g, dgn_b) = row_bwd(
        "rwkv_post_bwd", f_rwkv_post, [y_wkv, rr, k2, vv], 4, post_b_par, [ones_blocks], [dyb], tr)
    dr2, dwd, dk2, dv2, dan, dbb = wkv_bwd(rr, wd, k2, vv, an, bb, dy_wkv, states, ones_half)
    (du,), (dw0, da0, dk_k, dk_a, dw_lora) = row_bwd(
        "rwkv_pre_bwd", f_rwkv_pre, [u], 1, pre_par, [ones_blocks],
        [(dr1, dr2), dwd, (dk1, dk2), (dv1, dv2), dan, dbb], tr)
    dp_rwkv, dmu = shift_bwd(du, p_rwkv, mu_rwkv, tr)

    dq_f, dkv_f, dkpe = attn_bwd(q_f, kv_f, kpe, ya, lse, dya)
    (dp_mla,), (dqg, dkvg, dw_uq_p, dw_ukv) = row_bwd(
        "mla_pre_bwd", f_mla_pre, [p_mla, cs], 1, mla_par, [perm], [dq_f, dkv_f, dkpe], tr)

    (dx1,), (dsh1, dsc1, dw_g1) = row_bwd("in1_bwd", f_in1, [x2], 1, [shift, scale, w_g1], [], [dp_mla, dgpa], tr, [dx_res])
    (dx2,), (dsh2, dsc2, dw_g2) = row_bwd("in2_bwd", f_in2, [x2], 1, [shift, scale, w_g2], [], [dp_rwkv, dgpb], tr, [dx1])
    (dx3,), (dsh3, dsc3, dw_g3) = row_bwd("in3_bwd", f_in3, [x2], 1, [shift, scale, w_g3], [], [dma, dmb], tr, [dx2])
    grad_x = dx3[None]

    dada = jnp.concatenate([dsh1 + dsh2 + dsh3, dsc1 + dsc2 + dsc3, dgate], axis=1)
    local = {
        'w_in': jnp.concatenate([dw_g1[:, :384], dw_g1[:, 448:480], _unpad_heads_cols(dw_g1[:, 512:], HEAD, HEAD),
                                 dw_g2, dw_g3], axis=1),
        'w_uq': _unpad_heads_cols(dw_uq_p, MLA_QK, 0),
        'w_ukv': dw_ukv,
        'w_decay_up': dw_lora[:LORA, :WIDTH],
        'w_iclr_up': dw_lora[LORA:, WIDTH:],
        'w_proj_a': dw_pa_p.reshape(HEADS, LANES, D_MODEL)[:, HEAD:].reshape(WIDTH, D_MODEL),
        'w_proj_b': dw_pb,
        'w_out': dw_out,
    }
    small_local = {'b_ada': dada, 'q_norm_g': dqg, 'kv_norm_g': dkvg, 'mu_rwkv': dmu, 'w0': dw0, 'a0': da0,
                   'k_k': dk_k, 'k_a': dk_a, 'r_k': dr_k, 'gn_g': dgn_g, 'gn_b': dgn_b,
                   'post_g': dpost_g, 'post_b': dpost_b}

    def shard_of(g, axis, s):
        n = g.shape[axis] // 4
        return lax.slice_in_dim(g, s * n, (s + 1) * n, axis=axis)

    packed = jnp.stack([_pack_shard({n: shard_of(local[n], ax, s) for n, _, ax in _SHARDED if n != 'w_ada'}, F32, False)
                        for s in range(4)])
    keep = lax.dynamic_slice_in_dim(packed, my_c * GRAD_HALF, GRAD_HALF, 2).reshape(4 * PACK_ROWS, GRAD_HALF)
    give = lax.dynamic_slice_in_dim(packed, (1 - my_c) * GRAD_HALF, GRAD_HALF, 2).reshape(4 * PACK_ROWS, GRAD_HALF)
    pair_sum, = row_fwd("pair_sum", lambda p, q: (p + q,), [keep, pair_swap("swap_halves", give)], [], [],
                        [GRAD_HALF], PACK_ROWS // 2, BF16)
    received = chip_all_to_all("exchange_grads", pair_sum.reshape(4, PACK_ROWS, GRAD_HALF))
    my_sum = sum_slots(received, PACK_ROWS // 2)
    halves = sibling_gather("gather_halves", my_sum)
    g_shard = _unpack_shard(jnp.concatenate([halves[0], halves[1]], axis=1), False)

    small_pack = lambda d, extra=(): _pack_small([d[n] for n, _ in _SMALL] + list(extra))
    small_all = all_gather8("gather_small", small_pack(small_local, [c * jax.nn.sigmoid(c)]))
    sc_all = small_all[:, SMALL_USED:SMALL_USED + D_MODEL // LANES].reshape(N_DEV, D_MODEL)
    dada_all = small_all[:, :3 * D_MODEL // LANES].reshape(N_DEV, 3 * D_MODEL)
    my_cols = lax.dynamic_slice_in_dim(dada_all, (2 * lax.axis_index("x") + lax.axis_index("y")) * 768, 768, 1)
    g_shard['w_ada'] = ada_grad_shard([sc_all[b].reshape(D_MODEL, 1) for b in range(N_DEV)], my_cols)

    big = [{}, {}, {}, {}]
    for n in shard_names:
        w2, m2, v2 = given[n][0], given['m_' + n][0], given['v_' + n][0]
        cols = w2.shape[1]
        outs = row_fwd("adamw_" + n, f_adamw, [w2, g_shard[n], m2, v2], [], [], [cols] * 3, min(256, w2.shape[0]))
        for dst, val in zip(big, (g_shard[n], *outs)):
            dst[n] = val

    small_out = adamw_small(small_all, small_pack({n: given[n] for n, _ in _SMALL}),
                            small_pack({n: given['m_' + n] for n, _ in _SMALL}),
                            small_pack({n: given['v_' + n] for n, _ in _SMALL}))

    results = []
    for big_k, packed_small in zip(big, small_out):
        small = _unpack_small(packed_small)
        results.append([(big_k[n] if n in big_k else small[n]).reshape(given[n].shape) for n in _WEIGHTS])
    return (loss, grad_x, *results[0], *results[1], *results[2], *results[3])
```

```python
from typing import NamedTuple

import numpy as np
import jax
import jax.numpy as jnp
from jax import lax
from jax.experimental import pallas as pl
from jax.experimental.pallas import tpu as pltpu

F32 = jnp.float32
BF16 = jnp.bfloat16

D_MODEL = 1024
LN_EPS = 1e-5
RMS_EPS = 1e-6
GN_EPS = 64e-5
HEADS = 8
HEAD = 64
MLA_ROPE = 32
MLA_QK = HEAD + MLA_ROPE
ROPE_THETA = 10000.0
WIDTH = HEADS * HEAD
LORA = 64
SHIFT_W = 3 * WIDTH + 2 * LORA
CHUNK = 64
ALPHA = 2.0 ** 0.25

ADAM_LR, ADAM_B1, ADAM_B2, ADAM_EPS, ADAM_WD, ADAM_STEP = 0.001, 0.9, 0.999, 1e-08, 0.01, 10

LANES = 128
SUBLANES = 8
VMEM_LIMIT = 56 * 1024 * 1024
N_DEV = 8
MESH = pl.DeviceIdType.MESH
NEG = -1e30

_WEIGHTS = ['w_ada', 'b_ada', 'w_in', 'q_norm_g', 'w_uq', 'kv_norm_g', 'w_ukv', 'mu_rwkv', 'w0',
            'w_decay_up', 'a0', 'w_iclr_up', 'k_k', 'k_a', 'r_k', 'gn_g', 'gn_b', 'w_proj_a',
            'w_proj_b', 'w_out', 'post_g', 'post_b']
_SHARDED = [('w_ada', (1024, 3072), 1), ('w_in', (1024, 5152), 1), ('w_uq', (256, 768), 1),
            ('w_ukv', (128, 1024), 1), ('w_decay_up', (64, 512), 1), ('w_iclr_up', (64, 512), 1),
            ('w_proj_a', (512, 1024), 1), ('w_proj_b', (512, 1024), 1), ('w_out', (1024, 1024), 0)]
_SMALL = [('b_ada', 3072), ('q_norm_g', 256), ('kv_norm_g', 128), ('mu_rwkv', 1664), ('w0', 512),
          ('a0', 512), ('k_k', 512), ('k_a', 512), ('r_k', 512), ('gn_g', 512), ('gn_b', 512),
          ('post_g', 1024), ('post_b', 1024)]
PACK_ROWS = 256
PACK_COLS = 11264
HALF_COLS = PACK_COLS // 2
ADA_COLS = 4 * 768
GRAD_HALF = (PACK_COLS - ADA_COLS) // 2
SMALL_USED = 84
SMALL_ROWS = 96


def _bf(x):
    return x.astype(BF16)


def _dot(a, b, ca, cb):
    return lax.dot_general(a, b, (((ca,), (cb,)), ((), ())), preferred_element_type=F32)


class Weight(NamedTuple):
    value: jax.Array
    grad: jax.Array


@jax.custom_vjp
def _mm(a, w, w_grad):
    return _dot(_bf(a), _bf(w), 1, 0)


def _mm_fwd(a, w, w_grad):
    return _mm(a, w, w_grad), (a, w)


def _mm_bwd(res, g):
    a, w = res
    gb = _bf(g)
    return _dot(gb, _bf(w), 1, 1), jnp.zeros_like(w), _dot(_bf(a), gb, 0, 0)


_mm.defvjp(_mm_fwd, _mm_bwd)


def mm(a, w):
    if isinstance(w, Weight):
        return _mm(a, w.value, w.grad)
    return _dot(_bf(a), _bf(w), 1, 0)


def _split3(x):
    hi = _bf(x)
    r1 = x - hi.astype(F32)
    mid = _bf(r1)
    lo = _bf(r1 - mid.astype(F32))
    return hi, mid, lo


def _exact_dot(x, m, cm):
    hi, mid, lo = _split3(x)
    return _dot(hi, m, 1, cm) + _dot(mid, m, 1, cm) + _dot(lo, m, 1, cm)


@jax.custom_vjp
def segsum(x, ones_blocks):
    return _exact_dot(x, ones_blocks, 0)


def _segsum_fwd(x, ones_blocks):
    return segsum(x, ones_blocks), ones_blocks


def _segsum_bwd(ones_blocks, g):
    return _exact_dot(g, ones_blocks, 0), jnp.zeros_like(ones_blocks)


segsum.defvjp(_segsum_fwd, _segsum_bwd)


@jax.custom_vjp
def lane_perm(x, perm):
    return _exact_dot(x, perm, 0)


def _lane_perm_fwd(x, perm):
    return lane_perm(x, perm), perm


def _lane_perm_bwd(perm, g):
    return _exact_dot(g, perm, 1), jnp.zeros_like(perm)


lane_perm.defvjp(_lane_perm_fwd, _lane_perm_bwd)


def _silu(z):
    return z * jax.nn.sigmoid(z)


def _softplus(z):
    return jnp.maximum(z, 0.0) + jnp.log(1.0 + jnp.exp(-jnp.abs(z)))


def _layer_norm(x):
    xc = x - jnp.mean(x, -1, keepdims=True)
    return xc * lax.rsqrt(jnp.mean(xc * xc, -1, keepdims=True) + LN_EPS)


def _rope(t, cos_t, sin_t, perm):
    outs = []
    for h in range(t.shape[1] // LANES):
        th = t[:, h * LANES:(h + 1) * LANES]
        outs.append(th * cos_t + lane_perm(th, perm) * sin_t)
    return outs[0] if len(outs) == 1 else jnp.concatenate(outs, axis=1)


def _make_f_in(splits):
    def f_in(x, shift, scale, w):
        h = _layer_norm(x) * (1.0 + scale) + shift
        p = mm(h, w)
        outs, o = [], 0
        for s in splits:
            outs.append(p[:, o:o + s])
            o += s
        return tuple(outs)
    return f_in


def f_mla_pre(p, cs, qg, kvg, w_uq, w_ukv, perm):
    q_c, kv_c, k_r = p[:, :256], p[:, 256:384], p[:, 384:512]
    cos_t, sin_t = cs[:, :LANES], cs[:, LANES:]
    qn = q_c * lax.rsqrt(jnp.mean(q_c * q_c, -1, keepdims=True) + RMS_EPS) * qg
    kvn = kv_c * lax.rsqrt(jnp.mean(kv_c * kv_c, -1, keepdims=True) + RMS_EPS) * kvg
    q = _rope(mm(qn, w_uq), cos_t, sin_t, perm)
    kv = mm(kvn, w_ukv)
    return q, kv, _rope(k_r, cos_t, sin_t, perm)


def f_rwkv_pre(u, w0, a0, k_k, k_a, w_lora, ones_blocks):
    r, k, v, lo = u[:, :WIDTH], u[:, WIDTH:2 * WIDTH], u[:, 2 * WIDTH:3 * WIDTH], u[:, 3 * WIDTH:]
    lane = lax.broadcasted_iota(jnp.int32, lo.shape, 1)
    dl = mm(jnp.where(lane < LORA, jnp.tanh(lo), lo), w_lora)
    w_log = -_softplus(-(w0 + dl[:, :WIDTH])) - 0.5
    decay = jnp.exp(-jnp.exp(w_log))
    a = jax.nn.sigmoid(a0 + dl[:, WIDTH:])
    kk = k * k_k
    kk = kk / jnp.maximum(jnp.sqrt(segsum(kk * kk, ones_blocks)), 1e-12)
    k2 = k * (1.0 + (a - 1.0) * k_a)
    return r, decay, k2, v, -kk, kk * a


def f_rwkv_post(y, r, k2, v, r_k, gn_g, gn_b, ones_blocks):
    yc = y - segsum(y, ones_blocks) * (1.0 / HEAD)
    yn = yc * lax.rsqrt(segsum(yc * yc, ones_blocks) * (1.0 / HEAD) + GN_EPS)
    return (yn * gn_g + gn_b + segsum(r * k2 * r_k, ones_blocks) * v,)


def f_merge(ya, gpa, yb, gpb, ma, mb, w_pa, w_pb):
    pa = mm(ya * _silu(gpa), w_pa)
    pb = mm(yb * _silu(gpb), w_pb)
    return (jax.nn.sigmoid(ma) * pa + jax.nn.sigmoid(mb) * pb,)


def f_loss(merged, x, tgt, gate, post_g, post_b, w_out):
    z = ALPHA * x + (1.0 + gate) * mm(merged, w_out)
    err = _layer_norm(z) * post_g + post_b - tgt
    lrow = 0.5 * jnp.mean(err * err, -1, keepdims=True)
    return (jnp.broadcast_to(lrow, (lrow.shape[0], LANES)),)


def f_adamw(w, g, m, v):
    m2 = ADAM_B1 * m + (1.0 - ADAM_B1) * g
    v2 = ADAM_B2 * v + (1.0 - ADAM_B2) * jnp.square(g)
    m_hat = m2 / (1.0 - ADAM_B1 ** ADAM_STEP)
    v_hat = v2 / (1.0 - ADAM_B2 ** ADAM_STEP)
    return -ADAM_LR * (m_hat / (jnp.sqrt(v_hat) + ADAM_EPS) + ADAM_WD * w), m2, v2


def _params(sem=("arbitrary",)):
    return pltpu.CompilerParams(dimension_semantics=sem, vmem_limit_bytes=VMEM_LIMIT)


def _row_spec(tr, a):
    return pl.BlockSpec((tr, a.shape[1]), lambda i: (i, 0))


def _full_spec(a):
    return pl.BlockSpec(a.shape, lambda i: (0,) * a.ndim)


def row_fwd(name, f, rows, params, consts, out_widths, tr, out_dtype=F32):
    n_rows = rows[0].shape[0]
    nr, npar, ncon = len(rows), len(params), len(consts)

    def body(*refs):
        rv = [r[...] for r in refs[:nr]]
        pv = [r[...] for r in refs[nr:nr + npar]]
        cv = [r[...] for r in refs[nr + npar:nr + npar + ncon]]
        outs = f(*rv, *pv, *cv)
        for o_ref, o in zip(refs[nr + npar + ncon:], outs):
            o_ref[...] = o.astype(o_ref.dtype)

    return pl.pallas_call(
        body, name=name, grid=(n_rows // tr,),
        in_specs=[_row_spec(tr, a) for a in rows] + [_full_spec(a) for a in list(params) + list(consts)],
        out_specs=[pl.BlockSpec((tr, w), lambda i: (i, 0)) for w in out_widths],
        out_shape=[jax.ShapeDtypeStruct((n_rows, w), out_dtype) for w in out_widths],
        compiler_params=_params(),
    )(*rows, *params, *consts)


def row_bwd(name, f, rows, n_diff, params, consts, douts, tr, add_rows=None):
    n_rows = rows[0].shape[0]
    douts = [d if isinstance(d, (tuple, list)) else (d,) for d in douts]
    counts = [len(d) for d in douts]
    flat_d = [a for d in douts for a in d]
    add_rows = add_rows or [None] * n_diff
    adds = [a for a in add_rows if a is not None]
    nr, npar, ncon, nd, na = len(rows), len(params), len(consts), len(flat_d), len(adds)

    def body(*refs):
        o = 0
        rv = [r[...] for r in refs[o:o + nr]]; o += nr
        pv = [Weight(r[...], jnp.zeros(r.shape, F32)) if r.dtype == BF16 else r[...] for r in refs[o:o + npar]]
        o += npar
        cv = [r[...] for r in refs[o:o + ncon]]; o += ncon
        dv = []
        for cnt in counts:
            s = refs[o][...]
            for e in range(1, cnt):
                s = s + refs[o + e][...]
            dv.append(s)
            o += cnt
        add_v = [r[...] for r in refs[o:o + na]]; o += na
        drow_refs = refs[o:o + n_diff]; o += n_diff
        dpar_refs = refs[o:o + npar]

        def g(*args):
            return tuple(f(*args[:n_diff], *rv[n_diff:], *args[n_diff:], *cv))

        _, vjp = jax.vjp(g, *rv[:n_diff], *pv)
        grads = vjp(tuple(dv))
        ai = 0
        for j, (r, gr) in enumerate(zip(drow_refs, grads[:n_diff])):
            if add_rows[j] is not None:
                gr = gr + add_v[ai]
                ai += 1
            r[...] = gr

        @pl.when(pl.program_id(0) == 0)
        def _():
            for r in dpar_refs:
                r[...] = jnp.zeros_like(r)

        for r, gr in zip(dpar_refs, grads[n_diff:]):
            r[...] += gr.grad if isinstance(gr, Weight) else gr

    outs = pl.pallas_call(
        body, name=name, grid=(n_rows // tr,),
        in_specs=([_row_spec(tr, a) for a in rows] + [_full_spec(a) for a in list(params) + list(consts)]
                  + [_row_spec(tr, a) for a in flat_d + adds]),
        out_specs=[_row_spec(tr, a) for a in rows[:n_diff]] + [_full_spec(a) for a in params],
        out_shape=([jax.ShapeDtypeStruct(a.shape, F32) for a in rows[:n_diff]]
                   + [jax.ShapeDtypeStruct(a.shape, F32) for a in params]),
        compiler_params=_params(),
    )(*rows, *params, *consts, *flat_d, *adds)
    return outs[:n_diff], outs[n_diff:]


def shift_fwd(p, mu, tr):
    n_rows, w = p.shape

    def body(p_ref, mu_ref, u_ref, carry):
        @pl.when(pl.program_id(0) == 0)
        def _():
            carry[...] = jnp.zeros_like(carry)

        x = p_ref[...]
        rolled = pltpu.roll(x, 1, 0)
        head = pltpu.roll(carry[...], 1, 0)
        fixed = jnp.concatenate([head, rolled[SUBLANES:]], axis=0)
        row = lax.broadcasted_iota(jnp.int32, x.shape, 0)
        prev = jnp.where(row == 0, fixed, rolled)
        u_ref[...] = x + (prev - x) * mu_ref[...]
        carry[...] = x[tr - SUBLANES:]

    return pl.pallas_call(
        body, name="shift_fwd", grid=(n_rows // tr,),
        in_specs=[_row_spec(tr, p), _full_spec(mu)],
        out_specs=_row_spec(tr, p),
        out_shape=jax.ShapeDtypeStruct(p.shape, F32),
        scratch_shapes=[pltpu.VMEM((SUBLANES, w), F32)],
        compiler_params=_params(),
    )(p, mu)


def shift_bwd(du, p, mu, tr):
    n_rows, w = p.shape
    nb = n_rows // tr

    def body(du_ref, p_ref, mu_ref, dp_ref, dmu_ref, carry):
        @pl.when(pl.program_id(0) == 0)
        def _():
            carry[...] = jnp.zeros_like(carry)
            dmu_ref[...] = jnp.zeros_like(dmu_ref)

        d = du_ref[...]
        rolled = pltpu.roll(d, tr - 1, 0)
        tail = pltpu.roll(carry[...], SUBLANES - 1, 0)
        fixed = jnp.concatenate([rolled[:tr - SUBLANES], tail], axis=0)
        row = lax.broadcasted_iota(jnp.int32, d.shape, 0)
        nxt = jnp.where(row == tr - 1, fixed, rolled)
        mu_v = mu_ref[...]
        dp_ref[...] = d * (1.0 - mu_v) + nxt * mu_v
        dmu_ref[...] += jnp.sum(p_ref[...] * (nxt - d), axis=0, keepdims=True)
        carry[...] = d[:SUBLANES]

    rev = lambda i: (nb - 1 - i, 0)
    return pl.pallas_call(
        body, name="shift_bwd", grid=(nb,),
        in_specs=[pl.BlockSpec((tr, w), rev), pl.BlockSpec((tr, w), rev), _full_spec(mu)],
        out_specs=[pl.BlockSpec((tr, w), rev), _full_spec(mu)],
        out_shape=[jax.ShapeDtypeStruct(p.shape, F32), jax.ShapeDtypeStruct(mu.shape, F32)],
        scratch_shapes=[pltpu.VMEM((SUBLANES, w), F32)],
        compiler_params=_params(),
    )(du, p, mu)


ATT_T = 256


def _att_rows(j):
    return pl.ds(pl.multiple_of(j * ATT_T, ATT_T), ATT_T)


def _att_prep(kv_ref, kpe_ref, kf_scr, vf_scr, n_blocks):
    lane = lax.broadcasted_iota(jnp.int32, (ATT_T, LANES), 1)

    def prep(j, _):
        rows = _att_rows(j)
        kv = kv_ref[rows, :]
        kf_scr[rows, :] = _bf(jnp.where(lane < HEAD, kv, kpe_ref[rows, :]))
        vf_scr[rows, :] = _bf(jnp.where(lane >= HEAD, kv, 0.0))
        return 0

    lax.fori_loop(0, n_blocks, prep, 0)


def _att_diag_mask():
    shift = CHUNK.bit_length() - 1
    qc = jnp.right_shift(lax.broadcasted_iota(jnp.int32, (ATT_T, ATT_T), 0), shift)
    kc = jnp.right_shift(lax.broadcasted_iota(jnp.int32, (ATT_T, ATT_T), 1), shift)
    return kc <= qc


def _wide(x):
    return jnp.concatenate([x] * (ATT_T // LANES), axis=1)


def attn_fwd(q, kv, kpe):
    seq = q.shape[0]
    nb = seq // ATT_T
    assert seq % (2 * ATT_T) == 0, "blocks are taken two per trip"
    scale = MLA_QK ** -0.5

    def body(q_ref, kv_ref, kpe_ref, o_ref, lse_ref, kf_scr, vf_scr):
        _att_prep(kv_ref, kpe_ref, kf_scr, vf_scr, nb)
        mask = _att_diag_mask()

        def scores(qb, kj):
            return _dot(qb, kf_scr[_att_rows(kj), :], 1, 1) * scale

        def update(s, kj, carry, masked):
            m, l, acc = carry
            if masked:
                s = jnp.where(mask, s, NEG)
            m_new = jnp.maximum(m, jnp.broadcast_to(jnp.max(s, -1, keepdims=True), m.shape))
            alpha = jnp.exp(m - m_new)
            p = jnp.exp(s - _wide(m_new))
            l = alpha * l + jnp.broadcast_to(jnp.sum(p, -1, keepdims=True), l.shape)
            acc = alpha * acc + _dot(_bf(p), vf_scr[_att_rows(kj), :], 1, 0)
            return m_new, l, acc

        def finish(rows, carry):
            m, l, acc = carry
            o_ref[rows, :] = acc / l
            lse_ref[rows, :] = m + jnp.log(l)

        def q_pair(qp, _):
            rows_a, rows_b = _att_rows(2 * qp), _att_rows(2 * qp + 1)
            qa, qb = _bf(q_ref[rows_a, :]), _bf(q_ref[rows_b, :])
            init = (jnp.full((ATT_T, LANES), NEG, F32), jnp.zeros((ATT_T, LANES), F32),
                    jnp.zeros((ATT_T, LANES), F32))

            def trip(kj, c):
                ca, cb, sa, sb = c
                sa_next, sb_next = scores(qa, kj + 1), scores(qb, kj + 1)
                return update(sa, kj, ca, False), update(sb, kj, cb, False), sa_next, sb_next

            ca, cb, sa, sb = lax.fori_loop(0, 2 * qp, trip, (init, init, scores(qa, 0), scores(qb, 0)))
            sb_last = scores(qb, 2 * qp + 1)
            ca = update(sa, 2 * qp, ca, True)
            cb = update(sb_last, 2 * qp + 1, update(sb, 2 * qp, cb, False), True)
            finish(rows_a, ca)
            finish(rows_b, cb)
            return 0

        lax.fori_loop(0, nb // 2, q_pair, 0)

    head = pl.BlockSpec((seq, LANES), lambda h: (0, h))
    return pl.pallas_call(
        body, name="attn_fwd", grid=(HEADS,),
        in_specs=[head, head, pl.BlockSpec((seq, LANES), lambda h: (0, 0))],
        out_specs=[head, head],
        out_shape=[jax.ShapeDtypeStruct((seq, HEADS * LANES), F32)] * 2,
        scratch_shapes=[pltpu.VMEM((seq, LANES), BF16)] * 2,
        compiler_params=_params(),
    )(q, kv, kpe)


def attn_bwd(q, kv, kpe, o, lse, do):
    seq = q.shape[0]
    nb = seq // ATT_T
    assert seq % (2 * ATT_T) == 0, "blocks are taken two per trip"
    scale = MLA_QK ** -0.5

    def body(q_ref, kv_ref, kpe_ref, o_ref, lse_ref, do_ref, dq_ref, dkv_ref, dkpe_ref,
             kf_scr, vf_scr, qb_scr, dob_scr, dsum):
        lane = lax.broadcasted_iota(jnp.int32, (ATT_T, LANES), 1)

        @pl.when(pl.program_id(0) == 0)
        def _():
            dkpe_ref[...] = jnp.zeros_like(dkpe_ref)

        dq_ref[...] = jnp.zeros_like(dq_ref)
        _att_prep(kv_ref, kpe_ref, kf_scr, vf_scr, nb)

        def pre(j, _):
            rows = _att_rows(j)
            d = do_ref[rows, :]
            qb_scr[rows, :] = _bf(q_ref[rows, :])
            dob_scr[rows, :] = _bf(d)
            dsum[rows, :] = jnp.broadcast_to(jnp.sum(d * o_ref[rows, :], -1, keepdims=True), (ATT_T, LANES))
            return 0

        lax.fori_loop(0, nb, pre, 0)
        mask = _att_diag_mask()

        def front(kf, vf, qi):
            rows = _att_rows(qi)
            return _dot(qb_scr[rows, :], kf, 1, 1), _dot(dob_scr[rows, :], vf, 1, 1)

        def back(kf, qi, fr, carry, masked):
            s, dp = fr
            dk, dv = carry
            rows = _att_rows(qi)
            qb, dob = qb_scr[rows, :], dob_scr[rows, :]
            p = jnp.exp(s * scale - _wide(lse_ref[rows, :]))
            if masked:
                p = jnp.where(mask, p, 0.0)
            ds = _bf(p * (dp - _wide(dsum[rows, :])) * scale)
            return (dk + _dot(ds, qb, 0, 0), dv + _dot(_bf(p), dob, 0, 0)), _dot(ds, kf, 1, 0)

        def store(krows, carry):
            dk, dv = carry
            dkv_ref[krows, :] = jnp.where(lane < HEAD, dk, dv)
            dkpe_ref[krows, :] += jnp.where((lane >= HEAD) & (lane < MLA_QK), dk, 0.0)

        def k_pair(kp, _):
            ka, kb = 2 * kp, 2 * kp + 1
            rows_a, rows_b = _att_rows(ka), _att_rows(kb)
            kfa, vfa, kfb, vfb = kf_scr[rows_a, :], vf_scr[rows_a, :], kf_scr[rows_b, :], vf_scr[rows_b, :]
            zero = jnp.zeros((ATT_T, LANES), F32)
            ca, dq_a = back(kfa, ka, front(kfa, vfa, ka), (zero, zero), True)
            dq_ref[rows_a, :] += dq_a
            ca, dq_a = back(kfa, kb, front(kfa, vfa, kb), ca, False)
            cb, dq_b = back(kfb, kb, front(kfb, vfb, kb), (zero, zero), True)
            dq_ref[rows_b, :] += dq_a + dq_b

            def both(qi, c):
                ca, cb, fa, fb = c
                nxt = jnp.minimum(qi + 1, nb - 1)
                fa_next, fb_next = front(kfa, vfa, nxt), front(kfb, vfb, nxt)
                ca, dq_a = back(kfa, qi, fa, ca, False)
                cb, dq_b = back(kfb, qi, fb, cb, False)
                dq_ref[_att_rows(qi), :] += dq_a + dq_b
                return ca, cb, fa_next, fb_next

            first = jnp.minimum(kb + 1, nb - 1)
            ca, cb, _, _ = lax.fori_loop(kb + 1, nb, both, (ca, cb, front(kfa, vfa, first), front(kfb, vfb, first)))
            store(rows_a, ca)
            store(rows_b, cb)
            return 0

        lax.fori_loop(0, nb // 2, k_pair, 0)

    head = pl.BlockSpec((seq, LANES), lambda h: (0, h))
    shared = pl.BlockSpec((seq, LANES), lambda h: (0, 0))
    return pl.pallas_call(
        body, name="attn_bwd", grid=(HEADS,),
        in_specs=[head, head, shared, head, head, head],
        out_specs=[head, head, shared],
        out_shape=[jax.ShapeDtypeStruct((seq, HEADS * LANES), F32)] * 2
        + [jax.ShapeDtypeStruct((seq, LANES), F32)],
        scratch_shapes=[pltpu.VMEM((seq, LANES), BF16)] * 4 + [pltpu.VMEM((seq, LANES), F32)],
        compiler_params=_params(),
    )(q, kv, kpe, o, lse, do)


WKV_TB = 128
WKV_GROUP = SUBLANES
WKV_HALF = WIDTH // 2


def _wkv_consts():
    row = lax.broadcasted_iota(jnp.int32, (HEAD, WKV_HALF), 0)
    lane = lax.broadcasted_iota(jnp.int32, (HEAD, WKV_HALF), 1)
    diag = row == jnp.bitwise_and(lane, HEAD - 1)
    sub = lax.broadcasted_iota(jnp.int32, (WKV_GROUP, WKV_HALF), 0)
    return diag, sub


def _halves(x):
    return [x[:, :WKV_HALF], x[:, WKV_HALF:]]


def _diag_rows(row, diag):
    return _bf(jnp.where(diag, jnp.broadcast_to(row, diag.shape), 0.0))


def _put_row(tile, row, i, sub):
    return jnp.where(sub == i, jnp.broadcast_to(row, tile.shape), tile)


def _col_sum(x):
    return jnp.sum(x, axis=0, keepdims=True)


def _step(x, i):
    return x[i * HEAD:(i + 1) * HEAD]


def _expand_group(rows8, diag, ones_b):
    lhs = jnp.concatenate([_diag_rows(rows8[i:i + 1], diag) for i in range(WKV_GROUP)], axis=0)
    return _dot(lhs, ones_b, 1, 0)


def _head_dots(prods, ones_b, sub):
    tile = jnp.zeros((WKV_GROUP, WKV_HALF), F32)
    for i, p in enumerate(prods):
        tile = _put_row(tile, p, i, sub)
    res = _exact_dot(tile, ones_b, 0)
    return [res[i:i + 1] for i in range(len(prods))]


def _diag_group(x, diag, sub):
    out = jnp.zeros((WKV_GROUP, WKV_HALF), F32)
    for i in range(WKV_GROUP):
        out = _put_row(out, _col_sum(jnp.where(diag, _step(x, i), 0.0)), i, sub)
    return out


def wkv_fwd(r, w, k, v, a, b, ones_half):
    seq = r.shape[0]

    def body(r_ref, w_ref, k_ref, v_ref, a_ref, b_ref, ones_ref, y_ref, st_ref, s_scr):
        @pl.when(pl.program_id(0) == 0)
        def _():
            s_scr[...] = jnp.zeros_like(s_scr)

        ones_b = ones_ref[...]
        diag, sub = _wkv_consts()

        ng = WKV_TB // WKV_GROUP
        last = WKV_GROUP - 2

        def rows_of(g):
            return pl.ds(pl.multiple_of(g * WKV_GROUP, WKV_GROUP), WKV_GROUP)

        def pair_rows(x8, t):
            return jnp.concatenate([_diag_rows(x8[t:t + 1], diag), _diag_rows(x8[t + 1:t + 2], diag)], axis=0)

        def put_y(g, pairs_y):
            tile = _halves(y_ref[rows_of(g), :])
            for hf in range(2):
                tile[hf] = _put_row(_put_row(tile[hf], pairs_y[hf][0], last, sub), pairs_y[hf][1], last + 1, sub)
            y_ref[rows_of(g), :] = jnp.concatenate(tile, axis=1)

        def read_out(yexp):
            return _col_sum(jnp.where(diag, yexp[:HEAD], 0.0)), _col_sum(jnp.where(diag, yexp[HEAD:], 0.0))

        def group(g, carry):
            state, v_cur, read = (list(c) for c in carry)
            base = pl.multiple_of(g * WKV_GROUP, WKV_GROUP)
            rows = rows_of(g)
            r8, w8, k8, v8, a8, b8 = (_halves(ref[rows, :]) for ref in (r_ref, w_ref, k_ref, v_ref, a_ref, b_ref))
            v_after = _halves(v_ref[rows_of(jnp.minimum(g + 1, ng - 1)), :])
            evens = range(0, WKV_GROUP, 2)
            dots = [_head_dots([b8[hf][t:t + 1] * a8[hf][t + 1:t + 2] for t in evens]
                               + [k8[hf][t:t + 1] * a8[hf][t + 1:t + 2] for t in evens], ones_b, sub) for hf in range(2)]
            y8 = [jnp.zeros((WKV_GROUP, WKV_HALF), F32)] * 2
            y_before = [None, None]
            for t in evens:
                s0, s1 = slice(t, t + 1), slice(t + 1, t + 2)
                both = []
                for hf in range(2):
                    s_in = state[hf]
                    v_next = pair_rows(v8[hf], t + 2) if t < last else pair_rows(v_after[hf], 0)
                    res = _dot(jnp.concatenate([_bf(s_in * a8[hf][s0]), _bf(s_in * (w8[hf][s0] * a8[hf][s1])),
                                                v_next, read[hf]], axis=0), ones_b, 1, 0)
                    sa0, v0, v1 = res[:HEAD], v_cur[hf][:HEAD], v_cur[hf][HEAD:]
                    st0 = s_in * w8[hf][s0] + sa0 * b8[hf][s0] + v0 * k8[hf][s0]
                    sa1 = res[HEAD:2 * HEAD] + sa0 * dots[hf][t // 2] + v0 * dots[hf][WKV_GROUP // 2 + t // 2]
                    st1 = st0 * w8[hf][s1] + sa1 * b8[hf][s1] + v1 * k8[hf][s1]
                    both.append((st0, st1))
                    state[hf], v_cur[hf] = st1, res[2 * HEAD:4 * HEAD]
                    read[hf] = jnp.concatenate([_bf(st0 * r8[hf][s0]), _bf(st1 * r8[hf][s1])], axis=0)
                    ya, yb = read_out(res[4 * HEAD:])
                    if t == 0:
                        y_before[hf] = (ya, yb)
                    else:
                        y8[hf] = _put_row(_put_row(y8[hf], ya, t - 2, sub), yb, t - 1, sub)
                for j in range(2):
                    st_ref[base + t + j] = jnp.concatenate([both[0][j], both[1][j]], axis=1)
            y_ref[rows, :] = jnp.concatenate(y8, axis=1)
            put_y(jnp.maximum(g - 1, 0), y_before)
            return tuple(state), tuple(v_cur), tuple(read)

        v_first = _halves(v_ref[rows_of(0), :])
        init = (tuple(_halves(s_scr[...])),
                tuple(_dot(pair_rows(v_first[hf], 0), ones_b, 1, 0) for hf in range(2)),
                tuple(jnp.zeros((2 * HEAD, WKV_HALF), BF16) for _ in range(2)))
        fin, _, read = lax.fori_loop(0, ng, group, init)
        put_y(ng - 1, [read_out(_dot(read[hf], ones_b, 1, 0)) for hf in range(2)])
        s_scr[...] = jnp.concatenate(fin, axis=1)

    vec = pl.BlockSpec((WKV_TB, WIDTH), lambda i: (i, 0))
    return pl.pallas_call(
        body, name="wkv_fwd", grid=(seq // WKV_TB,),
        in_specs=[vec] * 6 + [_full_spec(ones_half)],
        out_specs=[vec, pl.BlockSpec((WKV_TB, HEAD, WIDTH), lambda i: (i, 0, 0))],
        out_shape=[jax.ShapeDtypeStruct((seq, WIDTH), F32), jax.ShapeDtypeStruct((seq, HEAD, WIDTH), F32)],
        scratch_shapes=[pltpu.VMEM((HEAD, WIDTH), F32)],
        compiler_params=_params(),
    )(r, w, k, v, a, b, ones_half)


def wkv_bwd(r, w, k, v, a, b, dy, states, ones_half):
    seq = r.shape[0]
    nb = seq // WKV_TB
    ng = WKV_TB // WKV_GROUP

    def body(r_ref, w_ref, k_ref, v_ref, a_ref, b_ref, dy_ref, st_ref, halo_ref, ones_ref,
             dr_ref, dw_ref, dk_ref, dv_ref, da_ref, db_ref, ds_scr):
        blk = nb - 1 - pl.program_id(0)

        @pl.when(pl.program_id(0) == 0)
        def _():
            ds_scr[...] = jnp.zeros_like(ds_scr)

        ones_b = ones_ref[...]
        diag, sub = _wkv_consts()
        before_block = jnp.where(blk == 0, 0.0, halo_ref[0])

        def rows_of(g):
            return pl.ds(pl.multiple_of(g * WKV_GROUP, WKV_GROUP), WKV_GROUP)

        def expand_rows(hf, dy8, v8, a8, t, s_t, s_u):
            s1, s0 = slice(t, t + 1), slice(t - 1, t)
            return jnp.concatenate([_diag_rows(dy8[hf][s1], diag), _diag_rows(dy8[hf][s0], diag),
                                    _diag_rows(v8[hf][s1], diag), _diag_rows(v8[hf][s0], diag),
                                    _bf(s_t[hf] * a8[hf][s1]), _bf(s_u[hf] * a8[hf][s0])], axis=0)

        def read_out(x):
            return _col_sum(jnp.where(diag, x[:HEAD], 0.0)), _col_sum(jnp.where(diag, x[HEAD:], 0.0))

        def put_dv(g, pair_dv):
            tile = _halves(dv_ref[rows_of(g), :])
            for hf in range(2):
                tile[hf] = _put_row(_put_row(tile[hf], pair_dv[hf][0], 1, sub), pair_dv[hf][1], 0, sub)
            dv_ref[rows_of(g), :] = jnp.concatenate(tile, axis=1)

        def group(gg, carry):
            dstate, e_cur, dv_pend = (list(c) for c in carry)
            g = ng - 1 - gg
            base = pl.multiple_of(g * WKV_GROUP, WKV_GROUP)
            rows = rows_of(g)
            r8, w8, k8, v8, a8, b8, dy8 = (
                _halves(ref[rows, :]) for ref in (r_ref, w_ref, k_ref, v_ref, a_ref, b_ref, dy_ref))
            g_next = jnp.maximum(g - 1, 0)
            base_next = pl.multiple_of(g_next * WKV_GROUP, WKV_GROUP)
            dy8n, v8n, a8n = (_halves(ref[rows_of(g_next), :]) for ref in (dy_ref, v_ref, a_ref))
            zero8 = jnp.zeros((WKV_GROUP, WKV_HALF), F32)
            out = {n: [zero8, zero8] for n in ("dr", "dw", "dk", "dv", "da", "db")}
            before_group = jnp.where(g == 0, before_block, st_ref[jnp.maximum(base - 1, 0)])
            states = [_halves(before_group)] + [_halves(st_ref[base + i]) for i in range(WKV_GROUP)]
            odds = range(1, WKV_GROUP, 2)
            dots = [_head_dots([a8[hf][t:t + 1] * b8[hf][t - 1:t] for t in odds]
                               + [r8[hf][t - 1:t] * b8[hf][t - 1:t] for t in odds], ones_b, sub) for hf in range(2)]
            dv_after = [None, None]

            def emit(hf, i, d_i, dsa_i, dy_i, v_i, sa_i):
                s_p, s_t = states[i][hf], states[i + 1][hf]
                for n, val in (("dr", _col_sum(s_t * dy_i)), ("dw", _col_sum(d_i * s_p)), ("db", _col_sum(d_i * sa_i)),
                               ("da", _col_sum(s_p * dsa_i)), ("dk", _col_sum(d_i * v_i))):
                    out[n][hf] = _put_row(out[n][hf], val, i, sub)

            for t in reversed(odds):
                s1, s0 = slice(t, t + 1), slice(t - 1, t)
                for hf in range(2):
                    dy1, dy0, v1, v0, sa1, sa0 = (_step(e_cur[hf], j) for j in range(6))
                    d1 = dstate[hf] + dy1 * r8[hf][s1]
                    if t > 1:
                        nxt = expand_rows(hf, dy8, v8, a8, t - 2, states[t - 2], states[t - 3])
                    else:
                        nxt = expand_rows(hf, dy8n, v8n, a8n, WKV_GROUP - 1, _halves(st_ref[base_next + WKV_GROUP - 2]),
                                          _halves(st_ref[base_next + WKV_GROUP - 3]))
                    res = _dot(jnp.concatenate([_bf(d1 * b8[hf][s1]), _bf(d1 * (w8[hf][s1] * b8[hf][s0])),
                                                nxt, dv_pend[hf]], axis=0), ones_b, 1, 0)
                    dsa1 = res[:HEAD]
                    d0 = d1 * w8[hf][s1] + dsa1 * a8[hf][s1] + dy0 * r8[hf][s0]
                    dsa0 = res[HEAD:2 * HEAD] + dsa1 * dots[hf][t // 2] + dy0 * dots[hf][WKV_GROUP // 2 + t // 2]
                    dstate[hf] = d0 * w8[hf][s0] + dsa0 * a8[hf][s0]
                    e_cur[hf] = res[2 * HEAD:8 * HEAD]
                    dv_pend[hf] = jnp.concatenate([_bf(d1 * k8[hf][s1]), _bf(d0 * k8[hf][s0])], axis=0)
                    emit(hf, t, d1, dsa1, dy1, v1, sa1)
                    emit(hf, t - 1, d0, dsa0, dy0, v0, sa0)
                    dv_a, dv_b = read_out(res[8 * HEAD:])
                    if t == WKV_GROUP - 1:
                        dv_after[hf] = (dv_a, dv_b)
                    else:
                        out["dv"][hf] = _put_row(_put_row(out["dv"][hf], dv_a, t + 2, sub), dv_b, t + 1, sub)
            for ref, n in ((dr_ref, "dr"), (dw_ref, "dw"), (dk_ref, "dk"), (dv_ref, "dv"), (da_ref, "da"), (db_ref, "db")):
                ref[rows, :] = jnp.concatenate(out[n], axis=1)
            put_dv(jnp.minimum(g + 1, ng - 1), dv_after)
            return tuple(dstate), tuple(e_cur), tuple(dv_pend)

        top = rows_of(ng - 1)
        dy8t, v8t, a8t = (_halves(ref[top, :]) for ref in (dy_ref, v_ref, a_ref))
        s_t, s_u = _halves(st_ref[WKV_TB - 2]), _halves(st_ref[WKV_TB - 3])
        init = (tuple(_halves(ds_scr[...])),
                tuple(_dot(expand_rows(hf, dy8t, v8t, a8t, WKV_GROUP - 1, s_t, s_u), ones_b, 1, 0) for hf in range(2)),
                tuple(jnp.zeros((2 * HEAD, WKV_HALF), BF16) for _ in range(2)))
        fin, _, dv_pend = lax.fori_loop(0, ng, group, init)
        put_dv(0, [read_out(_dot(dv_pend[hf], ones_b, 1, 0)) for hf in range(2)])
        ds_scr[...] = jnp.concatenate(fin, axis=1)

    vec = pl.BlockSpec((WKV_TB, WIDTH), lambda i: (nb - 1 - i, 0))
    return pl.pallas_call(
        body, name="wkv_bwd", grid=(nb,),
        in_specs=[vec] * 7 + [
            pl.BlockSpec((WKV_TB, HEAD, WIDTH), lambda i: (nb - 1 - i, 0, 0)),
            pl.BlockSpec((1, HEAD, WIDTH), lambda i: (jnp.maximum((nb - 1 - i) * WKV_TB - 1, 0), 0, 0)),
            _full_spec(ones_half)],
        out_specs=[vec] * 6,
        out_shape=[jax.ShapeDtypeStruct((seq, WIDTH), F32)] * 6,
        scratch_shapes=[pltpu.VMEM((HEAD, WIDTH), F32)],
        compiler_params=_params(),
    )(r, w, k, v, a, b, dy, states, states, ones_half)


def ada_fwd(c8, b_ada, gathered):
    cols = 3 * D_MODEL // 4

    def body(c_ref, b_ref, w_ref, o_ref):
        @pl.when(pl.program_id(1) == 0)
        def _():
            o_ref[...] = jnp.broadcast_to(b_ref[...], o_ref.shape)

        o_ref[...] += mm(_silu(c_ref[...]), w_ref[0])

    return pl.pallas_call(
        body, name="ada_fwd", grid=(4, D_MODEL // PACK_ROWS),
        in_specs=[pl.BlockSpec((SUBLANES, PACK_ROWS), lambda s, i: (0, i)),
                  pl.BlockSpec((1, cols), lambda s, i: (0, s)),
                  pl.BlockSpec((1, PACK_ROWS, cols), lambda s, i: (2 * s, 0, i))],
        out_specs=pl.BlockSpec((SUBLANES, cols), lambda s, i: (0, s)),
        out_shape=jax.ShapeDtypeStruct((SUBLANES, 3 * D_MODEL), F32),
        compiler_params=_params(("arbitrary", "arbitrary")),
    )(c8, b_ada, gathered)


def ada_grad_shard(sc_cols, dada_rows):
    n = len(sc_cols)

    def body(*refs):
        d_ref, o_ref = refs[n], refs[n + 1]
        acc = refs[0][...] * d_ref[0:1, :]
        for b in range(1, n):
            acc = acc + refs[b][...] * d_ref[b:b + 1, :]
        o_ref[...] = acc

    return pl.pallas_call(
        body, name="ada_grad_shard",
        out_shape=jax.ShapeDtypeStruct((sc_cols[0].shape[0], dada_rows.shape[1]), F32),
        compiler_params=pltpu.CompilerParams(vmem_limit_bytes=VMEM_LIMIT),
    )(*sc_cols, dada_rows)


def sum_slots(buf, tr):
    n, rows, cols = buf.shape

    def body(b_ref, o_ref):
        acc = b_ref[0].astype(F32)
        for s in range(1, n):
            acc = acc + b_ref[s].astype(F32)
        o_ref[...] = acc

    return pl.pallas_call(
        body, name="sum_slots", grid=(rows // tr,),
        in_specs=[pl.BlockSpec((n, tr, cols), lambda i: (0, i, 0))],
        out_specs=pl.BlockSpec((tr, cols), lambda i: (i, 0)),
        out_shape=jax.ShapeDtypeStruct((rows, cols), F32),
        compiler_params=_params(),
    )(buf)


def adamw_small(gathered, w, m, v):
    n = gathered.shape[0]

    def body(g_ref, w_ref, m_ref, v_ref, go_ref, d_ref, mo_ref, vo_ref):
        g = g_ref[0]
        for s in range(1, n):
            g = g + g_ref[s]
        go_ref[...] = g
        d_ref[...], mo_ref[...], vo_ref[...] = f_adamw(w_ref[...], g, m_ref[...], v_ref[...])

    return pl.pallas_call(
        body, name="adamw_small",
        out_shape=[jax.ShapeDtypeStruct(w.shape, F32)] * 4,
        compiler_params=pltpu.CompilerParams(vmem_limit_bytes=VMEM_LIMIT),
    )(gathered, w, m, v)


def _coords():
    return lax.axis_index("x"), lax.axis_index("y"), lax.axis_index("c")


def _flip(v, bit):
    return 1 - v if bit else v


def _hbm_call(body, name, out_shape, n_sems, *args):
    any_spec = pl.BlockSpec(memory_space=pl.ANY)
    return pl.pallas_call(
        body, name=name, out_shape=out_shape,
        in_specs=[any_spec] * len(args), out_specs=any_spec,
        scratch_shapes=[pltpu.SemaphoreType.DMA((n_sems,)), pltpu.SemaphoreType.DMA((n_sems,)),
                        pltpu.SemaphoreType.DMA],
    )(*args)


def all_gather8(name, block):
    def body(x_ref, out_ref, send_sems, recv_sems, local_sem):
        x, y, c = _coords()
        me, sibling = (x, y, c), (x, y, 1 - c)
        chips = [(1 - x, y), (x, 1 - y), (1 - x, 1 - y)]

        def slot(px, py, pc):
            return out_ref.at[4 * px + 2 * py + pc]

        def copy(k, blk, to, src=None):
            return pltpu.make_async_remote_copy(
                src_ref=slot(*blk) if src is None else src, dst_ref=slot(*blk),
                send_sem=send_sems.at[k], recv_sem=recv_sems.at[k], device_id=to, device_id_type=MESH)

        mine = pltpu.make_async_copy(x_ref, slot(*me), local_sem)
        mine.start()
        first = [copy(0, me, sibling, src=x_ref)]
        first += [copy(1 + j, me, (*chip, c), src=x_ref) for j, chip in enumerate(chips)]
        for cp in first:
            cp.start()
        passed = [copy(4 + j, (*chip, c), sibling) for j, chip in enumerate(chips)]
        for j, chip in enumerate(chips):
            copy(1 + j, (*chip, c), me).wait_recv()
            passed[j].start()
        copy(0, sibling, me).wait_recv()
        for j, chip in enumerate(chips):
            copy(4 + j, (*chip, 1 - c), me).wait_recv()
        for cp in first + passed:
            cp.wait_send()
        mine.wait()

    return _hbm_call(body, name, jax.ShapeDtypeStruct((N_DEV,) + block.shape, block.dtype), 7, block)


def pair_swap(name, block):
    def body(x_ref, out_ref, send_sems, recv_sems, local_sem):
        x, y, c = _coords()
        cp = pltpu.make_async_remote_copy(
            src_ref=x_ref, dst_ref=out_ref, send_sem=send_sems.at[0], recv_sem=recv_sems.at[0],
            device_id=(x, y, 1 - c), device_id_type=MESH)
        cp.start()
        cp.wait_recv()
        cp.wait_send()

    return _hbm_call(body, name, jax.ShapeDtypeStruct(block.shape, block.dtype), 1, block)


def chip_all_to_all(name, buf):
    def body(x_ref, out_ref, send_sems, recv_sems, local_sem):
        x, y, c = _coords()
        me = 2 * x + y
        mine = pltpu.make_async_copy(x_ref.at[me], out_ref.at[me], local_sem)
        mine.start()
        copies = []
        for k in range(1, 4):
            px, py = _flip(x, k & 2), _flip(y, k & 1)
            copies.append(pltpu.make_async_remote_copy(
                src_ref=x_ref.at[2 * px + py], dst_ref=out_ref.at[me],
                send_sem=send_sems.at[k - 1], recv_sem=recv_sems.at[k - 1],
                device_id=(px, py, c), device_id_type=MESH))
        for cp in copies:
            cp.start()
        for cp in copies:
            cp.wait_recv()
        for cp in copies:
            cp.wait_send()
        mine.wait()

    return _hbm_call(body, name, jax.ShapeDtypeStruct(buf.shape, buf.dtype), 3, buf)


def sibling_gather(name, block):
    def body(x_ref, out_ref, send_sems, recv_sems, local_sem):
        x, y, c = _coords()
        mine = pltpu.make_async_copy(x_ref, out_ref.at[c], local_sem)
        mine.start()
        cp = pltpu.make_async_remote_copy(
            src_ref=x_ref, dst_ref=out_ref.at[c], send_sem=send_sems.at[0], recv_sem=recv_sems.at[0],
            device_id=(x, y, 1 - c), device_id_type=MESH)
        cp.start()
        cp.wait_recv()
        cp.wait_send()
        mine.wait()

    return _hbm_call(body, name, jax.ShapeDtypeStruct((2,) + block.shape, block.dtype), 1, block)


def _col_blocks(a, cols):
    a = jnp.pad(a, ((0, 0), (0, cols - a.shape[1])))
    return [a[i * PACK_ROWS:(i + 1) * PACK_ROWS] for i in range(a.shape[0] // PACK_ROWS)]


def _pack_shard(sh, dtype, with_ada=True):
    lora = jnp.concatenate([sh['w_decay_up'], sh['w_iclr_up']], axis=1)
    misc = jnp.concatenate([sh['w_ukv'], lora, jnp.zeros((LORA, 2 * LANES), lora.dtype)], axis=0)
    blocks = ((_col_blocks(sh['w_ada'], 768) if with_ada else [])
              + _col_blocks(sh['w_in'], 1408) + _col_blocks(sh['w_proj_a'], 256)
              + _col_blocks(sh['w_proj_b'], 256) + [sh['w_out']] + _col_blocks(sh['w_uq'], 256) + [misc])
    return jnp.concatenate([b.astype(dtype) for b in blocks], axis=1)


def _unpack_shard(p, with_ada=True):
    o = [0]

    def take(n_blocks, cols, used):
        blocks = [p[:, o[0] + i * cols:o[0] + (i + 1) * cols] for i in range(n_blocks)]
        o[0] += n_blocks * cols
        return jnp.concatenate(blocks, axis=0)[:, :used]

    out = {'w_ada': take(4, 768, 768)} if with_ada else {}
    out.update({'w_in': take(4, 1408, 1288), 'w_proj_a': take(2, 256, 256),
                'w_proj_b': take(2, 256, 256), 'w_out': take(1, 1024, 1024), 'w_uq': take(1, 256, 192)})
    misc = take(1, 256, 256)
    out['w_ukv'] = misc[:2 * LORA]
    out['w_decay_up'] = misc[2 * LORA:3 * LORA, :LANES]
    out['w_iclr_up'] = misc[2 * LORA:3 * LORA, LANES:]
    return out


def _pack_small(parts):
    flat = jnp.concatenate([p.reshape(-1) for p in parts])
    return jnp.pad(flat, (0, SMALL_ROWS * LANES - flat.shape[0])).reshape(SMALL_ROWS, LANES)


def _unpack_small(packed):
    flat, out, o = packed.reshape(-1), {}, 0
    for name, n in _SMALL:
        out[name] = flat[o:o + n]
        o += n
    return out


def _pad_heads_cols(w, used, left):
    k = w.shape[0]
    return jnp.pad(w.reshape(k, HEADS, used), ((0, 0), (0, 0), (left, LANES - used - left))).reshape(k, HEADS * LANES)


def _unpad_heads_cols(w, used, left):
    k = w.shape[0]
    return w.reshape(k, HEADS, LANES)[:, :, left:left + used].reshape(k, HEADS * used)


def kernel(x, c, positions, w_ada, b_ada, w_in, q_norm_g, w_uq, kv_norm_g, w_ukv, mu_rwkv, w0, w_decay_up, a0, w_iclr_up, k_k, k_a, r_k, gn_g, gn_b, w_proj_a, w_proj_b, w_out, post_g, post_b, loss_target, m_w_ada, m_b_ada, m_w_in, m_q_norm_g, m_w_uq, m_kv_norm_g, m_w_ukv, m_mu_rwkv, m_w0, m_w_decay_up, m_a0, m_w_iclr_up, m_k_k, m_k_a, m_r_k, m_gn_g, m_gn_b, m_w_proj_a, m_w_proj_b, m_w_out, m_post_g, m_post_b, v_w_ada, v_b_ada, v_w_in, v_q_norm_g, v_w_uq, v_kv_norm_g, v_w_ukv, v_mu_rwkv, v_w0, v_w_decay_up, v_a0, v_w_iclr_up, v_k_k, v_k_a, v_r_k, v_gn_g, v_gn_b, v_w_proj_a, v_w_proj_b, v_w_out, v_post_g, v_post_b):
    given = dict(locals())
    seq = x.shape[1]
    my_c = lax.axis_index("c")

    shard_names = [n for n, _, _ in _SHARDED]
    w_pack = _pack_shard({n: given[n][0] for n in shard_names}, BF16)
    my_half = lax.dynamic_slice_in_dim(w_pack, my_c * HALF_COLS, HALF_COLS, 1)
    gathered = all_gather8("gather_weights", my_half)
    shards = [_unpack_shard(jnp.concatenate([gathered[2 * s], gathered[2 * s + 1]], axis=1)) for s in range(4)]
    full = {n: jnp.concatenate([sh[n] for sh in shards], axis=ax) for n, _, ax in _SHARDED}

    wi = full['w_in']
    zcol = lambda n: jnp.zeros((D_MODEL, n), BF16)
    w_g1 = jnp.concatenate([wi[:, :384], zcol(HEAD), wi[:, 384:416], zcol(LANES - MLA_QK),
                            _pad_heads_cols(wi[:, 416:928], HEAD, HEAD)], axis=1)
    w_g2 = wi[:, 928:3104]
    w_g3 = wi[:, 3104:5152]
    w_uq_p = _pad_heads_cols(full['w_uq'], MLA_QK, 0)
    w_pa_p = jnp.pad(full['w_proj_a'].reshape(HEADS, HEAD, D_MODEL), ((0, 0), (HEAD, 0), (0, 0))).reshape(HEADS * LANES, D_MODEL)
    zl = jnp.zeros((LORA, WIDTH), BF16)
    w_lora = jnp.concatenate([jnp.concatenate([full['w_decay_up'], zl], 1),
                              jnp.concatenate([zl, full['w_iclr_up']], 1)], 0)

    hd = np.arange(WIDTH) // HEAD
    ones_blocks = jnp.asarray(hd[:, None] == hd[None, :], BF16)
    ones_half = ones_blocks[:WKV_HALF, :WKV_HALF]
    perm_np = np.zeros((LANES, LANES), np.float32)
    for d in range(MLA_ROPE // 2):
        perm_np[HEAD + 16 + d, HEAD + d] = -1.0
        perm_np[HEAD + d, HEAD + 16 + d] = 1.0
    perm = jnp.asarray(perm_np, BF16)
    inv = ROPE_THETA ** (-jnp.arange(0, MLA_ROPE, 2, dtype=F32) / MLA_ROPE)
    ang = positions[0].astype(F32)[:, None] * inv
    cos_a, sin_a = jnp.cos(ang), jnp.sin(ang)
    cs = jnp.concatenate([jnp.ones((seq, HEAD), F32), cos_a, cos_a, jnp.zeros((seq, LANES - MLA_QK), F32),
                          jnp.zeros((seq, HEAD), F32), sin_a, sin_a, jnp.zeros((seq, LANES - MLA_QK), F32)], axis=1)

    x2, tgt = x[0], loss_target[0]
    r_k2 = r_k.reshape(1, WIDTH)

    c8 = jnp.broadcast_to(c, (SUBLANES, D_MODEL))
    ada = ada_fwd(c8, b_ada, gathered)[:1]
    shift, scale, gate = ada[:, :D_MODEL], ada[:, D_MODEL:2 * D_MODEL], ada[:, 2 * D_MODEL:]

    f_in1, f_in2, f_in3 = _make_f_in((512, 1024)), _make_f_in((SHIFT_W, WIDTH)), _make_f_in((1024, 1024))
    tr = min(256, seq)
    p_mla, gpa = row_fwd("in1_fwd", f_in1, [x2], [shift, scale, w_g1], [], [512, 1024], tr)
    p_rwkv, gpb = row_fwd("in2_fwd", f_in2, [x2], [shift, scale, w_g2], [], [SHIFT_W, WIDTH], tr)
    ma, mb = row_fwd("in3_fwd", f_in3, [x2], [shift, scale, w_g3], [], [1024, 1024], tr)

    mla_par = [q_norm_g, kv_norm_g, w_uq_p, full['w_ukv']]
    q_f, kv_f, kpe = row_fwd("mla_pre_fwd", f_mla_pre, [p_mla, cs], mla_par, [perm], [1024, 1024, LANES], tr)
    ya, lse = attn_fwd(q_f, kv_f, kpe)

    u = shift_fwd(p_rwkv, mu_rwkv, tr)
    pre_par = [w0, a0, k_k, k_a, w_lora]
    rr, wd, k2, vv, an, bb = row_fwd("rwkv_pre_fwd", f_rwkv_pre, [u], pre_par, [ones_blocks], [WIDTH] * 6, tr)
    y_wkv, states = wkv_fwd(rr, wd, k2, vv, an, bb, ones_half)
    post_b_par = [r_k2, gn_g, gn_b]
    yb, = row_fwd("rwkv_post_fwd", f_rwkv_post, [y_wkv, rr, k2, vv], post_b_par, [ones_blocks], [WIDTH], tr)

    merge_rows, merge_par = [ya, gpa, yb, gpb, ma, mb], [w_pa_p, full['w_proj_b']]
    merged, = row_fwd("merge_fwd", f_merge, merge_rows, merge_par, [], [D_MODEL], tr)
    loss_rows, loss_par = [merged, x2, tgt], [gate, post_g, post_b, full['w_out']]
    lrows, = row_fwd("loss_fwd", f_loss, loss_rows, loss_par, [], [LANES], tr)
    loss = lax.psum(jnp.sum(lrows[:, 0]), ("x", "y", "c"))

    dl = jnp.broadcast_to((jnp.arange(LANES) == 0).astype(F32), (seq, LANES))
    (dmerged, dx_res), (dgate, dpost_g, dpost_b, dw_out) = row_bwd("loss_bwd", f_loss, loss_rows, 2, loss_par, [], [dl], tr)
    (dya, dgpa, dyb, dgpb, dma, dmb), (dw_pa_p, dw_pb) = row_bwd(
        "merge_bwd", f_merge, merge_rows, 6, merge_par, [], [dmerged], tr)

    (dy_wkv, dr1, dk1, dv1), (dr_k, dgn_g, dgn_b) = row_bwd(
        "rwkv_post_bwd", f_rwkv_post, [y_wkv, rr, k2, vv], 4, post_b_par, [ones_blocks], [dyb], tr)
    dr2, dwd, dk2, dv2, dan, dbb = wkv_bwd(rr, wd, k2, vv, an, bb, dy_wkv, states, ones_half)
    (du,), (dw0, da0, dk_k, dk_a, dw_lora) = row_bwd(
        "rwkv_pre_bwd", f_rwkv_pre, [u], 1, pre_par, [ones_blocks],
        [(dr1, dr2), dwd, (dk1, dk2), (dv1, dv2), dan, dbb], tr)
    dp_rwkv, dmu = shift_bwd(du, p_rwkv, mu_rwkv, tr)

    dq_f, dkv_f, dkpe = attn_bwd(q_f, kv_f, kpe, ya, lse, dya)
    (dp_mla,), (dqg, dkvg, dw_uq_p, dw_ukv) = row_bwd(
        "mla_pre_bwd", f_mla_pre, [p_mla, cs], 1, mla_par, [perm], [dq_f, dkv_f, dkpe], tr)

    (dx1,), (dsh1, dsc1, dw_g1) = row_bwd("in1_bwd", f_in1, [x2], 1, [shift, scale, w_g1], [], [dp_mla, dgpa], tr, [dx_res])
    (dx2,), (dsh2, dsc2, dw_g2) = row_bwd("in2_bwd", f_in2, [x2], 1, [shift, scale, w_g2], [], [dp_rwkv, dgpb], tr, [dx1])
    (dx3,), (dsh3, dsc3, dw_g3) = row_bwd("in3_bwd", f_in3, [x2], 1, [shift, scale, w_g3], [], [dma, dmb], tr, [dx2])
    grad_x = dx3[None]

    dada = jnp.concatenate([dsh1 + dsh2 + dsh3, dsc1 + dsc2 + dsc3, dgate], axis=1)
    local = {
        'w_in': jnp.concatenate([dw_g1[:, :384], dw_g1[:, 448:480], _unpad_heads_cols(dw_g1[:, 512:], HEAD, HEAD),
                                 dw_g2, dw_g3], axis=1),
        'w_uq': _unpad_heads_cols(dw_uq_p, MLA_QK, 0),
        'w_ukv': dw_ukv,
        'w_decay_up': dw_lora[:LORA, :WIDTH],
        'w_iclr_up': dw_lora[LORA:, WIDTH:],
        'w_proj_a': dw_pa_p.reshape(HEADS, LANES, D_MODEL)[:, HEAD:].reshape(WIDTH, D_MODEL),
        'w_proj_b': dw_pb,
        'w_out': dw_out,
    }
    small_local = {'b_ada': dada, 'q_norm_g': dqg, 'kv_norm_g': dkvg, 'mu_rwkv': dmu, 'w0': dw0, 'a0': da0,
                   'k_k': dk_k, 'k_a': dk_a, 'r_k': dr_k, 'gn_g': dgn_g, 'gn_b': dgn_b,
                   'post_g': dpost_g, 'post_b': dpost_b}

    def shard_of(g, axis, s):
        n = g.shape[axis] // 4
        return lax.slice_in_dim(g, s * n, (s + 1) * n, axis=axis)

    packed = jnp.stack([_pack_shard({n: shard_of(local[n], ax, s) for n, _, ax in _SHARDED if n != 'w_ada'}, F32, False)
                        for s in range(4)])
    keep = lax.dynamic_slice_in_dim(packed, my_c * GRAD_HALF, GRAD_HALF, 2).reshape(4 * PACK_ROWS, GRAD_HALF)
    give = lax.dynamic_slice_in_dim(packed, (1 - my_c) * GRAD_HALF, GRAD_HALF, 2).reshape(4 * PACK_ROWS, GRAD_HALF)
    pair_sum, = row_fwd("pair_sum", lambda p, q: (p + q,), [keep, pair_swap("swap_halves", give)], [], [],
                        [GRAD_HALF], PACK_ROWS // 2, BF16)
    received = chip_all_to_all("exchange_grads", pair_sum.reshape(4, PACK_ROWS, GRAD_HALF))
    my_sum = sum_slots(received, PACK_ROWS // 2)
    halves = sibling_gather("gather_halves", my_sum)
    g_shard = _unpack_shard(jnp.concatenate([halves[0], halves[1]], axis=1), False)

    small_pack = lambda d, extra=(): _pack_small([d[n] for n, _ in _SMALL] + list(extra))
    small_all = all_gather8("gather_small", small_pack(small_local, [c * jax.nn.sigmoid(c)]))
    sc_all = small_all[:, SMALL_USED:SMALL_USED + D_MODEL // LANES].reshape(N_DEV, D_MODEL)
    dada_all = small_all[:, :3 * D_MODEL // LANES].reshape(N_DEV, 3 * D_MODEL)
    my_cols = lax.dynamic_slice_in_dim(dada_all, (2 * lax.axis_index("x") + lax.axis_index("y")) * 768, 768, 1)
    g_shard['w_ada'] = ada_grad_shard([sc_all[b].reshape(D_MODEL, 1) for b in range(N_DEV)], my_cols)

    big = [{}, {}, {}, {}]
    for n in shard_names:
        w2, m2, v2 = given[n][0], given['m_' + n][0], given['v_' + n][0]
        cols = w2.shape[1]
        outs = row_fwd("adamw_" + n, f_adamw, [w2, g_shard[n], m2, v2], [], [], [cols] * 3, min(256, w2.shape[0]))
        for dst, val in zip(big, (g_shard[n], *outs)):
            dst[n] = val

    small_out = adamw_small(small_all, small_pack({n: given[n] for n, _ in _SMALL}),
                            small_pack({n: given['m_' + n] for n, _ in _SMALL}),
                            small_pack({n: given['v_' + n] for n, _ in _SMALL}))

    results = []
    for big_k, packed_small in zip(big, small_out):
        small = _unpack_small(packed_small)
        results.append([(big_k[n] if n in big_k else small[n]).reshape(given[n].shape) for n in _WEIGHTS])
    return (loss, grad_x, *results[0], *results[1], *results[2], *results[3])
```

```python
from typing import NamedTuple

import numpy as np
import jax
import jax.numpy as jnp
from jax import lax
from jax.experimental import pallas as pl
from jax.experimental.pallas import tpu as pltpu

F32 = jnp.float32
BF16 = jnp.bfloat16

D_MODEL = 1024
LN_EPS = 1e-5
RMS_EPS = 1e-6
GN_EPS = 64e-5
HEADS = 8
HEAD = 64
MLA_ROPE = 32
MLA_QK = HEAD + MLA_ROPE
ROPE_THETA = 10000.0
WIDTH = HEADS * HEAD
LORA = 64
SHIFT_W = 3 * WIDTH + 2 * LORA
CHUNK = 64
ALPHA = 2.0 ** 0.25

ADAM_LR, ADAM_B1, ADAM_B2, ADAM_EPS, ADAM_WD, ADAM_STEP = 0.001, 0.9, 0.999, 1e-08, 0.01, 10

LANES = 128
SUBLANES = 8
VMEM_LIMIT = 56 * 1024 * 1024
N_DEV = 8
MESH = pl.DeviceIdType.MESH
NEG = -1e30

_WEIGHTS = ['w_ada', 'b_ada', 'w_in', 'q_norm_g', 'w_uq', 'kv_norm_g', 'w_ukv', 'mu_rwkv', 'w0',
            'w_decay_up', 'a0', 'w_iclr_up', 'k_k', 'k_a', 'r_k', 'gn_g', 'gn_b', 'w_proj_a',
            'w_proj_b', 'w_out', 'post_g', 'post_b']
_SHARDED = [('w_ada', (1024, 3072), 1), ('w_in', (1024, 5152), 1), ('w_uq', (256, 768), 1),
            ('w_ukv', (128, 1024), 1), ('w_decay_up', (64, 512), 1), ('w_iclr_up', (64, 512), 1),
            ('w_proj_a', (512, 1024), 1), ('w_proj_b', (512, 1024), 1), ('w_out', (1024, 1024), 0)]
_SMALL = [('b_ada', 3072), ('q_norm_g', 256), ('kv_norm_g', 128), ('mu_rwkv', 1664), ('w0', 512),
          ('a0', 512), ('k_k', 512), ('k_a', 512), ('r_k', 512), ('gn_g', 512), ('gn_b', 512),
          ('post_g', 1024), ('post_b', 1024)]
PACK_ROWS = 256
PACK_COLS = 11264
HALF_COLS = PACK_COLS // 2
ADA_COLS = 4 * 768
GRAD_HALF = (PACK_COLS - ADA_COLS) // 2
SMALL_USED = 84
SMALL_ROWS = 96


def _bf(x):
    return x.astype(BF16)


def _dot(a, b, ca, cb):
    return lax.dot_general(a, b, (((ca,), (cb,)), ((), ())), preferred_element_type=F32)


class Weight(NamedTuple):
    value: jax.Array
    grad: jax.Array


@jax.custom_vjp
def _mm(a, w, w_grad):
    return _dot(_bf(a), _bf(w), 1, 0)


def _mm_fwd(a, w, w_grad):
    return _mm(a, w, w_grad), (a, w)


def _mm_bwd(res, g):
    a, w = res
    gb = _bf(g)
    return _dot(gb, _bf(w), 1, 1), jnp.zeros_like(w), _dot(_bf(a), gb, 0, 0)


_mm.defvjp(_mm_fwd, _mm_bwd)


def mm(a, w):
    if isinstance(w, Weight):
        return _mm(a, w.value, w.grad)
    return _dot(_bf(a), _bf(w), 1, 0)


def _split3(x):
    hi = _bf(x)
    r1 = x - hi.astype(F32)
    mid = _bf(r1)
    lo = _bf(r1 - mid.astype(F32))
    return hi, mid, lo


def _exact_dot(x, m, cm):
    hi, mid, lo = _split3(x)
    return _dot(hi, m, 1, cm) + _dot(mid, m, 1, cm) + _dot(lo, m, 1, cm)


@jax.custom_vjp
def segsum(x, ones_blocks):
    return _exact_dot(x, ones_blocks, 0)


def _segsum_fwd(x, ones_blocks):
    return segsum(x, ones_blocks), ones_blocks


def _segsum_bwd(ones_blocks, g):
    return _exact_dot(g, ones_blocks, 0), jnp.zeros_like(ones_blocks)


segsum.defvjp(_segsum_fwd, _segsum_bwd)


@jax.custom_vjp
def lane_perm(x, perm):
    return _exact_dot(x, perm, 0)


def _lane_perm_fwd(x, perm):
    return lane_perm(x, perm), perm


def _lane_perm_bwd(perm, g):
    return _exact_dot(g, perm, 1), jnp.zeros_like(perm)


lane_perm.defvjp(_lane_perm_fwd, _lane_perm_bwd)


def _silu(z):
    return z * jax.nn.sigmoid(z)


def _softplus(z):
    return jnp.maximum(z, 0.0) + jnp.log(1.0 + jnp.exp(-jnp.abs(z)))


def _layer_norm(x):
    xc = x - jnp.mean(x, -1, keepdims=True)
    return xc * lax.rsqrt(jnp.mean(xc * xc, -1, keepdims=True) + LN_EPS)


def _rope(t, cos_t, sin_t, perm):
    outs = []
    for h in range(t.shape[1] // LANES):
        th = t[:, h * LANES:(h + 1) * LANES]
        outs.append(th * cos_t + lane_perm(th, perm) * sin_t)
    return outs[0] if len(outs) == 1 else jnp.concatenate(outs, axis=1)


def _make_f_in(splits):
    def f_in(x, shift, scale, w):
        h = _layer_norm(x) * (1.0 + scale) + shift
        p = mm(h, w)
        outs, o = [], 0
        for s in splits:
            outs.append(p[:, o:o + s])
            o += s
        return tuple(outs)
    return f_in


def f_mla_pre(p, cs, qg, kvg, w_uq, w_ukv, perm):
    q_c, kv_c, k_r = p[:, :256], p[:, 256:384], p[:, 384:512]
    cos_t, sin_t = cs[:, :LANES], cs[:, LANES:]
    qn = q_c * lax.rsqrt(jnp.mean(q_c * q_c, -1, keepdims=True) + RMS_EPS) * qg
    kvn = kv_c * lax.rsqrt(jnp.mean(kv_c * kv_c, -1, keepdims=True) + RMS_EPS) * kvg
    q = _rope(mm(qn, w_uq), cos_t, sin_t, perm)
    kv = mm(kvn, w_ukv)
    return q, kv, _rope(k_r, cos_t, sin_t, perm)


def f_rwkv_pre(u, w0, a0, k_k, k_a, w_lora, ones_blocks):
    r, k, v, lo = u[:, :WIDTH], u[:, WIDTH:2 * WIDTH], u[:, 2 * WIDTH:3 * WIDTH], u[:, 3 * WIDTH:]
    lane = lax.broadcasted_iota(jnp.int32, lo.shape, 1)
    dl = mm(jnp.where(lane < LORA, jnp.tanh(lo), lo), w_lora)
    w_log = -_softplus(-(w0 + dl[:, :WIDTH])) - 0.5
    decay = jnp.exp(-jnp.exp(w_log))
    a = jax.nn.sigmoid(a0 + dl[:, WIDTH:])
    kk = k * k_k
    kk = kk / jnp.maximum(jnp.sqrt(segsum(kk * kk, ones_blocks)), 1e-12)
    k2 = k * (1.0 + (a - 1.0) * k_a)
    return r, decay, k2, v, -kk, kk * a


def f_rwkv_post(y, r, k2, v, r_k, gn_g, gn_b, ones_blocks):
    yc = y - segsum(y, ones_blocks) * (1.0 / HEAD)
    yn = yc * lax.rsqrt(segsum(yc * yc, ones_blocks) * (1.0 / HEAD) + GN_EPS)
    return (yn * gn_g + gn_b + segsum(r * k2 * r_k, ones_blocks) * v,)


def f_merge(ya, gpa, yb, gpb, ma, mb, w_pa, w_pb):
    pa = mm(ya * _silu(gpa), w_pa)
    pb = mm(yb * _silu(gpb), w_pb)
    return (jax.nn.sigmoid(ma) * pa + jax.nn.sigmoid(mb) * pb,)


def f_loss(merged, x, tgt, gate, post_g, post_b, w_out):
    z = ALPHA * x + (1.0 + gate) * mm(merged, w_out)
    err = _layer_norm(z) * post_g + post_b - tgt
    lrow = 0.5 * jnp.mean(err * err, -1, keepdims=True)
    return (jnp.broadcast_to(lrow, (lrow.shape[0], LANES)),)


def f_adamw(w, g, m, v):
    m2 = ADAM_B1 * m + (1.0 - ADAM_B1) * g
    v2 = ADAM_B2 * v + (1.0 - ADAM_B2) * jnp.square(g)
    m_hat = m2 / (1.0 - ADAM_B1 ** ADAM_STEP)
    v_hat = v2 / (1.0 - ADAM_B2 ** ADAM_STEP)
    return -ADAM_LR * (m_hat / (jnp.sqrt(v_hat) + ADAM_EPS) + ADAM_WD * w), m2, v2


def _params(sem=("arbitrary",)):
    return pltpu.CompilerParams(dimension_semantics=sem, vmem_limit_bytes=VMEM_LIMIT)


def _row_spec(tr, a):
    return pl.BlockSpec((tr, a.shape[1]), lambda i: (i, 0))


def _full_spec(a):
    return pl.BlockSpec(a.shape, lambda i: (0,) * a.ndim)


def row_fwd(name, f, rows, params, consts, out_widths, tr, out_dtype=F32):
    n_rows = rows[0].shape[0]
    nr, npar, ncon = len(rows), len(params), len(consts)

    def body(*refs):
        rv = [r[...] for r in refs[:nr]]
        pv = [r[...] for r in refs[nr:nr + npar]]
        cv = [r[...] for r in refs[nr + npar:nr + npar + ncon]]
        outs = f(*rv, *pv, *cv)
        for o_ref, o in zip(refs[nr + npar + ncon:], outs):
            o_ref[...] = o.astype(o_ref.dtype)

    return pl.pallas_call(
        body, name=name, grid=(n_rows // tr,),
        in_specs=[_row_spec(tr, a) for a in rows] + [_full_spec(a) for a in list(params) + list(consts)],
        out_specs=[pl.BlockSpec((tr, w), lambda i: (i, 0)) for w in out_widths],
        out_shape=[jax.ShapeDtypeStruct((n_rows, w), out_dtype) for w in out_widths],
        compiler_params=_params(),
    )(*rows, *params, *consts)


def row_bwd(name, f, rows, n_diff, params, consts, douts, tr, add_rows=None):
    n_rows = rows[0].shape[0]
    douts = [d if isinstance(d, (tuple, list)) else (d,) for d in douts]
    counts = [len(d) for d in douts]
    flat_d = [a for d in douts for a in d]
    add_rows = add_rows or [None] * n_diff
    adds = [a for a in add_rows if a is not None]
    nr, npar, ncon, nd, na = len(rows), len(params), len(consts), len(flat_d), len(adds)

    def body(*refs):
        o = 0
        rv = [r[...] for r in refs[o:o + nr]]; o += nr
        pv = [Weight(r[...], jnp.zeros(r.shape, F32)) if r.dtype == BF16 else r[...] for r in refs[o:o + npar]]
        o += npar
        cv = [r[...] for r in refs[o:o + ncon]]; o += ncon
        dv = []
        for cnt in counts:
            s = refs[o][...]
            for e in range(1, cnt):
                s = s + refs[o + e][...]
            dv.append(s)
            o += cnt
        add_v = [r[...] for r in refs[o:o + na]]; o += na
        drow_refs = refs[o:o + n_diff]; o += n_diff
        dpar_refs = refs[o:o + npar]

        def g(*args):
            return tuple(f(*args[:n_diff], *rv[n_diff:], *args[n_diff:], *cv))

        _, vjp = jax.vjp(g, *rv[:n_diff], *pv)
        grads = vjp(tuple(dv))
        ai = 0
        for j, (r, gr) in enumerate(zip(drow_refs, grads[:n_diff])):
            if add_rows[j] is not None:
                gr = gr + add_v[ai]
                ai += 1
            r[...] = gr

        @pl.when(pl.program_id(0) == 0)
        def _():
            for r in dpar_refs:
                r[...] = jnp.zeros_like(r)

        for r, gr in zip(dpar_refs, grads[n_diff:]):
            r[...] += gr.grad if isinstance(gr, Weight) else gr

    outs = pl.pallas_call(
        body, name=name, grid=(n_rows // tr,),
        in_specs=([_row_spec(tr, a) for a in rows] + [_full_spec(a) for a in list(params) + list(consts)]
                  + [_row_spec(tr, a) for a in flat_d + adds]),
        out_specs=[_row_spec(tr, a) for a in rows[:n_diff]] + [_full_spec(a) for a in params],
        out_shape=([jax.ShapeDtypeStruct(a.shape, F32) for a in rows[:n_diff]]
                   + [jax.ShapeDtypeStruct(a.shape, F32) for a in params]),
        compiler_params=_params(),
    )(*rows, *params, *consts, *flat_d, *adds)
    return outs[:n_diff], outs[n_diff:]


def shift_fwd(p, mu, tr):
    n_rows, w = p.shape

    def body(p_ref, mu_ref, u_ref, carry):
        @pl.when(pl.program_id(0) == 0)
        def _():
            carry[...] = jnp.zeros_like(carry)

        x = p_ref[...]
        rolled = pltpu.roll(x, 1, 0)
        head = pltpu.roll(carry[...], 1, 0)
        fixed = jnp.concatenate([head, rolled[SUBLANES:]], axis=0)
        row = lax.broadcasted_iota(jnp.int32, x.shape, 0)
        prev = jnp.where(row == 0, fixed, rolled)
        u_ref[...] = x + (prev - x) * mu_ref[...]
        carry[...] = x[tr - SUBLANES:]

    return pl.pallas_call(
        body, name="shift_fwd", grid=(n_rows // tr,),
        in_specs=[_row_spec(tr, p), _full_spec(mu)],
        out_specs=_row_spec(tr, p),
        out_shape=jax.ShapeDtypeStruct(p.shape, F32),
        scratch_shapes=[pltpu.VMEM((SUBLANES, w), F32)],
        compiler_params=_params(),
    )(p, mu)


def shift_bwd(du, p, mu, tr):
    n_rows, w = p.shape
    nb = n_rows // tr

    def body(du_ref, p_ref, mu_ref, dp_ref, dmu_ref, carry):
        @pl.when(pl.program_id(0) == 0)
        def _():
            carry[...] = jnp.zeros_like(carry)
            dmu_ref[...] = jnp.zeros_like(dmu_ref)

        d = du_ref[...]
        rolled = pltpu.roll(d, tr - 1, 0)
        tail = pltpu.roll(carry[...], SUBLANES - 1, 0)
        fixed = jnp.concatenate([rolled[:tr - SUBLANES], tail], axis=0)
        row = lax.broadcasted_iota(jnp.int32, d.shape, 0)
        nxt = jnp.where(row == tr - 1, fixed, rolled)
        mu_v = mu_ref[...]
        dp_ref[...] = d * (1.0 - mu_v) + nxt * mu_v
        dmu_ref[...] += jnp.sum(p_ref[...] * (nxt - d), axis=0, keepdims=True)
        carry[...] = d[:SUBLANES]

    rev = lambda i: (nb - 1 - i, 0)
    return pl.pallas_call(
        body, name="shift_bwd", grid=(nb,),
        in_specs=[pl.BlockSpec((tr, w), rev), pl.BlockSpec((tr, w), rev), _full_spec(mu)],
        out_specs=[pl.BlockSpec((tr, w), rev), _full_spec(mu)],
        out_shape=[jax.ShapeDtypeStruct(p.shape, F32), jax.ShapeDtypeStruct(mu.shape, F32)],
        scratch_shapes=[pltpu.VMEM((SUBLANES, w), F32)],
        compiler_params=_params(),
    )(du, p, mu)


ATT_T = 256


def _att_rows(j):
    return pl.ds(pl.multiple_of(j * ATT_T, ATT_T), ATT_T)


def _att_prep(kv_ref, kpe_ref, kf_scr, vf_scr, n_blocks):
    lane = lax.broadcasted_iota(jnp.int32, (ATT_T, LANES), 1)

    def prep(j, _):
        rows = _att_rows(j)
        kv = kv_ref[rows, :]
        kf_scr[rows, :] = _bf(jnp.where(lane < HEAD, kv, kpe_ref[rows, :]))
        vf_scr[rows, :] = _bf(jnp.where(lane >= HEAD, kv, 0.0))
        return 0

    lax.fori_loop(0, n_blocks, prep, 0)


def _att_diag_mask():
    shift = CHUNK.bit_length() - 1
    qc = jnp.right_shift(lax.broadcasted_iota(jnp.int32, (ATT_T, ATT_T), 0), shift)
    kc = jnp.right_shift(lax.broadcasted_iota(jnp.int32, (ATT_T, ATT_T), 1), shift)
    return kc <= qc


def _wide(x):
    return jnp.concatenate([x] * (ATT_T // LANES), axis=1)


def attn_fwd(q, kv, kpe):
    seq = q.shape[0]
    nb = seq // ATT_T
    assert seq % (2 * ATT_T) == 0, "blocks are taken two per trip"
    scale = MLA_QK ** -0.5

    def body(q_ref, kv_ref, kpe_ref, o_ref, lse_ref, kf_scr, vf_scr):
        _att_prep(kv_ref, kpe_ref, kf_scr, vf_scr, nb)
        mask = _att_diag_mask()

        def scores(qb, kj):
            return _dot(qb, kf_scr[_att_rows(kj), :], 1, 1) * scale

        def update(s, kj, carry, masked):
            m, l, acc = carry
            if masked:
                s = jnp.where(mask, s, NEG)
            m_new = jnp.maximum(m, jnp.broadcast_to(jnp.max(s, -1, keepdims=True), m.shape))
            alpha = jnp.exp(m - m_new)
            p = jnp.exp(s - _wide(m_new))
            l = alpha * l + jnp.broadcast_to(jnp.sum(p, -1, keepdims=True), l.shape)
            acc = alpha * acc + _dot(_bf(p), vf_scr[_att_rows(kj), :], 1, 0)
            return m_new, l, acc

        def finish(rows, carry):
            m, l, acc = carry
            o_ref[rows, :] = acc / l
            lse_ref[rows, :] = m + jnp.log(l)

        def q_pair(qp, _):
            rows_a, rows_b = _att_rows(2 * qp), _att_rows(2 * qp + 1)
            qa, qb = _bf(q_ref[rows_a, :]), _bf(q_ref[rows_b, :])
            init = (jnp.full((ATT_T, LANES), NEG, F32), jnp.zeros((ATT_T, LANES), F32),
                    jnp.zeros((ATT_T, LANES), F32))

            def trip(kj, c):
                ca, cb, sa, sb = c
                sa_next, sb_next = scores(qa, kj + 1), scores(qb, kj + 1)
                return update(sa, kj, ca, False), update(sb, kj, cb, False), sa_next, sb_next

            ca, cb, sa, sb = lax.fori_loop(0, 2 * qp, trip, (init, init, scores(qa, 0), scores(qb, 0)))
            sb_last = scores(qb, 2 * qp + 1)
            ca = update(sa, 2 * qp, ca, True)
            cb = update(sb_last, 2 * qp + 1, update(sb, 2 * qp, cb, False), True)
            finish(rows_a, ca)
            finish(rows_b, cb)
            return 0

        lax.fori_loop(0, nb // 2, q_pair, 0)

    head = pl.BlockSpec((seq, LANES), lambda h: (0, h))
    return pl.pallas_call(
        body, name="attn_fwd", grid=(HEADS,),
        in_specs=[head, head, pl.BlockSpec((seq, LANES), lambda h: (0, 0))],
        out_specs=[head, head],
        out_shape=[jax.ShapeDtypeStruct((seq, HEADS * LANES), F32)] * 2,
        scratch_shapes=[pltpu.VMEM((seq, LANES), BF16)] * 2,
        compiler_params=_params(),
    )(q, kv, kpe)


def attn_bwd(q, kv, kpe, o, lse, do):
    seq = q.shape[0]
    nb = seq // ATT_T
    assert seq % (2 * ATT_T) == 0, "blocks are taken two per trip"
    scale = MLA_QK ** -0.5

    def body(q_ref, kv_ref, kpe_ref, o_ref, lse_ref, do_ref, dq_ref, dkv_ref, dkpe_ref,
             kf_scr, vf_scr, qb_scr, dob_scr, dsum):
        lane = lax.broadcasted_iota(jnp.int32, (ATT_T, LANES), 1)

        @pl.when(pl.program_id(0) == 0)
        def _():
            dkpe_ref[...] = jnp.zeros_like(dkpe_ref)

        dq_ref[...] = jnp.zeros_like(dq_ref)
        _att_prep(kv_ref, kpe_ref, kf_scr, vf_scr, nb)

        def pre(j, _):
            rows = _att_rows(j)
            d = do_ref[rows, :]
            qb_scr[rows, :] = _bf(q_ref[rows, :])
            dob_scr[rows, :] = _bf(d)
            dsum[rows, :] = jnp.broadcast_to(jnp.sum(d * o_ref[rows, :], -1, keepdims=True), (ATT_T, LANES))
            return 0

        lax.fori_loop(0, nb, pre, 0)
        mask = _att_diag_mask()

        def front(kf, vf, qi):
            rows = _att_rows(qi)
            return _dot(qb_scr[rows, :], kf, 1, 1), _dot(dob_scr[rows, :], vf, 1, 1)

        def back(kf, qi, fr, carry, masked):
            s, dp = fr
            dk, dv = carry
            rows = _att_rows(qi)
            qb, dob = qb_scr[rows, :], dob_scr[rows, :]
            p = jnp.exp(s * scale - _wide(lse_ref[rows, :]))
            if masked:
                p = jnp.where(mask, p, 0.0)
            ds = _bf(p * (dp - _wide(dsum[rows, :])) * scale)
            return (dk + _dot(ds, qb, 0, 0), dv + _dot(_bf(p), dob, 0, 0)), _dot(ds, kf, 1, 0)

        def store(krows, carry):
            dk, dv = carry
            dkv_ref[krows, :] = jnp.where(lane < HEAD, dk, dv)
            dkpe_ref[krows, :] += jnp.where((lane >= HEAD) & (lane < MLA_QK), dk, 0.0)

        def k_pair(kp, _):
            ka, kb = 2 * kp, 2 * kp + 1
            rows_a, rows_b = _att_rows(ka), _att_rows(kb)
            kfa, vfa, kfb, vfb = kf_scr[rows_a, :], vf_scr[rows_a, :], kf_scr[rows_b, :], vf_scr[rows_b, :]
            zero = jnp.zeros((ATT_T, LANES), F32)
            ca, dq_a = back(kfa, ka, front(kfa, vfa, ka), (zero, zero), True)
            dq_ref[rows_a, :] += dq_a
            ca, dq_a = back(kfa, kb, front(kfa, vfa, kb), ca, False)
            cb, dq_b = back(kfb, kb, front(kfb, vfb, kb), (zero, zero), True)
            dq_ref[rows_b, :] += dq_a + dq_b

            def both(qi, c):
                ca, cb, fa, fb = c
                nxt = jnp.minimum(qi + 1, nb - 1)
                fa_next, fb_next = front(kfa, vfa, nxt), front(kfb, vfb, nxt)
                ca, dq_a = back(kfa, qi, fa, ca, False)
                cb, dq_b = back(kfb, qi, fb, cb, False)
                dq_ref[_att_rows(qi), :] += dq_a + dq_b
                return ca, cb, fa_next, fb_next

            first = jnp.minimum(kb + 1, nb - 1)
            ca, cb, _, _ = lax.fori_loop(kb + 1, nb, both, (ca, cb, front(kfa, vfa, first), front(kfb, vfb, first)))
            store(rows_a, ca)
            store(rows_b, cb)
            return 0

        lax.fori_loop(0, nb // 2, k_pair, 0)

    head = pl.BlockSpec((seq, LANES), lambda h: (0, h))
    shared = pl.BlockSpec((seq, LANES), lambda h: (0, 0))
    return pl.pallas_call(
        body, name="attn_bwd", grid=(HEADS,),
        in_specs=[head, head, shared, head, head, head],
        out_specs=[head, head, shared],
        out_shape=[jax.ShapeDtypeStruct((seq, HEADS * LANES), F32)] * 2
        + [jax.ShapeDtypeStruct((seq, LANES), F32)],
        scratch_shapes=[pltpu.VMEM((seq, LANES), BF16)] * 4 + [pltpu.VMEM((seq, LANES), F32)],
        compiler_params=_params(),
    )(q, kv, kpe, o, lse, do)


WKV_TB = 128
WKV_GROUP = SUBLANES
WKV_HALF = WIDTH // 2


def _wkv_consts():
    row = lax.broadcasted_iota(jnp.int32, (HEAD, WKV_HALF), 0)
    lane = lax.broadcasted_iota(jnp.int32, (HEAD, WKV_HALF), 1)
    diag = row == jnp.bitwise_and(lane, HEAD - 1)
    sub = lax.broadcasted_iota(jnp.int32, (WKV_GROUP, WKV_HALF), 0)
    return diag, sub


def _halves(x):
    return [x[:, :WKV_HALF], x[:, WKV_HALF:]]


def _diag_rows(row, diag):
    return _bf(jnp.where(diag, jnp.broadcast_to(row, diag.shape), 0.0))


def _put_row(tile, row, i, sub):
    return jnp.where(sub == i, jnp.broadcast_to(row, tile.shape), tile)


def _col_sum(x):
    return jnp.sum(x, axis=0, keepdims=True)


def _step(x, i):
    return x[i * HEAD:(i + 1) * HEAD]


def _expand_group(rows8, diag, ones_b):
    lhs = jnp.concatenate([_diag_rows(rows8[i:i + 1], diag) for i in range(WKV_GROUP)], axis=0)
    return _dot(lhs, ones_b, 1, 0)


def _head_dots(prods, ones_b, sub):
    tile = jnp.zeros((WKV_GROUP, WKV_HALF), F32)
    for i, p in enumerate(prods):
        tile = _put_row(tile, p, i, sub)
    res = _exact_dot(tile, ones_b, 0)
    return [res[i:i + 1] for i in range(len(prods))]


def _diag_group(x, diag, sub):
    out = jnp.zeros((WKV_GROUP, WKV_HALF), F32)
    for i in range(WKV_GROUP):
        out = _put_row(out, _col_sum(jnp.where(diag, _step(x, i), 0.0)), i, sub)
    return out


def wkv_fwd(r, w, k, v, a, b, ones_half):
    seq = r.shape[0]

    def body(r_ref, w_ref, k_ref, v_ref, a_ref, b_ref, ones_ref, y_ref, st_ref, s_scr):
        @pl.when(pl.program_id(0) == 0)
        def _():
            s_scr[...] = jnp.zeros_like(s_scr)

        ones_b = ones_ref[...]
        diag, sub = _wkv_consts()

        ng = WKV_TB // WKV_GROUP
        last = WKV_GROUP - 2

        def rows_of(g):
            return pl.ds(pl.multiple_of(g * WKV_GROUP, WKV_GROUP), WKV_GROUP)

        def pair_rows(x8, t):
            return jnp.concatenate([_diag_rows(x8[t:t + 1], diag), _diag_rows(x8[t + 1:t + 2], diag)], axis=0)

        def put_y(g, pairs_y):
            tile = _halves(y_ref[rows_of(g), :])
            for hf in range(2):
                tile[hf] = _put_row(_put_row(tile[hf], pairs_y[hf][0], last, sub), pairs_y[hf][1], last + 1, sub)
            y_ref[rows_of(g), :] = jnp.concatenate(tile, axis=1)

        def read_out(yexp):
            return _col_sum(jnp.where(diag, yexp[:HEAD], 0.0)), _col_sum(jnp.where(diag, yexp[HEAD:], 0.0))

        def group(g, carry):
            state, v_cur, read = (list(c) for c in carry)
            base = pl.multiple_of(g * WKV_GROUP, WKV_GROUP)
            rows = rows_of(g)
            r8, w8, k8, v8, a8, b8 = (_halves(ref[rows, :]) for ref in (r_ref, w_ref, k_ref, v_ref, a_ref, b_ref))
            v_after = _halves(v_ref[rows_of(jnp.minimum(g + 1, ng - 1)), :])
            evens = range(0, WKV_GROUP, 2)
            dots = [_head_dots([b8[hf][t:t + 1] * a8[hf][t + 1:t + 2] for t in evens]
                               + [k8[hf][t:t + 1] * a8[hf][t + 1:t + 2] for t in evens], ones_b, sub) for hf in range(2)]
            y8 = [jnp.zeros((WKV_GROUP, WKV_HALF), F32)] * 2
            y_before = [None, None]
            for t in evens:
                s0, s1 = slice(t, t + 1), slice(t + 1, t + 2)
                both = []
                for hf in range(2):
                    s_in = state[hf]
                    v_next = pair_rows(v8[hf], t + 2) if t < last else pair_rows(v_after[hf], 0)
                    res = _dot(jnp.concatenate([_bf(s_in * a8[hf][s0]), _bf(s_in * (w8[hf][s0] * a8[hf][s1])),
                                                v_next, read[hf]], axis=0), ones_b, 1, 0)
                    sa0, v0, v1 = res[:HEAD], v_cur[hf][:HEAD], v_cur[hf][HEAD:]
                    st0 = s_in * w8[hf][s0] + sa0 * b8[hf][s0] + v0 * k8[hf][s0]
                    sa1 = res[HEAD:2 * HEAD] + sa0 * dots[hf][t // 2] + v0 * dots[hf][WKV_GROUP // 2 + t // 2]
                    st1 = st0 * w8[hf][s1] + sa1 * b8[hf][s1] + v1 * k8[hf][s1]
                    both.append((st0, st1))
                    state[hf], v_cur[hf] = st1, res[2 * HEAD:4 * HEAD]
                    read[hf] = jnp.concatenate([_bf(st0 * r8[hf][s0]), _bf(st1 * r8[hf][s1])], axis=0)
                    ya, yb = read_out(res[4 * HEAD:])
                    if t == 0:
                        y_before[hf] = (ya, yb)
                    else:
                        y8[hf] = _put_row(_put_row(y8[hf], ya, t - 2, sub), yb, t - 1, sub)
                for j in range(2):
                    st_ref[base + t + j] = jnp.concatenate([both[0][j], both[1][j]], axis=1)
            y_ref[rows, :] = jnp.concatenate(y8, axis=1)
            put_y(jnp.maximum(g - 1, 0), y_before)
            return tuple(state), tuple(v_cur), tuple(read)

        v_first = _halves(v_ref[rows_of(0), :])
        init = (tuple(_halves(s_scr[...])),
                tuple(_dot(pair_rows(v_first[hf], 0), ones_b, 1, 0) for hf in range(2)),
                tuple(jnp.zeros((2 * HEAD, WKV_HALF), BF16) for _ in range(2)))
        fin, _, read = lax.fori_loop(0, ng, group, init)
        put_y(ng - 1, [read_out(_dot(read[hf], ones_b, 1, 0)) for hf in range(2)])
        s_scr[...] = jnp.concatenate(fin, axis=1)

    vec = pl.BlockSpec((WKV_TB, WIDTH), lambda i: (i, 0))
    return pl.pallas_call(
        body, name="wkv_fwd", grid=(seq // WKV_TB,),
        in_specs=[vec] * 6 + [_full_spec(ones_half)],
        out_specs=[vec, pl.BlockSpec((WKV_TB, HEAD, WIDTH), lambda i: (i, 0, 0))],
        out_shape=[jax.ShapeDtypeStruct((seq, WIDTH), F32), jax.ShapeDtypeStruct((seq, HEAD, WIDTH), F32)],
        scratch_shapes=[pltpu.VMEM((HEAD, WIDTH), F32)],
        compiler_params=_params(),
    )(r, w, k, v, a, b, ones_half)


def wkv_bwd(r, w, k, v, a, b, dy, states, ones_half):
    seq = r.shape[0]
    nb = seq // WKV_TB
    ng = WKV_TB // WKV_GROUP

    def body(r_ref, w_ref, k_ref, v_ref, a_ref, b_ref, dy_ref, st_ref, halo_ref, ones_ref,
             dr_ref, dw_ref, dk_ref, dv_ref, da_ref, db_ref, ds_scr):
        blk = nb - 1 - pl.program_id(0)

        @pl.when(pl.program_id(0) == 0)
        def _():
            ds_scr[...] = jnp.zeros_like(ds_scr)

        ones_b = ones_ref[...]
        diag, sub = _wkv_consts()
        before_block = jnp.where(blk == 0, 0.0, halo_ref[0])

        def rows_of(g):
            return pl.ds(pl.multiple_of(g * WKV_GROUP, WKV_GROUP), WKV_GROUP)

        def expand_rows(hf, dy8, v8, a8, t, s_t, s_u):
            s1, s0 = slice(t, t + 1), slice(t - 1, t)
            return jnp.concatenate([_diag_rows(dy8[hf][s1], diag), _diag_rows(dy8[hf][s0], diag),
                                    _diag_rows(v8[hf][s1], diag), _diag_rows(v8[hf][s0], diag),
                                    _bf(s_t[hf] * a8[hf][s1]), _bf(s_u[hf] * a8[hf][s0])], axis=0)

        def read_out(x):
            return _col_sum(jnp.where(diag, x[:HEAD], 0.0)), _col_sum(jnp.where(diag, x[HEAD:], 0.0))

        def put_dv(g, pair_dv):
            tile = _halves(dv_ref[rows_of(g), :])
            for hf in range(2):
                tile[hf] = _put_row(_put_row(tile[hf], pair_dv[hf][0], 1, sub), pair_dv[hf][1], 0, sub)
            dv_ref[rows_of(g), :] = jnp.concatenate(tile, axis=1)

        def group(gg, carry):
            dstate, e_cur, dv_pend = (list(c) for c in carry)
            g = ng - 1 - gg
            base = pl.multiple_of(g * WKV_GROUP, WKV_GROUP)
            rows = rows_of(g)
            r8, w8, k8, v8, a8, b8, dy8 = (
                _halves(ref[rows, :]) for ref in (r_ref, w_ref, k_ref, v_ref, a_ref, b_ref, dy_ref))
            g_next = jnp.maximum(g - 1, 0)
            base_next = pl.multiple_of(g_next * WKV_GROUP, WKV_GROUP)
            dy8n, v8n, a8n = (_halves(ref[rows_of(g_next), :]) for ref in (dy_ref, v_ref, a_ref))
            zero8 = jnp.zeros((WKV_GROUP, WKV_HALF), F32)
            out = {n: [zero8, zero8] for n in ("dr", "dw", "dk", "dv", "da", "db")}
            before_group = jnp.where(g == 0, before_block, st_ref[jnp.maximum(base - 1, 0)])
            states = [_halves(before_group)] + [_halves(st_ref[base + i]) for i in range(WKV_GROUP)]
            odds = range(1, WKV_GROUP, 2)
            dots = [_head_dots([a8[hf][t:t + 1] * b8[hf][t - 1:t] for t in odds]
                               + [r8[hf][t - 1:t] * b8[hf][t - 1:t] for t in odds], ones_b, sub) for hf in range(2)]
            dv_after = [None, None]

            def emit(hf, i, d_i, dsa_i, dy_i, v_i, sa_i):
                s_p, s_t = states[i][hf], states[i + 1][hf]
                for n, val in (("dr", _col_sum(s_t * dy_i)), ("dw", _col_sum(d_i * s_p)), ("db", _col_sum(d_i * sa_i)),
                               ("da", _col_sum(s_p * dsa_i)), ("dk", _col_sum(d_i * v_i))):
                    out[n][hf] = _put_row(out[n][hf], val, i, sub)

            for t in reversed(odds):
                s1, s0 = slice(t, t + 1), slice(t - 1, t)
                for hf in range(2):
                    dy1, dy0, v1, v0, sa1, sa0 = (_step(e_cur[hf], j) for j in range(6))
                    d1 = dstate[hf] + dy1 * r8[hf][s1]
                    if t > 1:
                        nxt = expand_rows(hf, dy8, v8, a8, t - 2, states[t - 2], states[t - 3])
                    else:
                        nxt = expand_rows(hf, dy8n, v8n, a8n, WKV_GROUP - 1, _halves(st_ref[base_next + WKV_GROUP - 2]),
                                          _halves(st_ref[base_next + WKV_GROUP - 3]))
                    res = _dot(jnp.concatenate([_bf(d1 * b8[hf][s1]), _bf(d1 * (w8[hf][s1] * b8[hf][s0])),
                                                nxt, dv_pend[hf]], axis=0), ones_b, 1, 0)
                    dsa1 = res[:HEAD]
                    d0 = d1 * w8[hf][s1] + dsa1 * a8[hf][s1] + dy0 * r8[hf][s0]
                    dsa0 = res[HEAD:2 * HEAD] + dsa1 * dots[hf][t // 2] + dy0 * dots[hf][WKV_GROUP // 2 + t // 2]
                    dstate[hf] = d0 * w8[hf][s0] + dsa0 * a8[hf][s0]
                    e_cur[hf] = res[2 * HEAD:8 * HEAD]
                    dv_pend[hf] = jnp.concatenate([_bf(d1 * k8[hf][s1]), _bf(d0 * k8[hf][s0])], axis=0)
                    emit(hf, t, d1, dsa1, dy1, v1, sa1)
                    emit(hf, t - 1, d0, dsa0, dy0, v0, sa0)
                    dv_a, dv_b = read_out(res[8 * HEAD:])
                    if t == WKV_GROUP - 1:
                        dv_after[hf] = (dv_a, dv_b)
                    else:
                        out["dv"][hf] = _put_row(_put_row(out["dv"][hf], dv_a, t + 2, sub), dv_b, t + 1, sub)
            for ref, n in ((dr_ref, "dr"), (dw_ref, "dw"), (dk_ref, "dk"), (dv_ref, "dv"), (da_ref, "da"), (db_ref, "db")):
                ref[rows, :] = jnp.concatenate(out[n], axis=1)
            put_dv(jnp.minimum(g + 1, ng - 1), dv_after)
            return tuple(dstate), tuple(e_cur), tuple(dv_pend)

        top = rows_of(ng - 1)
        dy8t, v8t, a8t = (_halves(ref[top, :]) for ref in (dy_ref, v_ref, a_ref))
        s_t, s_u = _halves(st_ref[WKV_TB - 2]), _halves(st_ref[WKV_TB - 3])
        init = (tuple(_halves(ds_scr[...])),
                tuple(_dot(expand_rows(hf, dy8t, v8t, a8t, WKV_GROUP - 1, s_t, s_u), ones_b, 1, 0) for hf in range(2)),
                tuple(jnp.zeros((2 * HEAD, WKV_HALF), BF16) for _ in range(2)))
        fin, _, dv_pend = lax.fori_loop(0, ng, group, init)
        put_dv(0, [read_out(_dot(dv_pend[hf], ones_b, 1, 0)) for hf in range(2)])
        ds_scr[...] = jnp.concatenate(fin, axis=1)

    vec = pl.BlockSpec((WKV_TB, WIDTH), lambda i: (nb - 1 - i, 0))
    return pl.pallas_call(
        body, name="wkv_bwd", grid=(nb,),
        in_specs=[vec] * 7 + [
            pl.BlockSpec((WKV_TB, HEAD, WIDTH), lambda i: (nb - 1 - i, 0, 0)),
            pl.BlockSpec((1, HEAD, WIDTH), lambda i: (jnp.maximum((nb - 1 - i) * WKV_TB - 1, 0), 0, 0)),
            _full_spec(ones_half)],
        out_specs=[vec] * 6,
        out_shape=[jax.ShapeDtypeStruct((seq, WIDTH), F32)] * 6,
        scratch_shapes=[pltpu.VMEM((HEAD, WIDTH), F32)],
        compiler_params=_params(),
    )(r, w, k, v, a, b, dy, states, states, ones_half)


def ada_fwd(c8, b_ada, gathered):
    cols = 3 * D_MODEL // 4

    def body(c_ref, b_ref, w_ref, o_ref):
        @pl.when(pl.program_id(1) == 0)
        def _():
            o_ref[...] = jnp.broadcast_to(b_ref[...], o_ref.shape)

        o_ref[...] += mm(_silu(c_ref[...]), w_ref[0])

    return pl.pallas_call(
        body, name="ada_fwd", grid=(4, D_MODEL // PACK_ROWS),
        in_specs=[pl.BlockSpec((SUBLANES, PACK_ROWS), lambda s, i: (0, i)),
                  pl.BlockSpec((1, cols), lambda s, i: (0, s)),
                  pl.BlockSpec((1, PACK_ROWS, cols), lambda s, i: (2 * s, 0, i))],
        out_specs=pl.BlockSpec((SUBLANES, cols), lambda s, i: (0, s)),
        out_shape=jax.ShapeDtypeStruct((SUBLANES, 3 * D_MODEL), F32),
        compiler_params=_params(("arbitrary", "arbitrary")),
    )(c8, b_ada, gathered)


def ada_grad_shard(sc_cols, dada_rows):
    n = len(sc_cols)

    def body(*refs):
        d_ref, o_ref = refs[n], refs[n + 1]
        acc = refs[0][...] * d_ref[0:1, :]
        for b in range(1, n):
            acc = acc + refs[b][...] * d_ref[b:b + 1, :]
        o_ref[...] = acc

    return pl.pallas_call(
        body, name="ada_grad_shard",
        out_shape=jax.ShapeDtypeStruct((sc_cols[0].shape[0], dada_rows.shape[1]), F32),
        compiler_params=pltpu.CompilerParams(vmem_limit_bytes=VMEM_LIMIT),
    )(*sc_cols, dada_rows)


def sum_slots(buf, tr):
    n, rows, cols = buf.shape

    def body(b_ref, o_ref):
        acc = b_ref[0].astype(F32)
        for s in range(1, n):
            acc = acc + b_ref[s].astype(F32)
        o_ref[...] = acc

    return pl.pallas_call(
        body, name="sum_slots", grid=(rows // tr,),
        in_specs=[pl.BlockSpec((n, tr, cols), lambda i: (0, i, 0))],
        out_specs=pl.BlockSpec((tr, cols), lambda i: (i, 0)),
        out_shape=jax.ShapeDtypeStruct((rows, cols), F32),
        compiler_params=_params(),
    )(buf)


def adamw_small(gathered, w, m, v):
    n = gathered.shape[0]

    def body(g_ref, w_ref, m_ref, v_ref, go_ref, d_ref, mo_ref, vo_ref):
        g = g_ref[0]
        for s in range(1, n):
            g = g + g_ref[s]
        go_ref[...] = g
        d_ref[...], mo_ref[...], vo_ref[...] = f_adamw(w_ref[...], g, m_ref[...], v_ref[...])

    return pl.pallas_call(
        body, name="adamw_small",
        out_shape=[jax.ShapeDtypeStruct(w.shape, F32)] * 4,
        compiler_params=pltpu.CompilerParams(vmem_limit_bytes=VMEM_LIMIT),
    )(gathered, w, m, v)


def _coords():
    return lax.axis_index("x"), lax.axis_index("y"), lax.axis_index("c")


def _flip(v, bit):
    return 1 - v if bit else v


def _hbm_call(body, name, out_shape, n_sems, *args):
    any_spec = pl.BlockSpec(memory_space=pl.ANY)
    return pl.pallas_call(
        body, name=name, out_shape=out_shape,
        in_specs=[any_spec] * len(args), out_specs=any_spec,
        scratch_shapes=[pltpu.SemaphoreType.DMA((n_sems,)), pltpu.SemaphoreType.DMA((n_sems,)),
                        pltpu.SemaphoreType.DMA],
    )(*args)


def all_gather8(name, block):
    def body(x_ref, out_ref, send_sems, recv_sems, local_sem):
        x, y, c = _coords()
        me, sibling = (x, y, c), (x, y, 1 - c)
        chips = [(1 - x, y), (x, 1 - y), (1 - x, 1 - y)]

        def slot(px, py, pc):
            return out_ref.at[4 * px + 2 * py + pc]

        def copy(k, blk, to, src=None):
            return pltpu.make_async_remote_copy(
                src_ref=slot(*blk) if src is None else src, dst_ref=slot(*blk),
                send_sem=send_sems.at[k], recv_sem=recv_sems.at[k], device_id=to, device_id_type=MESH)

        mine = pltpu.make_async_copy(x_ref, slot(*me), local_sem)
        mine.start()
        first = [copy(0, me, sibling, src=x_ref)]
        first += [copy(1 + j, me, (*chip, c), src=x_ref) for j, chip in enumerate(chips)]
        for cp in first:
            cp.start()
        passed = [copy(4 + j, (*chip, c), sibling) for j, chip in enumerate(chips)]
        for j, chip in enumerate(chips):
            copy(1 + j, (*chip, c), me).wait_recv()
            passed[j].start()
        copy(0, sibling, me).wait_recv()
        for j, chip in enumerate(chips):
            copy(4 + j, (*chip, 1 - c), me).wait_recv()
        for cp in first + passed:
            cp.wait_send()
        mine.wait()

    return _hbm_call(body, name, jax.ShapeDtypeStruct((N_DEV,) + block.shape, block.dtype), 7, block)


def pair_swap(name, block):
    def body(x_ref, out_ref, send_sems, recv_sems, local_sem):
        x, y, c = _coords()
        cp = pltpu.make_async_remote_copy(
            src_ref=x_ref, dst_ref=out_ref, send_sem=send_sems.at[0], recv_sem=recv_sems.at[0],
            device_id=(x, y, 1 - c), device_id_type=MESH)
        cp.start()
        cp.wait_recv()
        cp.wait_send()

    return _hbm_call(body, name, jax.ShapeDtypeStruct(block.shape, block.dtype), 1, block)


def chip_all_to_all(name, buf):
    def body(x_ref, out_ref, send_sems, recv_sems, local_sem):
        x, y, c = _coords()
        me = 2 * x + y
        mine = pltpu.make_async_copy(x_ref.at[me], out_ref.at[me], local_sem)
        mine.start()
        copies = []
        for k in range(1, 4):
            px, py = _flip(x, k & 2), _flip(y, k & 1)
            copies.append(pltpu.make_async_remote_copy(
                src_ref=x_ref.at[2 * px + py], dst_ref=out_ref.at[me],
                send_sem=send_sems.at[k - 1], recv_sem=recv_sems.at[k - 1],
                device_id=(px, py, c), device_id_type=MESH))
        for cp in copies:
            cp.start()
        for cp in copies:
            cp.wait_recv()
        for cp in copies:
            cp.wait_send()
        mine.wait()

    return _hbm_call(body, name, jax.ShapeDtypeStruct(buf.shape, buf.dtype), 3, buf)


def _col_blocks(a, cols):
    a = jnp.pad(a, ((0, 0), (0, cols - a.shape[1])))
    return [a[i * PACK_ROWS:(i + 1) * PACK_ROWS] for i in range(a.shape[0] // PACK_ROWS)]


def _pack_shard(sh, dtype, with_ada=True):
    lora = jnp.concatenate([sh['w_decay_up'], sh['w_iclr_up']], axis=1)
    misc = jnp.concatenate([sh['w_ukv'], lora, jnp.zeros((LORA, 2 * LANES), lora.dtype)], axis=0)
    blocks = ((_col_blocks(sh['w_ada'], 768) if with_ada else [])
              + _col_blocks(sh['w_in'], 1408) + _col_blocks(sh['w_proj_a'], 256)
              + _col_blocks(sh['w_proj_b'], 256) + [sh['w_out']] + _col_blocks(sh['w_uq'], 256) + [misc])
    return jnp.concatenate([b.astype(dtype) for b in blocks], axis=1)


def _unpack_shard(p, with_ada=True):
    o = [0]

    def take(n_blocks, cols, used):
        blocks = [p[:, o[0] + i * cols:o[0] + (i + 1) * cols] for i in range(n_blocks)]
        o[0] += n_blocks * cols
        return jnp.concatenate(blocks, axis=0)[:, :used]

    out = {'w_ada': take(4, 768, 768)} if with_ada else {}
    out.update({'w_in': take(4, 1408, 1288), 'w_proj_a': take(2, 256, 256),
                'w_proj_b': take(2, 256, 256), 'w_out': take(1, 1024, 1024), 'w_uq': take(1, 256, 192)})
    misc = take(1, 256, 256)
    out['w_ukv'] = misc[:2 * LORA]
    out['w_decay_up'] = misc[2 * LORA:3 * LORA, :LANES]
    out['w_iclr_up'] = misc[2 * LORA:3 * LORA, LANES:]
    return out


def _pack_small(parts):
    flat = jnp.concatenate([p.reshape(-1) for p in parts])
    return jnp.pad(flat, (0, SMALL_ROWS * LANES - flat.shape[0])).reshape(SMALL_ROWS, LANES)


def _unpack_small(packed):
    flat, out, o = packed.reshape(-1), {}, 0
    for name, n in _SMALL:
        out[name] = flat[o:o + n]
        o += n
    return out


def _pad_heads_cols(w, used, left):
    k = w.shape[0]
    return jnp.pad(w.reshape(k, HEADS, used), ((0, 0), (0, 0), (left, LANES - used - left))).reshape(k, HEADS * LANES)


def _unpad_heads_cols(w, used, left):
    k = w.shape[0]
    return w.reshape(k, HEADS, LANES)[:, :, left:left + used].reshape(k, HEADS * used)


def kernel(x, c, positions, w_ada, b_ada, w_in, q_norm_g, w_uq, kv_norm_g, w_ukv, mu_rwkv, w0, w_decay_up, a0, w_iclr_up, k_k, k_a, r_k, gn_g, gn_b, w_proj_a, w_proj_b, w_out, post_g, post_b, loss_target, m_w_ada, m_b_ada, m_w_in, m_q_norm_g, m_w_uq, m_kv_norm_g, m_w_ukv, m_mu_rwkv, m_w0, m_w_decay_up, m_a0, m_w_iclr_up, m_k_k, m_k_a, m_r_k, m_gn_g, m_gn_b, m_w_proj_a, m_w_proj_b, m_w_out, m_post_g, m_post_b, v_w_ada, v_b_ada, v_w_in, v_q_norm_g, v_w_uq, v_kv_norm_g, v_w_ukv, v_mu_rwkv, v_w0, v_w_decay_up, v_a0, v_w_iclr_up, v_k_k, v_k_a, v_r_k, v_gn_g, v_gn_b, v_w_proj_a, v_w_proj_b, v_w_out, v_post_g, v_post_b):
    given = dict(locals())
    seq = x.shape[1]
    my_c = lax.axis_index("c")

    shard_names = [n for n, _, _ in _SHARDED]
    w_pack = _pack_shard({n: given[n][0] for n in shard_names}, BF16)
    my_half = lax.dynamic_slice_in_dim(w_pack, my_c * HALF_COLS, HALF_COLS, 1)
    gathered = all_gather8("gather_weights", my_half)
    shards = [_unpack_shard(jnp.concatenate([gathered[2 * s], gathered[2 * s + 1]], axis=1)) for s in range(4)]
    full = {n: jnp.concatenate([sh[n] for sh in shards], axis=ax) for n, _, ax in _SHARDED}

    wi = full['w_in']
    zcol = lambda n: jnp.zeros((D_MODEL, n), BF16)
    w_g1 = jnp.concatenate([wi[:, :384], zcol(HEAD), wi[:, 384:416], zcol(LANES - MLA_QK),
                            _pad_heads_cols(wi[:, 416:928], HEAD, HEAD)], axis=1)
    w_g2 = wi[:, 928:3104]
    w_g3 = wi[:, 3104:5152]
    w_uq_p = _pad_heads_cols(full['w_uq'], MLA_QK, 0)
    w_pa_p = jnp.pad(full['w_proj_a'].reshape(HEADS, HEAD, D_MODEL), ((0, 0), (HEAD, 0), (0, 0))).reshape(HEADS * LANES, D_MODEL)
    zl = jnp.zeros((LORA, WIDTH), BF16)
    w_lora = jnp.concatenate([jnp.concatenate([full['w_decay_up'], zl], 1),
                              jnp.concatenate([zl, full['w_iclr_up']], 1)], 0)

    hd = np.arange(WIDTH) // HEAD
    ones_blocks = jnp.asarray(hd[:, None] == hd[None, :], BF16)
    ones_half = ones_blocks[:WKV_HALF, :WKV_HALF]
    perm_np = np.zeros((LANES, LANES), np.float32)
    for d in range(MLA_ROPE // 2):
        perm_np[HEAD + 16 + d, HEAD + d] = -1.0
        perm_np[HEAD + d, HEAD + 16 + d] = 1.0
    perm = jnp.asarray(perm_np, BF16)
    inv = ROPE_THETA ** (-jnp.arange(0, MLA_ROPE, 2, dtype=F32) / MLA_ROPE)
    ang = positions[0].astype(F32)[:, None] * inv
    cos_a, sin_a = jnp.cos(ang), jnp.sin(ang)
    cs = jnp.concatenate([jnp.ones((seq, HEAD), F32), cos_a, cos_a, jnp.zeros((seq, LANES - MLA_QK), F32),
                          jnp.zeros((seq, HEAD), F32), sin_a, sin_a, jnp.zeros((seq, LANES - MLA_QK), F32)], axis=1)

    x2, tgt = x[0], loss_target[0]
    r_k2 = r_k.reshape(1, WIDTH)

    c8 = jnp.broadcast_to(c, (SUBLANES, D_MODEL))
    ada = ada_fwd(c8, b_ada, gathered)[:1]
    shift, scale, gate = ada[:, :D_MODEL], ada[:, D_MODEL:2 * D_MODEL], ada[:, 2 * D_MODEL:]

    f_in1, f_in2, f_in3 = _make_f_in((512, 1024)), _make_f_in((SHIFT_W, WIDTH)), _make_f_in((1024, 1024))
    tr = min(256, seq)
    p_mla, gpa = row_fwd("in1_fwd", f_in1, [x2], [shift, scale, w_g1], [], [512, 1024], tr)
    p_rwkv, gpb = row_fwd("in2_fwd", f_in2, [x2], [shift, scale, w_g2], [], [SHIFT_W, WIDTH], tr)
    ma, mb = row_fwd("in3_fwd", f_in3, [x2], [shift, scale, w_g3], [], [1024, 1024], tr)

    mla_par = [q_norm_g, kv_norm_g, w_uq_p, full['w_ukv']]
    q_f, kv_f, kpe = row_fwd("mla_pre_fwd", f_mla_pre, [p_mla, cs], mla_par, [perm], [1024, 1024, LANES], tr)
    ya, lse = attn_fwd(q_f, kv_f, kpe)

    u = shift_fwd(p_rwkv, mu_rwkv, tr)
    pre_par = [w0, a0, k_k, k_a, w_lora]
    rr, wd, k2, vv, an, bb = row_fwd("rwkv_pre_fwd", f_rwkv_pre, [u], pre_par, [ones_blocks], [WIDTH] * 6, tr)
    y_wkv, states = wkv_fwd(rr, wd, k2, vv, an, bb, ones_half)
    post_b_par = [r_k2, gn_g, gn_b]
    yb, = row_fwd("rwkv_post_fwd", f_rwkv_post, [y_wkv, rr, k2, vv], post_b_par, [ones_blocks], [WIDTH], tr)

    merge_rows, merge_par = [ya, gpa, yb, gpb, ma, mb], [w_pa_p, full['w_proj_b']]
    merged, = row_fwd("merge_fwd", f_merge, merge_rows, merge_par, [], [D_MODEL], tr)
    loss_rows, loss_par = [merged, x2, tgt], [gate, post_g, post_b, full['w_out']]
    lrows, = row_fwd("loss_fwd", f_loss, loss_rows, loss_par, [], [LANES], tr)
    loss = lax.psum(jnp.sum(lrows[:, 0]), ("x", "y", "c"))

    dl = jnp.broadcast_to((jnp.arange(LANES) == 0).astype(F32), (seq, LANES))
    (dmerged, dx_res), (dgate, dpost_g, dpost_b, dw_out) = row_bwd("loss_bwd", f_loss, loss_rows, 2, loss_par, [], [dl], tr)
    (dya, dgpa, dyb, dgpb, dma, dmb), (dw_pa_p, dw_pb) = row_bwd(
        "merge_bwd", f_merge, merge_rows, 6, merge_par, [], [dmerged], tr)

    (dy_wkv, dr1, dk1, dv1), (dr_k, dgn_g, dgn_b) = row_bwd(
        "rwkv_post_bwd", f_rwkv_post, [y_wkv, rr, k2, vv], 4, post_b_par, [ones_blocks], [dyb], tr)
    dr2, dwd, dk2, dv2, dan, dbb = wkv_bwd(rr, wd, k2, vv, an, bb, dy_wkv, states, ones_half)
    (du,), (dw0, da0, dk_k, dk_a, dw_lora) = row_bwd(
        "rwkv_pre_bwd", f_rwkv_pre, [u], 1, pre_par, [ones_blocks],
        [(dr1, dr2), dwd, (dk1, dk2), (dv1, dv2), dan, dbb], tr)
    dp_rwkv, dmu = shift_bwd(du, p_rwkv, mu_rwkv, tr)

    dq_f, dkv_f, dkpe = attn_bwd(q_f, kv_f, kpe, ya, lse, dya)
    (dp_mla,), (dqg, dkvg, dw_uq_p, dw_ukv) = row_bwd(
        "mla_pre_bwd", f_mla_pre, [p_mla, cs], 1, mla_par, [perm], [dq_f, dkv_f, dkpe], tr)

    (dx1,), (dsh1, dsc1, dw_g1) = row_bwd("in1_bwd", f_in1, [x2], 1, [shift, scale, w_g1], [], [dp_mla, dgpa], tr, [dx_res])
    (dx2,), (dsh2, dsc2, dw_g2) = row_bwd("in2_bwd", f_in2, [x2], 1, [shift, scale, w_g2], [], [dp_rwkv, dgpb], tr, [dx1])
    (dx3,), (dsh3, dsc3, dw_g3) = row_bwd("in3_bwd", f_in3, [x2], 1, [shift, scale, w_g3], [], [dma, dmb], tr, [dx2])
    grad_x = dx3[None]

    dada = jnp.concatenate([dsh1 + dsh2 + dsh3, dsc1 + dsc2 + dsc3, dgate], axis=1)
    local = {
        'w_in': jnp.concatenate([dw_g1[:, :384], dw_g1[:, 448:480], _unpad_heads_cols(dw_g1[:, 512:], HEAD, HEAD),
                                 dw_g2, dw_g3], axis=1),
        'w_uq': _unpad_heads_cols(dw_uq_p, MLA_QK, 0),
        'w_ukv': dw_ukv,
        'w_decay_up': dw_lora[:LORA, :WIDTH],
        'w_iclr_up': dw_lora[LORA:, WIDTH:],
        'w_proj_a': dw_pa_p.reshape(HEADS, LANES, D_MODEL)[:, HEAD:].reshape(WIDTH, D_MODEL),
        'w_proj_b': dw_pb,
        'w_out': dw_out,
    }
    small_local = {'b_ada': dada, 'q_norm_g': dqg, 'kv_norm_g': dkvg, 'mu_rwkv': dmu, 'w0': dw0, 'a0': da0,
                   'k_k': dk_k, 'k_a': dk_a, 'r_k': dr_k, 'gn_g': dgn_g, 'gn_b': dgn_b,
                   'post_g': dpost_g, 'post_b': dpost_b}

    def shard_of(g, axis, s):
        n = g.shape[axis] // 4
        return lax.slice_in_dim(g, s * n, (s + 1) * n, axis=axis)

    packed = jnp.stack([_pack_shard({n: shard_of(local[n], ax, s) for n, _, ax in _SHARDED if n != 'w_ada'}, F32, False)
                        for s in range(4)])
    keep = lax.dynamic_slice_in_dim(packed, my_c * GRAD_HALF, GRAD_HALF, 2).reshape(4 * PACK_ROWS, GRAD_HALF)
    give = lax.dynamic_slice_in_dim(packed, (1 - my_c) * GRAD_HALF, GRAD_HALF, 2).reshape(4 * PACK_ROWS, GRAD_HALF)
    pair_sum, = row_fwd("pair_sum", lambda p, q: (p + q,), [keep, pair_swap("swap_halves", give)], [], [],
                        [GRAD_HALF], PACK_ROWS // 2, BF16)
    received = chip_all_to_all("exchange_grads", pair_sum.reshape(4, PACK_ROWS, GRAD_HALF))
    my_sum = sum_slots(received, PACK_ROWS // 2)
    other_sum = pair_swap("swap_sums", my_sum)
    halves = [jnp.where(my_c == 0, my_sum, other_sum), jnp.where(my_c == 0, other_sum, my_sum)]
    g_shard = _unpack_shard(jnp.concatenate(halves, axis=1), False)

    small_pack = lambda d, extra=(): _pack_small([d[n] for n, _ in _SMALL] + list(extra))
    small_all = all_gather8("gather_small", small_pack(small_local, [c * jax.nn.sigmoid(c)]))
    sc_all = small_all[:, SMALL_USED:SMALL_USED + D_MODEL // LANES].reshape(N_DEV, D_MODEL)
    dada_all = small_all[:, :3 * D_MODEL // LANES].reshape(N_DEV, 3 * D_MODEL)
    my_cols = lax.dynamic_slice_in_dim(dada_all, (2 * lax.axis_index("x") + lax.axis_index("y")) * 768, 768, 1)
    g_shard['w_ada'] = ada_grad_shard([sc_all[b].reshape(D_MODEL, 1) for b in range(N_DEV)], my_cols)

    big = [{}, {}, {}, {}]
    for n in shard_names:
        w2, m2, v2 = given[n][0], given['m_' + n][0], given['v_' + n][0]
        cols = w2.shape[1]
        outs = row_fwd("adamw_" + n, f_adamw, [w2, g_shard[n], m2, v2], [], [], [cols] * 3, min(256, w2.shape[0]))
        for dst, val in zip(big, (g_shard[n], *outs)):
            dst[n] = val

    small_out = adamw_small(small_all, small_pack({n: given[n] for n, _ in _SMALL}),
                            small_pack({n: given['m_' + n] for n, _ in _SMALL}),
                            small_pack({n: given['v_' + n] for n, _ in _SMALL}))

    results = []
    for big_k, packed_small in zip(big, small_out):
        small = _unpack_small(packed_small)
        results.append([(big_k[n] if n in big_k else small[n]).reshape(given[n].shape) for n in _WEIGHTS])
    return (loss, grad_x, *results[0], *results[1], *results[2], *results[3])
```

```python
from typing import NamedTuple

import numpy as np
import jax
import jax.numpy as jnp
from jax import lax
from jax.experimental import pallas as pl
from jax.experimental.pallas import tpu as pltpu

F32 = jnp.float32
BF16 = jnp.bfloat16

D_MODEL = 1024
LN_EPS = 1e-5
RMS_EPS = 1e-6
GN_EPS = 64e-5
HEADS = 8
HEAD = 64
MLA_ROPE = 32
MLA_QK = HEAD + MLA_ROPE
ROPE_THETA = 10000.0
WIDTH = HEADS * HEAD
LORA = 64
SHIFT_W = 3 * WIDTH + 2 * LORA
CHUNK = 64
ALPHA = 2.0 ** 0.25

ADAM_LR, ADAM_B1, ADAM_B2, ADAM_EPS, ADAM_WD, ADAM_STEP = 0.001, 0.9, 0.999, 1e-08, 0.01, 10

LANES = 128
SUBLANES = 8
VMEM_LIMIT = 56 * 1024 * 1024
N_DEV = 8
MESH = pl.DeviceIdType.MESH
NEG = -1e30

_WEIGHTS = ['w_ada', 'b_ada', 'w_in', 'q_norm_g', 'w_uq', 'kv_norm_g', 'w_ukv', 'mu_rwkv', 'w0',
            'w_decay_up', 'a0', 'w_iclr_up', 'k_k', 'k_a', 'r_k', 'gn_g', 'gn_b', 'w_proj_a',
            'w_proj_b', 'w_out', 'post_g', 'post_b']
_SHARDED = [('w_ada', (1024, 3072), 1), ('w_in', (1024, 5152), 1), ('w_uq', (256, 768), 1),
            ('w_ukv', (128, 1024), 1), ('w_decay_up', (64, 512), 1), ('w_iclr_up', (64, 512), 1),
            ('w_proj_a', (512, 1024), 1), ('w_proj_b', (512, 1024), 1), ('w_out', (1024, 1024), 0)]
_SMALL = [('b_ada', 3072), ('q_norm_g', 256), ('kv_norm_g', 128), ('mu_rwkv', 1664), ('w0', 512),
          ('a0', 512), ('k_k', 512), ('k_a', 512), ('r_k', 512), ('gn_g', 512), ('gn_b', 512),
          ('post_g', 1024), ('post_b', 1024)]
PACK_ROWS = 256
PACK_COLS = 11264
HALF_COLS = PACK_COLS // 2
ADA_COLS = 4 * 768
GRAD_HALF = (PACK_COLS - ADA_COLS) // 2
SMALL_USED = 84
SMALL_ROWS = 96


def _bf(x):
    return x.astype(BF16)


def _dot(a, b, ca, cb):
    return lax.dot_general(a, b, (((ca,), (cb,)), ((), ())), preferred_element_type=F32)


class Weight(NamedTuple):
    value: jax.Array
    grad: jax.Array


@jax.custom_vjp
def _mm(a, w, w_grad):
    return _dot(_bf(a), _bf(w), 1, 0)


def _mm_fwd(a, w, w_grad):
    return _mm(a, w, w_grad), (a, w)


def _mm_bwd(res, g):
    a, w = res
    gb = _bf(g)
    return _dot(gb, _bf(w), 1, 1), jnp.zeros_like(w), _dot(_bf(a), gb, 0, 0)


_mm.defvjp(_mm_fwd, _mm_bwd)


def mm(a, w):
    if isinstance(w, Weight):
        return _mm(a, w.value, w.grad)
    return _dot(_bf(a), _bf(w), 1, 0)


def _split3(x):
    hi = _bf(x)
    r1 = x - hi.astype(F32)
    mid = _bf(r1)
    lo = _bf(r1 - mid.astype(F32))
    return hi, mid, lo


def _exact_dot(x, m, cm):
    hi, mid, lo = _split3(x)
    return _dot(hi, m, 1, cm) + _dot(mid, m, 1, cm) + _dot(lo, m, 1, cm)


@jax.custom_vjp
def segsum(x, ones_blocks):
    return _exact_dot(x, ones_blocks, 0)


def _segsum_fwd(x, ones_blocks):
    return segsum(x, ones_blocks), ones_blocks


def _segsum_bwd(ones_blocks, g):
    return _exact_dot(g, ones_blocks, 0), jnp.zeros_like(ones_blocks)


segsum.defvjp(_segsum_fwd, _segsum_bwd)


@jax.custom_vjp
def lane_perm(x, perm):
    return _exact_dot(x, perm, 0)


def _lane_perm_fwd(x, perm):
    return lane_perm(x, perm), perm


def _lane_perm_bwd(perm, g):
    return _exact_dot(g, perm, 1), jnp.zeros_like(perm)


lane_perm.defvjp(_lane_perm_fwd, _lane_perm_bwd)


def _silu(z):
    return z * jax.nn.sigmoid(z)


def _softplus(z):
    return jnp.maximum(z, 0.0) + jnp.log(1.0 + jnp.exp(-jnp.abs(z)))


def _layer_norm(x):
    xc = x - jnp.mean(x, -1, keepdims=True)
    return xc * lax.rsqrt(jnp.mean(xc * xc, -1, keepdims=True) + LN_EPS)


def _rope(t, cos_t, sin_t, perm):
    outs = []
    for h in range(t.shape[1] // LANES):
        th = t[:, h * LANES:(h + 1) * LANES]
        outs.append(th * cos_t + lane_perm(th, perm) * sin_t)
    return outs[0] if len(outs) == 1 else jnp.concatenate(outs, axis=1)


def _make_f_in(splits):
    def f_in(x, shift, scale, w):
        h = _layer_norm(x) * (1.0 + scale) + shift
        p = mm(h, w)
        outs, o = [], 0
        for s in splits:
            outs.append(p[:, o:o + s])
            o += s
        return tuple(outs)
    return f_in


def f_mla_pre(p, cs, qg, kvg, w_uq, w_ukv, perm):
    q_c, kv_c, k_r = p[:, :256], p[:, 256:384], p[:, 384:512]
    cos_t, sin_t = cs[:, :LANES], cs[:, LANES:]
    qn = q_c * lax.rsqrt(jnp.mean(q_c * q_c, -1, keepdims=True) + RMS_EPS) * qg
    kvn = kv_c * lax.rsqrt(jnp.mean(kv_c * kv_c, -1, keepdims=True) + RMS_EPS) * kvg
    q = _rope(mm(qn, w_uq), cos_t, sin_t, perm)
    kv = mm(kvn, w_ukv)
    return q, kv, _rope(k_r, cos_t, sin_t, perm)


def f_rwkv_pre(u, w0, a0, k_k, k_a, w_lora, ones_blocks):
    r, k, v, lo = u[:, :WIDTH], u[:, WIDTH:2 * WIDTH], u[:, 2 * WIDTH:3 * WIDTH], u[:, 3 * WIDTH:]
    lane = lax.broadcasted_iota(jnp.int32, lo.shape, 1)
    dl = mm(jnp.where(lane < LORA, jnp.tanh(lo), lo), w_lora)
    w_log = -_softplus(-(w0 + dl[:, :WIDTH])) - 0.5
    decay = jnp.exp(-jnp.exp(w_log))
    a = jax.nn.sigmoid(a0 + dl[:, WIDTH:])
    kk = k * k_k
    kk = kk / jnp.maximum(jnp.sqrt(segsum(kk * kk, ones_blocks)), 1e-12)
    k2 = k * (1.0 + (a - 1.0) * k_a)
    return r, decay, k2, v, -kk, kk * a


def f_rwkv_post(y, r, k2, v, r_k, gn_g, gn_b, ones_blocks):
    yc = y - segsum(y, ones_blocks) * (1.0 / HEAD)
    yn = yc * lax.rsqrt(segsum(yc * yc, ones_blocks) * (1.0 / HEAD) + GN_EPS)
    return (yn * gn_g + gn_b + segsum(r * k2 * r_k, ones_blocks) * v,)


def f_merge(ya, gpa, yb, gpb, ma, mb, w_pa, w_pb):
    pa = mm(ya * _silu(gpa), w_pa)
    pb = mm(yb * _silu(gpb), w_pb)
    return (jax.nn.sigmoid(ma) * pa + jax.nn.sigmoid(mb) * pb,)


def f_loss(merged, x, tgt, gate, post_g, post_b, w_out):
    z = ALPHA * x + (1.0 + gate) * mm(merged, w_out)
    err = _layer_norm(z) * post_g + post_b - tgt
    lrow = 0.5 * jnp.mean(err * err, -1, keepdims=True)
    return (jnp.broadcast_to(lrow, (lrow.shape[0], LANES)),)


def f_adamw(w, g, m, v):
    m2 = ADAM_B1 * m + (1.0 - ADAM_B1) * g
    v2 = ADAM_B2 * v + (1.0 - ADAM_B2) * jnp.square(g)
    m_hat = m2 / (1.0 - ADAM_B1 ** ADAM_STEP)
    v_hat = v2 / (1.0 - ADAM_B2 ** ADAM_STEP)
    return -ADAM_LR * (m_hat / (jnp.sqrt(v_hat) + ADAM_EPS) + ADAM_WD * w), m2, v2


def _params(sem=("arbitrary",)):
    return pltpu.CompilerParams(dimension_semantics=sem, vmem_limit_bytes=VMEM_LIMIT)


def _row_spec(tr, a):
    return pl.BlockSpec((tr, a.shape[1]), lambda i: (i, 0))


def _full_spec(a):
    return pl.BlockSpec(a.shape, lambda i: (0,) * a.ndim)


def row_fwd(name, f, rows, params, consts, out_widths, tr, out_dtype=F32):
    n_rows = rows[0].shape[0]
    nr, npar, ncon = len(rows), len(params), len(consts)

    def body(*refs):
        rv = [r[...] for r in refs[:nr]]
        pv = [r[...] for r in refs[nr:nr + npar]]
        cv = [r[...] for r in refs[nr + npar:nr + npar + ncon]]
        outs = f(*rv, *pv, *cv)
        for o_ref, o in zip(refs[nr + npar + ncon:], outs):
            o_ref[...] = o.astype(o_ref.dtype)

    return pl.pallas_call(
        body, name=name, grid=(n_rows // tr,),
        in_specs=[_row_spec(tr, a) for a in rows] + [_full_spec(a) for a in list(params) + list(consts)],
        out_specs=[pl.BlockSpec((tr, w), lambda i: (i, 0)) for w in out_widths],
        out_shape=[jax.ShapeDtypeStruct((n_rows, w), out_dtype) for w in out_widths],
        compiler_params=_params(),
    )(*rows, *params, *consts)


def row_bwd(name, f, rows, n_diff, params, consts, douts, tr, add_rows=None):
    n_rows = rows[0].shape[0]
    douts = [d if isinstance(d, (tuple, list)) else (d,) for d in douts]
    counts = [len(d) for d in douts]
    flat_d = [a for d in douts for a in d]
    add_rows = add_rows or [None] * n_diff
    adds = [a for a in add_rows if a is not None]
    nr, npar, ncon, nd, na = len(rows), len(params), len(consts), len(flat_d), len(adds)

    def body(*refs):
        o = 0
        rv = [r[...] for r in refs[o:o + nr]]; o += nr
        pv = [Weight(r[...], jnp.zeros(r.shape, F32)) if r.dtype == BF16 else r[...] for r in refs[o:o + npar]]
        o += npar
        cv = [r[...] for r in refs[o:o + ncon]]; o += ncon
        dv = []
        for cnt in counts:
            s = refs[o][...]
            for e in range(1, cnt):
                s = s + refs[o + e][...]
            dv.append(s)
            o += cnt
        add_v = [r[...] for r in refs[o:o + na]]; o += na
        drow_refs = refs[o:o + n_diff]; o += n_diff
        dpar_refs = refs[o:o + npar]

        def g(*args):
            return tuple(f(*args[:n_diff], *rv[n_diff:], *args[n_diff:], *cv))

        _, vjp = jax.vjp(g, *rv[:n_diff], *pv)
        grads = vjp(tuple(dv))
        ai = 0
        for j, (r, gr) in enumerate(zip(drow_refs, grads[:n_diff])):
            if add_rows[j] is not None:
                gr = gr + add_v[ai]
                ai += 1
            r[...] = gr

        @pl.when(pl.program_id(0) == 0)
        def _():
            for r in dpar_refs:
                r[...] = jnp.zeros_like(r)

        for r, gr in zip(dpar_refs, grads[n_diff:]):
            r[...] += gr.grad if isinstance(gr, Weight) else gr

    outs = pl.pallas_call(
        body, name=name, grid=(n_rows // tr,),
        in_specs=([_row_spec(tr, a) for a in rows] + [_full_spec(a) for a in list(params) + list(consts)]
                  + [_row_spec(tr, a) for a in flat_d + adds]),
        out_specs=[_row_spec(tr, a) for a in rows[:n_diff]] + [_full_spec(a) for a in params],
        out_shape=([jax.ShapeDtypeStruct(a.shape, F32) for a in rows[:n_diff]]
                   + [jax.ShapeDtypeStruct(a.shape, F32) for a in params]),
        compiler_params=_params(),
    )(*rows, *params, *consts, *flat_d, *adds)
    return outs[:n_diff], outs[n_diff:]


def shift_fwd(p, mu, tr):
    n_rows, w = p.shape

    def body(p_ref, mu_ref, u_ref, carry):
        @pl.when(pl.program_id(0) == 0)
        def _():
            carry[...] = jnp.zeros_like(carry)

        x = p_ref[...]
        rolled = pltpu.roll(x, 1, 0)
        head = pltpu.roll(carry[...], 1, 0)
        fixed = jnp.concatenate([head, rolled[SUBLANES:]], axis=0)
        row = lax.broadcasted_iota(jnp.int32, x.shape, 0)
        prev = jnp.where(row == 0, fixed, rolled)
        u_ref[...] = x + (prev - x) * mu_ref[...]
        carry[...] = x[tr - SUBLANES:]

    return pl.pallas_call(
        body, name="shift_fwd", grid=(n_rows // tr,),
        in_specs=[_row_spec(tr, p), _full_spec(mu)],
        out_specs=_row_spec(tr, p),
        out_shape=jax.ShapeDtypeStruct(p.shape, F32),
        scratch_shapes=[pltpu.VMEM((SUBLANES, w), F32)],
        compiler_params=_params(),
    )(p, mu)


def shift_bwd(du, p, mu, tr):
    n_rows, w = p.shape
    nb = n_rows // tr

    def body(du_ref, p_ref, mu_ref, dp_ref, dmu_ref, carry):
        @pl.when(pl.program_id(0) == 0)
        def _():
            carry[...] = jnp.zeros_like(carry)
            dmu_ref[...] = jnp.zeros_like(dmu_ref)

        d = du_ref[...]
        rolled = pltpu.roll(d, tr - 1, 0)
        tail = pltpu.roll(carry[...], SUBLANES - 1, 0)
        fixed = jnp.concatenate([rolled[:tr - SUBLANES], tail], axis=0)
        row = lax.broadcasted_iota(jnp.int32, d.shape, 0)
        nxt = jnp.where(row == tr - 1, fixed, rolled)
        mu_v = mu_ref[...]
        dp_ref[...] = d * (1.0 - mu_v) + nxt * mu_v
        dmu_ref[...] += jnp.sum(p_ref[...] * (nxt - d), axis=0, keepdims=True)
        carry[...] = d[:SUBLANES]

    rev = lambda i: (nb - 1 - i, 0)
    return pl.pallas_call(
        body, name="shift_bwd", grid=(nb,),
        in_specs=[pl.BlockSpec((tr, w), rev), pl.BlockSpec((tr, w), rev), _full_spec(mu)],
        out_specs=[pl.BlockSpec((tr, w), rev), _full_spec(mu)],
        out_shape=[jax.ShapeDtypeStruct(p.shape, F32), jax.ShapeDtypeStruct(mu.shape, F32)],
        scratch_shapes=[pltpu.VMEM((SUBLANES, w), F32)],
        compiler_params=_params(),
    )(du, p, mu)


ATT_T = 256


def _att_rows(j):
    return pl.ds(pl.multiple_of(j * ATT_T, ATT_T), ATT_T)


def _att_prep(kv_ref, kpe_ref, kf_scr, vf_scr, n_blocks):
    lane = lax.broadcasted_iota(jnp.int32, (ATT_T, LANES), 1)

    def prep(j, _):
        rows = _att_rows(j)
        kv = kv_ref[rows, :]
        kf_scr[rows, :] = _bf(jnp.where(lane < HEAD, kv, kpe_ref[rows, :]))
        vf_scr[rows, :] = _bf(jnp.where(lane >= HEAD, kv, 0.0))
        return 0

    lax.fori_loop(0, n_blocks, prep, 0)


def _att_diag_mask():
    shift = CHUNK.bit_length() - 1
    qc = jnp.right_shift(lax.broadcasted_iota(jnp.int32, (ATT_T, ATT_T), 0), shift)
    kc = jnp.right_shift(lax.broadcasted_iota(jnp.int32, (ATT_T, ATT_T), 1), shift)
    return kc <= qc


def _wide(x):
    return jnp.concatenate([x] * (ATT_T // LANES), axis=1)


def attn_fwd(q, kv, kpe):
    seq = q.shape[0]
    nb = seq // ATT_T
    assert seq % (2 * ATT_T) == 0, "blocks are taken two per trip"
    scale = MLA_QK ** -0.5

    def body(q_ref, kv_ref, kpe_ref, o_ref, lse_ref, kf_scr, vf_scr):
        _att_prep(kv_ref, kpe_ref, kf_scr, vf_scr, nb)
        mask = _att_diag_mask()

        def scores(qb, kj):
            return _dot(qb, kf_scr[_att_rows(kj), :], 1, 1) * scale

        def update(s, kj, carry, masked):
            m, l, acc = carry
            if masked:
                s = jnp.where(mask, s, NEG)
            m_new = jnp.maximum(m, jnp.broadcast_to(jnp.max(s, -1, keepdims=True), m.shape))
            alpha = jnp.exp(m - m_new)
            p = jnp.exp(s - _wide(m_new))
            l = alpha * l + jnp.broadcast_to(jnp.sum(p, -1, keepdims=True), l.shape)
            acc = alpha * acc + _dot(_bf(p), vf_scr[_att_rows(kj), :], 1, 0)
            return m_new, l, acc

        def finish(rows, carry):
            m, l, acc = carry
            o_ref[rows, :] = acc / l
            lse_ref[rows, :] = m + jnp.log(l)

        def q_pair(qp, _):
            rows_a, rows_b = _att_rows(2 * qp), _att_rows(2 * qp + 1)
            qa, qb = _bf(q_ref[rows_a, :]), _bf(q_ref[rows_b, :])
            init = (jnp.full((ATT_T, LANES), NEG, F32), jnp.zeros((ATT_T, LANES), F32),
                    jnp.zeros((ATT_T, LANES), F32))

            def trip(kj, c):
                ca, cb, sa, sb = c
                sa_next, sb_next = scores(qa, kj + 1), scores(qb, kj + 1)
                return update(sa, kj, ca, False), update(sb, kj, cb, False), sa_next, sb_next

            ca, cb, sa, sb = lax.fori_loop(0, 2 * qp, trip, (init, init, scores(qa, 0), scores(qb, 0)))
            sb_last = scores(qb, 2 * qp + 1)
            ca = update(sa, 2 * qp, ca, True)
            cb = update(sb_last, 2 * qp + 1, update(sb, 2 * qp, cb, False), True)
            finish(rows_a, ca)
            finish(rows_b, cb)
            return 0

        lax.fori_loop(0, nb // 2, q_pair, 0)

    head = pl.BlockSpec((seq, LANES), lambda h: (0, h))
    return pl.pallas_call(
        body, name="attn_fwd", grid=(HEADS,),
        in_specs=[head, head, pl.BlockSpec((seq, LANES), lambda h: (0, 0))],
        out_specs=[head, head],
        out_shape=[jax.ShapeDtypeStruct((seq, HEADS * LANES), F32)] * 2,
        scratch_shapes=[pltpu.VMEM((seq, LANES), BF16)] * 2,
        compiler_params=_params(),
    )(q, kv, kpe)


def attn_bwd(q, kv, kpe, o, lse, do):
    seq = q.shape[0]
    nb = seq // ATT_T
    assert seq % (2 * ATT_T) == 0, "blocks are taken two per trip"
    scale = MLA_QK ** -0.5

    def body(q_ref, kv_ref, kpe_ref, o_ref, lse_ref, do_ref, dq_ref, dkv_ref, dkpe_ref,
             kf_scr, vf_scr, qb_scr, dob_scr, dsum):
        lane = lax.broadcasted_iota(jnp.int32, (ATT_T, LANES), 1)

        @pl.when(pl.program_id(0) == 0)
        def _():
            dkpe_ref[...] = jnp.zeros_like(dkpe_ref)

        dq_ref[...] = jnp.zeros_like(dq_ref)
        _att_prep(kv_ref, kpe_ref, kf_scr, vf_scr, nb)

        def pre(j, _):
            rows = _att_rows(j)
            d = do_ref[rows, :]
            qb_scr[rows, :] = _bf(q_ref[rows, :])
            dob_scr[rows, :] = _bf(d)
            dsum[rows, :] = jnp.broadcast_to(jnp.sum(d * o_ref[rows, :], -1, keepdims=True), (ATT_T, LANES))
            return 0

        lax.fori_loop(0, nb, pre, 0)
        mask = _att_diag_mask()

        def front(kf, vf, qi):
            rows = _att_rows(qi)
            return _dot(qb_scr[rows, :], kf, 1, 1), _dot(dob_scr[rows, :], vf, 1, 1)

        def back(kf, qi, fr, carry, masked):
            s, dp = fr
            dk, dv = carry
            rows = _att_rows(qi)
            qb, dob = qb_scr[rows, :], dob_scr[rows, :]
            p = jnp.exp(s * scale - _wide(lse_ref[rows, :]))
            if masked:
                p = jnp.where(mask, p, 0.0)
            ds = _bf(p * (dp - _wide(dsum[rows, :])) * scale)
            return (dk + _dot(ds, qb, 0, 0), dv + _dot(_bf(p), dob, 0, 0)), _dot(ds, kf, 1, 0)

        def store(krows, carry):
            dk, dv = carry
            dkv_ref[krows, :] = jnp.where(lane < HEAD, dk, dv)
            dkpe_ref[krows, :] += jnp.where((lane >= HEAD) & (lane < MLA_QK), dk, 0.0)

        def k_pair(kp, _):
            ka, kb = 2 * kp, 2 * kp + 1
            rows_a, rows_b = _att_rows(ka), _att_rows(kb)
            kfa, vfa, kfb, vfb = kf_scr[rows_a, :], vf_scr[rows_a, :], kf_scr[rows_b, :], vf_scr[rows_b, :]
            zero = jnp.zeros((ATT_T, LANES), F32)
            ca, dq_a = back(kfa, ka, front(kfa, vfa, ka), (zero, zero), True)
            dq_ref[rows_a, :] += dq_a
            ca, dq_a = back(kfa, kb, front(kfa, vfa, kb), ca, False)
            cb, dq_b = back(kfb, kb, front(kfb, vfb, kb), (zero, zero), True)
            dq_ref[rows_b, :] += dq_a + dq_b

            def both(qi, c):
                ca, cb, fa, fb = c
                nxt = jnp.minimum(qi + 1, nb - 1)
                fa_next, fb_next = front(kfa, vfa, nxt), front(kfb, vfb, nxt)
                ca, dq_a = back(kfa, qi, fa, ca, False)
                cb, dq_b = back(kfb, qi, fb, cb, False)
                dq_ref[_att_rows(qi), :] += dq_a + dq_b
                return ca, cb, fa_next, fb_next

            first = jnp.minimum(kb + 1, nb - 1)
            ca, cb, _, _ = lax.fori_loop(kb + 1, nb, both, (ca, cb, front(kfa, vfa, first), front(kfb, vfb, first)))
            store(rows_a, ca)
            store(rows_b, cb)
            return 0

        lax.fori_loop(0, nb // 2, k_pair, 0)

    head = pl.BlockSpec((seq, LANES), lambda h: (0, h))
    shared = pl.BlockSpec((seq, LANES), lambda h: (0, 0))
    return pl.pallas_call(
        body, name="attn_bwd", grid=(HEADS,),
        in_specs=[head, head, shared, head, head, head],
        out_specs=[head, head, shared],
        out_shape=[jax.ShapeDtypeStruct((seq, HEADS * LANES), F32)] * 2
        + [jax.ShapeDtypeStruct((seq, LANES), F32)],
        scratch_shapes=[pltpu.VMEM((seq, LANES), BF16)] * 4 + [pltpu.VMEM((seq, LANES), F32)],
        compiler_params=_params(),
    )(q, kv, kpe, o, lse, do)


WKV_TB = 128
WKV_GROUP = SUBLANES
WKV_HALF = WIDTH // 2


def _wkv_consts():
    row = lax.broadcasted_iota(jnp.int32, (HEAD, WKV_HALF), 0)
    lane = lax.broadcasted_iota(jnp.int32, (HEAD, WKV_HALF), 1)
    diag = row == jnp.bitwise_and(lane, HEAD - 1)
    sub = lax.broadcasted_iota(jnp.int32, (WKV_GROUP, WKV_HALF), 0)
    return diag, sub


def _halves(x):
    return [x[:, :WKV_HALF], x[:, WKV_HALF:]]


def _diag_rows(row, diag):
    return _bf(jnp.where(diag, jnp.broadcast_to(row, diag.shape), 0.0))


def _put_row(tile, row, i, sub):
    return jnp.where(sub == i, jnp.broadcast_to(row, tile.shape), tile)


def _col_sum(x):
    return jnp.sum(x, axis=0, keepdims=True)


def _step(x, i):
    return x[i * HEAD:(i + 1) * HEAD]


def _expand_group(rows8, diag, ones_b):
    lhs = jnp.concatenate([_diag_rows(rows8[i:i + 1], diag) for i in range(WKV_GROUP)], axis=0)
    return _dot(lhs, ones_b, 1, 0)


def _head_dots(prods, ones_b, sub):
    tile = jnp.zeros((WKV_GROUP, WKV_HALF), F32)
    for i, p in enumerate(prods):
        tile = _put_row(tile, p, i, sub)
    res = _exact_dot(tile, ones_b, 0)
    return [res[i:i + 1] for i in range(len(prods))]


def _diag_group(x, diag, sub):
    out = jnp.zeros((WKV_GROUP, WKV_HALF), F32)
    for i in range(WKV_GROUP):
        out = _put_row(out, _col_sum(jnp.where(diag, _step(x, i), 0.0)), i, sub)
    return out


def wkv_fwd(r, w, k, v, a, b, ones_half):
    seq = r.shape[0]

    def body(r_ref, w_ref, k_ref, v_ref, a_ref, b_ref, ones_ref, y_ref, st_ref, s_scr):
        @pl.when(pl.program_id(0) == 0)
        def _():
            s_scr[...] = jnp.zeros_like(s_scr)

        ones_b = ones_ref[...]
        diag, sub = _wkv_consts()

        ng = WKV_TB // WKV_GROUP
        last = WKV_GROUP - 2

        def rows_of(g):
            return pl.ds(pl.multiple_of(g * WKV_GROUP, WKV_GROUP), WKV_GROUP)

        def pair_rows(x8, t):
            return jnp.concatenate([_diag_rows(x8[t:t + 1], diag), _diag_rows(x8[t + 1:t + 2], diag)], axis=0)

        def put_y(g, pairs_y):
            tile = _halves(y_ref[rows_of(g), :])
            for hf in range(2):
                tile[hf] = _put_row(_put_row(tile[hf], pairs_y[hf][0], last, sub), pairs_y[hf][1], last + 1, sub)
            y_ref[rows_of(g), :] = jnp.concatenate(tile, axis=1)

        def read_out(yexp):
            return _col_sum(jnp.where(diag, yexp[:HEAD], 0.0)), _col_sum(jnp.where(diag, yexp[HEAD:], 0.0))

        def group(g, carry):
            state, v_cur, read = (list(c) for c in carry)
            base = pl.multiple_of(g * WKV_GROUP, WKV_GROUP)
            rows = rows_of(g)
            r8, w8, k8, v8, a8, b8 = (_halves(ref[rows, :]) for ref in (r_ref, w_ref, k_ref, v_ref, a_ref, b_ref))
            v_after = _halves(v_ref[rows_of(jnp.minimum(g + 1, ng - 1)), :])
            evens = range(0, WKV_GROUP, 2)
            dots = [_head_dots([b8[hf][t:t + 1] * a8[hf][t + 1:t + 2] for t in evens]
                               + [k8[hf][t:t + 1] * a8[hf][t + 1:t + 2] for t in evens], ones_b, sub) for hf in range(2)]
            y8 = [jnp.zeros((WKV_GROUP, WKV_HALF), F32)] * 2
            y_before = [None, None]
            for t in evens:
                s0, s1 = slice(t, t + 1), slice(t + 1, t + 2)
                both = []
                for hf in range(2):
                    s_in = state[hf]
                    v_next = pair_rows(v8[hf], t + 2) if t < last else pair_rows(v_after[hf], 0)
                    res = _dot(jnp.concatenate([_bf(s_in * a8[hf][s0]), _bf(s_in * (w8[hf][s0] * a8[hf][s1])),
                                                v_next, read[hf]], axis=0), ones_b, 1, 0)
                    sa0, v0, v1 = res[:HEAD], v_cur[hf][:HEAD], v_cur[hf][HEAD:]
                    st0 = s_in * w8[hf][s0] + sa0 * b8[hf][s0] + v0 * k8[hf][s0]
                    sa1 = res[HEAD:2 * HEAD] + sa0 * dots[hf][t // 2] + v0 * dots[hf][WKV_GROUP // 2 + t // 2]
                    st1 = st0 * w8[hf][s1] + sa1 * b8[hf][s1] + v1 * k8[hf][s1]
                    both.append((st0, st1))
                    state[hf], v_cur[hf] = st1, res[2 * HEAD:4 * HEAD]
                    read[hf] = jnp.concatenate([_bf(st0 * r8[hf][s0]), _bf(st1 * r8[hf][s1])], axis=0)
                    ya, yb = read_out(res[4 * HEAD:])
                    if t == 0:
                        y_before[hf] = (ya, yb)
                    else:
                        y8[hf] = _put_row(_put_row(y8[hf], ya, t - 2, sub), yb, t - 1, sub)
                for j in range(2):
                    st_ref[base + t + j] = jnp.concatenate([both[0][j], both[1][j]], axis=1)
            y_ref[rows, :] = jnp.concatenate(y8, axis=1)
            put_y(jnp.maximum(g - 1, 0), y_before)
            return tuple(state), tuple(v_cur), tuple(read)

        v_first = _halves(v_ref[rows_of(0), :])
        init = (tuple(_halves(s_scr[...])),
                tuple(_dot(pair_rows(v_first[hf], 0), ones_b, 1, 0) for hf in range(2)),
                tuple(jnp.zeros((2 * HEAD, WKV_HALF), BF16) for _ in range(2)))
        fin, _, read = lax.fori_loop(0, ng, group, init)
        put_y(ng - 1, [read_out(_dot(read[hf], ones_b, 1, 0)) for hf in range(2)])
        s_scr[...] = jnp.concatenate(fin, axis=1)

    vec = pl.BlockSpec((WKV_TB, WIDTH), lambda i: (i, 0))
    return pl.pallas_call(
        body, name="wkv_fwd", grid=(seq // WKV_TB,),
        in_specs=[vec] * 6 + [_full_spec(ones_half)],
        out_specs=[vec, pl.BlockSpec((WKV_TB, HEAD, WIDTH), lambda i: (i, 0, 0))],
        out_shape=[jax.ShapeDtypeStruct((seq, WIDTH), F32), jax.ShapeDtypeStruct((seq, HEAD, WIDTH), F32)],
        scratch_shapes=[pltpu.VMEM((HEAD, WIDTH), F32)],
        compiler_params=_params(),
    )(r, w, k, v, a, b, ones_half)


def wkv_bwd(r, w, k, v, a, b, dy, states, ones_half):
    seq = r.shape[0]
    nb = seq // WKV_TB
    ng = WKV_TB // WKV_GROUP

    def body(r_ref, w_ref, k_ref, v_ref, a_ref, b_ref, dy_ref, st_ref, halo_ref, ones_ref,
             dr_ref, dw_ref, dk_ref, dv_ref, da_ref, db_ref, ds_scr):
        blk = nb - 1 - pl.program_id(0)

        @pl.when(pl.program_id(0) == 0)
        def _():
            ds_scr[...] = jnp.zeros_like(ds_scr)

        ones_b = ones_ref[...]
        diag, sub = _wkv_consts()
        before_block = jnp.where(blk == 0, 0.0, halo_ref[0])

        def rows_of(g):
            return pl.ds(pl.multiple_of(g * WKV_GROUP, WKV_GROUP), WKV_GROUP)

        def expand_rows(hf, dy8, v8, a8, t, s_t, s_u):
            s1, s0 = slice(t, t + 1), slice(t - 1, t)
            return jnp.concatenate([_diag_rows(dy8[hf][s1], diag), _diag_rows(dy8[hf][s0], diag),
                                    _diag_rows(v8[hf][s1], diag), _diag_rows(v8[hf][s0], diag),
                                    _bf(s_t[hf] * a8[hf][s1]), _bf(s_u[hf] * a8[hf][s0])], axis=0)

        def read_out(x):
            return _col_sum(jnp.where(diag, x[:HEAD], 0.0)), _col_sum(jnp.where(diag, x[HEAD:], 0.0))

        def put_dv(g, pair_dv):
            tile = _halves(dv_ref[rows_of(g), :])
            for hf in range(2):
                tile[hf] = _put_row(_put_row(tile[hf], pair_dv[hf][0], 1, sub), pair_dv[hf][1], 0, sub)
            dv_ref[rows_of(g), :] = jnp.concatenate(tile, axis=1)

        def group(gg, carry):
            dstate, e_cur, dv_pend = (list(c) for c in carry)
            g = ng - 1 - gg
            base = pl.multiple_of(g * WKV_GROUP, WKV_GROUP)
            rows = rows_of(g)
            r8, w8, k8, v8, a8, b8, dy8 = (
                _halves(ref[rows, :]) for ref in (r_ref, w_ref, k_ref, v_ref, a_ref, b_ref, dy_ref))
            g_next = jnp.maximum(g - 1, 0)
            base_next = pl.multiple_of(g_next * WKV_GROUP, WKV_GROUP)
            dy8n, v8n, a8n = (_halves(ref[rows_of(g_next), :]) for ref in (dy_ref, v_ref, a_ref))
            zero8 = jnp.zeros((WKV_GROUP, WKV_HALF), F32)
            out = {n: [zero8, zero8] for n in ("dr", "dw", "dk", "dv", "da", "db")}
            before_group = jnp.where(g == 0, before_block, st_ref[jnp.maximum(base - 1, 0)])
            states = [_halves(before_group)] + [_halves(st_ref[base + i]) for i in range(WKV_GROUP)]
            odds = range(1, WKV_GROUP, 2)
            dots = [_head_dots([a8[hf][t:t + 1] * b8[hf][t - 1:t] for t in odds]
                               + [r8[hf][t - 1:t] * b8[hf][t - 1:t] for t in odds], ones_b, sub) for hf in range(2)]
            dv_after = [None, None]

            def emit(hf, i, d_i, dsa_i, dy_i, v_i, sa_i):
                s_p, s_t = states[i][hf], states[i + 1][hf]
                for n, val in (("dr", _col_sum(s_t * dy_i)), ("dw", _col_sum(d_i * s_p)), ("db", _col_sum(d_i * sa_i)),
                               ("da", _col_sum(s_p * dsa_i)), ("dk", _col_sum(d_i * v_i))):
                    out[n][hf] = _put_row(out[n][hf], val, i, sub)

            for t in reversed(odds):
                s1, s0 = slice(t, t + 1), slice(t - 1, t)
                for hf in range(2):
                    dy1, dy0, v1, v0, sa1, sa0 = (_step(e_cur[hf], j) for j in range(6))
                    d1 = dstate[hf] + dy1 * r8[hf][s1]
                    if t > 1:
                        nxt = expand_rows(hf, dy8, v8, a8, t - 2, states[t - 2], states[t - 3])
                    else:
                        nxt = expand_rows(hf, dy8n, v8n, a8n, WKV_GROUP - 1, _halves(st_ref[base_next + WKV_GROUP - 2]),
                                          _halves(st_ref[base_next + WKV_GROUP - 3]))
                    res = _dot(jnp.concatenate([_bf(d1 * b8[hf][s1]), _bf(d1 * (w8[hf][s1] * b8[hf][s0])),
                                                nxt, dv_pend[hf]], axis=0), ones_b, 1, 0)
                    dsa1 = res[:HEAD]
                    d0 = d1 * w8[hf][s1] + dsa1 * a8[hf][s1] + dy0 * r8[hf][s0]
                    dsa0 = res[HEAD:2 * HEAD] + dsa1 * dots[hf][t // 2] + dy0 * dots[hf][WKV_GROUP // 2 + t // 2]
                    dstate[hf] = d0 * w8[hf][s0] + dsa0 * a8[hf][s0]
                    e_cur[hf] = res[2 * HEAD:8 * HEAD]
                    dv_pend[hf] = jnp.concatenate([_bf(d1 * k8[hf][s1]), _bf(d0 * k8[hf][s0])], axis=0)
                    emit(hf, t, d1, dsa1, dy1, v1, sa1)
                    emit(hf, t - 1, d0, dsa0, dy0, v0, sa0)
                    dv_a, dv_b = read_out(res[8 * HEAD:])
                    if t == WKV_GROUP - 1:
                        dv_after[hf] = (dv_a, dv_b)
                    else:
                        out["dv"][hf] = _put_row(_put_row(out["dv"][hf], dv_a, t + 2, sub), dv_b, t + 1, sub)
            for ref, n in ((dr_ref, "dr"), (dw_ref, "dw"), (dk_ref, "dk"), (dv_ref, "dv"), (da_ref, "da"), (db_ref, "db")):
                ref[rows, :] = jnp.concatenate(out[n], axis=1)
            put_dv(jnp.minimum(g + 1, ng - 1), dv_after)
            return tuple(dstate), tuple(e_cur), tuple(dv_pend)

        top = rows_of(ng - 1)
        dy8t, v8t, a8t = (_halves(ref[top, :]) for ref in (dy_ref, v_ref, a_ref))
        s_t, s_u = _halves(st_ref[WKV_TB - 2]), _halves(st_ref[WKV_TB - 3])
        init = (tuple(_halves(ds_scr[...])),
                tuple(_dot(expand_rows(hf, dy8t, v8t, a8t, WKV_GROUP - 1, s_t, s_u), ones_b, 1, 0) for hf in range(2)),
                tuple(jnp.zeros((2 * HEAD, WKV_HALF), BF16) for _ in range(2)))
        fin, _, dv_pend = lax.fori_loop(0, ng, group, init)
        put_dv(0, [read_out(_dot(dv_pend[hf], ones_b, 1, 0)) for hf in range(2)])
        ds_scr[...] = jnp.concatenate(fin, axis=1)

    vec = pl.BlockSpec((WKV_TB, WIDTH), lambda i: (nb - 1 - i, 0))
    return pl.pallas_call(
        body, name="wkv_bwd", grid=(nb,),
        in_specs=[vec] * 7 + [
            pl.BlockSpec((WKV_TB, HEAD, WIDTH), lambda i: (nb - 1 - i, 0, 0)),
            pl.BlockSpec((1, HEAD, WIDTH), lambda i: (jnp.maximum((nb - 1 - i) * WKV_TB - 1, 0), 0, 0)),
            _full_spec(ones_half)],
        out_specs=[vec] * 6,
        out_shape=[jax.ShapeDtypeStruct((seq, WIDTH), F32)] * 6,
        scratch_shapes=[pltpu.VMEM((HEAD, WIDTH), F32)],
        compiler_params=_params(),
    )(r, w, k, v, a, b, dy, states, states, ones_half)


def ada_fwd(c8, b_ada, gathered):
    cols = 3 * D_MODEL // 4

    def body(c_ref, b_ref, w_ref, o_ref):
        @pl.when(pl.program_id(1) == 0)
        def _():
            o_ref[...] = jnp.broadcast_to(b_ref[...], o_ref.shape)

        o_ref[...] += mm(_silu(c_ref[...]), w_ref[0])

    return pl.pallas_call(
        body, name="ada_fwd", grid=(4, D_MODEL // PACK_ROWS),
        in_specs=[pl.BlockSpec((SUBLANES, PACK_ROWS), lambda s, i: (0, i)),
                  pl.BlockSpec((1, cols), lambda s, i: (0, s)),
                  pl.BlockSpec((1, PACK_ROWS, cols), lambda s, i: (2 * s, 0, i))],
        out_specs=pl.BlockSpec((SUBLANES, cols), lambda s, i: (0, s)),
        out_shape=jax.ShapeDtypeStruct((SUBLANES, 3 * D_MODEL), F32),
        compiler_params=_params(("arbitrary", "arbitrary")),
    )(c8, b_ada, gathered)


def ada_grad_shard(sc_cols, dada_rows):
    n = len(sc_cols)

    def body(*refs):
        d_ref, o_ref = refs[n], refs[n + 1]
        acc = refs[0][...] * d_ref[0:1, :]
        for b in range(1, n):
            acc = acc + refs[b][...] * d_ref[b:b + 1, :]
        o_ref[...] = acc

    return pl.pallas_call(
        body, name="ada_grad_shard",
        out_shape=jax.ShapeDtypeStruct((sc_cols[0].shape[0], dada_rows.shape[1]), F32),
        compiler_params=pltpu.CompilerParams(vmem_limit_bytes=VMEM_LIMIT),
    )(*sc_cols, dada_rows)


def sum_slots(buf, tr):
    n, rows, cols = buf.shape

    def body(b_ref, o_ref):
        acc = b_ref[0].astype(F32)
        for s in range(1, n):
            acc = acc + b_ref[s].astype(F32)
        o_ref[...] = acc

    return pl.pallas_call(
        body, name="sum_slots", grid=(rows // tr,),
        in_specs=[pl.BlockSpec((n, tr, cols), lambda i: (0, i, 0))],
        out_specs=pl.BlockSpec((tr, cols), lambda i: (i, 0)),
        out_shape=jax.ShapeDtypeStruct((rows, cols), F32),
        compiler_params=_params(),
    )(buf)


def adamw_small(gathered, w, m, v):
    n = gathered.shape[0]

    def body(g_ref, w_ref, m_ref, v_ref, go_ref, d_ref, mo_ref, vo_ref):
        g = g_ref[0]
        for s in range(1, n):
            g = g + g_ref[s]
        go_ref[...] = g
        d_ref[...], mo_ref[...], vo_ref[...] = f_adamw(w_ref[...], g, m_ref[...], v_ref[...])

    return pl.pallas_call(
        body, name="adamw_small",
        out_shape=[jax.ShapeDtypeStruct(w.shape, F32)] * 4,
        compiler_params=pltpu.CompilerParams(vmem_limit_bytes=VMEM_LIMIT),
    )(gathered, w, m, v)


def _coords():
    return lax.axis_index("x"), lax.axis_index("y"), lax.axis_index("c")


def _flip(v, bit):
    return 1 - v if bit else v


def _hbm_call(body, name, out_shape, n_sems, *args):
    any_spec = pl.BlockSpec(memory_space=pl.ANY)
    return pl.pallas_call(
        body, name=name, out_shape=out_shape,
        in_specs=[any_spec] * len(args), out_specs=any_spec,
        scratch_shapes=[pltpu.SemaphoreType.DMA((n_sems,)), pltpu.SemaphoreType.DMA((n_sems,)),
                        pltpu.SemaphoreType.DMA],
    )(*args)


def all_gather8(name, block):
    def body(x_ref, out_ref, send_sems, recv_sems, local_sem):
        x, y, c = _coords()
        me, sibling = (x, y, c), (x, y, 1 - c)
        x_nbr, y_nbr, diagonal = (1 - x, y), (x, 1 - y), (1 - x, 1 - y)
        relay_from = (c * x + (1 - c) * (1 - x), c * (1 - y) + (1 - c) * y)
        relay_to = (c * (1 - x) + (1 - c) * x, c * y + (1 - c) * (1 - y))

        def slot(px, py, pc):
            return out_ref.at[4 * px + 2 * py + pc]

        def copy(k, blk, to, src=None):
            return pltpu.make_async_remote_copy(
                src_ref=slot(*blk) if src is None else src, dst_ref=slot(*blk),
                send_sem=send_sems.at[k], recv_sem=recv_sems.at[k], device_id=to, device_id_type=MESH)

        mine = pltpu.make_async_copy(x_ref, slot(*me), local_sem)
        mine.start()
        first = [copy(0, me, sibling, src=x_ref), copy(1, me, (*x_nbr, c), src=x_ref), copy(2, me, (*y_nbr, c), src=x_ref)]
        for cp in first:
            cp.start()
        copy(1, (*x_nbr, c), me).wait_recv()
        copy(2, (*y_nbr, c), me).wait_recv()
        later = [copy(3, (*relay_from, c), (*relay_to, c)), copy(4, (*x_nbr, c), sibling), copy(5, (*y_nbr, c), sibling)]
        for cp in later:
            cp.start()
        copy(3, (*diagonal, c), me).wait_recv()
        last = copy(6, (*diagonal, c), sibling)
        last.start()
        copy(0, sibling, me).wait_recv()
        for k, chip in ((4, x_nbr), (5, y_nbr), (6, diagonal)):
            copy(k, (*chip, 1 - c), me).wait_recv()
        for cp in first + later + [last]:
            cp.wait_send()
        mine.wait()

    return _hbm_call(body, name, jax.ShapeDtypeStruct((N_DEV,) + block.shape, block.dtype), 7, block)


def pair_swap(name, block):
    def body(x_ref, out_ref, send_sems, recv_sems, local_sem):
        x, y, c = _coords()
        cp = pltpu.make_async_remote_copy(
            src_ref=x_ref, dst_ref=out_ref, send_sem=send_sems.at[0], recv_sem=recv_sems.at[0],
            device_id=(x, y, 1 - c), device_id_type=MESH)
        cp.start()
        cp.wait_recv()
        cp.wait_send()

    return _hbm_call(body, name, jax.ShapeDtypeStruct(block.shape, block.dtype), 1, block)


def chip_all_to_all(name, buf):
    def body(x_ref, out_ref, send_sems, recv_sems, local_sem):
        x, y, c = _coords()
        me = 2 * x + y
        mine = pltpu.make_async_copy(x_ref.at[me], out_ref.at[me], local_sem)
        mine.start()
        copies = []
        for k in range(1, 4):
            px, py = _flip(x, k & 2), _flip(y, k & 1)
            copies.append(pltpu.make_async_remote_copy(
                src_ref=x_ref.at[2 * px + py], dst_ref=out_ref.at[me],
                send_sem=send_sems.at[k - 1], recv_sem=recv_sems.at[k - 1],
                device_id=(px, py, c), device_id_type=MESH))
        for cp in copies:
            cp.start()
        for cp in copies:
            cp.wait_recv()
        for cp in copies:
            cp.wait_send()
        mine.wait()

    return _hbm_call(body, name, jax.ShapeDtypeStruct(buf.shape, buf.dtype), 3, buf)


def _col_blocks(a, cols):
    a = jnp.pad(a, ((0, 0), (0, cols - a.shape[1])))
    return [a[i * PACK_ROWS:(i + 1) * PACK_ROWS] for i in range(a.shape[0] // PACK_ROWS)]


def _pack_shard(sh, dtype, with_ada=True):
    lora = jnp.concatenate([sh['w_decay_up'], sh['w_iclr_up']], axis=1)
    misc = jnp.concatenate([sh['w_ukv'], lora, jnp.zeros((LORA, 2 * LANES), lora.dtype)], axis=0)
    blocks = ((_col_blocks(sh['w_ada'], 768) if with_ada else [])
              + _col_blocks(sh['w_in'], 1408) + _col_blocks(sh['w_proj_a'], 256)
              + _col_blocks(sh['w_proj_b'], 256) + [sh['w_out']] + _col_blocks(sh['w_uq'], 256) + [misc])
    return jnp.concatenate([b.astype(dtype) for b in blocks], axis=1)


def _unpack_shard(p, with_ada=True):
    o = [0]

    def take(n_blocks, cols, used):
        blocks = [p[:, o[0] + i * cols:o[0] + (i + 1) * cols] for i in range(n_blocks)]
        o[0] += n_blocks * cols
        return jnp.concatenate(blocks, axis=0)[:, :used]

    out = {'w_ada': take(4, 768, 768)} if with_ada else {}
    out.update({'w_in': take(4, 1408, 1288), 'w_proj_a': take(2, 256, 256),
                'w_proj_b': take(2, 256, 256), 'w_out': take(1, 1024, 1024), 'w_uq': take(1, 256, 192)})
    misc = take(1, 256, 256)
    out['w_ukv'] = misc[:2 * LORA]
    out['w_decay_up'] = misc[2 * LORA:3 * LORA, :LANES]
    out['w_iclr_up'] = misc[2 * LORA:3 * LORA, LANES:]
    return out


def _pack_small(parts):
    flat = jnp.concatenate([p.reshape(-1) for p in parts])
    return jnp.pad(flat, (0, SMALL_ROWS * LANES - flat.shape[0])).reshape(SMALL_ROWS, LANES)


def _unpack_small(packed):
    flat, out, o = packed.reshape(-1), {}, 0
    for name, n in _SMALL:
        out[name] = flat[o:o + n]
        o += n
    return out


def _pad_heads_cols(w, used, left):
    k = w.shape[0]
    return jnp.pad(w.reshape(k, HEADS, used), ((0, 0), (0, 0), (left, LANES - used - left))).reshape(k, HEADS * LANES)


def _unpad_heads_cols(w, used, left):
    k = w.shape[0]
    return w.reshape(k, HEADS, LANES)[:, :, left:left + used].reshape(k, HEADS * used)


def kernel(x, c, positions, w_ada, b_ada, w_in, q_norm_g, w_uq, kv_norm_g, w_ukv, mu_rwkv, w0, w_decay_up, a0, w_iclr_up, k_k, k_a, r_k, gn_g, gn_b, w_proj_a, w_proj_b, w_out, post_g, post_b, loss_target, m_w_ada, m_b_ada, m_w_in, m_q_norm_g, m_w_uq, m_kv_norm_g, m_w_ukv, m_mu_rwkv, m_w0, m_w_decay_up, m_a0, m_w_iclr_up, m_k_k, m_k_a, m_r_k, m_gn_g, m_gn_b, m_w_proj_a, m_w_proj_b, m_w_out, m_post_g, m_post_b, v_w_ada, v_b_ada, v_w_in, v_q_norm_g, v_w_uq, v_kv_norm_g, v_w_ukv, v_mu_rwkv, v_w0, v_w_decay_up, v_a0, v_w_iclr_up, v_k_k, v_k_a, v_r_k, v_gn_g, v_gn_b, v_w_proj_a, v_w_proj_b, v_w_out, v_post_g, v_post_b):
    given = dict(locals())
    seq = x.shape[1]
    my_c = lax.axis_index("c")

    shard_names = [n for n, _, _ in _SHARDED]
    w_pack = _pack_shard({n: given[n][0] for n in shard_names}, BF16)
    my_half = lax.dynamic_slice_in_dim(w_pack, my_c * HALF_COLS, HALF_COLS, 1)
    gathered = all_gather8("gather_weights", my_half)
    shards = [_unpack_shard(jnp.concatenate([gathered[2 * s], gathered[2 * s + 1]], axis=1)) for s in range(4)]
    full = {n: jnp.concatenate([sh[n] for sh in shards], axis=ax) for n, _, ax in _SHARDED}

    wi = full['w_in']
    zcol = lambda n: jnp.zeros((D_MODEL, n), BF16)
    w_g1 = jnp.concatenate([wi[:, :384], zcol(HEAD), wi[:, 384:416], zcol(LANES - MLA_QK),
                            _pad_heads_cols(wi[:, 416:928], HEAD, HEAD)], axis=1)
    w_g2 = wi[:, 928:3104]
    w_g3 = wi[:, 3104:5152]
    w_uq_p = _pad_heads_cols(full['w_uq'], MLA_QK, 0)
    w_pa_p = jnp.pad(full['w_proj_a'].reshape(HEADS, HEAD, D_MODEL), ((0, 0), (HEAD, 0), (0, 0))).reshape(HEADS * LANES, D_MODEL)
    zl = jnp.zeros((LORA, WIDTH), BF16)
    w_lora = jnp.concatenate([jnp.concatenate([full['w_decay_up'], zl], 1),
                              jnp.concatenate([zl, full['w_iclr_up']], 1)], 0)

    hd = np.arange(WIDTH) // HEAD
    ones_blocks = jnp.asarray(hd[:, None] == hd[None, :], BF16)
    ones_half = ones_blocks[:WKV_HALF, :WKV_HALF]
    perm_np = np.zeros((LANES, LANES), np.float32)
    for d in range(MLA_ROPE // 2):
        perm_np[HEAD + 16 + d, HEAD + d] = -1.0
        perm_np[HEAD + d, HEAD + 16 + d] = 1.0
    perm = jnp.asarray(perm_np, BF16)
    inv = ROPE_THETA ** (-jnp.arange(0, MLA_ROPE, 2, dtype=F32) / MLA_ROPE)
    ang = positions[0].astype(F32)[:, None] * inv
    cos_a, sin_a = jnp.cos(ang), jnp.sin(ang)
    cs = jnp.concatenate([jnp.ones((seq, HEAD), F32), cos_a, cos_a, jnp.zeros((seq, LANES - MLA_QK), F32),
                          jnp.zeros((seq, HEAD), F32), sin_a, sin_a, jnp.zeros((seq, LANES - MLA_QK), F32)], axis=1)

    x2, tgt = x[0], loss_target[0]
    r_k2 = r_k.reshape(1, WIDTH)

    c8 = jnp.broadcast_to(c, (SUBLANES, D_MODEL))
    ada = ada_fwd(c8, b_ada, gathered)[:1]
    shift, scale, gate = ada[:, :D_MODEL], ada[:, D_MODEL:2 * D_MODEL], ada[:, 2 * D_MODEL:]

    f_in1, f_in2, f_in3 = _make_f_in((512, 1024)), _make_f_in((SHIFT_W, WIDTH)), _make_f_in((1024, 1024))
    tr = min(256, seq)
    p_mla, gpa = row_fwd("in1_fwd", f_in1, [x2], [shift, scale, w_g1], [], [512, 1024], tr)
    p_rwkv, gpb = row_fwd("in2_fwd", f_in2, [x2], [shift, scale, w_g2], [], [SHIFT_W, WIDTH], tr)
    ma, mb = row_fwd("in3_fwd", f_in3, [x2], [shift, scale, w_g3], [], [1024, 1024], tr)

    mla_par = [q_norm_g, kv_norm_g, w_uq_p, full['w_ukv']]
    q_f, kv_f, kpe = row_fwd("mla_pre_fwd", f_mla_pre, [p_mla, cs], mla_par, [perm], [1024, 1024, LANES], tr)
    ya, lse = attn_fwd(q_f, kv_f, kpe)

    u = shift_fwd(p_rwkv, mu_rwkv, tr)
    pre_par = [w0, a0, k_k, k_a, w_lora]
    rr, wd, k2, vv, an, bb = row_fwd("rwkv_pre_fwd", f_rwkv_pre, [u], pre_par, [ones_blocks], [WIDTH] * 6, tr)
    y_wkv, states = wkv_fwd(rr, wd, k2, vv, an, bb, ones_half)
    post_b_par = [r_k2, gn_g, gn_b]
    yb, = row_fwd("rwkv_post_fwd", f_rwkv_post, [y_wkv, rr, k2, vv], post_b_par, [ones_blocks], [WIDTH], tr)

    merge_rows, merge_par = [ya, gpa, yb, gpb, ma, mb], [w_pa_p, full['w_proj_b']]
    merged, = row_fwd("merge_fwd", f_merge, merge_rows, merge_par, [], [D_MODEL], tr)
    loss_rows, loss_par = [merged, x2, tgt], [gate, post_g, post_b, full['w_out']]
    lrows, = row_fwd("loss_fwd", f_loss, loss_rows, loss_par, [], [LANES], tr)
    loss = lax.psum(jnp.sum(lrows[:, 0]), ("x", "y", "c"))

    dl = jnp.broadcast_to((jnp.arange(LANES) == 0).astype(F32), (seq, LANES))
    (dmerged, dx_res), (dgate, dpost_g, dpost_b, dw_out) = row_bwd("loss_bwd", f_loss, loss_rows, 2, loss_par, [], [dl], tr)
    (dya, dgpa, dyb, dgpb, dma, dmb), (dw_pa_p, dw_pb) = row_bwd(
        "merge_bwd", f_merge, merge_rows, 6, merge_par, [], [dmerged], tr)

    (dy_wkv, dr1, dk1, dv1), (dr_k, dgn_g, dgn_b) = row_bwd(
        "rwkv_post_bwd", f_rwkv_post, [y_wkv, rr, k2, vv], 4, post_b_par, [ones_blocks], [dyb], tr)
    dr2, dwd, dk2, dv2, dan, dbb = wkv_bwd(rr, wd, k2, vv, an, bb, dy_wkv, states, ones_half)
    (du,), (dw0, da0, dk_k, dk_a, dw_lora) = row_bwd(
        "rwkv_pre_bwd", f_rwkv_pre, [u], 1, pre_par, [ones_blocks],
        [(dr1, dr2), dwd, (dk1, dk2), (dv1, dv2), dan, dbb], tr)
    dp_rwkv, dmu = shift_bwd(du, p_rwkv, mu_rwkv, tr)

    dq_f, dkv_f, dkpe = attn_bwd(q_f, kv_f, kpe, ya, lse, dya)
    (dp_mla,), (dqg, dkvg, dw_uq_p, dw_ukv) = row_bwd(
        "mla_pre_bwd", f_mla_pre, [p_mla, cs], 1, mla_par, [perm], [dq_f, dkv_f, dkpe], tr)

    (dx1,), (dsh1, dsc1, dw_g1) = row_bwd("in1_bwd", f_in1, [x2], 1, [shift, scale, w_g1], [], [dp_mla, dgpa], tr, [dx_res])
    (dx2,), (dsh2, dsc2, dw_g2) = row_bwd("in2_bwd", f_in2, [x2], 1, [shift, scale, w_g2], [], [dp_rwkv, dgpb], tr, [dx1])
    (dx3,), (dsh3, dsc3, dw_g3) = row_bwd("in3_bwd", f_in3, [x2], 1, [shift, scale, w_g3], [], [dma, dmb], tr, [dx2])
    grad_x = dx3[None]

    dada = jnp.concatenate([dsh1 + dsh2 + dsh3, dsc1 + dsc2 + dsc3, dgate], axis=1)
    local = {
        'w_in': jnp.concatenate([dw_g1[:, :384], dw_g1[:, 448:480], _unpad_heads_cols(dw_g1[:, 512:], HEAD, HEAD),
                                 dw_g2, dw_g3], axis=1),
        'w_uq': _unpad_heads_cols(dw_uq_p, MLA_QK, 0),
        'w_ukv': dw_ukv,
        'w_decay_up': dw_lora[:LORA, :WIDTH],
        'w_iclr_up': dw_lora[LORA:, WIDTH:],
        'w_proj_a': dw_pa_p.reshape(HEADS, LANES, D_MODEL)[:, HEAD:].reshape(WIDTH, D_MODEL),
        'w_proj_b': dw_pb,
        'w_out': dw_out,
    }
    small_local = {'b_ada': dada, 'q_norm_g': dqg, 'kv_norm_g': dkvg, 'mu_rwkv': dmu, 'w0': dw0, 'a0': da0,
                   'k_k': dk_k, 'k_a': dk_a, 'r_k': dr_k, 'gn_g': dgn_g, 'gn_b': dgn_b,
                   'post_g': dpost_g, 'post_b': dpost_b}

    def shard_of(g, axis, s):
        n = g.shape[axis] // 4
        return lax.slice_in_dim(g, s * n, (s + 1) * n, axis=axis)

    packed = jnp.stack([_pack_shard({n: shard_of(local[n], ax, s) for n, _, ax in _SHARDED if n != 'w_ada'}, F32, False)
                        for s in range(4)])
    keep = lax.dynamic_slice_in_dim(packed, my_c * GRAD_HALF, GRAD_HALF, 2).reshape(4 * PACK_ROWS, GRAD_HALF)
    give = lax.dynamic_slice_in_dim(packed, (1 - my_c) * GRAD_HALF, GRAD_HALF, 2).reshape(4 * PACK_ROWS, GRAD_HALF)
    pair_sum, = row_fwd("pair_sum", lambda p, q: (p + q,), [keep, pair_swap("swap_halves", give)], [], [],
                        [GRAD_HALF], PACK_ROWS // 2, BF16)
    received = chip_all_to_all("exchange_grads", pair_sum.reshape(4, PACK_ROWS, GRAD_HALF))
    my_sum = sum_slots(received, PACK_ROWS // 2)
    other_sum = pair_swap("swap_sums", my_sum)
    halves = [jnp.where(my_c == 0, my_sum, other_sum), jnp.where(my_c == 0, other_sum, my_sum)]
    g_shard = _unpack_shard(jnp.concatenate(halves, axis=1), False)

    small_pack = lambda d, extra=(): _pack_small([d[n] for n, _ in _SMALL] + list(extra))
    small_all = all_gather8("gather_small", small_pack(small_local, [c * jax.nn.sigmoid(c)]))
    sc_all = small_all[:, SMALL_USED:SMALL_USED + D_MODEL // LANES].reshape(N_DEV, D_MODEL)
    dada_all = small_all[:, :3 * D_MODEL // LANES].reshape(N_DEV, 3 * D_MODEL)
    my_cols = lax.dynamic_slice_in_dim(dada_all, (2 * lax.axis_index("x") + lax.axis_index("y")) * 768, 768, 1)
    g_shard['w_ada'] = ada_grad_shard([sc_all[b].reshape(D_MODEL, 1) for b in range(N_DEV)], my_cols)

    big = [{}, {}, {}, {}]
    for n in shard_names:
        w2, m2, v2 = given[n][0], given['m_' + n][0], given['v_' + n][0]
        cols = w2.shape[1]
        outs = row_fwd("adamw_" + n, f_adamw, [w2, g_shard[n], m2, v2], [], [], [cols] * 3, min(256, w2.shape[0]))
        for dst, val in zip(big, (g_shard[n], *outs)):
            dst[n] = val

    small_out = adamw_small(small_all, small_pack({n: given[n] for n, _ in _SMALL}),
                            small_pack({n: given['m_' + n] for n, _ in _SMALL}),
                            small_pack({n: given['v_' + n] for n, _ in _SMALL}))

    results = []
    for big_k, packed_small in zip(big, small_out):
        small = _unpack_small(packed_small)
        results.append([(big_k[n] if n in big_k else small[n]).reshape(given[n].shape) for n in _WEIGHTS])
    return (loss, grad_x, *results[0], *results[1], *results[2], *results[3])
```

```python
from typing import NamedTuple

import numpy as np
import jax
import jax.numpy as jnp
from jax import lax
from jax.experimental import pallas as pl
from jax.experimental.pallas import tpu as pltpu

F32 = jnp.float32
BF16 = jnp.bfloat16

D_MODEL = 1024
LN_EPS = 1e-5
RMS_EPS = 1e-6
GN_EPS = 64e-5
HEADS = 8
HEAD = 64
MLA_ROPE = 32
MLA_QK = HEAD + MLA_ROPE
ROPE_THETA = 10000.0
WIDTH = HEADS * HEAD
LORA = 64
SHIFT_W = 3 * WIDTH + 2 * LORA
CHUNK = 64
ALPHA = 2.0 ** 0.25

ADAM_LR, ADAM_B1, ADAM_B2, ADAM_EPS, ADAM_WD, ADAM_STEP = 0.001, 0.9, 0.999, 1e-08, 0.01, 10

LANES = 128
SUBLANES = 8
VMEM_LIMIT = 56 * 1024 * 1024
N_DEV = 8
MESH = pl.DeviceIdType.MESH
NEG = -1e30

_WEIGHTS = ['w_ada', 'b_ada', 'w_in', 'q_norm_g', 'w_uq', 'kv_norm_g', 'w_ukv', 'mu_rwkv', 'w0',
            'w_decay_up', 'a0', 'w_iclr_up', 'k_k', 'k_a', 'r_k', 'gn_g', 'gn_b', 'w_proj_a',
            'w_proj_b', 'w_out', 'post_g', 'post_b']
_SHARDED = [('w_ada', (1024, 3072), 1), ('w_in', (1024, 5152), 1), ('w_uq', (256, 768), 1),
            ('w_ukv', (128, 1024), 1), ('w_decay_up', (64, 512), 1), ('w_iclr_up', (64, 512), 1),
            ('w_proj_a', (512, 1024), 1), ('w_proj_b', (512, 1024), 1), ('w_out', (1024, 1024), 0)]
_SMALL = [('b_ada', 3072), ('q_norm_g', 256), ('kv_norm_g', 128), ('mu_rwkv', 1664), ('w0', 512),
          ('a0', 512), ('k_k', 512), ('k_a', 512), ('r_k', 512), ('gn_g', 512), ('gn_b', 512),
          ('post_g', 1024), ('post_b', 1024)]
PACK_ROWS = 256
PACK_COLS = 11264
HALF_COLS = PACK_COLS // 2
ADA_COLS = 4 * 768
GRAD_HALF = (PACK_COLS - ADA_COLS) // 2
SMALL_USED = 84
SMALL_ROWS = 96


def _bf(x):
    return x.astype(BF16)


def _dot(a, b, ca, cb):
    return lax.dot_general(a, b, (((ca,), (cb,)), ((), ())), preferred_element_type=F32)


class Weight(NamedTuple):
    value: jax.Array
    grad: jax.Array


@jax.custom_vjp
def _mm(a, w, w_grad):
    return _dot(_bf(a), _bf(w), 1, 0)


def _mm_fwd(a, w, w_grad):
    return _mm(a, w, w_grad), (a, w)


def _mm_bwd(res, g):
    a, w = res
    gb = _bf(g)
    return _dot(gb, _bf(w), 1, 1), jnp.zeros_like(w), _dot(_bf(a), gb, 0, 0)


_mm.defvjp(_mm_fwd, _mm_bwd)


def mm(a, w):
    if isinstance(w, Weight):
        return _mm(a, w.value, w.grad)
    return _dot(_bf(a), _bf(w), 1, 0)


def _split3(x):
    hi = _bf(x)
    r1 = x - hi.astype(F32)
    mid = _bf(r1)
    lo = _bf(r1 - mid.astype(F32))
    return hi, mid, lo


def _exact_dot(x, m, cm):
    hi, mid, lo = _split3(x)
    return _dot(hi, m, 1, cm) + _dot(mid, m, 1, cm) + _dot(lo, m, 1, cm)


@jax.custom_vjp
def segsum(x, ones_blocks):
    return _exact_dot(x, ones_blocks, 0)


def _segsum_fwd(x, ones_blocks):
    return segsum(x, ones_blocks), ones_blocks


def _segsum_bwd(ones_blocks, g):
    return _exact_dot(g, ones_blocks, 0), jnp.zeros_like(ones_blocks)


segsum.defvjp(_segsum_fwd, _segsum_bwd)


@jax.custom_vjp
def lane_perm(x, perm):
    return _exact_dot(x, perm, 0)


def _lane_perm_fwd(x, perm):
    return lane_perm(x, perm), perm


def _lane_perm_bwd(perm, g):
    return _exact_dot(g, perm, 1), jnp.zeros_like(perm)


lane_perm.defvjp(_lane_perm_fwd, _lane_perm_bwd)


def _silu(z):
    return z * jax.nn.sigmoid(z)


def _softplus(z):
    return jnp.maximum(z, 0.0) + jnp.log(1.0 + jnp.exp(-jnp.abs(z)))


def _layer_norm(x):
    xc = x - jnp.mean(x, -1, keepdims=True)
    return xc * lax.rsqrt(jnp.mean(xc * xc, -1, keepdims=True) + LN_EPS)


def _rope(t, cos_t, sin_t, perm):
    outs = []
    for h in range(t.shape[1] // LANES):
        th = t[:, h * LANES:(h + 1) * LANES]
        outs.append(th * cos_t + lane_perm(th, perm) * sin_t)
    return outs[0] if len(outs) == 1 else jnp.concatenate(outs, axis=1)


def _make_f_in(*split_groups):
    def f_in(x, shift, scale, *weights):
        h = _layer_norm(x) * (1.0 + scale) + shift
        outs = []
        for w, splits in zip(weights, split_groups):
            p, o = mm(h, w), 0
            for s in splits:
                outs.append(p[:, o:o + s])
                o += s
        return tuple(outs)
    return f_in


def f_mla_pre(p, cs, qg, kvg, w_uq, w_ukv, perm):
    q_c, kv_c, k_r = p[:, :256], p[:, 256:384], p[:, 384:512]
    cos_t, sin_t = cs[:, :LANES], cs[:, LANES:]
    qn = q_c * lax.rsqrt(jnp.mean(q_c * q_c, -1, keepdims=True) + RMS_EPS) * qg
    kvn = kv_c * lax.rsqrt(jnp.mean(kv_c * kv_c, -1, keepdims=True) + RMS_EPS) * kvg
    q = _rope(mm(qn, w_uq), cos_t, sin_t, perm)
    kv = mm(kvn, w_ukv)
    return q, kv, _rope(k_r, cos_t, sin_t, perm)


def f_rwkv_pre(u, w0, a0, k_k, k_a, w_lora, ones_blocks):
    r, k, v, lo = u[:, :WIDTH], u[:, WIDTH:2 * WIDTH], u[:, 2 * WIDTH:3 * WIDTH], u[:, 3 * WIDTH:]
    lane = lax.broadcasted_iota(jnp.int32, lo.shape, 1)
    dl = mm(jnp.where(lane < LORA, jnp.tanh(lo), lo), w_lora)
    w_log = -_softplus(-(w0 + dl[:, :WIDTH])) - 0.5
    decay = jnp.exp(-jnp.exp(w_log))
    a = jax.nn.sigmoid(a0 + dl[:, WIDTH:])
    kk = k * k_k
    kk = kk / jnp.maximum(jnp.sqrt(segsum(kk * kk, ones_blocks)), 1e-12)
    k2 = k * (1.0 + (a - 1.0) * k_a)
    return r, decay, k2, v, -kk, kk * a


def f_rwkv_post(y, r, k2, v, r_k, gn_g, gn_b, ones_blocks):
    yc = y - segsum(y, ones_blocks) * (1.0 / HEAD)
    yn = yc * lax.rsqrt(segsum(yc * yc, ones_blocks) * (1.0 / HEAD) + GN_EPS)
    return (yn * gn_g + gn_b + segsum(r * k2 * r_k, ones_blocks) * v,)


def f_merge(ya, gpa, yb, gpb, ma, mb, w_pa, w_pb):
    pa = mm(ya * _silu(gpa), w_pa)
    pb = mm(yb * _silu(gpb), w_pb)
    return (jax.nn.sigmoid(ma) * pa + jax.nn.sigmoid(mb) * pb,)


def f_loss(merged, x, tgt, gate, post_g, post_b, w_out):
    z = ALPHA * x + (1.0 + gate) * mm(merged, w_out)
    err = _layer_norm(z) * post_g + post_b - tgt
    lrow = 0.5 * jnp.mean(err * err, -1, keepdims=True)
    return (jnp.broadcast_to(lrow, (lrow.shape[0], LANES)),)


def f_adamw(w, g, m, v):
    m2 = ADAM_B1 * m + (1.0 - ADAM_B1) * g
    v2 = ADAM_B2 * v + (1.0 - ADAM_B2) * jnp.square(g)
    m_hat = m2 / (1.0 - ADAM_B1 ** ADAM_STEP)
    v_hat = v2 / (1.0 - ADAM_B2 ** ADAM_STEP)
    return -ADAM_LR * (m_hat / (jnp.sqrt(v_hat) + ADAM_EPS) + ADAM_WD * w), m2, v2


def _params(sem=("arbitrary",)):
    return pltpu.CompilerParams(dimension_semantics=sem, vmem_limit_bytes=VMEM_LIMIT)


def _row_spec(tr, a):
    return pl.BlockSpec((tr, a.shape[1]), lambda i: (i, 0))


def _full_spec(a):
    return pl.BlockSpec(a.shape, lambda i: (0,) * a.ndim)


def row_fwd(name, f, rows, params, consts, out_widths, tr, out_dtype=F32):
    n_rows = rows[0].shape[0]
    nr, npar, ncon = len(rows), len(params), len(consts)

    def body(*refs):
        rv = [r[...] for r in refs[:nr]]
        pv = [r[...] for r in refs[nr:nr + npar]]
        cv = [r[...] for r in refs[nr + npar:nr + npar + ncon]]
        outs = f(*rv, *pv, *cv)
        for o_ref, o in zip(refs[nr + npar + ncon:], outs):
            o_ref[...] = o.astype(o_ref.dtype)

    return pl.pallas_call(
        body, name=name, grid=(n_rows // tr,),
        in_specs=[_row_spec(tr, a) for a in rows] + [_full_spec(a) for a in list(params) + list(consts)],
        out_specs=[pl.BlockSpec((tr, w), lambda i: (i, 0)) for w in out_widths],
        out_shape=[jax.ShapeDtypeStruct((n_rows, w), out_dtype) for w in out_widths],
        compiler_params=_params(),
    )(*rows, *params, *consts)


def row_bwd(name, f, rows, n_diff, params, consts, douts, tr, add_rows=None):
    n_rows = rows[0].shape[0]
    douts = [d if isinstance(d, (tuple, list)) else (d,) for d in douts]
    counts = [len(d) for d in douts]
    flat_d = [a for d in douts for a in d]
    add_rows = add_rows or [None] * n_diff
    adds = [a for a in add_rows if a is not None]
    nr, npar, ncon, nd, na = len(rows), len(params), len(consts), len(flat_d), len(adds)

    def body(*refs):
        o = 0
        rv = [r[...] for r in refs[o:o + nr]]; o += nr
        pv = [Weight(r[...], jnp.zeros(r.shape, F32)) if r.dtype == BF16 else r[...] for r in refs[o:o + npar]]
        o += npar
        cv = [r[...] for r in refs[o:o + ncon]]; o += ncon
        dv = []
        for cnt in counts:
            s = refs[o][...]
            for e in range(1, cnt):
                s = s + refs[o + e][...]
            dv.append(s)
            o += cnt
        add_v = [r[...] for r in refs[o:o + na]]; o += na
        drow_refs = refs[o:o + n_diff]; o += n_diff
        dpar_refs = refs[o:o + npar]

        def g(*args):
            return tuple(f(*args[:n_diff], *rv[n_diff:], *args[n_diff:], *cv))

        _, vjp = jax.vjp(g, *rv[:n_diff], *pv)
        grads = vjp(tuple(dv))
        ai = 0
        for j, (r, gr) in enumerate(zip(drow_refs, grads[:n_diff])):
            if add_rows[j] is not None:
                gr = gr + add_v[ai]
                ai += 1
            r[...] = gr

        @pl.when(pl.program_id(0) == 0)
        def _():
            for r in dpar_refs:
                r[...] = jnp.zeros_like(r)

        for r, gr in zip(dpar_refs, grads[n_diff:]):
            r[...] += gr.grad if isinstance(gr, Weight) else gr

    outs = pl.pallas_call(
        body, name=name, grid=(n_rows // tr,),
        in_specs=([_row_spec(tr, a) for a in rows] + [_full_spec(a) for a in list(params) + list(consts)]
                  + [_row_spec(tr, a) for a in flat_d + adds]),
        out_specs=[_row_spec(tr, a) for a in rows[:n_diff]] + [_full_spec(a) for a in params],
        out_shape=([jax.ShapeDtypeStruct(a.shape, F32) for a in rows[:n_diff]]
                   + [jax.ShapeDtypeStruct(a.shape, F32) for a in params]),
        compiler_params=_params(),
    )(*rows, *params, *consts, *flat_d, *adds)
    return outs[:n_diff], outs[n_diff:]


def shift_fwd(p, mu, tr):
    n_rows, w = p.shape

    def body(p_ref, mu_ref, u_ref, carry):
        @pl.when(pl.program_id(0) == 0)
        def _():
            carry[...] = jnp.zeros_like(carry)

        x = p_ref[...]
        rolled = pltpu.roll(x, 1, 0)
        head = pltpu.roll(carry[...], 1, 0)
        fixed = jnp.concatenate([head, rolled[SUBLANES:]], axis=0)
        row = lax.broadcasted_iota(jnp.int32, x.shape, 0)
        prev = jnp.where(row == 0, fixed, rolled)
        u_ref[...] = x + (prev - x) * mu_ref[...]
        carry[...] = x[tr - SUBLANES:]

    return pl.pallas_call(
        body, name="shift_fwd", grid=(n_rows // tr,),
        in_specs=[_row_spec(tr, p), _full_spec(mu)],
        out_specs=_row_spec(tr, p),
        out_shape=jax.ShapeDtypeStruct(p.shape, F32),
        scratch_shapes=[pltpu.VMEM((SUBLANES, w), F32)],
        compiler_params=_params(),
    )(p, mu)


def shift_bwd(du, p, mu, tr):
    n_rows, w = p.shape
    nb = n_rows // tr

    def body(du_ref, p_ref, mu_ref, dp_ref, dmu_ref, carry):
        @pl.when(pl.program_id(0) == 0)
        def _():
            carry[...] = jnp.zeros_like(carry)
            dmu_ref[...] = jnp.zeros_like(dmu_ref)

        d = du_ref[...]
        rolled = pltpu.roll(d, tr - 1, 0)
        tail = pltpu.roll(carry[...], SUBLANES - 1, 0)
        fixed = jnp.concatenate([rolled[:tr - SUBLANES], tail], axis=0)
        row = lax.broadcasted_iota(jnp.int32, d.shape, 0)
        nxt = jnp.where(row == tr - 1, fixed, rolled)
        mu_v = mu_ref[...]
        dp_ref[...] = d * (1.0 - mu_v) + nxt * mu_v
        dmu_ref[...] += jnp.sum(p_ref[...] * (nxt - d), axis=0, keepdims=True)
        carry[...] = d[:SUBLANES]

    rev = lambda i: (nb - 1 - i, 0)
    return pl.pallas_call(
        body, name="shift_bwd", grid=(nb,),
        in_specs=[pl.BlockSpec((tr, w), rev), pl.BlockSpec((tr, w), rev), _full_spec(mu)],
        out_specs=[pl.BlockSpec((tr, w), rev), _full_spec(mu)],
        out_shape=[jax.ShapeDtypeStruct(p.shape, F32), jax.ShapeDtypeStruct(mu.shape, F32)],
        scratch_shapes=[pltpu.VMEM((SUBLANES, w), F32)],
        compiler_params=_params(),
    )(du, p, mu)


ATT_T = 256


def _att_rows(j):
    return pl.ds(pl.multiple_of(j * ATT_T, ATT_T), ATT_T)


def _att_prep(kv_ref, kpe_ref, kf_scr, vf_scr, n_blocks):
    lane = lax.broadcasted_iota(jnp.int32, (ATT_T, LANES), 1)

    def prep(j, _):
        rows = _att_rows(j)
        kv = kv_ref[rows, :]
        kf_scr[rows, :] = _bf(jnp.where(lane < HEAD, kv, kpe_ref[rows, :]))
        vf_scr[rows, :] = _bf(jnp.where(lane >= HEAD, kv, 0.0))
        return 0

    lax.fori_loop(0, n_blocks, prep, 0)


def _att_diag_mask():
    shift = CHUNK.bit_length() - 1
    qc = jnp.right_shift(lax.broadcasted_iota(jnp.int32, (ATT_T, ATT_T), 0), shift)
    kc = jnp.right_shift(lax.broadcasted_iota(jnp.int32, (ATT_T, ATT_T), 1), shift)
    return kc <= qc


def _wide(x):
    return jnp.concatenate([x] * (ATT_T // LANES), axis=1)


def attn_fwd(q, kv, kpe):
    seq = q.shape[0]
    nb = seq // ATT_T
    assert seq % (2 * ATT_T) == 0, "blocks are taken two per trip"
    scale = MLA_QK ** -0.5

    def body(q_ref, kv_ref, kpe_ref, o_ref, lse_ref, kf_scr, vf_scr):
        _att_prep(kv_ref, kpe_ref, kf_scr, vf_scr, nb)
        mask = _att_diag_mask()

        def scores(qb, kj):
            return _dot(qb, kf_scr[_att_rows(kj), :], 1, 1) * scale

        def update(s, kj, carry, masked):
            m, l, acc = carry
            if masked:
                s = jnp.where(mask, s, NEG)
            m_new = jnp.maximum(m, jnp.broadcast_to(jnp.max(s, -1, keepdims=True), m.shape))
            alpha = jnp.exp(m - m_new)
            p = jnp.exp(s - _wide(m_new))
            l = alpha * l + jnp.broadcast_to(jnp.sum(p, -1, keepdims=True), l.shape)
            acc = alpha * acc + _dot(_bf(p), vf_scr[_att_rows(kj), :], 1, 0)
            return m_new, l, acc

        def finish(rows, carry):
            m, l, acc = carry
            o_ref[rows, :] = acc / l
            lse_ref[rows, :] = m + jnp.log(l)

        def q_pair(qp, _):
            rows_a, rows_b = _att_rows(2 * qp), _att_rows(2 * qp + 1)
            qa, qb = _bf(q_ref[rows_a, :]), _bf(q_ref[rows_b, :])
            init = (jnp.full((ATT_T, LANES), NEG, F32), jnp.zeros((ATT_T, LANES), F32),
                    jnp.zeros((ATT_T, LANES), F32))

            def trip(kj, c):
                ca, cb, sa, sb = c
                sa_next, sb_next = scores(qa, kj + 1), scores(qb, kj + 1)
                return update(sa, kj, ca, False), update(sb, kj, cb, False), sa_next, sb_next

            ca, cb, sa, sb = lax.fori_loop(0, 2 * qp, trip, (init, init, scores(qa, 0), scores(qb, 0)))
            sb_last = scores(qb, 2 * qp + 1)
            ca = update(sa, 2 * qp, ca, True)
            cb = update(sb_last, 2 * qp + 1, update(sb, 2 * qp, cb, False), True)
            finish(rows_a, ca)
            finish(rows_b, cb)
            return 0

        lax.fori_loop(0, nb // 2, q_pair, 0)

    head = pl.BlockSpec((seq, LANES), lambda h: (0, h))
    return pl.pallas_call(
        body, name="attn_fwd", grid=(HEADS,),
        in_specs=[head, head, pl.BlockSpec((seq, LANES), lambda h: (0, 0))],
        out_specs=[head, head],
        out_shape=[jax.ShapeDtypeStruct((seq, HEADS * LANES), F32)] * 2,
        scratch_shapes=[pltpu.VMEM((seq, LANES), BF16)] * 2,
        compiler_params=_params(),
    )(q, kv, kpe)


def attn_bwd(q, kv, kpe, o, lse, do):
    seq = q.shape[0]
    nb = seq // ATT_T
    assert seq % (2 * ATT_T) == 0, "blocks are taken two per trip"
    scale = MLA_QK ** -0.5

    def body(q_ref, kv_ref, kpe_ref, o_ref, lse_ref, do_ref, dq_ref, dkv_ref, dkpe_ref,
             kf_scr, vf_scr, qb_scr, dob_scr, dsum):
        lane = lax.broadcasted_iota(jnp.int32, (ATT_T, LANES), 1)

        @pl.when(pl.program_id(0) == 0)
        def _():
            dkpe_ref[...] = jnp.zeros_like(dkpe_ref)

        dq_ref[...] = jnp.zeros_like(dq_ref)
        _att_prep(kv_ref, kpe_ref, kf_scr, vf_scr, nb)

        def pre(j, _):
            rows = _att_rows(j)
            d = do_ref[rows, :]
            qb_scr[rows, :] = _bf(q_ref[rows, :])
            dob_scr[rows, :] = _bf(d)
            dsum[rows, :] = jnp.broadcast_to(jnp.sum(d * o_ref[rows, :], -1, keepdims=True), (ATT_T, LANES))
            return 0

        lax.fori_loop(0, nb, pre, 0)
        mask = _att_diag_mask()

        def front(kf, vf, qi):
            rows = _att_rows(qi)
            return _dot(qb_scr[rows, :], kf, 1, 1), _dot(dob_scr[rows, :], vf, 1, 1)

        def back(kf, qi, fr, carry, masked):
            s, dp = fr
            dk, dv = carry
            rows = _att_rows(qi)
            qb, dob = qb_scr[rows, :], dob_scr[rows, :]
            p = jnp.exp(s * scale - _wide(lse_ref[rows, :]))
            if masked:
                p = jnp.where(mask, p, 0.0)
            ds = _bf(p * (dp - _wide(dsum[rows, :])) * scale)
            return (dk + _dot(ds, qb, 0, 0), dv + _dot(_bf(p), dob, 0, 0)), _dot(ds, kf, 1, 0)

        def store(krows, carry):
            dk, dv = carry
            dkv_ref[krows, :] = jnp.where(lane < HEAD, dk, dv)
            dkpe_ref[krows, :] += jnp.where((lane >= HEAD) & (lane < MLA_QK), dk, 0.0)

        def k_pair(kp, _):
            ka, kb = 2 * kp, 2 * kp + 1
            rows_a, rows_b = _att_rows(ka), _att_rows(kb)
            kfa, vfa, kfb, vfb = kf_scr[rows_a, :], vf_scr[rows_a, :], kf_scr[rows_b, :], vf_scr[rows_b, :]
            zero = jnp.zeros((ATT_T, LANES), F32)
            ca, dq_a = back(kfa, ka, front(kfa, vfa, ka), (zero, zero), True)
            dq_ref[rows_a, :] += dq_a
            ca, dq_a = back(kfa, kb, front(kfa, vfa, kb), ca, False)
            cb, dq_b = back(kfb, kb, front(kfb, vfb, kb), (zero, zero), True)
            dq_ref[rows_b, :] += dq_a + dq_b

            def both(qi, c):
                ca, cb, fa, fb = c
                nxt = jnp.minimum(qi + 1, nb - 1)
                fa_next, fb_next = front(kfa, vfa, nxt), front(kfb, vfb, nxt)
                ca, dq_a = back(kfa, qi, fa, ca, False)
                cb, dq_b = back(kfb, qi, fb, cb, False)
                dq_ref[_att_rows(qi), :] += dq_a + dq_b
                return ca, cb, fa_next, fb_next

            first = jnp.minimum(kb + 1, nb - 1)
            ca, cb, _, _ = lax.fori_loop(kb + 1, nb, both, (ca, cb, front(kfa, vfa, first), front(kfb, vfb, first)))
            store(rows_a, ca)
            store(rows_b, cb)
            return 0

        lax.fori_loop(0, nb // 2, k_pair, 0)

    head = pl.BlockSpec((seq, LANES), lambda h: (0, h))
    shared = pl.BlockSpec((seq, LANES), lambda h: (0, 0))
    return pl.pallas_call(
        body, name="attn_bwd", grid=(HEADS,),
        in_specs=[head, head, shared, head, head, head],
        out_specs=[head, head, shared],
        out_shape=[jax.ShapeDtypeStruct((seq, HEADS * LANES), F32)] * 2
        + [jax.ShapeDtypeStruct((seq, LANES), F32)],
        scratch_shapes=[pltpu.VMEM((seq, LANES), BF16)] * 4 + [pltpu.VMEM((seq, LANES), F32)],
        compiler_params=_params(),
    )(q, kv, kpe, o, lse, do)


WKV_TB = 128
WKV_GROUP = SUBLANES
WKV_HALF = WIDTH // 2


def _wkv_consts():
    row = lax.broadcasted_iota(jnp.int32, (HEAD, WKV_HALF), 0)
    lane = lax.broadcasted_iota(jnp.int32, (HEAD, WKV_HALF), 1)
    diag = row == jnp.bitwise_and(lane, HEAD - 1)
    sub = lax.broadcasted_iota(jnp.int32, (WKV_GROUP, WKV_HALF), 0)
    return diag, sub


def _halves(x):
    return [x[:, :WKV_HALF], x[:, WKV_HALF:]]


def _diag_rows(row, diag):
    return _bf(jnp.where(diag, jnp.broadcast_to(row, diag.shape), 0.0))


def _put_row(tile, row, i, sub):
    return jnp.where(sub == i, jnp.broadcast_to(row, tile.shape), tile)


def _col_sum(x):
    return jnp.sum(x, axis=0, keepdims=True)


def _step(x, i):
    return x[i * HEAD:(i + 1) * HEAD]


def _head_dots(prods, ones_b, sub):
    tile = jnp.zeros((WKV_GROUP, WKV_HALF), F32)
    for i, p in enumerate(prods):
        tile = _put_row(tile, p, i, sub)
    res = _exact_dot(tile, ones_b, 0)
    return [res[i:i + 1] for i in range(len(prods))]


def wkv_fwd(r, w, k, v, a, b, ones_half):
    seq = r.shape[0]

    def body(r_ref, w_ref, k_ref, v_ref, a_ref, b_ref, ones_ref, y_ref, st_ref, s_scr):
        @pl.when(pl.program_id(0) == 0)
        def _():
            s_scr[...] = jnp.zeros_like(s_scr)

        ones_b = ones_ref[...]
        diag, sub = _wkv_consts()

        ng = WKV_TB // WKV_GROUP
        last = WKV_GROUP - 2

        def rows_of(g):
            return pl.ds(pl.multiple_of(g * WKV_GROUP, WKV_GROUP), WKV_GROUP)

        def pair_rows(x8, t):
            return jnp.concatenate([_diag_rows(x8[t:t + 1], diag), _diag_rows(x8[t + 1:t + 2], diag)], axis=0)

        def put_y(g, pairs_y):
            tile = _halves(y_ref[rows_of(g), :])
            for hf in range(2):
                tile[hf] = _put_row(_put_row(tile[hf], pairs_y[hf][0], last, sub), pairs_y[hf][1], last + 1, sub)
            y_ref[rows_of(g), :] = jnp.concatenate(tile, axis=1)

        def read_out(yexp):
            return _col_sum(jnp.where(diag, yexp[:HEAD], 0.0)), _col_sum(jnp.where(diag, yexp[HEAD:], 0.0))

        def group(g, carry):
            state, v_cur, read = (list(c) for c in carry)
            base = pl.multiple_of(g * WKV_GROUP, WKV_GROUP)
            rows = rows_of(g)
            r8, w8, k8, v8, a8, b8 = (_halves(ref[rows, :]) for ref in (r_ref, w_ref, k_ref, v_ref, a_ref, b_ref))
            v_after = _halves(v_ref[rows_of(jnp.minimum(g + 1, ng - 1)), :])
            evens = range(0, WKV_GROUP, 2)
            dots = [_head_dots([b8[hf][t:t + 1] * a8[hf][t + 1:t + 2] for t in evens]
                               + [k8[hf][t:t + 1] * a8[hf][t + 1:t + 2] for t in evens], ones_b, sub) for hf in range(2)]
            y8 = [jnp.zeros((WKV_GROUP, WKV_HALF), F32)] * 2
            y_before = [None, None]
            for t in evens:
                s0, s1 = slice(t, t + 1), slice(t + 1, t + 2)
                both = []
                for hf in range(2):
                    s_in = state[hf]
                    v_next = pair_rows(v8[hf], t + 2) if t < last else pair_rows(v_after[hf], 0)
                    res = _dot(jnp.concatenate([_bf(s_in * a8[hf][s0]), _bf(s_in * (w8[hf][s0] * a8[hf][s1])),
                                                v_next, read[hf]], axis=0), ones_b, 1, 0)
                    sa0, v0, v1 = res[:HEAD], v_cur[hf][:HEAD], v_cur[hf][HEAD:]
                    st0 = s_in * w8[hf][s0] + sa0 * b8[hf][s0] + v0 * k8[hf][s0]
                    sa1 = res[HEAD:2 * HEAD] + sa0 * dots[hf][t // 2] + v0 * dots[hf][WKV_GROUP // 2 + t // 2]
                    st1 = st0 * w8[hf][s1] + sa1 * b8[hf][s1] + v1 * k8[hf][s1]
                    both.append((st0, st1))
                    state[hf], v_cur[hf] = st1, res[2 * HEAD:4 * HEAD]
                    read[hf] = jnp.concatenate([_bf(st0 * r8[hf][s0]), _bf(st1 * r8[hf][s1])], axis=0)
                    ya, yb = read_out(res[4 * HEAD:])
                    if t == 0:
                        y_before[hf] = (ya, yb)
                    else:
                        y8[hf] = _put_row(_put_row(y8[hf], ya, t - 2, sub), yb, t - 1, sub)
                for j in range(2):
                    st_ref[base + t + j] = jnp.concatenate([both[0][j], both[1][j]], axis=1)
            y_ref[rows, :] = jnp.concatenate(y8, axis=1)
            put_y(jnp.maximum(g - 1, 0), y_before)
            return tuple(state), tuple(v_cur), tuple(read)

        v_first = _halves(v_ref[rows_of(0), :])
        init = (tuple(_halves(s_scr[...])),
                tuple(_dot(pair_rows(v_first[hf], 0), ones_b, 1, 0) for hf in range(2)),
                tuple(jnp.zeros((2 * HEAD, WKV_HALF), BF16) for _ in range(2)))
        fin, _, read = lax.fori_loop(0, ng, group, init)
        put_y(ng - 1, [read_out(_dot(read[hf], ones_b, 1, 0)) for hf in range(2)])
        s_scr[...] = jnp.concatenate(fin, axis=1)

    vec = pl.BlockSpec((WKV_TB, WIDTH), lambda i: (i, 0))
    return pl.pallas_call(
        body, name="wkv_fwd", grid=(seq // WKV_TB,),
        in_specs=[vec] * 6 + [_full_spec(ones_half)],
        out_specs=[vec, pl.BlockSpec((WKV_TB, HEAD, WIDTH), lambda i: (i, 0, 0))],
        out_shape=[jax.ShapeDtypeStruct((seq, WIDTH), F32), jax.ShapeDtypeStruct((seq, HEAD, WIDTH), F32)],
        scratch_shapes=[pltpu.VMEM((HEAD, WIDTH), F32)],
        compiler_params=_params(),
    )(r, w, k, v, a, b, ones_half)


def wkv_bwd(r, w, k, v, a, b, dy, states, ones_half):
    seq = r.shape[0]
    nb = seq // WKV_TB
    ng = WKV_TB // WKV_GROUP

    def body(r_ref, w_ref, k_ref, v_ref, a_ref, b_ref, dy_ref, st_ref, halo_ref, ones_ref,
             dr_ref, dw_ref, dk_ref, dv_ref, da_ref, db_ref, ds_scr):
        blk = nb - 1 - pl.program_id(0)

        @pl.when(pl.program_id(0) == 0)
        def _():
            ds_scr[...] = jnp.zeros_like(ds_scr)

        ones_b = ones_ref[...]
        diag, sub = _wkv_consts()
        before_block = jnp.where(blk == 0, 0.0, halo_ref[0])

        def rows_of(g):
            return pl.ds(pl.multiple_of(g * WKV_GROUP, WKV_GROUP), WKV_GROUP)

        def expand_rows(hf, dy8, v8, a8, t, s_t, s_u):
            s1, s0 = slice(t, t + 1), slice(t - 1, t)
            return jnp.concatenate([_diag_rows(dy8[hf][s1], diag), _diag_rows(dy8[hf][s0], diag),
                                    _diag_rows(v8[hf][s1], diag), _diag_rows(v8[hf][s0], diag),
                                    _bf(s_t[hf] * a8[hf][s1]), _bf(s_u[hf] * a8[hf][s0])], axis=0)

        def read_out(x):
            return _col_sum(jnp.where(diag, x[:HEAD], 0.0)), _col_sum(jnp.where(diag, x[HEAD:], 0.0))

        def put_dv(g, pair_dv):
            tile = _halves(dv_ref[rows_of(g), :])
            for hf in range(2):
                tile[hf] = _put_row(_put_row(tile[hf], pair_dv[hf][0], 1, sub), pair_dv[hf][1], 0, sub)
            dv_ref[rows_of(g), :] = jnp.concatenate(tile, axis=1)

        def group(gg, carry):
            dstate, e_cur, dv_pend = (list(c) for c in carry)
            g = ng - 1 - gg
            base = pl.multiple_of(g * WKV_GROUP, WKV_GROUP)
            rows = rows_of(g)
            r8, w8, k8, v8, a8, b8, dy8 = (
                _halves(ref[rows, :]) for ref in (r_ref, w_ref, k_ref, v_ref, a_ref, b_ref, dy_ref))
            g_next = jnp.maximum(g - 1, 0)
            base_next = pl.multiple_of(g_next * WKV_GROUP, WKV_GROUP)
            dy8n, v8n, a8n = (_halves(ref[rows_of(g_next), :]) for ref in (dy_ref, v_ref, a_ref))
            zero8 = jnp.zeros((WKV_GROUP, WKV_HALF), F32)
            out = {n: [zero8, zero8] for n in ("dr", "dw", "dk", "dv", "da", "db")}
            before_group = jnp.where(g == 0, before_block, st_ref[jnp.maximum(base - 1, 0)])
            states = [_halves(before_group)] + [_halves(st_ref[base + i]) for i in range(WKV_GROUP)]
            odds = range(1, WKV_GROUP, 2)
            dots = [_head_dots([a8[hf][t:t + 1] * b8[hf][t - 1:t] for t in odds]
                               + [r8[hf][t - 1:t] * b8[hf][t - 1:t] for t in odds], ones_b, sub) for hf in range(2)]
            dv_after = [None, None]

            def emit(hf, i, d_i, dsa_i, dy_i, v_i, sa_i):
                s_p, s_t = states[i][hf], states[i + 1][hf]
                for n, val in (("dr", _col_sum(s_t * dy_i)), ("dw", _col_sum(d_i * s_p)), ("db", _col_sum(d_i * sa_i)),
                               ("da", _col_sum(s_p * dsa_i)), ("dk", _col_sum(d_i * v_i))):
                    out[n][hf] = _put_row(out[n][hf], val, i, sub)

            for t in reversed(odds):
                s1, s0 = slice(t, t + 1), slice(t - 1, t)
                for hf in range(2):
                    dy1, dy0, v1, v0, sa1, sa0 = (_step(e_cur[hf], j) for j in range(6))
                    d1 = dstate[hf] + dy1 * r8[hf][s1]
                    if t > 1:
                        nxt = expand_rows(hf, dy8, v8, a8, t - 2, states[t - 2], states[t - 3])
                    else:
                        nxt = expand_rows(hf, dy8n, v8n, a8n, WKV_GROUP - 1, _halves(st_ref[base_next + WKV_GROUP - 2]),
                                          _halves(st_ref[base_next + WKV_GROUP - 3]))
                    res = _dot(jnp.concatenate([_bf(d1 * b8[hf][s1]), _bf(d1 * (w8[hf][s1] * b8[hf][s0])),
                                                nxt, dv_pend[hf]], axis=0), ones_b, 1, 0)
                    dsa1 = res[:HEAD]
                    d0 = d1 * w8[hf][s1] + dsa1 * a8[hf][s1] + dy0 * r8[hf][s0]
                    dsa0 = res[HEAD:2 * HEAD] + dsa1 * dots[hf][t // 2] + dy0 * dots[hf][WKV_GROUP // 2 + t // 2]
                    dstate[hf] = d0 * w8[hf][s0] + dsa0 * a8[hf][s0]
                    e_cur[hf] = res[2 * HEAD:8 * HEAD]
                    dv_pend[hf] = jnp.concatenate([_bf(d1 * k8[hf][s1]), _bf(d0 * k8[hf][s0])], axis=0)
                    emit(hf, t, d1, dsa1, dy1, v1, sa1)
                    emit(hf, t - 1, d0, dsa0, dy0, v0, sa0)
                    dv_a, dv_b = read_out(res[8 * HEAD:])
                    if t == WKV_GROUP - 1:
                        dv_after[hf] = (dv_a, dv_b)
                    else:
                        out["dv"][hf] = _put_row(_put_row(out["dv"][hf], dv_a, t + 2, sub), dv_b, t + 1, sub)
            for ref, n in ((dr_ref, "dr"), (dw_ref, "dw"), (dk_ref, "dk"), (dv_ref, "dv"), (da_ref, "da"), (db_ref, "db")):
                ref[rows, :] = jnp.concatenate(out[n], axis=1)
            put_dv(jnp.minimum(g + 1, ng - 1), dv_after)
            return tuple(dstate), tuple(e_cur), tuple(dv_pend)

        top = rows_of(ng - 1)
        dy8t, v8t, a8t = (_halves(ref[top, :]) for ref in (dy_ref, v_ref, a_ref))
        s_t, s_u = _halves(st_ref[WKV_TB - 2]), _halves(st_ref[WKV_TB - 3])
        init = (tuple(_halves(ds_scr[...])),
                tuple(_dot(expand_rows(hf, dy8t, v8t, a8t, WKV_GROUP - 1, s_t, s_u), ones_b, 1, 0) for hf in range(2)),
                tuple(jnp.zeros((2 * HEAD, WKV_HALF), BF16) for _ in range(2)))
        fin, _, dv_pend = lax.fori_loop(0, ng, group, init)
        put_dv(0, [read_out(_dot(dv_pend[hf], ones_b, 1, 0)) for hf in range(2)])
        ds_scr[...] = jnp.concatenate(fin, axis=1)

    vec = pl.BlockSpec((WKV_TB, WIDTH), lambda i: (nb - 1 - i, 0))
    return pl.pallas_call(
        body, name="wkv_bwd", grid=(nb,),
        in_specs=[vec] * 7 + [
            pl.BlockSpec((WKV_TB, HEAD, WIDTH), lambda i: (nb - 1 - i, 0, 0)),
            pl.BlockSpec((1, HEAD, WIDTH), lambda i: (jnp.maximum((nb - 1 - i) * WKV_TB - 1, 0), 0, 0)),
            _full_spec(ones_half)],
        out_specs=[vec] * 6,
        out_shape=[jax.ShapeDtypeStruct((seq, WIDTH), F32)] * 6,
        scratch_shapes=[pltpu.VMEM((HEAD, WIDTH), F32)],
        compiler_params=_params(),
    )(r, w, k, v, a, b, dy, states, states, ones_half)


def ada_fwd(c8, b_ada, gathered):
    cols = 3 * D_MODEL // 4

    def body(c_ref, b_ref, w_ref, o_ref):
        @pl.when(pl.program_id(1) == 0)
        def _():
            o_ref[...] = jnp.broadcast_to(b_ref[...], o_ref.shape)

        o_ref[...] += mm(_silu(c_ref[...]), w_ref[0])

    return pl.pallas_call(
        body, name="ada_fwd", grid=(4, D_MODEL // PACK_ROWS),
        in_specs=[pl.BlockSpec((SUBLANES, PACK_ROWS), lambda s, i: (0, i)),
                  pl.BlockSpec((1, cols), lambda s, i: (0, s)),
                  pl.BlockSpec((1, PACK_ROWS, cols), lambda s, i: (2 * s, 0, i))],
        out_specs=pl.BlockSpec((SUBLANES, cols), lambda s, i: (0, s)),
        out_shape=jax.ShapeDtypeStruct((SUBLANES, 3 * D_MODEL), F32),
        compiler_params=_params(("arbitrary", "arbitrary")),
    )(c8, b_ada, gathered)


def ada_grad_shard(sc_cols, dada_rows):
    n = len(sc_cols)

    def body(*refs):
        d_ref, o_ref = refs[n], refs[n + 1]
        acc = refs[0][...] * d_ref[0:1, :]
        for b in range(1, n):
            acc = acc + refs[b][...] * d_ref[b:b + 1, :]
        o_ref[...] = acc

    return pl.pallas_call(
        body, name="ada_grad_shard",
        out_shape=jax.ShapeDtypeStruct((sc_cols[0].shape[0], dada_rows.shape[1]), F32),
        compiler_params=pltpu.CompilerParams(vmem_limit_bytes=VMEM_LIMIT),
    )(*sc_cols, dada_rows)


def sum_slots(buf, tr):
    n, rows, cols = buf.shape

    def body(b_ref, o_ref):
        acc = b_ref[0].astype(F32)
        for s in range(1, n):
            acc = acc + b_ref[s].astype(F32)
        o_ref[...] = acc

    return pl.pallas_call(
        body, name="sum_slots", grid=(rows // tr,),
        in_specs=[pl.BlockSpec((n, tr, cols), lambda i: (0, i, 0))],
        out_specs=pl.BlockSpec((tr, cols), lambda i: (i, 0)),
        out_shape=jax.ShapeDtypeStruct((rows, cols), F32),
        compiler_params=_params(),
    )(buf)


def adamw_small(gathered, w, m, v):
    n = gathered.shape[0]

    def body(g_ref, w_ref, m_ref, v_ref, go_ref, d_ref, mo_ref, vo_ref):
        g = g_ref[0]
        for s in range(1, n):
            g = g + g_ref[s]
        go_ref[...] = g
        d_ref[...], mo_ref[...], vo_ref[...] = f_adamw(w_ref[...], g, m_ref[...], v_ref[...])

    return pl.pallas_call(
        body, name="adamw_small",
        out_shape=[jax.ShapeDtypeStruct(w.shape, F32)] * 4,
        compiler_params=pltpu.CompilerParams(vmem_limit_bytes=VMEM_LIMIT),
    )(gathered, w, m, v)


def _coords():
    return lax.axis_index("x"), lax.axis_index("y"), lax.axis_index("c")


def _flip(v, bit):
    return 1 - v if bit else v


def _hbm_call(body, name, out_shape, n_sems, *args):
    any_spec = pl.BlockSpec(memory_space=pl.ANY)
    return pl.pallas_call(
        body, name=name, out_shape=out_shape,
        in_specs=[any_spec] * len(args), out_specs=any_spec,
        scratch_shapes=[pltpu.SemaphoreType.DMA((n_sems,)), pltpu.SemaphoreType.DMA((n_sems,)),
                        pltpu.SemaphoreType.DMA],
    )(*args)


def all_gather8(name, block):
    def body(x_ref, out_ref, send_sems, recv_sems, local_sem):
        x, y, c = _coords()
        me, sibling = (x, y, c), (x, y, 1 - c)
        x_nbr, y_nbr, diagonal = (1 - x, y), (x, 1 - y), (1 - x, 1 - y)
        relay_from = (c * x + (1 - c) * (1 - x), c * (1 - y) + (1 - c) * y)
        relay_to = (c * (1 - x) + (1 - c) * x, c * y + (1 - c) * (1 - y))

        def slot(px, py, pc):
            return out_ref.at[4 * px + 2 * py + pc]

        def copy(k, blk, to, src=None):
            return pltpu.make_async_remote_copy(
                src_ref=slot(*blk) if src is None else src, dst_ref=slot(*blk),
                send_sem=send_sems.at[k], recv_sem=recv_sems.at[k], device_id=to, device_id_type=MESH)

        mine = pltpu.make_async_copy(x_ref, slot(*me), local_sem)
        mine.start()
        first = [copy(0, me, sibling, src=x_ref), copy(1, me, (*x_nbr, c), src=x_ref), copy(2, me, (*y_nbr, c), src=x_ref)]
        for cp in first:
            cp.start()
        copy(1, (*x_nbr, c), me).wait_recv()
        copy(2, (*y_nbr, c), me).wait_recv()
        later = [copy(3, (*relay_from, c), (*relay_to, c)), copy(4, (*x_nbr, c), sibling), copy(5, (*y_nbr, c), sibling)]
        for cp in later:
            cp.start()
        copy(3, (*diagonal, c), me).wait_recv()
        last = copy(6, (*diagonal, c), sibling)
        last.start()
        copy(0, sibling, me).wait_recv()
        for k, chip in ((4, x_nbr), (5, y_nbr), (6, diagonal)):
            copy(k, (*chip, 1 - c), me).wait_recv()
        for cp in first + later + [last]:
            cp.wait_send()
        mine.wait()

    return _hbm_call(body, name, jax.ShapeDtypeStruct((N_DEV,) + block.shape, block.dtype), 7, block)


def pair_swap(name, block):
    def body(x_ref, out_ref, send_sems, recv_sems, local_sem):
        x, y, c = _coords()
        cp = pltpu.make_async_remote_copy(
            src_ref=x_ref, dst_ref=out_ref, send_sem=send_sems.at[0], recv_sem=recv_sems.at[0],
            device_id=(x, y, 1 - c), device_id_type=MESH)
        cp.start()
        cp.wait_recv()
        cp.wait_send()

    return _hbm_call(body, name, jax.ShapeDtypeStruct(block.shape, block.dtype), 1, block)


def chip_all_to_all(name, buf):
    def body(x_ref, out_ref, send_sems, recv_sems, local_sem):
        x, y, c = _coords()
        me = 2 * x + y
        mine = pltpu.make_async_copy(x_ref.at[me], out_ref.at[me], local_sem)
        mine.start()
        copies = []
        for k in range(1, 4):
            px, py = _flip(x, k & 2), _flip(y, k & 1)
            copies.append(pltpu.make_async_remote_copy(
                src_ref=x_ref.at[2 * px + py], dst_ref=out_ref.at[me],
                send_sem=send_sems.at[k - 1], recv_sem=recv_sems.at[k - 1],
                device_id=(px, py, c), device_id_type=MESH))
        for cp in copies:
            cp.start()
        for cp in copies:
            cp.wait_recv()
        for cp in copies:
            cp.wait_send()
        mine.wait()

    return _hbm_call(body, name, jax.ShapeDtypeStruct(buf.shape, buf.dtype), 3, buf)


def _col_blocks(a, cols):
    a = jnp.pad(a, ((0, 0), (0, cols - a.shape[1])))
    return [a[i * PACK_ROWS:(i + 1) * PACK_ROWS] for i in range(a.shape[0] // PACK_ROWS)]


def _pack_shard(sh, dtype, with_ada=True):
    lora = jnp.concatenate([sh['w_decay_up'], sh['w_iclr_up']], axis=1)
    misc = jnp.concatenate([sh['w_ukv'], lora, jnp.zeros((LORA, 2 * LANES), lora.dtype)], axis=0)
    blocks = ((_col_blocks(sh['w_ada'], 768) if with_ada else [])
              + _col_blocks(sh['w_in'], 1408) + _col_blocks(sh['w_proj_a'], 256)
              + _col_blocks(sh['w_proj_b'], 256) + [sh['w_out']] + _col_blocks(sh['w_uq'], 256) + [misc])
    return jnp.concatenate([b.astype(dtype) for b in blocks], axis=1)


def _unpack_shard(p, with_ada=True):
    o = [0]

    def take(n_blocks, cols, used):
        blocks = [p[:, o[0] + i * cols:o[0] + (i + 1) * cols] for i in range(n_blocks)]
        o[0] += n_blocks * cols
        return jnp.concatenate(blocks, axis=0)[:, :used]

    out = {'w_ada': take(4, 768, 768)} if with_ada else {}
    out.update({'w_in': take(4, 1408, 1288), 'w_proj_a': take(2, 256, 256),
                'w_proj_b': take(2, 256, 256), 'w_out': take(1, 1024, 1024), 'w_uq': take(1, 256, 192)})
    misc = take(1, 256, 256)
    out['w_ukv'] = misc[:2 * LORA]
    out['w_decay_up'] = misc[2 * LORA:3 * LORA, :LANES]
    out['w_iclr_up'] = misc[2 * LORA:3 * LORA, LANES:]
    return out


def _pack_small(parts):
    flat = jnp.concatenate([p.reshape(-1) for p in parts])
    return jnp.pad(flat, (0, SMALL_ROWS * LANES - flat.shape[0])).reshape(SMALL_ROWS, LANES)


def _unpack_small(packed):
    flat, out, o = packed.reshape(-1), {}, 0
    for name, n in _SMALL:
        out[name] = flat[o:o + n]
        o += n
    return out


def _pad_heads_cols(w, used, left):
    k = w.shape[0]
    return jnp.pad(w.reshape(k, HEADS, used), ((0, 0), (0, 0), (left, LANES - used - left))).reshape(k, HEADS * LANES)


def _unpad_heads_cols(w, used, left):
    k = w.shape[0]
    return w.reshape(k, HEADS, LANES)[:, :, left:left + used].reshape(k, HEADS * used)


def kernel(x, c, positions, w_ada, b_ada, w_in, q_norm_g, w_uq, kv_norm_g, w_ukv, mu_rwkv, w0, w_decay_up, a0, w_iclr_up, k_k, k_a, r_k, gn_g, gn_b, w_proj_a, w_proj_b, w_out, post_g, post_b, loss_target, m_w_ada, m_b_ada, m_w_in, m_q_norm_g, m_w_uq, m_kv_norm_g, m_w_ukv, m_mu_rwkv, m_w0, m_w_decay_up, m_a0, m_w_iclr_up, m_k_k, m_k_a, m_r_k, m_gn_g, m_gn_b, m_w_proj_a, m_w_proj_b, m_w_out, m_post_g, m_post_b, v_w_ada, v_b_ada, v_w_in, v_q_norm_g, v_w_uq, v_kv_norm_g, v_w_ukv, v_mu_rwkv, v_w0, v_w_decay_up, v_a0, v_w_iclr_up, v_k_k, v_k_a, v_r_k, v_gn_g, v_gn_b, v_w_proj_a, v_w_proj_b, v_w_out, v_post_g, v_post_b):
    given = dict(locals())
    seq = x.shape[1]
    my_c = lax.axis_index("c")

    shard_names = [n for n, _, _ in _SHARDED]
    w_pack = _pack_shard({n: given[n][0] for n in shard_names}, BF16)
    my_half = lax.dynamic_slice_in_dim(w_pack, my_c * HALF_COLS, HALF_COLS, 1)
    gathered = all_gather8("gather_weights", my_half)
    shards = [_unpack_shard(jnp.concatenate([gathered[2 * s], gathered[2 * s + 1]], axis=1)) for s in range(4)]
    full = {n: jnp.concatenate([sh[n] for sh in shards], axis=ax) for n, _, ax in _SHARDED}

    wi = full['w_in']
    zcol = lambda n: jnp.zeros((D_MODEL, n), BF16)
    w_g1 = jnp.concatenate([wi[:, :384], zcol(HEAD), wi[:, 384:416], zcol(LANES - MLA_QK),
                            _pad_heads_cols(wi[:, 416:928], HEAD, HEAD)], axis=1)
    w_g2 = wi[:, 928:3104]
    w_g3 = wi[:, 3104:5152]
    w_uq_p = _pad_heads_cols(full['w_uq'], MLA_QK, 0)
    w_pa_p = jnp.pad(full['w_proj_a'].reshape(HEADS, HEAD, D_MODEL), ((0, 0), (HEAD, 0), (0, 0))).reshape(HEADS * LANES, D_MODEL)
    zl = jnp.zeros((LORA, WIDTH), BF16)
    w_lora = jnp.concatenate([jnp.concatenate([full['w_decay_up'], zl], 1),
                              jnp.concatenate([zl, full['w_iclr_up']], 1)], 0)

    hd = np.arange(WIDTH) // HEAD
    ones_blocks = jnp.asarray(hd[:, None] == hd[None, :], BF16)
    ones_half = ones_blocks[:WKV_HALF, :WKV_HALF]
    perm_np = np.zeros((LANES, LANES), np.float32)
    for d in range(MLA_ROPE // 2):
        perm_np[HEAD + 16 + d, HEAD + d] = -1.0
        perm_np[HEAD + d, HEAD + 16 + d] = 1.0
    perm = jnp.asarray(perm_np, BF16)
    inv = ROPE_THETA ** (-jnp.arange(0, MLA_ROPE, 2, dtype=F32) / MLA_ROPE)
    ang = positions[0].astype(F32)[:, None] * inv
    cos_a, sin_a = jnp.cos(ang), jnp.sin(ang)
    cs = jnp.concatenate([jnp.ones((seq, HEAD), F32), cos_a, cos_a, jnp.zeros((seq, LANES - MLA_QK), F32),
                          jnp.zeros((seq, HEAD), F32), sin_a, sin_a, jnp.zeros((seq, LANES - MLA_QK), F32)], axis=1)

    x2, tgt = x[0], loss_target[0]
    r_k2 = r_k.reshape(1, WIDTH)

    c8 = jnp.broadcast_to(c, (SUBLANES, D_MODEL))
    ada = ada_fwd(c8, b_ada, gathered)[:1]
    shift, scale, gate = ada[:, :D_MODEL], ada[:, D_MODEL:2 * D_MODEL], ada[:, 2 * D_MODEL:]

    f_in1, f_in2, f_in3 = _make_f_in((512, 1024)), _make_f_in((SHIFT_W, WIDTH)), _make_f_in((1024, 1024))
    tr = min(256, seq)
    p_mla, gpa, p_rwkv, gpb, ma, mb = row_fwd(
        "in_fwd", _make_f_in((512, 1024), (SHIFT_W, WIDTH), (1024, 1024)), [x2], [shift, scale, w_g1, w_g2, w_g3], [],
        [512, 1024, SHIFT_W, WIDTH, 1024, 1024], tr)

    mla_par = [q_norm_g, kv_norm_g, w_uq_p, full['w_ukv']]
    q_f, kv_f, kpe = row_fwd("mla_pre_fwd", f_mla_pre, [p_mla, cs], mla_par, [perm], [1024, 1024, LANES], tr)
    ya, lse = attn_fwd(q_f, kv_f, kpe)

    u = shift_fwd(p_rwkv, mu_rwkv, tr)
    pre_par = [w0, a0, k_k, k_a, w_lora]
    rr, wd, k2, vv, an, bb = row_fwd("rwkv_pre_fwd", f_rwkv_pre, [u], pre_par, [ones_blocks], [WIDTH] * 6, tr)
    y_wkv, states = wkv_fwd(rr, wd, k2, vv, an, bb, ones_half)
    post_b_par = [r_k2, gn_g, gn_b]
    yb, = row_fwd("rwkv_post_fwd", f_rwkv_post, [y_wkv, rr, k2, vv], post_b_par, [ones_blocks], [WIDTH], tr)

    merge_rows, merge_par = [ya, gpa, yb, gpb, ma, mb], [w_pa_p, full['w_proj_b']]
    merged, = row_fwd("merge_fwd", f_merge, merge_rows, merge_par, [], [D_MODEL], tr)
    loss_rows, loss_par = [merged, x2, tgt], [gate, post_g, post_b, full['w_out']]
    lrows, = row_fwd("loss_fwd", f_loss, loss_rows, loss_par, [], [LANES], tr)
    loss = lax.psum(jnp.sum(lrows[:, 0]), ("x", "y", "c"))

    dl = jnp.broadcast_to((jnp.arange(LANES) == 0).astype(F32), (seq, LANES))
    (dmerged, dx_res), (dgate, dpost_g, dpost_b, dw_out) = row_bwd("loss_bwd", f_loss, loss_rows, 2, loss_par, [], [dl], tr)
    (dya, dgpa, dyb, dgpb, dma, dmb), (dw_pa_p, dw_pb) = row_bwd(
        "merge_bwd", f_merge, merge_rows, 6, merge_par, [], [dmerged], tr)

    (dy_wkv, dr1, dk1, dv1), (dr_k, dgn_g, dgn_b) = row_bwd(
        "rwkv_post_bwd", f_rwkv_post, [y_wkv, rr, k2, vv], 4, post_b_par, [ones_blocks], [dyb], tr)
    dr2, dwd, dk2, dv2, dan, dbb = wkv_bwd(rr, wd, k2, vv, an, bb, dy_wkv, states, ones_half)
    (du,), (dw0, da0, dk_k, dk_a, dw_lora) = row_bwd(
        "rwkv_pre_bwd", f_rwkv_pre, [u], 1, pre_par, [ones_blocks],
        [(dr1, dr2), dwd, (dk1, dk2), (dv1, dv2), dan, dbb], tr)
    dp_rwkv, dmu = shift_bwd(du, p_rwkv, mu_rwkv, tr)

    dq_f, dkv_f, dkpe = attn_bwd(q_f, kv_f, kpe, ya, lse, dya)
    (dp_mla,), (dqg, dkvg, dw_uq_p, dw_ukv) = row_bwd(
        "mla_pre_bwd", f_mla_pre, [p_mla, cs], 1, mla_par, [perm], [dq_f, dkv_f, dkpe], tr)

    (dx1,), (dsh1, dsc1, dw_g1) = row_bwd("in1_bwd", f_in1, [x2], 1, [shift, scale, w_g1], [], [dp_mla, dgpa], tr, [dx_res])
    (dx2,), (dsh2, dsc2, dw_g2) = row_bwd("in2_bwd", f_in2, [x2], 1, [shift, scale, w_g2], [], [dp_rwkv, dgpb], tr, [dx1])
    (dx3,), (dsh3, dsc3, dw_g3) = row_bwd("in3_bwd", f_in3, [x2], 1, [shift, scale, w_g3], [], [dma, dmb], tr, [dx2])
    grad_x = dx3[None]

    dada = jnp.concatenate([dsh1 + dsh2 + dsh3, dsc1 + dsc2 + dsc3, dgate], axis=1)
    local = {
        'w_in': jnp.concatenate([dw_g1[:, :384], dw_g1[:, 448:480], _unpad_heads_cols(dw_g1[:, 512:], HEAD, HEAD),
                                 dw_g2, dw_g3], axis=1),
        'w_uq': _unpad_heads_cols(dw_uq_p, MLA_QK, 0),
        'w_ukv': dw_ukv,
        'w_decay_up': dw_lora[:LORA, :WIDTH],
        'w_iclr_up': dw_lora[LORA:, WIDTH:],
        'w_proj_a': dw_pa_p.reshape(HEADS, LANES, D_MODEL)[:, HEAD:].reshape(WIDTH, D_MODEL),
        'w_proj_b': dw_pb,
        'w_out': dw_out,
    }
    small_local = {'b_ada': dada, 'q_norm_g': dqg, 'kv_norm_g': dkvg, 'mu_rwkv': dmu, 'w0': dw0, 'a0': da0,
                   'k_k': dk_k, 'k_a': dk_a, 'r_k': dr_k, 'gn_g': dgn_g, 'gn_b': dgn_b,
                   'post_g': dpost_g, 'post_b': dpost_b}

    def shard_of(g, axis, s):
        n = g.shape[axis] // 4
        return lax.slice_in_dim(g, s * n, (s + 1) * n, axis=axis)

    packed = jnp.stack([_pack_shard({n: shard_of(local[n], ax, s) for n, _, ax in _SHARDED if n != 'w_ada'}, F32, False)
                        for s in range(4)])
    keep = lax.dynamic_slice_in_dim(packed, my_c * GRAD_HALF, GRAD_HALF, 2).reshape(4 * PACK_ROWS, GRAD_HALF)
    give = lax.dynamic_slice_in_dim(packed, (1 - my_c) * GRAD_HALF, GRAD_HALF, 2).reshape(4 * PACK_ROWS, GRAD_HALF)
    pair_sum, = row_fwd("pair_sum", lambda p, q: (p + q,), [keep, pair_swap("swap_halves", give)], [], [],
                        [GRAD_HALF], PACK_ROWS // 2, BF16)
    received = chip_all_to_all("exchange_grads", pair_sum.reshape(4, PACK_ROWS, GRAD_HALF))
    my_sum = sum_slots(received, PACK_ROWS // 2)
    other_sum = pair_swap("swap_sums", my_sum)
    halves = [jnp.where(my_c == 0, my_sum, other_sum), jnp.where(my_c == 0, other_sum, my_sum)]
    g_shard = _unpack_shard(jnp.concatenate(halves, axis=1), False)

    small_pack = lambda d, extra=(): _pack_small([d[n] for n, _ in _SMALL] + list(extra))
    small_all = all_gather8("gather_small", small_pack(small_local, [c * jax.nn.sigmoid(c)]))
    sc_all = small_all[:, SMALL_USED:SMALL_USED + D_MODEL // LANES].reshape(N_DEV, D_MODEL)
    dada_all = small_all[:, :3 * D_MODEL // LANES].reshape(N_DEV, 3 * D_MODEL)
    my_cols = lax.dynamic_slice_in_dim(dada_all, (2 * lax.axis_index("x") + lax.axis_index("y")) * 768, 768, 1)
    g_shard['w_ada'] = ada_grad_shard([sc_all[b].reshape(D_MODEL, 1) for b in range(N_DEV)], my_cols)

    big = [{}, {}, {}, {}]
    for n in shard_names:
        w2, m2, v2 = given[n][0], given['m_' + n][0], given['v_' + n][0]
        cols = w2.shape[1]
        outs = row_fwd("adamw_" + n, f_adamw, [w2, g_shard[n], m2, v2], [], [], [cols] * 3, min(256, w2.shape[0]))
        for dst, val in zip(big, (g_shard[n], *outs)):
            dst[n] = val

    small_out = adamw_small(small_all, small_pack({n: given[n] for n, _ in _SMALL}),
                            small_pack({n: given['m_' + n] for n, _ in _SMALL}),
                            small_pack({n: given['v_' + n] for n, _ in _SMALL}))

    results = []
    for big_k, packed_small in zip(big, small_out):
        small = _unpack_small(packed_small)
        results.append([(big_k[n] if n in big_k else small[n]).reshape(given[n].shape) for n in _WEIGHTS])
    return (loss, grad_x, *results[0], *results[1], *results[2], *results[3])
```

```python
from typing import NamedTuple

import numpy as np
import jax
import jax.numpy as jnp
from jax import lax
from jax.experimental import pallas as pl
from jax.experimental.pallas import tpu as pltpu

F32 = jnp.float32
BF16 = jnp.bfloat16

D_MODEL = 1024
LN_EPS = 1e-5
RMS_EPS = 1e-6
GN_EPS = 64e-5
HEADS = 8
HEAD = 64
MLA_ROPE = 32
MLA_QK = HEAD + MLA_ROPE
ROPE_THETA = 10000.0
WIDTH = HEADS * HEAD
LORA = 64
SHIFT_W = 3 * WIDTH + 2 * LORA
CHUNK = 64
ALPHA = 2.0 ** 0.25

ADAM_LR, ADAM_B1, ADAM_B2, ADAM_EPS, ADAM_WD, ADAM_STEP = 0.001, 0.9, 0.999, 1e-08, 0.01, 10

LANES = 128
SUBLANES = 8
VMEM_LIMIT = 56 * 1024 * 1024
N_DEV = 8
MESH = pl.DeviceIdType.MESH
NEG = -1e30

_WEIGHTS = ['w_ada', 'b_ada', 'w_in', 'q_norm_g', 'w_uq', 'kv_norm_g', 'w_ukv', 'mu_rwkv', 'w0',
            'w_decay_up', 'a0', 'w_iclr_up', 'k_k', 'k_a', 'r_k', 'gn_g', 'gn_b', 'w_proj_a',
            'w_proj_b', 'w_out', 'post_g', 'post_b']
_SHARDED = [('w_ada', (1024, 3072), 1), ('w_in', (1024, 5152), 1), ('w_uq', (256, 768), 1),
            ('w_ukv', (128, 1024), 1), ('w_decay_up', (64, 512), 1), ('w_iclr_up', (64, 512), 1),
            ('w_proj_a', (512, 1024), 1), ('w_proj_b', (512, 1024), 1), ('w_out', (1024, 1024), 0)]
_SMALL = [('b_ada', 3072), ('q_norm_g', 256), ('kv_norm_g', 128), ('mu_rwkv', 1664), ('w0', 512),
          ('a0', 512), ('k_k', 512), ('k_a', 512), ('r_k', 512), ('gn_g', 512), ('gn_b', 512),
          ('post_g', 1024), ('post_b', 1024)]
PACK_ROWS = 256
PACK_COLS = 11264
HALF_COLS = PACK_COLS // 2
ADA_COLS = 4 * 768
GRAD_HALF = (PACK_COLS - ADA_COLS) // 2
SMALL_USED = 84
SMALL_ROWS = 96


def _bf(x):
    return x.astype(BF16)


def _dot(a, b, ca, cb):
    return lax.dot_general(a, b, (((ca,), (cb,)), ((), ())), preferred_element_type=F32)


class Weight(NamedTuple):
    value: jax.Array
    grad: jax.Array


@jax.custom_vjp
def _mm(a, w, w_grad):
    return _dot(_bf(a), _bf(w), 1, 0)


def _mm_fwd(a, w, w_grad):
    return _mm(a, w, w_grad), (a, w)


def _mm_bwd(res, g):
    a, w = res
    gb = _bf(g)
    return _dot(gb, _bf(w), 1, 1), jnp.zeros_like(w), _dot(_bf(a), gb, 0, 0)


_mm.defvjp(_mm_fwd, _mm_bwd)


def mm(a, w):
    if isinstance(w, Weight):
        return _mm(a, w.value, w.grad)
    return _dot(_bf(a), _bf(w), 1, 0)


def _split3(x):
    hi = _bf(x)
    r1 = x - hi.astype(F32)
    mid = _bf(r1)
    lo = _bf(r1 - mid.astype(F32))
    return hi, mid, lo


def _exact_dot(x, m, cm):
    hi, mid, lo = _split3(x)
    return _dot(hi, m, 1, cm) + _dot(mid, m, 1, cm) + _dot(lo, m, 1, cm)


def _head_sums(x, ones_blocks):
    n = ones_blocks.shape[0]
    parts = [_exact_dot(x[:, o:o + n], ones_blocks, 0) for o in range(0, x.shape[1], n)]
    return parts[0] if len(parts) == 1 else jnp.concatenate(parts, axis=1)


@jax.custom_vjp
def segsum(x, ones_blocks):
    return _head_sums(x, ones_blocks)


def _segsum_fwd(x, ones_blocks):
    return segsum(x, ones_blocks), ones_blocks


def _segsum_bwd(ones_blocks, g):
    return _head_sums(g, ones_blocks), jnp.zeros_like(ones_blocks)


segsum.defvjp(_segsum_fwd, _segsum_bwd)


@jax.custom_vjp
def lane_perm(x, perm):
    return _exact_dot(x, perm, 0)


def _lane_perm_fwd(x, perm):
    return lane_perm(x, perm), perm


def _lane_perm_bwd(perm, g):
    return _exact_dot(g, perm, 1), jnp.zeros_like(perm)


lane_perm.defvjp(_lane_perm_fwd, _lane_perm_bwd)


def _silu(z):
    return z * jax.nn.sigmoid(z)


def _softplus(z):
    return jnp.maximum(z, 0.0) + jnp.log(1.0 + jnp.exp(-jnp.abs(z)))


def _layer_norm(x):
    xc = x - jnp.mean(x, -1, keepdims=True)
    return xc * lax.rsqrt(jnp.mean(xc * xc, -1, keepdims=True) + LN_EPS)


def _rope(t, cos_t, sin_t, perm):
    outs = []
    for h in range(t.shape[1] // LANES):
        th = t[:, h * LANES:(h + 1) * LANES]
        outs.append(th * cos_t + lane_perm(th, perm) * sin_t)
    return outs[0] if len(outs) == 1 else jnp.concatenate(outs, axis=1)


def _make_f_in(*split_groups):
    def f_in(x, shift, scale, *weights):
        h = _layer_norm(x) * (1.0 + scale) + shift
        outs = []
        for w, splits in zip(weights, split_groups):
            p, o = mm(h, w), 0
            for s in splits:
                outs.append(p[:, o:o + s])
                o += s
        return tuple(outs)
    return f_in


def f_mla_pre(p, cs, qg, kvg, w_uq, w_ukv, perm):
    q_c, kv_c, k_r = p[:, :256], p[:, 256:384], p[:, 384:512]
    cos_t, sin_t = cs[:, :LANES], cs[:, LANES:]
    qn = q_c * lax.rsqrt(jnp.mean(q_c * q_c, -1, keepdims=True) + RMS_EPS) * qg
    kvn = kv_c * lax.rsqrt(jnp.mean(kv_c * kv_c, -1, keepdims=True) + RMS_EPS) * kvg
    q = _rope(mm(qn, w_uq), cos_t, sin_t, perm)
    kv = mm(kvn, w_ukv)
    return q, kv, _rope(k_r, cos_t, sin_t, perm)


def f_rwkv_pre(u, w0, a0, k_k, k_a, w_lora, ones_blocks):
    r, k, v, lo = u[:, :WIDTH], u[:, WIDTH:2 * WIDTH], u[:, 2 * WIDTH:3 * WIDTH], u[:, 3 * WIDTH:]
    lane = lax.broadcasted_iota(jnp.int32, lo.shape, 1)
    dl = mm(jnp.where(lane < LORA, jnp.tanh(lo), lo), w_lora)
    w_log = -_softplus(-(w0 + dl[:, :WIDTH])) - 0.5
    decay = jnp.exp(-jnp.exp(w_log))
    a = jax.nn.sigmoid(a0 + dl[:, WIDTH:])
    kk = k * k_k
    kk = kk / jnp.maximum(jnp.sqrt(segsum(kk * kk, ones_blocks)), 1e-12)
    k2 = k * (1.0 + (a - 1.0) * k_a)
    return r, decay, k2, v, -kk, kk * a


def f_rwkv_post(y, r, k2, v, r_k, gn_g, gn_b, ones_blocks):
    yc = y - segsum(y, ones_blocks) * (1.0 / HEAD)
    yn = yc * lax.rsqrt(segsum(yc * yc, ones_blocks) * (1.0 / HEAD) + GN_EPS)
    return (yn * gn_g + gn_b + segsum(r * k2 * r_k, ones_blocks) * v,)


def f_merge(ya, gpa, yb, gpb, ma, mb, w_pa, w_pb):
    pa = mm(ya * _silu(gpa), w_pa)
    pb = mm(yb * _silu(gpb), w_pb)
    return (jax.nn.sigmoid(ma) * pa + jax.nn.sigmoid(mb) * pb,)


def f_loss(merged, x, tgt, gate, post_g, post_b, w_out):
    z = ALPHA * x + (1.0 + gate) * mm(merged, w_out)
    err = _layer_norm(z) * post_g + post_b - tgt
    lrow = 0.5 * jnp.mean(err * err, -1, keepdims=True)
    return (jnp.broadcast_to(lrow, (lrow.shape[0], LANES)),)


def f_adamw(w, g, m, v):
    m2 = ADAM_B1 * m + (1.0 - ADAM_B1) * g
    v2 = ADAM_B2 * v + (1.0 - ADAM_B2) * jnp.square(g)
    m_hat = m2 / (1.0 - ADAM_B1 ** ADAM_STEP)
    v_hat = v2 / (1.0 - ADAM_B2 ** ADAM_STEP)
    return -ADAM_LR * (m_hat / (jnp.sqrt(v_hat) + ADAM_EPS) + ADAM_WD * w), m2, v2


def _params(sem=("arbitrary",)):
    return pltpu.CompilerParams(dimension_semantics=sem, vmem_limit_bytes=VMEM_LIMIT)


def _row_spec(tr, a):
    return pl.BlockSpec((tr, a.shape[1]), lambda i: (i, 0))


def _full_spec(a):
    return pl.BlockSpec(a.shape, lambda i: (0,) * a.ndim)


def row_fwd(name, f, rows, params, consts, out_widths, tr, out_dtype=F32):
    n_rows = rows[0].shape[0]
    nr, npar, ncon = len(rows), len(params), len(consts)

    def body(*refs):
        rv = [r[...] for r in refs[:nr]]
        pv = [r[...] for r in refs[nr:nr + npar]]
        cv = [r[...] for r in refs[nr + npar:nr + npar + ncon]]
        outs = f(*rv, *pv, *cv)
        for o_ref, o in zip(refs[nr + npar + ncon:], outs):
            o_ref[...] = o.astype(o_ref.dtype)

    return pl.pallas_call(
        body, name=name, grid=(n_rows // tr,),
        in_specs=[_row_spec(tr, a) for a in rows] + [_full_spec(a) for a in list(params) + list(consts)],
        out_specs=[pl.BlockSpec((tr, w), lambda i: (i, 0)) for w in out_widths],
        out_shape=[jax.ShapeDtypeStruct((n_rows, w), out_dtype) for w in out_widths],
        compiler_params=_params(),
    )(*rows, *params, *consts)


def row_bwd(name, f, rows, n_diff, params, consts, douts, tr, add_rows=None):
    n_rows = rows[0].shape[0]
    douts = [d if isinstance(d, (tuple, list)) else (d,) for d in douts]
    counts = [len(d) for d in douts]
    flat_d = [a for d in douts for a in d]
    add_rows = add_rows or [None] * n_diff
    adds = [a for a in add_rows if a is not None]
    nr, npar, ncon, nd, na = len(rows), len(params), len(consts), len(flat_d), len(adds)

    def body(*refs):
        o = 0
        rv = [r[...] for r in refs[o:o + nr]]; o += nr
        pv = [Weight(r[...], jnp.zeros(r.shape, F32)) if r.dtype == BF16 else r[...] for r in refs[o:o + npar]]
        o += npar
        cv = [r[...] for r in refs[o:o + ncon]]; o += ncon
        dv = []
        for cnt in counts:
            s = refs[o][...]
            for e in range(1, cnt):
                s = s + refs[o + e][...]
            dv.append(s)
            o += cnt
        add_v = [r[...] for r in refs[o:o + na]]; o += na
        drow_refs = refs[o:o + n_diff]; o += n_diff
        dpar_refs = refs[o:o + npar]

        def g(*args):
            return tuple(f(*args[:n_diff], *rv[n_diff:], *args[n_diff:], *cv))

        _, vjp = jax.vjp(g, *rv[:n_diff], *pv)
        grads = vjp(tuple(dv))
        ai = 0
        for j, (r, gr) in enumerate(zip(drow_refs, grads[:n_diff])):
            if add_rows[j] is not None:
                gr = gr + add_v[ai]
                ai += 1
            r[...] = gr

        @pl.when(pl.program_id(0) == 0)
        def _():
            for r in dpar_refs:
                r[...] = jnp.zeros_like(r)

        for r, gr in zip(dpar_refs, grads[n_diff:]):
            r[...] += gr.grad if isinstance(gr, Weight) else gr

    outs = pl.pallas_call(
        body, name=name, grid=(n_rows // tr,),
        in_specs=([_row_spec(tr, a) for a in rows] + [_full_spec(a) for a in list(params) + list(consts)]
                  + [_row_spec(tr, a) for a in flat_d + adds]),
        out_specs=[_row_spec(tr, a) for a in rows[:n_diff]] + [_full_spec(a) for a in params],
        out_shape=([jax.ShapeDtypeStruct(a.shape, F32) for a in rows[:n_diff]]
                   + [jax.ShapeDtypeStruct(a.shape, F32) for a in params]),
        compiler_params=_params(),
    )(*rows, *params, *consts, *flat_d, *adds)
    return outs[:n_diff], outs[n_diff:]


def shift_stage_fwd(name, f, p, mu, params, consts, out_widths, tr):
    n_rows, w = p.shape
    npar, ncon = len(params), len(consts)

    def body(*refs):
        p_ref, mu_ref = refs[:2]
        pv = [r[...] for r in refs[2:2 + npar]]
        cv = [r[...] for r in refs[2 + npar:2 + npar + ncon]]
        u_ref, out_refs, carry = refs[2 + npar + ncon], refs[3 + npar + ncon:-1], refs[-1]

        @pl.when(pl.program_id(0) == 0)
        def _():
            carry[...] = jnp.zeros_like(carry)

        x = p_ref[...]
        rolled = pltpu.roll(x, 1, 0)
        head = pltpu.roll(carry[...], 1, 0)
        fixed = jnp.concatenate([head, rolled[SUBLANES:]], axis=0)
        row = lax.broadcasted_iota(jnp.int32, x.shape, 0)
        prev = jnp.where(row == 0, fixed, rolled)
        u = x + (prev - x) * mu_ref[...]
        u_ref[...] = u
        carry[...] = x[tr - SUBLANES:]
        for o_ref, o in zip(out_refs, f(u, *pv, *cv)):
            o_ref[...] = o

    return pl.pallas_call(
        body, name=name, grid=(n_rows // tr,),
        in_specs=[_row_spec(tr, p), _full_spec(mu)] + [_full_spec(a) for a in list(params) + list(consts)],
        out_specs=[_row_spec(tr, p)] + [pl.BlockSpec((tr, ow), lambda i: (i, 0)) for ow in out_widths],
        out_shape=[jax.ShapeDtypeStruct(p.shape, F32)] + [jax.ShapeDtypeStruct((n_rows, ow), F32) for ow in out_widths],
        scratch_shapes=[pltpu.VMEM((SUBLANES, w), F32)],
        compiler_params=_params(),
    )(p, mu, *params, *consts)


def shift_bwd(du, p, mu, tr):
    n_rows, w = p.shape
    nb = n_rows // tr

    def body(du_ref, p_ref, mu_ref, dp_ref, dmu_ref, carry):
        @pl.when(pl.program_id(0) == 0)
        def _():
            carry[...] = jnp.zeros_like(carry)
            dmu_ref[...] = jnp.zeros_like(dmu_ref)

        d = du_ref[...]
        rolled = pltpu.roll(d, tr - 1, 0)
        tail = pltpu.roll(carry[...], SUBLANES - 1, 0)
        fixed = jnp.concatenate([rolled[:tr - SUBLANES], tail], axis=0)
        row = lax.broadcasted_iota(jnp.int32, d.shape, 0)
        nxt = jnp.where(row == tr - 1, fixed, rolled)
        mu_v = mu_ref[...]
        dp_ref[...] = d * (1.0 - mu_v) + nxt * mu_v
        dmu_ref[...] += jnp.sum(p_ref[...] * (nxt - d), axis=0, keepdims=True)
        carry[...] = d[:SUBLANES]

    rev = lambda i: (nb - 1 - i, 0)
    return pl.pallas_call(
        body, name="shift_bwd", grid=(nb,),
        in_specs=[pl.BlockSpec((tr, w), rev), pl.BlockSpec((tr, w), rev), _full_spec(mu)],
        out_specs=[pl.BlockSpec((tr, w), rev), _full_spec(mu)],
        out_shape=[jax.ShapeDtypeStruct(p.shape, F32), jax.ShapeDtypeStruct(mu.shape, F32)],
        scratch_shapes=[pltpu.VMEM((SUBLANES, w), F32)],
        compiler_params=_params(),
    )(du, p, mu)


ATT_T = 256


def _att_rows(j):
    return pl.ds(pl.multiple_of(j * ATT_T, ATT_T), ATT_T)


def _att_prep(kv_ref, kpe_ref, kf_scr, vf_scr, n_blocks):
    lane = lax.broadcasted_iota(jnp.int32, (ATT_T, LANES), 1)

    def prep(j, _):
        rows = _att_rows(j)
        kv = kv_ref[rows, :]
        kf_scr[rows, :] = _bf(jnp.where(lane < HEAD, kv, kpe_ref[rows, :]))
        vf_scr[rows, :] = _bf(jnp.where(lane >= HEAD, kv, 0.0))
        return 0

    lax.fori_loop(0, n_blocks, prep, 0)


def _att_diag_mask():
    shift = CHUNK.bit_length() - 1
    qc = jnp.right_shift(lax.broadcasted_iota(jnp.int32, (ATT_T, ATT_T), 0), shift)
    kc = jnp.right_shift(lax.broadcasted_iota(jnp.int32, (ATT_T, ATT_T), 1), shift)
    return kc <= qc


def _wide(x):
    return jnp.concatenate([x] * (ATT_T // LANES), axis=1)


def attn_fwd(q, kv, kpe):
    seq = q.shape[0]
    nb = seq // ATT_T
    assert seq % (2 * ATT_T) == 0, "blocks are taken two per trip"
    scale = MLA_QK ** -0.5

    def body(q_ref, kv_ref, kpe_ref, o_ref, lse_ref, kf_scr, vf_scr):
        _att_prep(kv_ref, kpe_ref, kf_scr, vf_scr, nb)
        mask = _att_diag_mask()

        def scores(qb, kj):
            return _dot(qb, kf_scr[_att_rows(kj), :], 1, 1) * scale

        def update(s, kj, carry, masked):
            m, l, acc = carry
            if masked:
                s = jnp.where(mask, s, NEG)
            m_new = jnp.maximum(m, jnp.broadcast_to(jnp.max(s, -1, keepdims=True), m.shape))
            alpha = jnp.exp(m - m_new)
            p = jnp.exp(s - _wide(m_new))
            l = alpha * l + jnp.broadcast_to(jnp.sum(p, -1, keepdims=True), l.shape)
            acc = alpha * acc + _dot(_bf(p), vf_scr[_att_rows(kj), :], 1, 0)
            return m_new, l, acc

        def finish(rows, carry):
            m, l, acc = carry
            o_ref[rows, :] = acc / l
            lse_ref[rows, :] = m + jnp.log(l)

        def q_pair(qp, _):
            rows_a, rows_b = _att_rows(2 * qp), _att_rows(2 * qp + 1)
            qa, qb = _bf(q_ref[rows_a, :]), _bf(q_ref[rows_b, :])
            init = (jnp.full((ATT_T, LANES), NEG, F32), jnp.zeros((ATT_T, LANES), F32),
                    jnp.zeros((ATT_T, LANES), F32))

            def trip(kj, c):
                ca, cb, sa, sb = c
                sa_next, sb_next = scores(qa, kj + 1), scores(qb, kj + 1)
                return update(sa, kj, ca, False), update(sb, kj, cb, False), sa_next, sb_next

            ca, cb, sa, sb = lax.fori_loop(0, 2 * qp, trip, (init, init, scores(qa, 0), scores(qb, 0)))
            sb_last = scores(qb, 2 * qp + 1)
            ca = update(sa, 2 * qp, ca, True)
            cb = update(sb_last, 2 * qp + 1, update(sb, 2 * qp, cb, False), True)
            finish(rows_a, ca)
            finish(rows_b, cb)
            return 0

        lax.fori_loop(0, nb // 2, q_pair, 0)

    head = pl.BlockSpec((seq, LANES), lambda h: (0, h))
    return pl.pallas_call(
        body, name="attn_fwd", grid=(HEADS,),
        in_specs=[head, head, pl.BlockSpec((seq, LANES), lambda h: (0, 0))],
        out_specs=[head, head],
        out_shape=[jax.ShapeDtypeStruct((seq, HEADS * LANES), F32)] * 2,
        scratch_shapes=[pltpu.VMEM((seq, LANES), BF16)] * 2,
        compiler_params=_params(),
    )(q, kv, kpe)


def attn_bwd(q, kv, kpe, o, lse, do):
    seq = q.shape[0]
    nb = seq // ATT_T
    assert seq % (2 * ATT_T) == 0, "blocks are taken two per trip"
    scale = MLA_QK ** -0.5

    def body(q_ref, kv_ref, kpe_ref, o_ref, lse_ref, do_ref, dq_ref, dkv_ref, dkpe_ref,
             kf_scr, vf_scr, qb_scr, dob_scr, dsum):
        lane = lax.broadcasted_iota(jnp.int32, (ATT_T, LANES), 1)

        @pl.when(pl.program_id(0) == 0)
        def _():
            dkpe_ref[...] = jnp.zeros_like(dkpe_ref)

        dq_ref[...] = jnp.zeros_like(dq_ref)
        _att_prep(kv_ref, kpe_ref, kf_scr, vf_scr, nb)

        def pre(j, _):
            rows = _att_rows(j)
            d = do_ref[rows, :]
            qb_scr[rows, :] = _bf(q_ref[rows, :])
            dob_scr[rows, :] = _bf(d)
            dsum[rows, :] = jnp.broadcast_to(jnp.sum(d * o_ref[rows, :], -1, keepdims=True), (ATT_T, LANES))
            return 0

        lax.fori_loop(0, nb, pre, 0)
        mask = _att_diag_mask()

        def front(kf, vf, qi):
            rows = _att_rows(qi)
            return _dot(qb_scr[rows, :], kf, 1, 1), _dot(dob_scr[rows, :], vf, 1, 1)

        def back(kf, qi, fr, carry, masked):
            s, dp = fr
            dk, dv = carry
            rows = _att_rows(qi)
            qb, dob = qb_scr[rows, :], dob_scr[rows, :]
            p = jnp.exp(s * scale - _wide(lse_ref[rows, :]))
            if masked:
                p = jnp.where(mask, p, 0.0)
            ds = _bf(p * (dp - _wide(dsum[rows, :])) * scale)
            return (dk + _dot(ds, qb, 0, 0), dv + _dot(_bf(p), dob, 0, 0)), _dot(ds, kf, 1, 0)

        def store(krows, carry):
            dk, dv = carry
            dkv_ref[krows, :] = jnp.where(lane < HEAD, dk, dv)
            dkpe_ref[krows, :] += jnp.where((lane >= HEAD) & (lane < MLA_QK), dk, 0.0)

        def k_pair(kp, _):
            ka, kb = 2 * kp, 2 * kp + 1
            rows_a, rows_b = _att_rows(ka), _att_rows(kb)
            kfa, vfa, kfb, vfb = kf_scr[rows_a, :], vf_scr[rows_a, :], kf_scr[rows_b, :], vf_scr[rows_b, :]
            zero = jnp.zeros((ATT_T, LANES), F32)
            ca, dq_a = back(kfa, ka, front(kfa, vfa, ka), (zero, zero), True)
            dq_ref[rows_a, :] += dq_a
            ca, dq_a = back(kfa, kb, front(kfa, vfa, kb), ca, False)
            cb, dq_b = back(kfb, kb, front(kfb, vfb, kb), (zero, zero), True)
            dq_ref[rows_b, :] += dq_a + dq_b

            def both(qi, c):
                ca, cb, fa, fb = c
                nxt = jnp.minimum(qi + 1, nb - 1)
                fa_next, fb_next = front(kfa, vfa, nxt), front(kfb, vfb, nxt)
                ca, dq_a = back(kfa, qi, fa, ca, False)
                cb, dq_b = back(kfb, qi, fb, cb, False)
                dq_ref[_att_rows(qi), :] += dq_a + dq_b
                return ca, cb, fa_next, fb_next

            first = jnp.minimum(kb + 1, nb - 1)
            ca, cb, _, _ = lax.fori_loop(kb + 1, nb, both, (ca, cb, front(kfa, vfa, first), front(kfb, vfb, first)))
            store(rows_a, ca)
            store(rows_b, cb)
            return 0

        lax.fori_loop(0, nb // 2, k_pair, 0)

    head = pl.BlockSpec((seq, LANES), lambda h: (0, h))
    shared = pl.BlockSpec((seq, LANES), lambda h: (0, 0))
    return pl.pallas_call(
        body, name="attn_bwd", grid=(HEADS,),
        in_specs=[head, head, shared, head, head, head],
        out_specs=[head, head, shared],
        out_shape=[jax.ShapeDtypeStruct((seq, HEADS * LANES), F32)] * 2
        + [jax.ShapeDtypeStruct((seq, LANES), F32)],
        scratch_shapes=[pltpu.VMEM((seq, LANES), BF16)] * 4 + [pltpu.VMEM((seq, LANES), F32)],
        compiler_params=_params(),
    )(q, kv, kpe, o, lse, do)


WKV_TB = 128
WKV_GROUP = SUBLANES
WKV_HALF = WIDTH // 2


def _wkv_consts():
    row = lax.broadcasted_iota(jnp.int32, (HEAD, WKV_HALF), 0)
    lane = lax.broadcasted_iota(jnp.int32, (HEAD, WKV_HALF), 1)
    diag = row == jnp.bitwise_and(lane, HEAD - 1)
    sub = lax.broadcasted_iota(jnp.int32, (WKV_GROUP, WKV_HALF), 0)
    return diag, sub


def _halves(x):
    return [x[:, :WKV_HALF], x[:, WKV_HALF:]]


def _diag_rows(row, diag):
    return _bf(jnp.where(diag, jnp.broadcast_to(row, diag.shape), 0.0))


def _put_row(tile, row, i, sub):
    return jnp.where(sub == i, jnp.broadcast_to(row, tile.shape), tile)


def _col_sum(x):
    return jnp.sum(x, axis=0, keepdims=True)


def _step(x, i):
    return x[i * HEAD:(i + 1) * HEAD]


def _head_dots(prods, ones_b, sub):
    tile = jnp.zeros((WKV_GROUP, WKV_HALF), F32)
    for i, p in enumerate(prods):
        tile = _put_row(tile, p, i, sub)
    res = _exact_dot(tile, ones_b, 0)
    return [res[i:i + 1] for i in range(len(prods))]


def wkv_fwd(r, w, k, v, a, b, ones_half):
    seq = r.shape[0]

    def body(r_ref, w_ref, k_ref, v_ref, a_ref, b_ref, ones_ref, y_ref, st_ref, s_scr):
        @pl.when(pl.program_id(0) == 0)
        def _():
            s_scr[...] = jnp.zeros_like(s_scr)

        ones_b = ones_ref[...]
        diag, sub = _wkv_consts()

        ng = WKV_TB // WKV_GROUP
        last = WKV_GROUP - 2

        def rows_of(g):
            return pl.ds(pl.multiple_of(g * WKV_GROUP, WKV_GROUP), WKV_GROUP)

        def pair_rows(x8, t):
            return jnp.concatenate([_diag_rows(x8[t:t + 1], diag), _diag_rows(x8[t + 1:t + 2], diag)], axis=0)

        def put_y(g, pairs_y):
            tile = _halves(y_ref[rows_of(g), :])
            for hf in range(2):
                tile[hf] = _put_row(_put_row(tile[hf], pairs_y[hf][0], last, sub), pairs_y[hf][1], last + 1, sub)
            y_ref[rows_of(g), :] = jnp.concatenate(tile, axis=1)

        def read_out(yexp):
            return _col_sum(jnp.where(diag, yexp[:HEAD], 0.0)), _col_sum(jnp.where(diag, yexp[HEAD:], 0.0))

        def group(g, carry):
            state, v_cur, read = (list(c) for c in carry)
            base = pl.multiple_of(g * WKV_GROUP, WKV_GROUP)
            rows = rows_of(g)
            r8, w8, k8, v8, a8, b8 = (_halves(ref[rows, :]) for ref in (r_ref, w_ref, k_ref, v_ref, a_ref, b_ref))
            v_after = _halves(v_ref[rows_of(jnp.minimum(g + 1, ng - 1)), :])
            evens = range(0, WKV_GROUP, 2)
            dots = [_head_dots([b8[hf][t:t + 1] * a8[hf][t + 1:t + 2] for t in evens]
                               + [k8[hf][t:t + 1] * a8[hf][t + 1:t + 2] for t in evens], ones_b, sub) for hf in range(2)]
            y8 = [jnp.zeros((WKV_GROUP, WKV_HALF), F32)] * 2
            y_before = [None, None]
            for t in evens:
                s0, s1 = slice(t, t + 1), slice(t + 1, t + 2)
                both = []
                for hf in range(2):
                    s_in = state[hf]
                    v_next = pair_rows(v8[hf], t + 2) if t < last else pair_rows(v_after[hf], 0)
                    res = _dot(jnp.concatenate([_bf(s_in * a8[hf][s0]), _bf(s_in * (w8[hf][s0] * a8[hf][s1])),
                                                v_next, read[hf]], axis=0), ones_b, 1, 0)
                    sa0, v0, v1 = res[:HEAD], v_cur[hf][:HEAD], v_cur[hf][HEAD:]
                    st0 = s_in * w8[hf][s0] + sa0 * b8[hf][s0] + v0 * k8[hf][s0]
                    sa1 = res[HEAD:2 * HEAD] + sa0 * dots[hf][t // 2] + v0 * dots[hf][WKV_GROUP // 2 + t // 2]
                    st1 = st0 * w8[hf][s1] + sa1 * b8[hf][s1] + v1 * k8[hf][s1]
                    both.append((st0, st1))
                    state[hf], v_cur[hf] = st1, res[2 * HEAD:4 * HEAD]
                    read[hf] = jnp.concatenate([_bf(st0 * r8[hf][s0]), _bf(st1 * r8[hf][s1])], axis=0)
                    ya, yb = read_out(res[4 * HEAD:])
                    if t == 0:
                        y_before[hf] = (ya, yb)
                    else:
                        y8[hf] = _put_row(_put_row(y8[hf], ya, t - 2, sub), yb, t - 1, sub)
                for j in range(2):
                    st_ref[base + t + j] = jnp.concatenate([both[0][j], both[1][j]], axis=1)
            y_ref[rows, :] = jnp.concatenate(y8, axis=1)
            put_y(jnp.maximum(g - 1, 0), y_before)
            return tuple(state), tuple(v_cur), tuple(read)

        v_first = _halves(v_ref[rows_of(0), :])
        init = (tuple(_halves(s_scr[...])),
                tuple(_dot(pair_rows(v_first[hf], 0), ones_b, 1, 0) for hf in range(2)),
                tuple(jnp.zeros((2 * HEAD, WKV_HALF), BF16) for _ in range(2)))
        fin, _, read = lax.fori_loop(0, ng, group, init)
        put_y(ng - 1, [read_out(_dot(read[hf], ones_b, 1, 0)) for hf in range(2)])
        s_scr[...] = jnp.concatenate(fin, axis=1)

    vec = pl.BlockSpec((WKV_TB, WIDTH), lambda i: (i, 0))
    return pl.pallas_call(
        body, name="wkv_fwd", grid=(seq // WKV_TB,),
        in_specs=[vec] * 6 + [_full_spec(ones_half)],
        out_specs=[vec, pl.BlockSpec((WKV_TB, HEAD, WIDTH), lambda i: (i, 0, 0))],
        out_shape=[jax.ShapeDtypeStruct((seq, WIDTH), F32), jax.ShapeDtypeStruct((seq, HEAD, WIDTH), F32)],
        scratch_shapes=[pltpu.VMEM((HEAD, WIDTH), F32)],
        compiler_params=_params(),
    )(r, w, k, v, a, b, ones_half)


def wkv_bwd(r, w, k, v, a, b, dy, states, ones_half):
    seq = r.shape[0]
    nb = seq // WKV_TB
    ng = WKV_TB // WKV_GROUP

    def body(r_ref, w_ref, k_ref, v_ref, a_ref, b_ref, dy_ref, st_ref, halo_ref, ones_ref,
             dr_ref, dw_ref, dk_ref, dv_ref, da_ref, db_ref, ds_scr):
        blk = nb - 1 - pl.program_id(0)

        @pl.when(pl.program_id(0) == 0)
        def _():
            ds_scr[...] = jnp.zeros_like(ds_scr)

        ones_b = ones_ref[...]
        diag, sub = _wkv_consts()
        before_block = jnp.where(blk == 0, 0.0, halo_ref[0])

        def rows_of(g):
            return pl.ds(pl.multiple_of(g * WKV_GROUP, WKV_GROUP), WKV_GROUP)

        def expand_rows(hf, dy8, v8, a8, t, s_t, s_u):
            s1, s0 = slice(t, t + 1), slice(t - 1, t)
            return jnp.concatenate([_diag_rows(dy8[hf][s1], diag), _diag_rows(dy8[hf][s0], diag),
                                    _diag_rows(v8[hf][s1], diag), _diag_rows(v8[hf][s0], diag),
                                    _bf(s_t[hf] * a8[hf][s1]), _bf(s_u[hf] * a8[hf][s0])], axis=0)

        def read_out(x):
            return _col_sum(jnp.where(diag, x[:HEAD], 0.0)), _col_sum(jnp.where(diag, x[HEAD:], 0.0))

        def put_dv(g, pair_dv):
            tile = _halves(dv_ref[rows_of(g), :])
            for hf in range(2):
                tile[hf] = _put_row(_put_row(tile[hf], pair_dv[hf][0], 1, sub), pair_dv[hf][1], 0, sub)
            dv_ref[rows_of(g), :] = jnp.concatenate(tile, axis=1)

        def group(gg, carry):
            dstate, e_cur, dv_pend = (list(c) for c in carry)
            g = ng - 1 - gg
            base = pl.multiple_of(g * WKV_GROUP, WKV_GROUP)
            rows = rows_of(g)
            r8, w8, k8, v8, a8, b8, dy8 = (
                _halves(ref[rows, :]) for ref in (r_ref, w_ref, k_ref, v_ref, a_ref, b_ref, dy_ref))
            g_next = jnp.maximum(g - 1, 0)
            base_next = pl.multiple_of(g_next * WKV_GROUP, WKV_GROUP)
            dy8n, v8n, a8n = (_halves(ref[rows_of(g_next), :]) for ref in (dy_ref, v_ref, a_ref))
            zero8 = jnp.zeros((WKV_GROUP, WKV_HALF), F32)
            out = {n: [zero8, zero8] for n in ("dr", "dw", "dk", "dv", "da", "db")}
            before_group = jnp.where(g == 0, before_block, st_ref[jnp.maximum(base - 1, 0)])
            states = [_halves(before_group)] + [_halves(st_ref[base + i]) for i in range(WKV_GROUP)]
            odds = range(1, WKV_GROUP, 2)
            dots = [_head_dots([a8[hf][t:t + 1] * b8[hf][t - 1:t] for t in odds]
                               + [r8[hf][t - 1:t] * b8[hf][t - 1:t] for t in odds], ones_b, sub) for hf in range(2)]
            dv_after = [None, None]

            def emit(hf, i, d_i, dsa_i, dy_i, v_i, sa_i):
                s_p, s_t = states[i][hf], states[i + 1][hf]
                for n, val in (("dr", _col_sum(s_t * dy_i)), ("dw", _col_sum(d_i * s_p)), ("db", _col_sum(d_i * sa_i)),
                               ("da", _col_sum(s_p * dsa_i)), ("dk", _col_sum(d_i * v_i))):
                    out[n][hf] = _put_row(out[n][hf], val, i, sub)

            for t in reversed(odds):
                s1, s0 = slice(t, t + 1), slice(t - 1, t)
                for hf in range(2):
                    dy1, dy0, v1, v0, sa1, sa0 = (_step(e_cur[hf], j) for j in range(6))
                    d1 = dstate[hf] + dy1 * r8[hf][s1]
                    if t > 1:
                        nxt = expand_rows(hf, dy8, v8, a8, t - 2, states[t - 2], states[t - 3])
                    else:
                        nxt = expand_rows(hf, dy8n, v8n, a8n, WKV_GROUP - 1, _halves(st_ref[base_next + WKV_GROUP - 2]),
                                          _halves(st_ref[base_next + WKV_GROUP - 3]))
                    res = _dot(jnp.concatenate([_bf(d1 * b8[hf][s1]), _bf(d1 * (w8[hf][s1] * b8[hf][s0])),
                                                nxt, dv_pend[hf]], axis=0), ones_b, 1, 0)
                    dsa1 = res[:HEAD]
                    d0 = d1 * w8[hf][s1] + dsa1 * a8[hf][s1] + dy0 * r8[hf][s0]
                    dsa0 = res[HEAD:2 * HEAD] + dsa1 * dots[hf][t // 2] + dy0 * dots[hf][WKV_GROUP // 2 + t // 2]
                    dstate[hf] = d0 * w8[hf][s0] + dsa0 * a8[hf][s0]
                    e_cur[hf] = res[2 * HEAD:8 * HEAD]
                    dv_pend[hf] = jnp.concatenate([_bf(d1 * k8[hf][s1]), _bf(d0 * k8[hf][s0])], axis=0)
                    emit(hf, t, d1, dsa1, dy1, v1, sa1)
                    emit(hf, t - 1, d0, dsa0, dy0, v0, sa0)
                    dv_a, dv_b = read_out(res[8 * HEAD:])
                    if t == WKV_GROUP - 1:
                        dv_after[hf] = (dv_a, dv_b)
                    else:
                        out["dv"][hf] = _put_row(_put_row(out["dv"][hf], dv_a, t + 2, sub), dv_b, t + 1, sub)
            for ref, n in ((dr_ref, "dr"), (dw_ref, "dw"), (dk_ref, "dk"), (dv_ref, "dv"), (da_ref, "da"), (db_ref, "db")):
                ref[rows, :] = jnp.concatenate(out[n], axis=1)
            put_dv(jnp.minimum(g + 1, ng - 1), dv_after)
            return tuple(dstate), tuple(e_cur), tuple(dv_pend)

        top = rows_of(ng - 1)
        dy8t, v8t, a8t = (_halves(ref[top, :]) for ref in (dy_ref, v_ref, a_ref))
        s_t, s_u = _halves(st_ref[WKV_TB - 2]), _halves(st_ref[WKV_TB - 3])
        init = (tuple(_halves(ds_scr[...])),
                tuple(_dot(expand_rows(hf, dy8t, v8t, a8t, WKV_GROUP - 1, s_t, s_u), ones_b, 1, 0) for hf in range(2)),
                tuple(jnp.zeros((2 * HEAD, WKV_HALF), BF16) for _ in range(2)))
        fin, _, dv_pend = lax.fori_loop(0, ng, group, init)
        put_dv(0, [read_out(_dot(dv_pend[hf], ones_b, 1, 0)) for hf in range(2)])
        ds_scr[...] = jnp.concatenate(fin, axis=1)

    vec = pl.BlockSpec((WKV_TB, WIDTH), lambda i: (nb - 1 - i, 0))
    return pl.pallas_call(
        body, name="wkv_bwd", grid=(nb,),
        in_specs=[vec] * 7 + [
            pl.BlockSpec((WKV_TB, HEAD, WIDTH), lambda i: (nb - 1 - i, 0, 0)),
            pl.BlockSpec((1, HEAD, WIDTH), lambda i: (jnp.maximum((nb - 1 - i) * WKV_TB - 1, 0), 0, 0)),
            _full_spec(ones_half)],
        out_specs=[vec] * 6,
        out_shape=[jax.ShapeDtypeStruct((seq, WIDTH), F32)] * 6,
        scratch_shapes=[pltpu.VMEM((HEAD, WIDTH), F32)],
        compiler_params=_params(),
    )(r, w, k, v, a, b, dy, states, states, ones_half)


def ada_fwd(c8, b_ada, gathered):
    cols = 3 * D_MODEL // 4

    def body(c_ref, b_ref, w_ref, o_ref):
        @pl.when(pl.program_id(1) == 0)
        def _():
            o_ref[...] = jnp.broadcast_to(b_ref[...], o_ref.shape)

        o_ref[...] += mm(_silu(c_ref[...]), w_ref[0])

    return pl.pallas_call(
        body, name="ada_fwd", grid=(4, D_MODEL // PACK_ROWS),
        in_specs=[pl.BlockSpec((SUBLANES, PACK_ROWS), lambda s, i: (0, i)),
                  pl.BlockSpec((1, cols), lambda s, i: (0, s)),
                  pl.BlockSpec((1, PACK_ROWS, cols), lambda s, i: (2 * s, 0, i))],
        out_specs=pl.BlockSpec((SUBLANES, cols), lambda s, i: (0, s)),
        out_shape=jax.ShapeDtypeStruct((SUBLANES, 3 * D_MODEL), F32),
        compiler_params=_params(("arbitrary", "arbitrary")),
    )(c8, b_ada, gathered)


def ada_grad_shard(sc_cols, dada_rows):
    n = len(sc_cols)

    def body(*refs):
        d_ref, o_ref = refs[n], refs[n + 1]
        acc = refs[0][...] * d_ref[0:1, :]
        for b in range(1, n):
            acc = acc + refs[b][...] * d_ref[b:b + 1, :]
        o_ref[...] = acc

    return pl.pallas_call(
        body, name="ada_grad_shard",
        out_shape=jax.ShapeDtypeStruct((sc_cols[0].shape[0], dada_rows.shape[1]), F32),
        compiler_params=pltpu.CompilerParams(vmem_limit_bytes=VMEM_LIMIT),
    )(*sc_cols, dada_rows)


def sum_slots(buf, tr):
    n, rows, cols = buf.shape

    def body(b_ref, o_ref):
        acc = b_ref[0].astype(F32)
        for s in range(1, n):
            acc = acc + b_ref[s].astype(F32)
        o_ref[...] = acc

    return pl.pallas_call(
        body, name="sum_slots", grid=(rows // tr,),
        in_specs=[pl.BlockSpec((n, tr, cols), lambda i: (0, i, 0))],
        out_specs=pl.BlockSpec((tr, cols), lambda i: (i, 0)),
        out_shape=jax.ShapeDtypeStruct((rows, cols), F32),
        compiler_params=_params(),
    )(buf)


def adamw_small(gathered, w, m, v):
    n = gathered.shape[0]

    def body(g_ref, w_ref, m_ref, v_ref, go_ref, d_ref, mo_ref, vo_ref):
        g = g_ref[0]
        for s in range(1, n):
            g = g + g_ref[s]
        go_ref[...] = g
        d_ref[...], mo_ref[...], vo_ref[...] = f_adamw(w_ref[...], g, m_ref[...], v_ref[...])

    return pl.pallas_call(
        body, name="adamw_small",
        out_shape=[jax.ShapeDtypeStruct(w.shape, F32)] * 4,
        compiler_params=pltpu.CompilerParams(vmem_limit_bytes=VMEM_LIMIT),
    )(gathered, w, m, v)


def _coords():
    return lax.axis_index("x"), lax.axis_index("y"), lax.axis_index("c")


def _flip(v, bit):
    return 1 - v if bit else v


def _hbm_call(body, name, out_shape, n_sems, *args):
    any_spec = pl.BlockSpec(memory_space=pl.ANY)
    return pl.pallas_call(
        body, name=name, out_shape=out_shape,
        in_specs=[any_spec] * len(args), out_specs=any_spec,
        scratch_shapes=[pltpu.SemaphoreType.DMA((n_sems,)), pltpu.SemaphoreType.DMA((n_sems,)),
                        pltpu.SemaphoreType.DMA],
    )(*args)


def all_gather8(name, block):
    def body(x_ref, out_ref, send_sems, recv_sems, local_sem):
        x, y, c = _coords()
        me, sibling = (x, y, c), (x, y, 1 - c)
        x_nbr, y_nbr, diagonal = (1 - x, y), (x, 1 - y), (1 - x, 1 - y)
        relay_from = (c * x + (1 - c) * (1 - x), c * (1 - y) + (1 - c) * y)
        relay_to = (c * (1 - x) + (1 - c) * x, c * y + (1 - c) * (1 - y))

        def slot(px, py, pc):
            return out_ref.at[4 * px + 2 * py + pc]

        def copy(k, blk, to, src=None):
            return pltpu.make_async_remote_copy(
                src_ref=slot(*blk) if src is None else src, dst_ref=slot(*blk),
                send_sem=send_sems.at[k], recv_sem=recv_sems.at[k], device_id=to, device_id_type=MESH)

        mine = pltpu.make_async_copy(x_ref, slot(*me), local_sem)
        mine.start()
        first = [copy(0, me, sibling, src=x_ref), copy(1, me, (*x_nbr, c), src=x_ref), copy(2, me, (*y_nbr, c), src=x_ref)]
        for cp in first:
            cp.start()
        copy(1, (*x_nbr, c), me).wait_recv()
        copy(2, (*y_nbr, c), me).wait_recv()
        later = [copy(3, (*relay_from, c), (*relay_to, c)), copy(4, (*x_nbr, c), sibling), copy(5, (*y_nbr, c), sibling)]
        for cp in later:
            cp.start()
        copy(3, (*diagonal, c), me).wait_recv()
        last = copy(6, (*diagonal, c), sibling)
        last.start()
        copy(0, sibling, me).wait_recv()
        for k, chip in ((4, x_nbr), (5, y_nbr), (6, diagonal)):
            copy(k, (*chip, 1 - c), me).wait_recv()
        for cp in first + later + [last]:
            cp.wait_send()
        mine.wait()

    return _hbm_call(body, name, jax.ShapeDtypeStruct((N_DEV,) + block.shape, block.dtype), 7, block)


def pair_swap(name, block):
    def body(x_ref, out_ref, send_sems, recv_sems, local_sem):
        x, y, c = _coords()
        cp = pltpu.make_async_remote_copy(
            src_ref=x_ref, dst_ref=out_ref, send_sem=send_sems.at[0], recv_sem=recv_sems.at[0],
            device_id=(x, y, 1 - c), device_id_type=MESH)
        cp.start()
        cp.wait_recv()
        cp.wait_send()

    return _hbm_call(body, name, jax.ShapeDtypeStruct(block.shape, block.dtype), 1, block)


def chip_all_to_all(name, buf):
    def body(x_ref, out_ref, send_sems, recv_sems, local_sem):
        x, y, c = _coords()
        me = 2 * x + y
        mine = pltpu.make_async_copy(x_ref.at[me], out_ref.at[me], local_sem)
        mine.start()
        copies = []
        for k in range(1, 4):
            px, py = _flip(x, k & 2), _flip(y, k & 1)
            copies.append(pltpu.make_async_remote_copy(
                src_ref=x_ref.at[2 * px + py], dst_ref=out_ref.at[me],
                send_sem=send_sems.at[k - 1], recv_sem=recv_sems.at[k - 1],
                device_id=(px, py, c), device_id_type=MESH))
        for cp in copies:
            cp.start()
        for cp in copies:
            cp.wait_recv()
        for cp in copies:
            cp.wait_send()
        mine.wait()

    return _hbm_call(body, name, jax.ShapeDtypeStruct(buf.shape, buf.dtype), 3, buf)


def _col_blocks(a, cols):
    a = jnp.pad(a, ((0, 0), (0, cols - a.shape[1])))
    return [a[i * PACK_ROWS:(i + 1) * PACK_ROWS] for i in range(a.shape[0] // PACK_ROWS)]


def _pack_shard(sh, dtype, with_ada=True):
    lora = jnp.concatenate([sh['w_decay_up'], sh['w_iclr_up']], axis=1)
    misc = jnp.concatenate([sh['w_ukv'], lora, jnp.zeros((LORA, 2 * LANES), lora.dtype)], axis=0)
    blocks = ((_col_blocks(sh['w_ada'], 768) if with_ada else [])
              + _col_blocks(sh['w_in'], 1408) + _col_blocks(sh['w_proj_a'], 256)
              + _col_blocks(sh['w_proj_b'], 256) + [sh['w_out']] + _col_blocks(sh['w_uq'], 256) + [misc])
    return jnp.concatenate([b.astype(dtype) for b in blocks], axis=1)


def _unpack_shard(p, with_ada=True):
    o = [0]

    def take(n_blocks, cols, used):
        blocks = [p[:, o[0] + i * cols:o[0] + (i + 1) * cols] for i in range(n_blocks)]
        o[0] += n_blocks * cols
        return jnp.concatenate(blocks, axis=0)[:, :used]

    out = {'w_ada': take(4, 768, 768)} if with_ada else {}
    out.update({'w_in': take(4, 1408, 1288), 'w_proj_a': take(2, 256, 256),
                'w_proj_b': take(2, 256, 256), 'w_out': take(1, 1024, 1024), 'w_uq': take(1, 256, 192)})
    misc = take(1, 256, 256)
    out['w_ukv'] = misc[:2 * LORA]
    out['w_decay_up'] = misc[2 * LORA:3 * LORA, :LANES]
    out['w_iclr_up'] = misc[2 * LORA:3 * LORA, LANES:]
    return out


def _pack_small(parts):
    flat = jnp.concatenate([p.reshape(-1) for p in parts])
    return jnp.pad(flat, (0, SMALL_ROWS * LANES - flat.shape[0])).reshape(SMALL_ROWS, LANES)


def _unpack_small(packed):
    flat, out, o = packed.reshape(-1), {}, 0
    for name, n in _SMALL:
        out[name] = flat[o:o + n]
        o += n
    return out


def _pad_heads_cols(w, used, left):
    k = w.shape[0]
    return jnp.pad(w.reshape(k, HEADS, used), ((0, 0), (0, 0), (left, LANES - used - left))).reshape(k, HEADS * LANES)


def _unpad_heads_cols(w, used, left):
    k = w.shape[0]
    return w.reshape(k, HEADS, LANES)[:, :, left:left + used].reshape(k, HEADS * used)


def kernel(x, c, positions, w_ada, b_ada, w_in, q_norm_g, w_uq, kv_norm_g, w_ukv, mu_rwkv, w0, w_decay_up, a0, w_iclr_up, k_k, k_a, r_k, gn_g, gn_b, w_proj_a, w_proj_b, w_out, post_g, post_b, loss_target, m_w_ada, m_b_ada, m_w_in, m_q_norm_g, m_w_uq, m_kv_norm_g, m_w_ukv, m_mu_rwkv, m_w0, m_w_decay_up, m_a0, m_w_iclr_up, m_k_k, m_k_a, m_r_k, m_gn_g, m_gn_b, m_w_proj_a, m_w_proj_b, m_w_out, m_post_g, m_post_b, v_w_ada, v_b_ada, v_w_in, v_q_norm_g, v_w_uq, v_kv_norm_g, v_w_ukv, v_mu_rwkv, v_w0, v_w_decay_up, v_a0, v_w_iclr_up, v_k_k, v_k_a, v_r_k, v_gn_g, v_gn_b, v_w_proj_a, v_w_proj_b, v_w_out, v_post_g, v_post_b):
    given = dict(locals())
    seq = x.shape[1]
    my_c = lax.axis_index("c")

    shard_names = [n for n, _, _ in _SHARDED]
    w_pack = _pack_shard({n: given[n][0] for n in shard_names}, BF16)
    my_half = lax.dynamic_slice_in_dim(w_pack, my_c * HALF_COLS, HALF_COLS, 1)
    gathered = all_gather8("gather_weights", my_half)
    shards = [_unpack_shard(jnp.concatenate([gathered[2 * s], gathered[2 * s + 1]], axis=1)) for s in range(4)]
    full = {n: jnp.concatenate([sh[n] for sh in shards], axis=ax) for n, _, ax in _SHARDED}

    wi = full['w_in']
    zcol = lambda n: jnp.zeros((D_MODEL, n), BF16)
    w_g1 = jnp.concatenate([wi[:, :384], zcol(HEAD), wi[:, 384:416], zcol(LANES - MLA_QK),
                            _pad_heads_cols(wi[:, 416:928], HEAD, HEAD)], axis=1)
    w_g2 = wi[:, 928:3104]
    w_g3 = wi[:, 3104:5152]
    w_uq_p = _pad_heads_cols(full['w_uq'], MLA_QK, 0)
    w_pa_p = jnp.pad(full['w_proj_a'].reshape(HEADS, HEAD, D_MODEL), ((0, 0), (HEAD, 0), (0, 0))).reshape(HEADS * LANES, D_MODEL)
    zl = jnp.zeros((LORA, WIDTH), BF16)
    w_lora = jnp.concatenate([jnp.concatenate([full['w_decay_up'], zl], 1),
                              jnp.concatenate([zl, full['w_iclr_up']], 1)], 0)

    hd = np.arange(WKV_HALF) // HEAD
    ones_half = jnp.asarray(hd[:, None] == hd[None, :], BF16)
    perm_np = np.zeros((LANES, LANES), np.float32)
    for d in range(MLA_ROPE // 2):
        perm_np[HEAD + 16 + d, HEAD + d] = -1.0
        perm_np[HEAD + d, HEAD + 16 + d] = 1.0
    perm = jnp.asarray(perm_np, BF16)
    inv = ROPE_THETA ** (-jnp.arange(0, MLA_ROPE, 2, dtype=F32) / MLA_ROPE)
    ang = positions[0].astype(F32)[:, None] * inv
    cos_a, sin_a = jnp.cos(ang), jnp.sin(ang)
    cs = jnp.concatenate([jnp.ones((seq, HEAD), F32), cos_a, cos_a, jnp.zeros((seq, LANES - MLA_QK), F32),
                          jnp.zeros((seq, HEAD), F32), sin_a, sin_a, jnp.zeros((seq, LANES - MLA_QK), F32)], axis=1)

    x2, tgt = x[0], loss_target[0]
    r_k2 = r_k.reshape(1, WIDTH)

    c8 = jnp.broadcast_to(c, (SUBLANES, D_MODEL))
    ada = ada_fwd(c8, b_ada, gathered)[:1]
    shift, scale, gate = ada[:, :D_MODEL], ada[:, D_MODEL:2 * D_MODEL], ada[:, 2 * D_MODEL:]

    f_in1, f_in2, f_in3 = _make_f_in((512, 1024)), _make_f_in((SHIFT_W, WIDTH)), _make_f_in((1024, 1024))
    tr = min(256, seq)
    p_mla, gpa, p_rwkv, gpb, ma, mb = row_fwd(
        "in_fwd", _make_f_in((512, 1024), (SHIFT_W, WIDTH), (1024, 1024)), [x2], [shift, scale, w_g1, w_g2, w_g3], [],
        [512, 1024, SHIFT_W, WIDTH, 1024, 1024], tr)

    mla_par = [q_norm_g, kv_norm_g, w_uq_p, full['w_ukv']]
    q_f, kv_f, kpe = row_fwd("mla_pre_fwd", f_mla_pre, [p_mla, cs], mla_par, [perm], [1024, 1024, LANES], tr)
    ya, lse = attn_fwd(q_f, kv_f, kpe)

    pre_par = [w0, a0, k_k, k_a, w_lora]
    u, rr, wd, k2, vv, an, bb = shift_stage_fwd("rwkv_pre_fwd", f_rwkv_pre, p_rwkv, mu_rwkv, pre_par, [ones_half],
                                                [WIDTH] * 6, tr)
    y_wkv, states = wkv_fwd(rr, wd, k2, vv, an, bb, ones_half)
    post_b_par = [r_k2, gn_g, gn_b]
    yb, = row_fwd("rwkv_post_fwd", f_rwkv_post, [y_wkv, rr, k2, vv], post_b_par, [ones_half], [WIDTH], tr)

    merge_rows, merge_par = [ya, gpa, yb, gpb, ma, mb], [w_pa_p, full['w_proj_b']]
    merged, = row_fwd("merge_fwd", f_merge, merge_rows, merge_par, [], [D_MODEL], tr)
    loss_rows, loss_par = [merged, x2, tgt], [gate, post_g, post_b, full['w_out']]
    lrows, = row_fwd("loss_fwd", f_loss, loss_rows, loss_par, [], [LANES], tr)
    loss = lax.psum(jnp.sum(lrows[:, 0]), ("x", "y", "c"))

    dl = jnp.broadcast_to((jnp.arange(LANES) == 0).astype(F32), (seq, LANES))
    (dmerged, dx_res), (dgate, dpost_g, dpost_b, dw_out) = row_bwd("loss_bwd", f_loss, loss_rows, 2, loss_par, [], [dl], tr)
    (dya, dgpa, dyb, dgpb, dma, dmb), (dw_pa_p, dw_pb) = row_bwd(
        "merge_bwd", f_merge, merge_rows, 6, merge_par, [], [dmerged], tr)

    (dy_wkv, dr1, dk1, dv1), (dr_k, dgn_g, dgn_b) = row_bwd(
        "rwkv_post_bwd", f_rwkv_post, [y_wkv, rr, k2, vv], 4, post_b_par, [ones_half], [dyb], tr)
    dr2, dwd, dk2, dv2, dan, dbb = wkv_bwd(rr, wd, k2, vv, an, bb, dy_wkv, states, ones_half)
    (du,), (dw0, da0, dk_k, dk_a, dw_lora) = row_bwd(
        "rwkv_pre_bwd", f_rwkv_pre, [u], 1, pre_par, [ones_half],
        [(dr1, dr2), dwd, (dk1, dk2), (dv1, dv2), dan, dbb], tr)
    dp_rwkv, dmu = shift_bwd(du, p_rwkv, mu_rwkv, tr)

    dq_f, dkv_f, dkpe = attn_bwd(q_f, kv_f, kpe, ya, lse, dya)
    (dp_mla,), (dqg, dkvg, dw_uq_p, dw_ukv) = row_bwd(
        "mla_pre_bwd", f_mla_pre, [p_mla, cs], 1, mla_par, [perm], [dq_f, dkv_f, dkpe], tr)

    (dx1,), (dsh1, dsc1, dw_g1) = row_bwd("in1_bwd", f_in1, [x2], 1, [shift, scale, w_g1], [], [dp_mla, dgpa], tr, [dx_res])
    (dx2,), (dsh2, dsc2, dw_g2) = row_bwd("in2_bwd", f_in2, [x2], 1, [shift, scale, w_g2], [], [dp_rwkv, dgpb], tr, [dx1])
    (dx3,), (dsh3, dsc3, dw_g3) = row_bwd("in3_bwd", f_in3, [x2], 1, [shift, scale, w_g3], [], [dma, dmb], tr, [dx2])
    grad_x = dx3[None]

    dada = jnp.concatenate([dsh1 + dsh2 + dsh3, dsc1 + dsc2 + dsc3, dgate], axis=1)
    local = {
        'w_in': jnp.concatenate([dw_g1[:, :384], dw_g1[:, 448:480], _unpad_heads_cols(dw_g1[:, 512:], HEAD, HEAD),
                                 dw_g2, dw_g3], axis=1),
        'w_uq': _unpad_heads_cols(dw_uq_p, MLA_QK, 0),
        'w_ukv': dw_ukv,
        'w_decay_up': dw_lora[:LORA, :WIDTH],
        'w_iclr_up': dw_lora[LORA:, WIDTH:],
        'w_proj_a': dw_pa_p.reshape(HEADS, LANES, D_MODEL)[:, HEAD:].reshape(WIDTH, D_MODEL),
        'w_proj_b': dw_pb,
        'w_out': dw_out,
    }
    small_local = {'b_ada': dada, 'q_norm_g': dqg, 'kv_norm_g': dkvg, 'mu_rwkv': dmu, 'w0': dw0, 'a0': da0,
                   'k_k': dk_k, 'k_a': dk_a, 'r_k': dr_k, 'gn_g': dgn_g, 'gn_b': dgn_b,
                   'post_g': dpost_g, 'post_b': dpost_b}

    def shard_of(g, axis, s):
        n = g.shape[axis] // 4
        return lax.slice_in_dim(g, s * n, (s + 1) * n, axis=axis)

    packed = jnp.stack([_pack_shard({n: shard_of(local[n], ax, s) for n, _, ax in _SHARDED if n != 'w_ada'}, F32, False)
                        for s in range(4)])
    keep = lax.dynamic_slice_in_dim(packed, my_c * GRAD_HALF, GRAD_HALF, 2).reshape(4 * PACK_ROWS, GRAD_HALF)
    give = lax.dynamic_slice_in_dim(packed, (1 - my_c) * GRAD_HALF, GRAD_HALF, 2).reshape(4 * PACK_ROWS, GRAD_HALF)
    pair_sum, = row_fwd("pair_sum", lambda p, q: (p + q,), [keep, pair_swap("swap_halves", give)], [], [],
                        [GRAD_HALF], PACK_ROWS // 2, BF16)
    received = chip_all_to_all("exchange_grads", pair_sum.reshape(4, PACK_ROWS, GRAD_HALF))
    my_sum = sum_slots(received, PACK_ROWS // 2)
    other_sum = pair_swap("swap_sums", my_sum)
    halves = [jnp.where(my_c == 0, my_sum, other_sum), jnp.where(my_c == 0, other_sum, my_sum)]
    g_shard = _unpack_shard(jnp.concatenate(halves, axis=1), False)

    small_pack = lambda d, extra=(): _pack_small([d[n] for n, _ in _SMALL] + list(extra))
    small_all = all_gather8("gather_small", small_pack(small_local, [c * jax.nn.sigmoid(c)]))
    sc_all = small_all[:, SMALL_USED:SMALL_USED + D_MODEL // LANES].reshape(N_DEV, D_MODEL)
    dada_all = small_all[:, :3 * D_MODEL // LANES].reshape(N_DEV, 3 * D_MODEL)
    my_cols = lax.dynamic_slice_in_dim(dada_all, (2 * lax.axis_index("x") + lax.axis_index("y")) * 768, 768, 1)
    g_shard['w_ada'] = ada_grad_shard([sc_all[b].reshape(D_MODEL, 1) for b in range(N_DEV)], my_cols)

    big = [{}, {}, {}, {}]
    for n in shard_names:
        w2, m2, v2 = given[n][0], given['m_' + n][0], given['v_' + n][0]
        cols = w2.shape[1]
        outs = row_fwd("adamw_" + n, f_adamw, [w2, g_shard[n], m2, v2], [], [], [cols] * 3, min(256, w2.shape[0]))
        for dst, val in zip(big, (g_shard[n], *outs)):
            dst[n] = val

    small_out = adamw_small(small_all, small_pack({n: given[n] for n, _ in _SMALL}),
                            small_pack({n: given['m_' + n] for n, _ in _SMALL}),
                            small_pack({n: given['v_' + n] for n, _ in _SMALL}))

    results = []
    for big_k, packed_small in zip(big, small_out):
        small = _unpack_small(packed_small)
        results.append([(big_k[n] if n in big_k else small[n]).reshape(given[n].shape) for n in _WEIGHTS])
    return (loss, grad_x, *results[0], *results[1], *results[2], *results[3])
```

```python
from typing import NamedTuple

import numpy as np
import jax
import jax.numpy as jnp
from jax import lax
from jax.experimental import pallas as pl
from jax.experimental.pallas import tpu as pltpu

F32 = jnp.float32
BF16 = jnp.bfloat16

D_MODEL = 1024
LN_EPS = 1e-5
RMS_EPS = 1e-6
GN_EPS = 64e-5
HEADS = 8
HEAD = 64
MLA_ROPE = 32
MLA_QK = HEAD + MLA_ROPE
ROPE_THETA = 10000.0
WIDTH = HEADS * HEAD
LORA = 64
SHIFT_W = 3 * WIDTH + 2 * LORA
CHUNK = 64
ALPHA = 2.0 ** 0.25

ADAM_LR, ADAM_B1, ADAM_B2, ADAM_EPS, ADAM_WD, ADAM_STEP = 0.001, 0.9, 0.999, 1e-08, 0.01, 10

LANES = 128
SUBLANES = 8
VMEM_LIMIT = 56 * 1024 * 1024
N_DEV = 8
MESH = pl.DeviceIdType.MESH
NEG = -1e30

_WEIGHTS = ['w_ada', 'b_ada', 'w_in', 'q_norm_g', 'w_uq', 'kv_norm_g', 'w_ukv', 'mu_rwkv', 'w0',
            'w_decay_up', 'a0', 'w_iclr_up', 'k_k', 'k_a', 'r_k', 'gn_g', 'gn_b', 'w_proj_a',
            'w_proj_b', 'w_out', 'post_g', 'post_b']
_SHARDED = [('w_ada', (1024, 3072), 1), ('w_in', (1024, 5152), 1), ('w_uq', (256, 768), 1),
            ('w_ukv', (128, 1024), 1), ('w_decay_up', (64, 512), 1), ('w_iclr_up', (64, 512), 1),
            ('w_proj_a', (512, 1024), 1), ('w_proj_b', (512, 1024), 1), ('w_out', (1024, 1024), 0)]
_SMALL = [('b_ada', 3072), ('q_norm_g', 256), ('kv_norm_g', 128), ('mu_rwkv', 1664), ('w0', 512),
          ('a0', 512), ('k_k', 512), ('k_a', 512), ('r_k', 512), ('gn_g', 512), ('gn_b', 512),
          ('post_g', 1024), ('post_b', 1024)]
PACK_ROWS = 256
PACK_COLS = 11264
HALF_COLS = PACK_COLS // 2
ADA_COLS = 4 * 768
GRAD_HALF = (PACK_COLS - ADA_COLS) // 2
SMALL_USED = 84
SMALL_ROWS = 96


def _bf(x):
    return x.astype(BF16)


def _dot(a, b, ca, cb):
    return lax.dot_general(a, b, (((ca,), (cb,)), ((), ())), preferred_element_type=F32)


class Weight(NamedTuple):
    value: jax.Array
    grad: jax.Array


@jax.custom_vjp
def _mm(a, w, w_grad):
    return _dot(_bf(a), _bf(w), 1, 0)


def _mm_fwd(a, w, w_grad):
    return _mm(a, w, w_grad), (a, w)


def _mm_bwd(res, g):
    a, w = res
    gb = _bf(g)
    return _dot(gb, _bf(w), 1, 1), jnp.zeros_like(w), _dot(_bf(a), gb, 0, 0)


_mm.defvjp(_mm_fwd, _mm_bwd)


def mm(a, w):
    if isinstance(w, Weight):
        return _mm(a, w.value, w.grad)
    return _dot(_bf(a), _bf(w), 1, 0)


def _split3(x):
    hi = _bf(x)
    r1 = x - hi.astype(F32)
    mid = _bf(r1)
    lo = _bf(r1 - mid.astype(F32))
    return hi, mid, lo


def _exact_dot(x, m, cm):
    hi, mid, lo = _split3(x)
    return _dot(hi, m, 1, cm) + _dot(mid, m, 1, cm) + _dot(lo, m, 1, cm)


def _head_sums(x, ones_blocks):
    n = ones_blocks.shape[0]
    parts = [_exact_dot(x[:, o:o + n], ones_blocks, 0) for o in range(0, x.shape[1], n)]
    return parts[0] if len(parts) == 1 else jnp.concatenate(parts, axis=1)


@jax.custom_vjp
def segsum(x, ones_blocks):
    return _head_sums(x, ones_blocks)


def _segsum_fwd(x, ones_blocks):
    return segsum(x, ones_blocks), ones_blocks


def _segsum_bwd(ones_blocks, g):
    return _head_sums(g, ones_blocks), jnp.zeros_like(ones_blocks)


segsum.defvjp(_segsum_fwd, _segsum_bwd)


@jax.custom_vjp
def lane_perm(x, perm):
    return _exact_dot(x, perm, 0)


def _lane_perm_fwd(x, perm):
    return lane_perm(x, perm), perm


def _lane_perm_bwd(perm, g):
    return _exact_dot(g, perm, 1), jnp.zeros_like(perm)


lane_perm.defvjp(_lane_perm_fwd, _lane_perm_bwd)


def _silu(z):
    return z * jax.nn.sigmoid(z)


def _softplus(z):
    return jnp.maximum(z, 0.0) + jnp.log(1.0 + jnp.exp(-jnp.abs(z)))


def _layer_norm(x):
    xc = x - jnp.mean(x, -1, keepdims=True)
    return xc * lax.rsqrt(jnp.mean(xc * xc, -1, keepdims=True) + LN_EPS)


def _rope(t, cos_t, sin_t, perm):
    outs = []
    for h in range(t.shape[1] // LANES):
        th = t[:, h * LANES:(h + 1) * LANES]
        outs.append(th * cos_t + lane_perm(th, perm) * sin_t)
    return outs[0] if len(outs) == 1 else jnp.concatenate(outs, axis=1)


def _make_f_in(*split_groups):
    def f_in(x, shift, scale, *weights):
        h = _layer_norm(x) * (1.0 + scale) + shift
        outs = []
        for w, splits in zip(weights, split_groups):
            p, o = mm(h, w), 0
            for s in splits:
                outs.append(p[:, o:o + s])
                o += s
        return tuple(outs)
    return f_in


def f_mla_pre(p, cs, qg, kvg, w_uq, w_ukv, perm):
    q_c, kv_c, k_r = p[:, :256], p[:, 256:384], p[:, 384:512]
    cos_t, sin_t = cs[:, :LANES], cs[:, LANES:]
    qn = q_c * lax.rsqrt(jnp.mean(q_c * q_c, -1, keepdims=True) + RMS_EPS) * qg
    kvn = kv_c * lax.rsqrt(jnp.mean(kv_c * kv_c, -1, keepdims=True) + RMS_EPS) * kvg
    q = _rope(mm(qn, w_uq), cos_t, sin_t, perm)
    kv = mm(kvn, w_ukv)
    return q, kv, _rope(k_r, cos_t, sin_t, perm)


def f_rwkv_pre(u, w0, a0, k_k, k_a, w_lora, ones_blocks):
    r, k, v, lo = u[:, :WIDTH], u[:, WIDTH:2 * WIDTH], u[:, 2 * WIDTH:3 * WIDTH], u[:, 3 * WIDTH:]
    lane = lax.broadcasted_iota(jnp.int32, lo.shape, 1)
    dl = mm(jnp.where(lane < LORA, jnp.tanh(lo), lo), w_lora)
    w_log = -_softplus(-(w0 + dl[:, :WIDTH])) - 0.5
    decay = jnp.exp(-jnp.exp(w_log))
    a = jax.nn.sigmoid(a0 + dl[:, WIDTH:])
    kk = k * k_k
    kk = kk / jnp.maximum(jnp.sqrt(segsum(kk * kk, ones_blocks)), 1e-12)
    k2 = k * (1.0 + (a - 1.0) * k_a)
    return r, decay, k2, v, -kk, kk * a


def f_rwkv_post(y, r, k2, v, r_k, gn_g, gn_b, ones_blocks):
    yc = y - segsum(y, ones_blocks) * (1.0 / HEAD)
    yn = yc * lax.rsqrt(segsum(yc * yc, ones_blocks) * (1.0 / HEAD) + GN_EPS)
    return (yn * gn_g + gn_b + segsum(r * k2 * r_k, ones_blocks) * v,)


def f_merge(ya, gpa, yb, gpb, ma, mb, w_pa, w_pb):
    pa = mm(ya * _silu(gpa), w_pa)
    pb = mm(yb * _silu(gpb), w_pb)
    return (jax.nn.sigmoid(ma) * pa + jax.nn.sigmoid(mb) * pb,)


def f_loss(merged, x, tgt, gate, post_g, post_b, w_out):
    z = ALPHA * x + (1.0 + gate) * mm(merged, w_out)
    err = _layer_norm(z) * post_g + post_b - tgt
    lrow = 0.5 * jnp.mean(err * err, -1, keepdims=True)
    return (jnp.broadcast_to(lrow, (lrow.shape[0], LANES)),)


def f_adamw(w, g, m, v):
    m2 = ADAM_B1 * m + (1.0 - ADAM_B1) * g
    v2 = ADAM_B2 * v + (1.0 - ADAM_B2) * jnp.square(g)
    m_hat = m2 / (1.0 - ADAM_B1 ** ADAM_STEP)
    v_hat = v2 / (1.0 - ADAM_B2 ** ADAM_STEP)
    return -ADAM_LR * (m_hat / (jnp.sqrt(v_hat) + ADAM_EPS) + ADAM_WD * w), m2, v2


def _params(sem=("arbitrary",)):
    return pltpu.CompilerParams(dimension_semantics=sem, vmem_limit_bytes=VMEM_LIMIT)


def _row_spec(tr, a):
    return pl.BlockSpec((tr, a.shape[1]), lambda i: (i, 0))


def _full_spec(a):
    return pl.BlockSpec(a.shape, lambda i: (0,) * a.ndim)


def row_fwd(name, f, rows, params, consts, out_widths, tr, out_dtype=F32):
    n_rows = rows[0].shape[0]
    nr, npar, ncon = len(rows), len(params), len(consts)

    def body(*refs):
        rv = [r[...] for r in refs[:nr]]
        pv = [r[...] for r in refs[nr:nr + npar]]
        cv = [r[...] for r in refs[nr + npar:nr + npar + ncon]]
        outs = f(*rv, *pv, *cv)
        for o_ref, o in zip(refs[nr + npar + ncon:], outs):
            o_ref[...] = o.astype(o_ref.dtype)

    return pl.pallas_call(
        body, name=name, grid=(n_rows // tr,),
        in_specs=[_row_spec(tr, a) for a in rows] + [_full_spec(a) for a in list(params) + list(consts)],
        out_specs=[pl.BlockSpec((tr, w), lambda i: (i, 0)) for w in out_widths],
        out_shape=[jax.ShapeDtypeStruct((n_rows, w), out_dtype) for w in out_widths],
        compiler_params=_params(),
    )(*rows, *params, *consts)


def row_bwd(name, f, rows, n_diff, params, consts, douts, tr, add_rows=None):
    n_rows = rows[0].shape[0]
    douts = [d if isinstance(d, (tuple, list)) else (d,) for d in douts]
    counts = [len(d) for d in douts]
    flat_d = [a for d in douts for a in d]
    add_rows = add_rows or [None] * n_diff
    adds = [a for a in add_rows if a is not None]
    nr, npar, ncon, nd, na = len(rows), len(params), len(consts), len(flat_d), len(adds)

    def body(*refs):
        o = 0
        rv = [r[...] for r in refs[o:o + nr]]; o += nr
        pv = [Weight(r[...], jnp.zeros(r.shape, F32)) if r.dtype == BF16 else r[...] for r in refs[o:o + npar]]
        o += npar
        cv = [r[...] for r in refs[o:o + ncon]]; o += ncon
        dv = []
        for cnt in counts:
            s = refs[o][...]
            for e in range(1, cnt):
                s = s + refs[o + e][...]
            dv.append(s)
            o += cnt
        add_v = [r[...] for r in refs[o:o + na]]; o += na
        drow_refs = refs[o:o + n_diff]; o += n_diff
        dpar_refs = refs[o:o + npar]

        def g(*args):
            return tuple(f(*args[:n_diff], *rv[n_diff:], *args[n_diff:], *cv))

        _, vjp = jax.vjp(g, *rv[:n_diff], *pv)
        grads = vjp(tuple(dv))
        ai = 0
        for j, (r, gr) in enumerate(zip(drow_refs, grads[:n_diff])):
            if add_rows[j] is not None:
                gr = gr + add_v[ai]
                ai += 1
            r[...] = gr

        @pl.when(pl.program_id(0) == 0)
        def _():
            for r in dpar_refs:
                r[...] = jnp.zeros_like(r)

        for r, gr in zip(dpar_refs, grads[n_diff:]):
            r[...] += gr.grad if isinstance(gr, Weight) else gr

    outs = pl.pallas_call(
        body, name=name, grid=(n_rows // tr,),
        in_specs=([_row_spec(tr, a) for a in rows] + [_full_spec(a) for a in list(params) + list(consts)]
                  + [_row_spec(tr, a) for a in flat_d + adds]),
        out_specs=[_row_spec(tr, a) for a in rows[:n_diff]] + [_full_spec(a) for a in params],
        out_shape=([jax.ShapeDtypeStruct(a.shape, F32) for a in rows[:n_diff]]
                   + [jax.ShapeDtypeStruct(a.shape, F32) for a in params]),
        compiler_params=_params(),
    )(*rows, *params, *consts, *flat_d, *adds)
    return outs[:n_diff], outs[n_diff:]


def shift_stage_fwd(name, f, p, mu, params, consts, out_widths, tr):
    n_rows, w = p.shape
    npar, ncon = len(params), len(consts)

    def body(*refs):
        p_ref, mu_ref = refs[:2]
        pv = [r[...] for r in refs[2:2 + npar]]
        cv = [r[...] for r in refs[2 + npar:2 + npar + ncon]]
        u_ref, out_refs, carry = refs[2 + npar + ncon], refs[3 + npar + ncon:-1], refs[-1]

        @pl.when(pl.program_id(0) == 0)
        def _():
            carry[...] = jnp.zeros_like(carry)

        x = p_ref[...]
        rolled = pltpu.roll(x, 1, 0)
        head = pltpu.roll(carry[...], 1, 0)
        fixed = jnp.concatenate([head, rolled[SUBLANES:]], axis=0)
        row = lax.broadcasted_iota(jnp.int32, x.shape, 0)
        prev = jnp.where(row == 0, fixed, rolled)
        u = x + (prev - x) * mu_ref[...]
        u_ref[...] = u
        carry[...] = x[tr - SUBLANES:]
        for o_ref, o in zip(out_refs, f(u, *pv, *cv)):
            o_ref[...] = o

    return pl.pallas_call(
        body, name=name, grid=(n_rows // tr,),
        in_specs=[_row_spec(tr, p), _full_spec(mu)] + [_full_spec(a) for a in list(params) + list(consts)],
        out_specs=[_row_spec(tr, p)] + [pl.BlockSpec((tr, ow), lambda i: (i, 0)) for ow in out_widths],
        out_shape=[jax.ShapeDtypeStruct(p.shape, F32)] + [jax.ShapeDtypeStruct((n_rows, ow), F32) for ow in out_widths],
        scratch_shapes=[pltpu.VMEM((SUBLANES, w), F32)],
        compiler_params=_params(),
    )(p, mu, *params, *consts)


def shift_stage_bwd(name, f, p, u, mu, params, consts, douts, tr):
    n_rows, w = p.shape
    nb = n_rows // tr
    douts = [d if isinstance(d, (tuple, list)) else (d,) for d in douts]
    counts = [len(d) for d in douts]
    flat_d = [a for d in douts for a in d]
    npar, ncon, nd = len(params), len(consts), len(flat_d)

    def body(*refs):
        p_ref, u_ref, mu_ref = refs[:3]
        o = 3
        pv = [Weight(r[...], jnp.zeros(r.shape, F32)) if r.dtype == BF16 else r[...] for r in refs[o:o + npar]]
        o += npar
        cv = [r[...] for r in refs[o:o + ncon]]; o += ncon
        dv = []
        for cnt in counts:
            s = refs[o][...]
            for e in range(1, cnt):
                s = s + refs[o + e][...]
            dv.append(s)
            o += cnt
        dp_ref, dmu_ref = refs[o], refs[o + 1]
        dpar_refs, carry = refs[o + 2:o + 2 + npar], refs[-1]

        @pl.when(pl.program_id(0) == 0)
        def _():
            carry[...] = jnp.zeros_like(carry)
            dmu_ref[...] = jnp.zeros_like(dmu_ref)
            for r in dpar_refs:
                r[...] = jnp.zeros_like(r)

        _, vjp = jax.vjp(lambda uu, *pp: tuple(f(uu, *pp, *cv)), u_ref[...], *pv)
        grads = vjp(tuple(dv))
        for r, gr in zip(dpar_refs, grads[1:]):
            r[...] += gr.grad if isinstance(gr, Weight) else gr

        d = grads[0]
        rolled = pltpu.roll(d, tr - 1, 0)
        tail = pltpu.roll(carry[...], SUBLANES - 1, 0)
        fixed = jnp.concatenate([rolled[:tr - SUBLANES], tail], axis=0)
        row = lax.broadcasted_iota(jnp.int32, d.shape, 0)
        nxt = jnp.where(row == tr - 1, fixed, rolled)
        mu_v = mu_ref[...]
        dp_ref[...] = d * (1.0 - mu_v) + nxt * mu_v
        dmu_ref[...] += jnp.sum(p_ref[...] * (nxt - d), axis=0, keepdims=True)
        carry[...] = d[:SUBLANES]

    rev = lambda i: (nb - 1 - i, 0)
    rows_rev = lambda a: pl.BlockSpec((tr, a.shape[1]), rev)
    outs = pl.pallas_call(
        body, name=name, grid=(nb,),
        in_specs=([rows_rev(p), rows_rev(u), _full_spec(mu)] + [_full_spec(a) for a in list(params) + list(consts)]
                  + [rows_rev(a) for a in flat_d]),
        out_specs=[rows_rev(p), _full_spec(mu)] + [_full_spec(a) for a in params],
        out_shape=([jax.ShapeDtypeStruct(p.shape, F32), jax.ShapeDtypeStruct(mu.shape, F32)]
                   + [jax.ShapeDtypeStruct(a.shape, F32) for a in params]),
        scratch_shapes=[pltpu.VMEM((SUBLANES, w), F32)],
        compiler_params=_params(),
    )(p, u, mu, *params, *consts, *flat_d)
    return outs[0], outs[1], outs[2:]


ATT_T = 256


def _att_rows(j):
    return pl.ds(pl.multiple_of(j * ATT_T, ATT_T), ATT_T)


def _att_prep(kv_ref, kpe_ref, kf_scr, vf_scr, n_blocks):
    lane = lax.broadcasted_iota(jnp.int32, (ATT_T, LANES), 1)

    def prep(j, _):
        rows = _att_rows(j)
        kv = kv_ref[rows, :]
        kf_scr[rows, :] = _bf(jnp.where(lane < HEAD, kv, kpe_ref[rows, :]))
        vf_scr[rows, :] = _bf(jnp.where(lane >= HEAD, kv, 0.0))
        return 0

    lax.fori_loop(0, n_blocks, prep, 0)


def _att_diag_mask():
    shift = CHUNK.bit_length() - 1
    qc = jnp.right_shift(lax.broadcasted_iota(jnp.int32, (ATT_T, ATT_T), 0), shift)
    kc = jnp.right_shift(lax.broadcasted_iota(jnp.int32, (ATT_T, ATT_T), 1), shift)
    return kc <= qc


def _wide(x):
    return jnp.concatenate([x] * (ATT_T // LANES), axis=1)


def attn_fwd(q, kv, kpe):
    seq = q.shape[0]
    nb = seq // ATT_T
    assert seq % (2 * ATT_T) == 0, "blocks are taken two per trip"
    scale = MLA_QK ** -0.5

    def body(q_ref, kv_ref, kpe_ref, o_ref, lse_ref, kf_scr, vf_scr):
        _att_prep(kv_ref, kpe_ref, kf_scr, vf_scr, nb)
        mask = _att_diag_mask()

        def scores(qb, kj):
            return _dot(qb, kf_scr[_att_rows(kj), :], 1, 1) * scale

        def update(s, kj, carry, masked):
            m, l, acc = carry
            if masked:
                s = jnp.where(mask, s, NEG)
            m_new = jnp.maximum(m, jnp.broadcast_to(jnp.max(s, -1, keepdims=True), m.shape))
            alpha = jnp.exp(m - m_new)
            p = jnp.exp(s - _wide(m_new))
            l = alpha * l + jnp.broadcast_to(jnp.sum(p, -1, keepdims=True), l.shape)
            acc = alpha * acc + _dot(_bf(p), vf_scr[_att_rows(kj), :], 1, 0)
            return m_new, l, acc

        def finish(rows, carry):
            m, l, acc = carry
            o_ref[rows, :] = acc / l
            lse_ref[rows, :] = m + jnp.log(l)

        def q_pair(qp, _):
            rows_a, rows_b = _att_rows(2 * qp), _att_rows(2 * qp + 1)
            qa, qb = _bf(q_ref[rows_a, :]), _bf(q_ref[rows_b, :])
            init = (jnp.full((ATT_T, LANES), NEG, F32), jnp.zeros((ATT_T, LANES), F32),
                    jnp.zeros((ATT_T, LANES), F32))

            def trip(kj, c):
                ca, cb, sa, sb = c
                sa_next, sb_next = scores(qa, kj + 1), scores(qb, kj + 1)
                return update(sa, kj, ca, False), update(sb, kj, cb, False), sa_next, sb_next

            ca, cb, sa, sb = lax.fori_loop(0, 2 * qp, trip, (init, init, scores(qa, 0), scores(qb, 0)))
            sb_last = scores(qb, 2 * qp + 1)
            ca = update(sa, 2 * qp, ca, True)
            cb = update(sb_last, 2 * qp + 1, update(sb, 2 * qp, cb, False), True)
            finish(rows_a, ca)
            finish(rows_b, cb)
            return 0

        lax.fori_loop(0, nb // 2, q_pair, 0)

    head = pl.BlockSpec((seq, LANES), lambda h: (0, h))
    return pl.pallas_call(
        body, name="attn_fwd", grid=(HEADS,),
        in_specs=[head, head, pl.BlockSpec((seq, LANES), lambda h: (0, 0))],
        out_specs=[head, head],
        out_shape=[jax.ShapeDtypeStruct((seq, HEADS * LANES), F32)] * 2,
        scratch_shapes=[pltpu.VMEM((seq, LANES), BF16)] * 2,
        compiler_params=_params(),
    )(q, kv, kpe)


def attn_bwd(q, kv, kpe, o, lse, do):
    seq = q.shape[0]
    nb = seq // ATT_T
    assert seq % (2 * ATT_T) == 0, "blocks are taken two per trip"
    scale = MLA_QK ** -0.5

    def body(q_ref, kv_ref, kpe_ref, o_ref, lse_ref, do_ref, dq_ref, dkv_ref, dkpe_ref,
             kf_scr, vf_scr, qb_scr, dob_scr, dsum):
        lane = lax.broadcasted_iota(jnp.int32, (ATT_T, LANES), 1)

        @pl.when(pl.program_id(0) == 0)
        def _():
            dkpe_ref[...] = jnp.zeros_like(dkpe_ref)

        dq_ref[...] = jnp.zeros_like(dq_ref)
        _att_prep(kv_ref, kpe_ref, kf_scr, vf_scr, nb)

        def pre(j, _):
            rows = _att_rows(j)
            d = do_ref[rows, :]
            qb_scr[rows, :] = _bf(q_ref[rows, :])
            dob_scr[rows, :] = _bf(d)
            dsum[rows, :] = jnp.broadcast_to(jnp.sum(d * o_ref[rows, :], -1, keepdims=True), (ATT_T, LANES))
            return 0

        lax.fori_loop(0, nb, pre, 0)
        mask = _att_diag_mask()

        def front(kf, vf, qi):
            rows = _att_rows(qi)
            return _dot(qb_scr[rows, :], kf, 1, 1), _dot(dob_scr[rows, :], vf, 1, 1)

        def back(kf, qi, fr, carry, masked):
            s, dp = fr
            dk, dv = carry
            rows = _att_rows(qi)
            qb, dob = qb_scr[rows, :], dob_scr[rows, :]
            p = jnp.exp(s * scale - _wide(lse_ref[rows, :]))
            if masked:
                p = jnp.where(mask, p, 0.0)
            ds = _bf(p * (dp - _wide(dsum[rows, :])) * scale)
            return (dk + _dot(ds, qb, 0, 0), dv + _dot(_bf(p), dob, 0, 0)), _dot(ds, kf, 1, 0)

        def store(krows, carry):
            dk, dv = carry
            dkv_ref[krows, :] = jnp.where(lane < HEAD, dk, dv)
            dkpe_ref[krows, :] += jnp.where((lane >= HEAD) & (lane < MLA_QK), dk, 0.0)

        def k_pair(kp, _):
            ka, kb = 2 * kp, 2 * kp + 1
            rows_a, rows_b = _att_rows(ka), _att_rows(kb)
            kfa, vfa, kfb, vfb = kf_scr[rows_a, :], vf_scr[rows_a, :], kf_scr[rows_b, :], vf_scr[rows_b, :]
            zero = jnp.zeros((ATT_T, LANES), F32)
            ca, dq_a = back(kfa, ka, front(kfa, vfa, ka), (zero, zero), True)
            dq_ref[rows_a, :] += dq_a
            ca, dq_a = back(kfa, kb, front(kfa, vfa, kb), ca, False)
            cb, dq_b = back(kfb, kb, front(kfb, vfb, kb), (zero, zero), True)
            dq_ref[rows_b, :] += dq_a + dq_b

            def both(qi, c):
                ca, cb, fa, fb = c
                nxt = jnp.minimum(qi + 1, nb - 1)
                fa_next, fb_next = front(kfa, vfa, nxt), front(kfb, vfb, nxt)
                ca, dq_a = back(kfa, qi, fa, ca, False)
                cb, dq_b = back(kfb, qi, fb, cb, False)
                dq_ref[_att_rows(qi), :] += dq_a + dq_b
                return ca, cb, fa_next, fb_next

            first = jnp.minimum(kb + 1, nb - 1)
            ca, cb, _, _ = lax.fori_loop(kb + 1, nb, both, (ca, cb, front(kfa, vfa, first), front(kfb, vfb, first)))
            store(rows_a, ca)
            store(rows_b, cb)
            return 0

        lax.fori_loop(0, nb // 2, k_pair, 0)

    head = pl.BlockSpec((seq, LANES), lambda h: (0, h))
    shared = pl.BlockSpec((seq, LANES), lambda h: (0, 0))
    return pl.pallas_call(
        body, name="attn_bwd", grid=(HEADS,),
        in_specs=[head, head, shared, head, head, head],
        out_specs=[head, head, shared],
        out_shape=[jax.ShapeDtypeStruct((seq, HEADS * LANES), F32)] * 2
        + [jax.ShapeDtypeStruct((seq, LANES), F32)],
        scratch_shapes=[pltpu.VMEM((seq, LANES), BF16)] * 4 + [pltpu.VMEM((seq, LANES), F32)],
        compiler_params=_params(),
    )(q, kv, kpe, o, lse, do)


WKV_TB = 128
WKV_GROUP = SUBLANES
WKV_HALF = WIDTH // 2


def _wkv_consts():
    row = lax.broadcasted_iota(jnp.int32, (HEAD, WKV_HALF), 0)
    lane = lax.broadcasted_iota(jnp.int32, (HEAD, WKV_HALF), 1)
    diag = row == jnp.bitwise_and(lane, HEAD - 1)
    sub = lax.broadcasted_iota(jnp.int32, (WKV_GROUP, WKV_HALF), 0)
    return diag, sub


def _halves(x):
    return [x[:, :WKV_HALF], x[:, WKV_HALF:]]


def _diag_rows(row, diag):
    return _bf(jnp.where(diag, jnp.broadcast_to(row, diag.shape), 0.0))


def _put_row(tile, row, i, sub):
    return jnp.where(sub == i, jnp.broadcast_to(row, tile.shape), tile)


def _col_sum(x):
    return jnp.sum(x, axis=0, keepdims=True)


def _step(x, i):
    return x[i * HEAD:(i + 1) * HEAD]


def _head_dots(prods, ones_b, sub):
    tile = jnp.zeros((WKV_GROUP, WKV_HALF), F32)
    for i, p in enumerate(prods):
        tile = _put_row(tile, p, i, sub)
    res = _exact_dot(tile, ones_b, 0)
    return [res[i:i + 1] for i in range(len(prods))]


def wkv_fwd(r, w, k, v, a, b, ones_half):
    seq = r.shape[0]

    def body(r_ref, w_ref, k_ref, v_ref, a_ref, b_ref, ones_ref, y_ref, st_ref, s_scr):
        @pl.when(pl.program_id(0) == 0)
        def _():
            s_scr[...] = jnp.zeros_like(s_scr)

        ones_b = ones_ref[...]
        diag, sub = _wkv_consts()

        ng = WKV_TB // WKV_GROUP
        last = WKV_GROUP - 2

        def rows_of(g):
            return pl.ds(pl.multiple_of(g * WKV_GROUP, WKV_GROUP), WKV_GROUP)

        def pair_rows(x8, t):
            return jnp.concatenate([_diag_rows(x8[t:t + 1], diag), _diag_rows(x8[t + 1:t + 2], diag)], axis=0)

        def put_y(g, pairs_y):
            tile = _halves(y_ref[rows_of(g), :])
            for hf in range(2):
                tile[hf] = _put_row(_put_row(tile[hf], pairs_y[hf][0], last, sub), pairs_y[hf][1], last + 1, sub)
            y_ref[rows_of(g), :] = jnp.concatenate(tile, axis=1)

        def read_out(yexp):
            return _col_sum(jnp.where(diag, yexp[:HEAD], 0.0)), _col_sum(jnp.where(diag, yexp[HEAD:], 0.0))

        def group(g, carry):
            state, v_cur, read = (list(c) for c in carry)
            base = pl.multiple_of(g * WKV_GROUP, WKV_GROUP)
            rows = rows_of(g)
            r8, w8, k8, v8, a8, b8 = (_halves(ref[rows, :]) for ref in (r_ref, w_ref, k_ref, v_ref, a_ref, b_ref))
            v_after = _halves(v_ref[rows_of(jnp.minimum(g + 1, ng - 1)), :])
            evens = range(0, WKV_GROUP, 2)
            dots = [_head_dots([b8[hf][t:t + 1] * a8[hf][t + 1:t + 2] for t in evens]
                               + [k8[hf][t:t + 1] * a8[hf][t + 1:t + 2] for t in evens], ones_b, sub) for hf in range(2)]
            y8 = [jnp.zeros((WKV_GROUP, WKV_HALF), F32)] * 2
            y_before = [None, None]
            for t in evens:
                s0, s1 = slice(t, t + 1), slice(t + 1, t + 2)
                both = []
                for hf in range(2):
                    s_in = state[hf]
                    v_next = pair_rows(v8[hf], t + 2) if t < last else pair_rows(v_after[hf], 0)
                    res = _dot(jnp.concatenate([_bf(s_in * a8[hf][s0]), _bf(s_in * (w8[hf][s0] * a8[hf][s1])),
                                                v_next, read[hf]], axis=0), ones_b, 1, 0)
                    sa0, v0, v1 = res[:HEAD], v_cur[hf][:HEAD], v_cur[hf][HEAD:]
                    st0 = s_in * w8[hf][s0] + sa0 * b8[hf][s0] + v0 * k8[hf][s0]
                    sa1 = res[HEAD:2 * HEAD] + sa0 * dots[hf][t // 2] + v0 * dots[hf][WKV_GROUP // 2 + t // 2]
                    st1 = st0 * w8[hf][s1] + sa1 * b8[hf][s1] + v1 * k8[hf][s1]
                    both.append((st0, st1))
                    state[hf], v_cur[hf] = st1, res[2 * HEAD:4 * HEAD]
                    read[hf] = jnp.concatenate([_bf(st0 * r8[hf][s0]), _bf(st1 * r8[hf][s1])], axis=0)
                    ya, yb = read_out(res[4 * HEAD:])
                    if t == 0:
                        y_before[hf] = (ya, yb)
                    else:
                        y8[hf] = _put_row(_put_row(y8[hf], ya, t - 2, sub), yb, t - 1, sub)
                for j in range(2):
                    st_ref[base + t + j] = jnp.concatenate([both[0][j], both[1][j]], axis=1)
            y_ref[rows, :] = jnp.concatenate(y8, axis=1)
            put_y(jnp.maximum(g - 1, 0), y_before)
            return tuple(state), tuple(v_cur), tuple(read)

        v_first = _halves(v_ref[rows_of(0), :])
        init = (tuple(_halves(s_scr[...])),
                tuple(_dot(pair_rows(v_first[hf], 0), ones_b, 1, 0) for hf in range(2)),
                tuple(jnp.zeros((2 * HEAD, WKV_HALF), BF16) for _ in range(2)))
        fin, _, read = lax.fori_loop(0, ng, group, init)
        put_y(ng - 1, [read_out(_dot(read[hf], ones_b, 1, 0)) for hf in range(2)])
        s_scr[...] = jnp.concatenate(fin, axis=1)

    vec = pl.BlockSpec((WKV_TB, WIDTH), lambda i: (i, 0))
    return pl.pallas_call(
        body, name="wkv_fwd", grid=(seq // WKV_TB,),
        in_specs=[vec] * 6 + [_full_spec(ones_half)],
        out_specs=[vec, pl.BlockSpec((WKV_TB, HEAD, WIDTH), lambda i: (i, 0, 0))],
        out_shape=[jax.ShapeDtypeStruct((seq, WIDTH), F32), jax.ShapeDtypeStruct((seq, HEAD, WIDTH), F32)],
        scratch_shapes=[pltpu.VMEM((HEAD, WIDTH), F32)],
        compiler_params=_params(),
    )(r, w, k, v, a, b, ones_half)


def wkv_bwd(r, w, k, v, a, b, dy, states, ones_half):
    seq = r.shape[0]
    nb = seq // WKV_TB
    ng = WKV_TB // WKV_GROUP

    def body(r_ref, w_ref, k_ref, v_ref, a_ref, b_ref, dy_ref, st_ref, halo_ref, ones_ref,
             dr_ref, dw_ref, dk_ref, dv_ref, da_ref, db_ref, ds_scr):
        blk = nb - 1 - pl.program_id(0)

        @pl.when(pl.program_id(0) == 0)
        def _():
            ds_scr[...] = jnp.zeros_like(ds_scr)

        ones_b = ones_ref[...]
        diag, sub = _wkv_consts()
        before_block = jnp.where(blk == 0, 0.0, halo_ref[0])

        def rows_of(g):
            return pl.ds(pl.multiple_of(g * WKV_GROUP, WKV_GROUP), WKV_GROUP)

        def expand_rows(hf, dy8, v8, a8, t, s_t, s_u):
            s1, s0 = slice(t, t + 1), slice(t - 1, t)
            return jnp.concatenate([_diag_rows(dy8[hf][s1], diag), _diag_rows(dy8[hf][s0], diag),
                                    _diag_rows(v8[hf][s1], diag), _diag_rows(v8[hf][s0], diag),
                                    _bf(s_t[hf] * a8[hf][s1]), _bf(s_u[hf] * a8[hf][s0])], axis=0)

        def read_out(x):
            return _col_sum(jnp.where(diag, x[:HEAD], 0.0)), _col_sum(jnp.where(diag, x[HEAD:], 0.0))

        def put_dv(g, pair_dv):
            tile = _halves(dv_ref[rows_of(g), :])
            for hf in range(2):
                tile[hf] = _put_row(_put_row(tile[hf], pair_dv[hf][0], 1, sub), pair_dv[hf][1], 0, sub)
            dv_ref[rows_of(g), :] = jnp.concatenate(tile, axis=1)

        def group(gg, carry):
            dstate, e_cur, dv_pend = (list(c) for c in carry)
            g = ng - 1 - gg
            base = pl.multiple_of(g * WKV_GROUP, WKV_GROUP)
            rows = rows_of(g)
            r8, w8, k8, v8, a8, b8, dy8 = (
                _halves(ref[rows, :]) for ref in (r_ref, w_ref, k_ref, v_ref, a_ref, b_ref, dy_ref))
            g_next = jnp.maximum(g - 1, 0)
            base_next = pl.multiple_of(g_next * WKV_GROUP, WKV_GROUP)
            dy8n, v8n, a8n = (_halves(ref[rows_of(g_next), :]) for ref in (dy_ref, v_ref, a_ref))
            zero8 = jnp.zeros((WKV_GROUP, WKV_HALF), F32)
            out = {n: [zero8, zero8] for n in ("dr", "dw", "dk", "dv", "da", "db")}
            before_group = jnp.where(g == 0, before_block, st_ref[jnp.maximum(base - 1, 0)])
            states = [_halves(before_group)] + [_halves(st_ref[base + i]) for i in range(WKV_GROUP)]
            odds = range(1, WKV_GROUP, 2)
            dots = [_head_dots([a8[hf][t:t + 1] * b8[hf][t - 1:t] for t in odds]
                               + [r8[hf][t - 1:t] * b8[hf][t - 1:t] for t in odds], ones_b, sub) for hf in range(2)]
            dv_after = [None, None]

            def emit(hf, i, d_i, dsa_i, dy_i, v_i, sa_i):
                s_p, s_t = states[i][hf], states[i + 1][hf]
                for n, val in (("dr", _col_sum(s_t * dy_i)), ("dw", _col_sum(d_i * s_p)), ("db", _col_sum(d_i * sa_i)),
                               ("da", _col_sum(s_p * dsa_i)), ("dk", _col_sum(d_i * v_i))):
                    out[n][hf] = _put_row(out[n][hf], val, i, sub)

            for t in reversed(odds):
                s1, s0 = slice(t, t + 1), slice(t - 1, t)
                for hf in range(2):
                    dy1, dy0, v1, v0, sa1, sa0 = (_step(e_cur[hf], j) for j in range(6))
                    d1 = dstate[hf] + dy1 * r8[hf][s1]
                    if t > 1:
                        nxt = expand_rows(hf, dy8, v8, a8, t - 2, states[t - 2], states[t - 3])
                    else:
                        nxt = expand_rows(hf, dy8n, v8n, a8n, WKV_GROUP - 1, _halves(st_ref[base_next + WKV_GROUP - 2]),
                                          _halves(st_ref[base_next + WKV_GROUP - 3]))
                    res = _dot(jnp.concatenate([_bf(d1 * b8[hf][s1]), _bf(d1 * (w8[hf][s1] * b8[hf][s0])),
                                                nxt, dv_pend[hf]], axis=0), ones_b, 1, 0)
                    dsa1 = res[:HEAD]
                    d0 = d1 * w8[hf][s1] + dsa1 * a8[hf][s1] + dy0 * r8[hf][s0]
                    dsa0 = res[HEAD:2 * HEAD] + dsa1 * dots[hf][t // 2] + dy0 * dots[hf][WKV_GROUP // 2 + t // 2]
                    dstate[hf] = d0 * w8[hf][s0] + dsa0 * a8[hf][s0]
                    e_cur[hf] = res[2 * HEAD:8 * HEAD]
                    dv_pend[hf] = jnp.concatenate([_bf(d1 * k8[hf][s1]), _bf(d0 * k8[hf][s0])], axis=0)
                    emit(hf, t, d1, dsa1, dy1, v1, sa1)
                    emit(hf, t - 1, d0, dsa0, dy0, v0, sa0)
                    dv_a, dv_b = read_out(res[8 * HEAD:])
                    if t == WKV_GROUP - 1:
                        dv_after[hf] = (dv_a, dv_b)
                    else:
                        out["dv"][hf] = _put_row(_put_row(out["dv"][hf], dv_a, t + 2, sub), dv_b, t + 1, sub)
            for ref, n in ((dr_ref, "dr"), (dw_ref, "dw"), (dk_ref, "dk"), (dv_ref, "dv"), (da_ref, "da"), (db_ref, "db")):
                ref[rows, :] = jnp.concatenate(out[n], axis=1)
            put_dv(jnp.minimum(g + 1, ng - 1), dv_after)
            return tuple(dstate), tuple(e_cur), tuple(dv_pend)

        top = rows_of(ng - 1)
        dy8t, v8t, a8t = (_halves(ref[top, :]) for ref in (dy_ref, v_ref, a_ref))
        s_t, s_u = _halves(st_ref[WKV_TB - 2]), _halves(st_ref[WKV_TB - 3])
        init = (tuple(_halves(ds_scr[...])),
                tuple(_dot(expand_rows(hf, dy8t, v8t, a8t, WKV_GROUP - 1, s_t, s_u), ones_b, 1, 0) for hf in range(2)),
                tuple(jnp.zeros((2 * HEAD, WKV_HALF), BF16) for _ in range(2)))
        fin, _, dv_pend = lax.fori_loop(0, ng, group, init)
        put_dv(0, [read_out(_dot(dv_pend[hf], ones_b, 1, 0)) for hf in range(2)])
        ds_scr[...] = jnp.concatenate(fin, axis=1)

    vec = pl.BlockSpec((WKV_TB, WIDTH), lambda i: (nb - 1 - i, 0))
    return pl.pallas_call(
        body, name="wkv_bwd", grid=(nb,),
        in_specs=[vec] * 7 + [
            pl.BlockSpec((WKV_TB, HEAD, WIDTH), lambda i: (nb - 1 - i, 0, 0)),
            pl.BlockSpec((1, HEAD, WIDTH), lambda i: (jnp.maximum((nb - 1 - i) * WKV_TB - 1, 0), 0, 0)),
            _full_spec(ones_half)],
        out_specs=[vec] * 6,
        out_shape=[jax.ShapeDtypeStruct((seq, WIDTH), F32)] * 6,
        scratch_shapes=[pltpu.VMEM((HEAD, WIDTH), F32)],
        compiler_params=_params(),
    )(r, w, k, v, a, b, dy, states, states, ones_half)


def ada_fwd(c8, b_ada, gathered):
    cols = 3 * D_MODEL // 4

    def body(c_ref, b_ref, w_ref, o_ref):
        @pl.when(pl.program_id(1) == 0)
        def _():
            o_ref[...] = jnp.broadcast_to(b_ref[...], o_ref.shape)

        o_ref[...] += mm(_silu(c_ref[...]), w_ref[0])

    return pl.pallas_call(
        body, name="ada_fwd", grid=(4, D_MODEL // PACK_ROWS),
        in_specs=[pl.BlockSpec((SUBLANES, PACK_ROWS), lambda s, i: (0, i)),
                  pl.BlockSpec((1, cols), lambda s, i: (0, s)),
                  pl.BlockSpec((1, PACK_ROWS, cols), lambda s, i: (2 * s, 0, i))],
        out_specs=pl.BlockSpec((SUBLANES, cols), lambda s, i: (0, s)),
        out_shape=jax.ShapeDtypeStruct((SUBLANES, 3 * D_MODEL), F32),
        compiler_params=_params(("arbitrary", "arbitrary")),
    )(c8, b_ada, gathered)


def ada_grad_shard(sc_cols, dada_rows):
    n = len(sc_cols)

    def body(*refs):
        d_ref, o_ref = refs[n], refs[n + 1]
        acc = refs[0][...] * d_ref[0:1, :]
        for b in range(1, n):
            acc = acc + refs[b][...] * d_ref[b:b + 1, :]
        o_ref[...] = acc

    return pl.pallas_call(
        body, name="ada_grad_shard",
        out_shape=jax.ShapeDtypeStruct((sc_cols[0].shape[0], dada_rows.shape[1]), F32),
        compiler_params=pltpu.CompilerParams(vmem_limit_bytes=VMEM_LIMIT),
    )(*sc_cols, dada_rows)


def sum_slots(buf, tr):
    n, rows, cols = buf.shape

    def body(b_ref, o_ref):
        acc = b_ref[0].astype(F32)
        for s in range(1, n):
            acc = acc + b_ref[s].astype(F32)
        o_ref[...] = acc

    return pl.pallas_call(
        body, name="sum_slots", grid=(rows // tr,),
        in_specs=[pl.BlockSpec((n, tr, cols), lambda i: (0, i, 0))],
        out_specs=pl.BlockSpec((tr, cols), lambda i: (i, 0)),
        out_shape=jax.ShapeDtypeStruct((rows, cols), F32),
        compiler_params=_params(),
    )(buf)


def adamw_small(gathered, w, m, v):
    n = gathered.shape[0]

    def body(g_ref, w_ref, m_ref, v_ref, go_ref, d_ref, mo_ref, vo_ref):
        g = g_ref[0]
        for s in range(1, n):
            g = g + g_ref[s]
        go_ref[...] = g
        d_ref[...], mo_ref[...], vo_ref[...] = f_adamw(w_ref[...], g, m_ref[...], v_ref[...])

    return pl.pallas_call(
        body, name="adamw_small",
        out_shape=[jax.ShapeDtypeStruct(w.shape, F32)] * 4,
        compiler_params=pltpu.CompilerParams(vmem_limit_bytes=VMEM_LIMIT),
    )(gathered, w, m, v)


def _coords():
    return lax.axis_index("x"), lax.axis_index("y"), lax.axis_index("c")


def _flip(v, bit):
    return 1 - v if bit else v


def _hbm_call(body, name, out_shape, n_sems, *args):
    any_spec = pl.BlockSpec(memory_space=pl.ANY)
    return pl.pallas_call(
        body, name=name, out_shape=out_shape,
        in_specs=[any_spec] * len(args), out_specs=any_spec,
        scratch_shapes=[pltpu.SemaphoreType.DMA((n_sems,)), pltpu.SemaphoreType.DMA((n_sems,)),
                        pltpu.SemaphoreType.DMA],
    )(*args)


def all_gather8(name, block):
    def body(x_ref, out_ref, send_sems, recv_sems, local_sem):
        x, y, c = _coords()
        me, sibling = (x, y, c), (x, y, 1 - c)
        x_nbr, y_nbr, diagonal = (1 - x, y), (x, 1 - y), (1 - x, 1 - y)
        relay_from = (c * x + (1 - c) * (1 - x), c * (1 - y) + (1 - c) * y)
        relay_to = (c * (1 - x) + (1 - c) * x, c * y + (1 - c) * (1 - y))

        def slot(px, py, pc):
            return out_ref.at[4 * px + 2 * py + pc]

        def copy(k, blk, to, src=None):
            return pltpu.make_async_remote_copy(
                src_ref=slot(*blk) if src is None else src, dst_ref=slot(*blk),
                send_sem=send_sems.at[k], recv_sem=recv_sems.at[k], device_id=to, device_id_type=MESH)

        mine = pltpu.make_async_copy(x_ref, slot(*me), local_sem)
        mine.start()
        first = [copy(0, me, sibling, src=x_ref), copy(1, me, (*x_nbr, c), src=x_ref), copy(2, me, (*y_nbr, c), src=x_ref)]
        for cp in first:
            cp.start()
        copy(1, (*x_nbr, c), me).wait_recv()
        copy(2, (*y_nbr, c), me).wait_recv()
        later = [copy(3, (*relay_from, c), (*relay_to, c)), copy(4, (*x_nbr, c), sibling), copy(5, (*y_nbr, c), sibling)]
        for cp in later:
            cp.start()
        copy(3, (*diagonal, c), me).wait_recv()
        last = copy(6, (*diagonal, c), sibling)
        last.start()
        copy(0, sibling, me).wait_recv()
        for k, chip in ((4, x_nbr), (5, y_nbr), (6, diagonal)):
            copy(k, (*chip, 1 - c), me).wait_recv()
        for cp in first + later + [last]:
            cp.wait_send()
        mine.wait()

    return _hbm_call(body, name, jax.ShapeDtypeStruct((N_DEV,) + block.shape, block.dtype), 7, block)


def pair_swap(name, block):
    def body(x_ref, out_ref, send_sems, recv_sems, local_sem):
        x, y, c = _coords()
        cp = pltpu.make_async_remote_copy(
            src_ref=x_ref, dst_ref=out_ref, send_sem=send_sems.at[0], recv_sem=recv_sems.at[0],
            device_id=(x, y, 1 - c), device_id_type=MESH)
        cp.start()
        cp.wait_recv()
        cp.wait_send()

    return _hbm_call(body, name, jax.ShapeDtypeStruct(block.shape, block.dtype), 1, block)


def chip_all_to_all(name, buf):
    def body(x_ref, out_ref, send_sems, recv_sems, local_sem):
        x, y, c = _coords()
        me = 2 * x + y
        mine = pltpu.make_async_copy(x_ref.at[me], out_ref.at[me], local_sem)
        mine.start()
        copies = []
        for k in range(1, 4):
            px, py = _flip(x, k & 2), _flip(y, k & 1)
            copies.append(pltpu.make_async_remote_copy(
                src_ref=x_ref.at[2 * px + py], dst_ref=out_ref.at[me],
                send_sem=send_sems.at[k - 1], recv_sem=recv_sems.at[k - 1],
                device_id=(px, py, c), device_id_type=MESH))
        for cp in copies:
            cp.start()
        for cp in copies:
            cp.wait_recv()
        for cp in copies:
            cp.wait_send()
        mine.wait()

    return _hbm_call(body, name, jax.ShapeDtypeStruct(buf.shape, buf.dtype), 3, buf)


def _col_blocks(a, cols):
    a = jnp.pad(a, ((0, 0), (0, cols - a.shape[1])))
    return [a[i * PACK_ROWS:(i + 1) * PACK_ROWS] for i in range(a.shape[0] // PACK_ROWS)]


def _pack_shard(sh, dtype, with_ada=True):
    lora = jnp.concatenate([sh['w_decay_up'], sh['w_iclr_up']], axis=1)
    misc = jnp.concatenate([sh['w_ukv'], lora, jnp.zeros((LORA, 2 * LANES), lora.dtype)], axis=0)
    blocks = ((_col_blocks(sh['w_ada'], 768) if with_ada else [])
              + _col_blocks(sh['w_in'], 1408) + _col_blocks(sh['w_proj_a'], 256)
              + _col_blocks(sh['w_proj_b'], 256) + [sh['w_out']] + _col_blocks(sh['w_uq'], 256) + [misc])
    return jnp.concatenate([b.astype(dtype) for b in blocks], axis=1)


def _unpack_shard(p, with_ada=True):
    o = [0]

    def take(n_blocks, cols, used):
        blocks = [p[:, o[0] + i * cols:o[0] + (i + 1) * cols] for i in range(n_blocks)]
        o[0] += n_blocks * cols
        return jnp.concatenate(blocks, axis=0)[:, :used]

    out = {'w_ada': take(4, 768, 768)} if with_ada else {}
    out.update({'w_in': take(4, 1408, 1288), 'w_proj_a': take(2, 256, 256),
                'w_proj_b': take(2, 256, 256), 'w_out': take(1, 1024, 1024), 'w_uq': take(1, 256, 192)})
    misc = take(1, 256, 256)
    out['w_ukv'] = misc[:2 * LORA]
    out['w_decay_up'] = misc[2 * LORA:3 * LORA, :LANES]
    out['w_iclr_up'] = misc[2 * LORA:3 * LORA, LANES:]
    return out


def _pack_small(parts):
    flat = jnp.concatenate([p.reshape(-1) for p in parts])
    return jnp.pad(flat, (0, SMALL_ROWS * LANES - flat.shape[0])).reshape(SMALL_ROWS, LANES)


def _unpack_small(packed):
    flat, out, o = packed.reshape(-1), {}, 0
    for name, n in _SMALL:
        out[name] = flat[o:o + n]
        o += n
    return out


def _pad_heads_cols(w, used, left):
    k = w.shape[0]
    return jnp.pad(w.reshape(k, HEADS, used), ((0, 0), (0, 0), (left, LANES - used - left))).reshape(k, HEADS * LANES)


def _unpad_heads_cols(w, used, left):
    k = w.shape[0]
    return w.reshape(k, HEADS, LANES)[:, :, left:left + used].reshape(k, HEADS * used)


def kernel(x, c, positions, w_ada, b_ada, w_in, q_norm_g, w_uq, kv_norm_g, w_ukv, mu_rwkv, w0, w_decay_up, a0, w_iclr_up, k_k, k_a, r_k, gn_g, gn_b, w_proj_a, w_proj_b, w_out, post_g, post_b, loss_target, m_w_ada, m_b_ada, m_w_in, m_q_norm_g, m_w_uq, m_kv_norm_g, m_w_ukv, m_mu_rwkv, m_w0, m_w_decay_up, m_a0, m_w_iclr_up, m_k_k, m_k_a, m_r_k, m_gn_g, m_gn_b, m_w_proj_a, m_w_proj_b, m_w_out, m_post_g, m_post_b, v_w_ada, v_b_ada, v_w_in, v_q_norm_g, v_w_uq, v_kv_norm_g, v_w_ukv, v_mu_rwkv, v_w0, v_w_decay_up, v_a0, v_w_iclr_up, v_k_k, v_k_a, v_r_k, v_gn_g, v_gn_b, v_w_proj_a, v_w_proj_b, v_w_out, v_post_g, v_post_b):
    given = dict(locals())
    seq = x.shape[1]
    my_c = lax.axis_index("c")

    shard_names = [n for n, _, _ in _SHARDED]
    w_pack = _pack_shard({n: given[n][0] for n in shard_names}, BF16)
    my_half = lax.dynamic_slice_in_dim(w_pack, my_c * HALF_COLS, HALF_COLS, 1)
    gathered = all_gather8("gather_weights", my_half)
    shards = [_unpack_shard(jnp.concatenate([gathered[2 * s], gathered[2 * s + 1]], axis=1)) for s in range(4)]
    full = {n: jnp.concatenate([sh[n] for sh in shards], axis=ax) for n, _, ax in _SHARDED}

    wi = full['w_in']
    zcol = lambda n: jnp.zeros((D_MODEL, n), BF16)
    w_g1 = jnp.concatenate([wi[:, :384], zcol(HEAD), wi[:, 384:416], zcol(LANES - MLA_QK),
                            _pad_heads_cols(wi[:, 416:928], HEAD, HEAD)], axis=1)
    w_g2 = wi[:, 928:3104]
    w_g3 = wi[:, 3104:5152]
    w_uq_p = _pad_heads_cols(full['w_uq'], MLA_QK, 0)
    w_pa_p = jnp.pad(full['w_proj_a'].reshape(HEADS, HEAD, D_MODEL), ((0, 0), (HEAD, 0), (0, 0))).reshape(HEADS * LANES, D_MODEL)
    zl = jnp.zeros((LORA, WIDTH), BF16)
    w_lora = jnp.concatenate([jnp.concatenate([full['w_decay_up'], zl], 1),
                              jnp.concatenate([zl, full['w_iclr_up']], 1)], 0)

    hd = np.arange(WKV_HALF) // HEAD
    ones_half = jnp.asarray(hd[:, None] == hd[None, :], BF16)
    perm_np = np.zeros((LANES, LANES), np.float32)
    for d in range(MLA_ROPE // 2):
        perm_np[HEAD + 16 + d, HEAD + d] = -1.0
        perm_np[HEAD + d, HEAD + 16 + d] = 1.0
    perm = jnp.asarray(perm_np, BF16)
    inv = ROPE_THETA ** (-jnp.arange(0, MLA_ROPE, 2, dtype=F32) / MLA_ROPE)
    ang = positions[0].astype(F32)[:, None] * inv
    cos_a, sin_a = jnp.cos(ang), jnp.sin(ang)
    cs = jnp.concatenate([jnp.ones((seq, HEAD), F32), cos_a, cos_a, jnp.zeros((seq, LANES - MLA_QK), F32),
                          jnp.zeros((seq, HEAD), F32), sin_a, sin_a, jnp.zeros((seq, LANES - MLA_QK), F32)], axis=1)

    x2, tgt = x[0], loss_target[0]
    r_k2 = r_k.reshape(1, WIDTH)

    c8 = jnp.broadcast_to(c, (SUBLANES, D_MODEL))
    ada = ada_fwd(c8, b_ada, gathered)[:1]
    shift, scale, gate = ada[:, :D_MODEL], ada[:, D_MODEL:2 * D_MODEL], ada[:, 2 * D_MODEL:]

    f_in1, f_in2, f_in3 = _make_f_in((512, 1024)), _make_f_in((SHIFT_W, WIDTH)), _make_f_in((1024, 1024))
    tr = min(256, seq)
    p_mla, gpa, p_rwkv, gpb, ma, mb = row_fwd(
        "in_fwd", _make_f_in((512, 1024), (SHIFT_W, WIDTH), (1024, 1024)), [x2], [shift, scale, w_g1, w_g2, w_g3], [],
        [512, 1024, SHIFT_W, WIDTH, 1024, 1024], tr)

    mla_par = [q_norm_g, kv_norm_g, w_uq_p, full['w_ukv']]
    q_f, kv_f, kpe = row_fwd("mla_pre_fwd", f_mla_pre, [p_mla, cs], mla_par, [perm], [1024, 1024, LANES], tr)
    ya, lse = attn_fwd(q_f, kv_f, kpe)

    pre_par = [w0, a0, k_k, k_a, w_lora]
    u, rr, wd, k2, vv, an, bb = shift_stage_fwd("rwkv_pre_fwd", f_rwkv_pre, p_rwkv, mu_rwkv, pre_par, [ones_half],
                                                [WIDTH] * 6, tr)
    y_wkv, states = wkv_fwd(rr, wd, k2, vv, an, bb, ones_half)
    post_b_par = [r_k2, gn_g, gn_b]
    yb, = row_fwd("rwkv_post_fwd", f_rwkv_post, [y_wkv, rr, k2, vv], post_b_par, [ones_half], [WIDTH], tr)

    merge_rows, merge_par = [ya, gpa, yb, gpb, ma, mb], [w_pa_p, full['w_proj_b']]
    merged, = row_fwd("merge_fwd", f_merge, merge_rows, merge_par, [], [D_MODEL], tr)
    loss_rows, loss_par = [merged, x2, tgt], [gate, post_g, post_b, full['w_out']]
    lrows, = row_fwd("loss_fwd", f_loss, loss_rows, loss_par, [], [LANES], tr)
    loss = lax.psum(jnp.sum(lrows[:, 0]), ("x", "y", "c"))

    dl = jnp.broadcast_to((jnp.arange(LANES) == 0).astype(F32), (seq, LANES))
    (dmerged, dx_res), (dgate, dpost_g, dpost_b, dw_out) = row_bwd("loss_bwd", f_loss, loss_rows, 2, loss_par, [], [dl], tr)
    (dya, dgpa, dyb, dgpb, dma, dmb), (dw_pa_p, dw_pb) = row_bwd(
        "merge_bwd", f_merge, merge_rows, 6, merge_par, [], [dmerged], tr)

    (dy_wkv, dr1, dk1, dv1), (dr_k, dgn_g, dgn_b) = row_bwd(
        "rwkv_post_bwd", f_rwkv_post, [y_wkv, rr, k2, vv], 4, post_b_par, [ones_half], [dyb], tr)
    dr2, dwd, dk2, dv2, dan, dbb = wkv_bwd(rr, wd, k2, vv, an, bb, dy_wkv, states, ones_half)
    dp_rwkv, dmu, (dw0, da0, dk_k, dk_a, dw_lora) = shift_stage_bwd(
        "rwkv_pre_bwd", f_rwkv_pre, p_rwkv, u, mu_rwkv, pre_par, [ones_half],
        [(dr1, dr2), dwd, (dk1, dk2), (dv1, dv2), dan, dbb], tr)

    dq_f, dkv_f, dkpe = attn_bwd(q_f, kv_f, kpe, ya, lse, dya)
    (dp_mla,), (dqg, dkvg, dw_uq_p, dw_ukv) = row_bwd(
        "mla_pre_bwd", f_mla_pre, [p_mla, cs], 1, mla_par, [perm], [dq_f, dkv_f, dkpe], tr)

    (dx1,), (dsh1, dsc1, dw_g1) = row_bwd("in1_bwd", f_in1, [x2], 1, [shift, scale, w_g1], [], [dp_mla, dgpa], tr, [dx_res])
    (dx2,), (dsh2, dsc2, dw_g2) = row_bwd("in2_bwd", f_in2, [x2], 1, [shift, scale, w_g2], [], [dp_rwkv, dgpb], tr, [dx1])
    (dx3,), (dsh3, dsc3, dw_g3) = row_bwd("in3_bwd", f_in3, [x2], 1, [shift, scale, w_g3], [], [dma, dmb], tr, [dx2])
    grad_x = dx3[None]

    dada = jnp.concatenate([dsh1 + dsh2 + dsh3, dsc1 + dsc2 + dsc3, dgate], axis=1)
    local = {
        'w_in': jnp.concatenate([dw_g1[:, :384], dw_g1[:, 448:480], _unpad_heads_cols(dw_g1[:, 512:], HEAD, HEAD),
                                 dw_g2, dw_g3], axis=1),
        'w_uq': _unpad_heads_cols(dw_uq_p, MLA_QK, 0),
        'w_ukv': dw_ukv,
        'w_decay_up': dw_lora[:LORA, :WIDTH],
        'w_iclr_up': dw_lora[LORA:, WIDTH:],
        'w_proj_a': dw_pa_p.reshape(HEADS, LANES, D_MODEL)[:, HEAD:].reshape(WIDTH, D_MODEL),
        'w_proj_b': dw_pb,
        'w_out': dw_out,
    }
    small_local = {'b_ada': dada, 'q_norm_g': dqg, 'kv_norm_g': dkvg, 'mu_rwkv': dmu, 'w0': dw0, 'a0': da0,
                   'k_k': dk_k, 'k_a': dk_a, 'r_k': dr_k, 'gn_g': dgn_g, 'gn_b': dgn_b,
                   'post_g': dpost_g, 'post_b': dpost_b}

    def shard_of(g, axis, s):
        n = g.shape[axis] // 4
        return lax.slice_in_dim(g, s * n, (s + 1) * n, axis=axis)

    packed = jnp.stack([_pack_shard({n: shard_of(local[n], ax, s) for n, _, ax in _SHARDED if n != 'w_ada'}, F32, False)
                        for s in range(4)])
    keep = lax.dynamic_slice_in_dim(packed, my_c * GRAD_HALF, GRAD_HALF, 2).reshape(4 * PACK_ROWS, GRAD_HALF)
    give = lax.dynamic_slice_in_dim(packed, (1 - my_c) * GRAD_HALF, GRAD_HALF, 2).reshape(4 * PACK_ROWS, GRAD_HALF)
    pair_sum, = row_fwd("pair_sum", lambda p, q: (p + q,), [keep, pair_swap("swap_halves", give)], [], [],
                        [GRAD_HALF], PACK_ROWS // 2, BF16)
    received = chip_all_to_all("exchange_grads", pair_sum.reshape(4, PACK_ROWS, GRAD_HALF))
    my_sum = sum_slots(received, PACK_ROWS // 2)
    other_sum = pair_swap("swap_sums", my_sum)
    halves = [jnp.where(my_c == 0, my_sum, other_sum), jnp.where(my_c == 0, other_sum, my_sum)]
    g_shard = _unpack_shard(jnp.concatenate(halves, axis=1), False)

    small_pack = lambda d, extra=(): _pack_small([d[n] for n, _ in _SMALL] + list(extra))
    small_all = all_gather8("gather_small", small_pack(small_local, [c * jax.nn.sigmoid(c)]))
    sc_all = small_all[:, SMALL_USED:SMALL_USED + D_MODEL // LANES].reshape(N_DEV, D_MODEL)
    dada_all = small_all[:, :3 * D_MODEL // LANES].reshape(N_DEV, 3 * D_MODEL)
    my_cols = lax.dynamic_slice_in_dim(dada_all, (2 * lax.axis_index("x") + lax.axis_index("y")) * 768, 768, 1)
    g_shard['w_ada'] = ada_grad_shard([sc_all[b].reshape(D_MODEL, 1) for b in range(N_DEV)], my_cols)

    big = [{}, {}, {}, {}]
    for n in shard_names:
        w2, m2, v2 = given[n][0], given['m_' + n][0], given['v_' + n][0]
        cols = w2.shape[1]
        outs = row_fwd("adamw_" + n, f_adamw, [w2, g_shard[n], m2, v2], [], [], [cols] * 3, min(256, w2.shape[0]))
        for dst, val in zip(big, (g_shard[n], *outs)):
            dst[n] = val

    small_out = adamw_small(small_all, small_pack({n: given[n] for n, _ in _SMALL}),
                            small_pack({n: given['m_' + n] for n, _ in _SMALL}),
                            small_pack({n: given['v_' + n] for n, _ in _SMALL}))

    results = []
    for big_k, packed_small in zip(big, small_out):
        small = _unpack_small(packed_small)
        results.append([(big_k[n] if n in big_k else small[n]).reshape(given[n].shape) for n in _WEIGHTS])
    return (loss, grad_x, *results[0], *results[1], *results[2], *results[3])
```

```python
from typing import NamedTuple

import numpy as np
import jax
import jax.numpy as jnp
from jax import lax
from jax.experimental import pallas as pl
from jax.experimental.pallas import tpu as pltpu

F32 = jnp.float32
BF16 = jnp.bfloat16

D_MODEL = 1024
LN_EPS = 1e-5
RMS_EPS = 1e-6
GN_EPS = 64e-5
HEADS = 8
HEAD = 64
MLA_ROPE = 32
MLA_QK = HEAD + MLA_ROPE
ROPE_THETA = 10000.0
WIDTH = HEADS * HEAD
LORA = 64
SHIFT_W = 3 * WIDTH + 2 * LORA
CHUNK = 64
ALPHA = 2.0 ** 0.25

ADAM_LR, ADAM_B1, ADAM_B2, ADAM_EPS, ADAM_WD, ADAM_STEP = 0.001, 0.9, 0.999, 1e-08, 0.01, 10

LANES = 128
SUBLANES = 8
VMEM_LIMIT = 56 * 1024 * 1024
N_DEV = 8
MESH = pl.DeviceIdType.MESH
NEG = -1e30

_WEIGHTS = ['w_ada', 'b_ada', 'w_in', 'q_norm_g', 'w_uq', 'kv_norm_g', 'w_ukv', 'mu_rwkv', 'w0',
            'w_decay_up', 'a0', 'w_iclr_up', 'k_k', 'k_a', 'r_k', 'gn_g', 'gn_b', 'w_proj_a',
            'w_proj_b', 'w_out', 'post_g', 'post_b']
_SHARDED = [('w_ada', (1024, 3072), 1), ('w_in', (1024, 5152), 1), ('w_uq', (256, 768), 1),
            ('w_ukv', (128, 1024), 1), ('w_decay_up', (64, 512), 1), ('w_iclr_up', (64, 512), 1),
            ('w_proj_a', (512, 1024), 1), ('w_proj_b', (512, 1024), 1), ('w_out', (1024, 1024), 0)]
_SMALL = [('b_ada', 3072), ('q_norm_g', 256), ('kv_norm_g', 128), ('mu_rwkv', 1664), ('w0', 512),
          ('a0', 512), ('k_k', 512), ('k_a', 512), ('r_k', 512), ('gn_g', 512), ('gn_b', 512),
          ('post_g', 1024), ('post_b', 1024)]
PACK_ROWS = 256
PACK_COLS = 11264
HALF_COLS = PACK_COLS // 2
ADA_COLS = 4 * 768
GRAD_HALF = (PACK_COLS - ADA_COLS) // 2
SMALL_USED = 84
SMALL_ROWS = 96


def _bf(x):
    return x.astype(BF16)


def _dot(a, b, ca, cb):
    return lax.dot_general(a, b, (((ca,), (cb,)), ((), ())), preferred_element_type=F32)


class Weight(NamedTuple):
    value: jax.Array
    grad: jax.Array


@jax.custom_vjp
def _mm(a, w, w_grad):
    return _dot(_bf(a), _bf(w), 1, 0)


def _mm_fwd(a, w, w_grad):
    return _mm(a, w, w_grad), (a, w)


def _mm_bwd(res, g):
    a, w = res
    gb = _bf(g)
    return _dot(gb, _bf(w), 1, 1), jnp.zeros_like(w), _dot(_bf(a), gb, 0, 0)


_mm.defvjp(_mm_fwd, _mm_bwd)


def mm(a, w):
    if isinstance(w, Weight):
        return _mm(a, w.value, w.grad)
    return _dot(_bf(a), _bf(w), 1, 0)


def _split3(x):
    hi = _bf(x)
    r1 = x - hi.astype(F32)
    mid = _bf(r1)
    lo = _bf(r1 - mid.astype(F32))
    return hi, mid, lo


def _exact_dot(x, m, cm):
    hi, mid, lo = _split3(x)
    return _dot(hi, m, 1, cm) + _dot(mid, m, 1, cm) + _dot(lo, m, 1, cm)


def _head_sums(x, ones_blocks):
    n = ones_blocks.shape[0]
    parts = [_exact_dot(x[:, o:o + n], ones_blocks, 0) for o in range(0, x.shape[1], n)]
    return parts[0] if len(parts) == 1 else jnp.concatenate(parts, axis=1)


@jax.custom_vjp
def segsum(x, ones_blocks):
    return _head_sums(x, ones_blocks)


def _segsum_fwd(x, ones_blocks):
    return segsum(x, ones_blocks), ones_blocks


def _segsum_bwd(ones_blocks, g):
    return _head_sums(g, ones_blocks), jnp.zeros_like(ones_blocks)


segsum.defvjp(_segsum_fwd, _segsum_bwd)


@jax.custom_vjp
def lane_perm(x, perm):
    return _exact_dot(x, perm, 0)


def _lane_perm_fwd(x, perm):
    return lane_perm(x, perm), perm


def _lane_perm_bwd(perm, g):
    return _exact_dot(g, perm, 1), jnp.zeros_like(perm)


lane_perm.defvjp(_lane_perm_fwd, _lane_perm_bwd)


def _silu(z):
    return z * jax.nn.sigmoid(z)


def _softplus(z):
    return jnp.maximum(z, 0.0) + jnp.log(1.0 + jnp.exp(-jnp.abs(z)))


def _layer_norm(x):
    xc = x - jnp.mean(x, -1, keepdims=True)
    return xc * lax.rsqrt(jnp.mean(xc * xc, -1, keepdims=True) + LN_EPS)


def _rope(t, cos_t, sin_t, perm):
    outs = []
    for h in range(t.shape[1] // LANES):
        th = t[:, h * LANES:(h + 1) * LANES]
        outs.append(th * cos_t + lane_perm(th, perm) * sin_t)
    return outs[0] if len(outs) == 1 else jnp.concatenate(outs, axis=1)


def _make_f_in(*split_groups):
    def f_in(x, shift, scale, *weights):
        h = _layer_norm(x) * (1.0 + scale) + shift
        outs = []
        for w, splits in zip(weights, split_groups):
            p, o = mm(h, w), 0
            for s in splits:
                outs.append(p[:, o:o + s])
                o += s
        return tuple(outs)
    return f_in


def f_mla_pre(p, cs, qg, kvg, w_uq, w_ukv, perm):
    q_c, kv_c, k_r = p[:, :256], p[:, 256:384], p[:, 384:512]
    cos_t, sin_t = cs[:, :LANES], cs[:, LANES:]
    qn = q_c * lax.rsqrt(jnp.mean(q_c * q_c, -1, keepdims=True) + RMS_EPS) * qg
    kvn = kv_c * lax.rsqrt(jnp.mean(kv_c * kv_c, -1, keepdims=True) + RMS_EPS) * kvg
    q = _rope(mm(qn, w_uq), cos_t, sin_t, perm)
    kv = mm(kvn, w_ukv)
    return q, kv, _rope(k_r, cos_t, sin_t, perm)


def f_rwkv_pre(u, w0, a0, k_k, k_a, w_lora, ones_blocks):
    r, k, v, lo = u[:, :WIDTH], u[:, WIDTH:2 * WIDTH], u[:, 2 * WIDTH:3 * WIDTH], u[:, 3 * WIDTH:]
    lane = lax.broadcasted_iota(jnp.int32, lo.shape, 1)
    dl = mm(jnp.where(lane < LORA, jnp.tanh(lo), lo), w_lora)
    w_log = -_softplus(-(w0 + dl[:, :WIDTH])) - 0.5
    decay = jnp.exp(-jnp.exp(w_log))
    a = jax.nn.sigmoid(a0 + dl[:, WIDTH:])
    kk = k * k_k
    kk = kk / jnp.maximum(jnp.sqrt(segsum(kk * kk, ones_blocks)), 1e-12)
    k2 = k * (1.0 + (a - 1.0) * k_a)
    return r, decay, k2, v, -kk, kk * a


def f_rwkv_post(y, r, k2, v, r_k, gn_g, gn_b, ones_blocks):
    yc = y - segsum(y, ones_blocks) * (1.0 / HEAD)
    yn = yc * lax.rsqrt(segsum(yc * yc, ones_blocks) * (1.0 / HEAD) + GN_EPS)
    return (yn * gn_g + gn_b + segsum(r * k2 * r_k, ones_blocks) * v,)


def f_merge(ya, gpa, yb, gpb, ma, mb, w_pa, w_pb):
    pa = mm(ya * _silu(gpa), w_pa)
    pb = mm(yb * _silu(gpb), w_pb)
    return (jax.nn.sigmoid(ma) * pa + jax.nn.sigmoid(mb) * pb,)


def f_loss(merged, x, tgt, gate, post_g, post_b, w_out):
    z = ALPHA * x + (1.0 + gate) * mm(merged, w_out)
    err = _layer_norm(z) * post_g + post_b - tgt
    lrow = 0.5 * jnp.mean(err * err, -1, keepdims=True)
    return (jnp.broadcast_to(lrow, (lrow.shape[0], LANES)),)


def f_adamw(w, g, m, v):
    m2 = ADAM_B1 * m + (1.0 - ADAM_B1) * g
    v2 = ADAM_B2 * v + (1.0 - ADAM_B2) * jnp.square(g)
    m_hat = m2 / (1.0 - ADAM_B1 ** ADAM_STEP)
    v_hat = v2 / (1.0 - ADAM_B2 ** ADAM_STEP)
    return -ADAM_LR * (m_hat / (jnp.sqrt(v_hat) + ADAM_EPS) + ADAM_WD * w), m2, v2


def _params(sem=("arbitrary",)):
    return pltpu.CompilerParams(dimension_semantics=sem, vmem_limit_bytes=VMEM_LIMIT)


def _row_spec(tr, a):
    return pl.BlockSpec((tr, a.shape[1]), lambda i: (i, 0))


def _full_spec(a):
    return pl.BlockSpec(a.shape, lambda i: (0,) * a.ndim)


def row_fwd(name, f, rows, params, consts, out_widths, tr, out_dtype=F32):
    n_rows = rows[0].shape[0]
    nr, npar, ncon = len(rows), len(params), len(consts)

    def body(*refs):
        rv = [r[...] for r in refs[:nr]]
        pv = [r[...] for r in refs[nr:nr + npar]]
        cv = [r[...] for r in refs[nr + npar:nr + npar + ncon]]
        outs = f(*rv, *pv, *cv)
        for o_ref, o in zip(refs[nr + npar + ncon:], outs):
            o_ref[...] = o.astype(o_ref.dtype)

    return pl.pallas_call(
        body, name=name, grid=(n_rows // tr,),
        in_specs=[_row_spec(tr, a) for a in rows] + [_full_spec(a) for a in list(params) + list(consts)],
        out_specs=[pl.BlockSpec((tr, w), lambda i: (i, 0)) for w in out_widths],
        out_shape=[jax.ShapeDtypeStruct((n_rows, w), out_dtype) for w in out_widths],
        compiler_params=_params(),
    )(*rows, *params, *consts)


def row_bwd(name, f, rows, n_diff, params, consts, douts, tr, add_rows=None):
    n_rows = rows[0].shape[0]
    douts = [d if isinstance(d, (tuple, list)) else (d,) for d in douts]
    counts = [len(d) for d in douts]
    flat_d = [a for d in douts for a in d]
    add_rows = add_rows or [None] * n_diff
    adds = [a for a in add_rows if a is not None]
    nr, npar, ncon, nd, na = len(rows), len(params), len(consts), len(flat_d), len(adds)

    def body(*refs):
        o = 0
        rv = [r[...] for r in refs[o:o + nr]]; o += nr
        pv = [Weight(r[...], jnp.zeros(r.shape, F32)) if r.dtype == BF16 else r[...] for r in refs[o:o + npar]]
        o += npar
        cv = [r[...] for r in refs[o:o + ncon]]; o += ncon
        dv = []
        for cnt in counts:
            s = refs[o][...]
            for e in range(1, cnt):
                s = s + refs[o + e][...]
            dv.append(s)
            o += cnt
        add_v = [r[...] for r in refs[o:o + na]]; o += na
        drow_refs = refs[o:o + n_diff]; o += n_diff
        dpar_refs = refs[o:o + npar]

        def g(*args):
            return tuple(f(*args[:n_diff], *rv[n_diff:], *args[n_diff:], *cv))

        _, vjp = jax.vjp(g, *rv[:n_diff], *pv)
        grads = vjp(tuple(dv))
        ai = 0
        for j, (r, gr) in enumerate(zip(drow_refs, grads[:n_diff])):
            if add_rows[j] is not None:
                gr = gr + add_v[ai]
                ai += 1
            r[...] = gr

        @pl.when(pl.program_id(0) == 0)
        def _():
            for r in dpar_refs:
                r[...] = jnp.zeros_like(r)

        for r, gr in zip(dpar_refs, grads[n_diff:]):
            r[...] += gr.grad if isinstance(gr, Weight) else gr

    outs = pl.pallas_call(
        body, name=name, grid=(n_rows // tr,),
        in_specs=([_row_spec(tr, a) for a in rows] + [_full_spec(a) for a in list(params) + list(consts)]
                  + [_row_spec(tr, a) for a in flat_d + adds]),
        out_specs=[_row_spec(tr, a) for a in rows[:n_diff]] + [_full_spec(a) for a in params],
        out_shape=([jax.ShapeDtypeStruct(a.shape, F32) for a in rows[:n_diff]]
                   + [jax.ShapeDtypeStruct(a.shape, F32) for a in params]),
        compiler_params=_params(),
    )(*rows, *params, *consts, *flat_d, *adds)
    return outs[:n_diff], outs[n_diff:]


def shift_stage_fwd(name, f, p, mu, params, consts, out_widths, tr):
    n_rows, w = p.shape
    npar, ncon = len(params), len(consts)

    def body(*refs):
        p_ref, mu_ref = refs[:2]
        pv = [r[...] for r in refs[2:2 + npar]]
        cv = [r[...] for r in refs[2 + npar:2 + npar + ncon]]
        u_ref, out_refs, carry = refs[2 + npar + ncon], refs[3 + npar + ncon:-1], refs[-1]

        @pl.when(pl.program_id(0) == 0)
        def _():
            carry[...] = jnp.zeros_like(carry)

        x = p_ref[...]
        rolled = pltpu.roll(x, 1, 0)
        head = pltpu.roll(carry[...], 1, 0)
        fixed = jnp.concatenate([head, rolled[SUBLANES:]], axis=0)
        row = lax.broadcasted_iota(jnp.int32, x.shape, 0)
        prev = jnp.where(row == 0, fixed, rolled)
        u = x + (prev - x) * mu_ref[...]
        u_ref[...] = u
        carry[...] = x[tr - SUBLANES:]
        for o_ref, o in zip(out_refs, f(u, *pv, *cv)):
            o_ref[...] = o

    return pl.pallas_call(
        body, name=name, grid=(n_rows // tr,),
        in_specs=[_row_spec(tr, p), _full_spec(mu)] + [_full_spec(a) for a in list(params) + list(consts)],
        out_specs=[_row_spec(tr, p)] + [pl.BlockSpec((tr, ow), lambda i: (i, 0)) for ow in out_widths],
        out_shape=[jax.ShapeDtypeStruct(p.shape, F32)] + [jax.ShapeDtypeStruct((n_rows, ow), F32) for ow in out_widths],
        scratch_shapes=[pltpu.VMEM((SUBLANES, w), F32)],
        compiler_params=_params(),
    )(p, mu, *params, *consts)


def shift_stage_bwd(name, f, p, u, mu, params, consts, douts, tr):
    n_rows, w = p.shape
    nb = n_rows // tr
    douts = [d if isinstance(d, (tuple, list)) else (d,) for d in douts]
    counts = [len(d) for d in douts]
    flat_d = [a for d in douts for a in d]
    npar, ncon, nd = len(params), len(consts), len(flat_d)

    def body(*refs):
        p_ref, u_ref, mu_ref = refs[:3]
        o = 3
        pv = [Weight(r[...], jnp.zeros(r.shape, F32)) if r.dtype == BF16 else r[...] for r in refs[o:o + npar]]
        o += npar
        cv = [r[...] for r in refs[o:o + ncon]]; o += ncon
        dv = []
        for cnt in counts:
            s = refs[o][...]
            for e in range(1, cnt):
                s = s + refs[o + e][...]
            dv.append(s)
            o += cnt
        dp_ref, dmu_ref = refs[o], refs[o + 1]
        dpar_refs, carry = refs[o + 2:o + 2 + npar], refs[-1]

        @pl.when(pl.program_id(0) == 0)
        def _():
            carry[...] = jnp.zeros_like(carry)
            dmu_ref[...] = jnp.zeros_like(dmu_ref)
            for r in dpar_refs:
                r[...] = jnp.zeros_like(r)

        _, vjp = jax.vjp(lambda uu, *pp: tuple(f(uu, *pp, *cv)), u_ref[...], *pv)
        grads = vjp(tuple(dv))
        for r, gr in zip(dpar_refs, grads[1:]):
            r[...] += gr.grad if isinstance(gr, Weight) else gr

        d = grads[0]
        rolled = pltpu.roll(d, tr - 1, 0)
        tail = pltpu.roll(carry[...], SUBLANES - 1, 0)
        fixed = jnp.concatenate([rolled[:tr - SUBLANES], tail], axis=0)
        row = lax.broadcasted_iota(jnp.int32, d.shape, 0)
        nxt = jnp.where(row == tr - 1, fixed, rolled)
        mu_v = mu_ref[...]
        dp_ref[...] = d * (1.0 - mu_v) + nxt * mu_v
        dmu_ref[...] += jnp.sum(p_ref[...] * (nxt - d), axis=0, keepdims=True)
        carry[...] = d[:SUBLANES]

    rev = lambda i: (nb - 1 - i, 0)
    rows_rev = lambda a: pl.BlockSpec((tr, a.shape[1]), rev)
    outs = pl.pallas_call(
        body, name=name, grid=(nb,),
        in_specs=([rows_rev(p), rows_rev(u), _full_spec(mu)] + [_full_spec(a) for a in list(params) + list(consts)]
                  + [rows_rev(a) for a in flat_d]),
        out_specs=[rows_rev(p), _full_spec(mu)] + [_full_spec(a) for a in params],
        out_shape=([jax.ShapeDtypeStruct(p.shape, F32), jax.ShapeDtypeStruct(mu.shape, F32)]
                   + [jax.ShapeDtypeStruct(a.shape, F32) for a in params]),
        scratch_shapes=[pltpu.VMEM((SUBLANES, w), F32)],
        compiler_params=_params(),
    )(p, u, mu, *params, *consts, *flat_d)
    return outs[0], outs[1], outs[2:]


ATT_T = 256


def _att_rows(j):
    return pl.ds(pl.multiple_of(j * ATT_T, ATT_T), ATT_T)


def _att_prep(kv_ref, kpe_ref, kf_scr, vf_scr, n_blocks):
    lane = lax.broadcasted_iota(jnp.int32, (ATT_T, LANES), 1)

    def prep(j, _):
        rows = _att_rows(j)
        kv = kv_ref[rows, :]
        kf_scr[rows, :] = _bf(jnp.where(lane < HEAD, kv, kpe_ref[rows, :]))
        vf_scr[rows, :] = _bf(jnp.where(lane >= HEAD, kv, 0.0))
        return 0

    lax.fori_loop(0, n_blocks, prep, 0)


def _att_diag_mask():
    shift = CHUNK.bit_length() - 1
    qc = jnp.right_shift(lax.broadcasted_iota(jnp.int32, (ATT_T, ATT_T), 0), shift)
    kc = jnp.right_shift(lax.broadcasted_iota(jnp.int32, (ATT_T, ATT_T), 1), shift)
    return kc <= qc


def _wide(x):
    return jnp.concatenate([x] * (ATT_T // LANES), axis=1)


def attn_fwd(q, kv, kpe):
    seq = q.shape[0]
    nb = seq // ATT_T
    assert seq % (2 * ATT_T) == 0, "blocks are taken two per trip"
    scale = MLA_QK ** -0.5

    def body(q_ref, kv_ref, kpe_ref, o_ref, lse_ref, kf_scr, vf_scr):
        _att_prep(kv_ref, kpe_ref, kf_scr, vf_scr, nb)
        mask = _att_diag_mask()

        def scores(qb, kj):
            return _dot(qb, kf_scr[_att_rows(kj), :], 1, 1)

        def update(s, kj, carry, masked):
            m, l, acc = carry
            if masked:
                s = jnp.where(mask, s, NEG)
            m_new = jnp.maximum(m, jnp.broadcast_to(jnp.max(s, -1, keepdims=True), m.shape))
            alpha = jnp.exp(m - m_new)
            p = jnp.exp(s - _wide(m_new))
            l = alpha * l + jnp.broadcast_to(jnp.sum(p, -1, keepdims=True), l.shape)
            acc = alpha * acc + _dot(_bf(p), vf_scr[_att_rows(kj), :], 1, 0)
            return m_new, l, acc

        def finish(rows, carry):
            m, l, acc = carry
            o_ref[rows, :] = acc / l
            lse_ref[rows, :] = m + jnp.log(l)

        def q_pair(qp, _):
            rows_a, rows_b = _att_rows(2 * qp), _att_rows(2 * qp + 1)
            qa, qb = _bf(q_ref[rows_a, :] * scale), _bf(q_ref[rows_b, :] * scale)
            init = (jnp.full((ATT_T, LANES), NEG, F32), jnp.zeros((ATT_T, LANES), F32),
                    jnp.zeros((ATT_T, LANES), F32))

            def trip(kj, c):
                ca, cb, sa, sb = c
                sa_next, sb_next = scores(qa, kj + 1), scores(qb, kj + 1)
                return update(sa, kj, ca, False), update(sb, kj, cb, False), sa_next, sb_next

            ca, cb, sa, sb = lax.fori_loop(0, 2 * qp, trip, (init, init, scores(qa, 0), scores(qb, 0)))
            sb_last = scores(qb, 2 * qp + 1)
            ca = update(sa, 2 * qp, ca, True)
            cb = update(sb_last, 2 * qp + 1, update(sb, 2 * qp, cb, False), True)
            finish(rows_a, ca)
            finish(rows_b, cb)
            return 0

        lax.fori_loop(0, nb // 2, q_pair, 0)

    head = pl.BlockSpec((seq, LANES), lambda h: (0, h))
    return pl.pallas_call(
        body, name="attn_fwd", grid=(HEADS,),
        in_specs=[head, head, pl.BlockSpec((seq, LANES), lambda h: (0, 0))],
        out_specs=[head, head],
        out_shape=[jax.ShapeDtypeStruct((seq, HEADS * LANES), F32)] * 2,
        scratch_shapes=[pltpu.VMEM((seq, LANES), BF16)] * 2,
        compiler_params=_params(),
    )(q, kv, kpe)


def attn_bwd(q, kv, kpe, o, lse, do):
    seq = q.shape[0]
    nb = seq // ATT_T
    assert seq % (2 * ATT_T) == 0, "blocks are taken two per trip"
    scale = MLA_QK ** -0.5

    def body(q_ref, kv_ref, kpe_ref, o_ref, lse_ref, do_ref, dq_ref, dkv_ref, dkpe_ref,
             kf_scr, vf_scr, qb_scr, dob_scr, dsum):
        lane = lax.broadcasted_iota(jnp.int32, (ATT_T, LANES), 1)

        @pl.when(pl.program_id(0) == 0)
        def _():
            dkpe_ref[...] = jnp.zeros_like(dkpe_ref)

        dq_ref[...] = jnp.zeros_like(dq_ref)
        _att_prep(kv_ref, kpe_ref, kf_scr, vf_scr, nb)

        def pre(j, _):
            rows = _att_rows(j)
            d = do_ref[rows, :]
            qb_scr[rows, :] = _bf(q_ref[rows, :] * scale)
            dob_scr[rows, :] = _bf(d)
            dsum[rows, :] = jnp.broadcast_to(jnp.sum(d * o_ref[rows, :], -1, keepdims=True), (ATT_T, LANES))
            return 0

        lax.fori_loop(0, nb, pre, 0)
        mask = _att_diag_mask()

        def front(kf, vf, qi):
            rows = _att_rows(qi)
            return _dot(qb_scr[rows, :], kf, 1, 1), _dot(dob_scr[rows, :], vf, 1, 1)

        def back(kf, qi, fr, carry, masked):
            s, dp = fr
            dk, dv = carry
            rows = _att_rows(qi)
            qb, dob = qb_scr[rows, :], dob_scr[rows, :]
            p = jnp.exp(s - _wide(lse_ref[rows, :]))
            if masked:
                p = jnp.where(mask, p, 0.0)
            ds = _bf(p * (dp - _wide(dsum[rows, :])))
            return (dk + _dot(ds, qb, 0, 0), dv + _dot(_bf(p), dob, 0, 0)), _dot(ds, kf, 1, 0) * scale

        def store(krows, carry):
            dk, dv = carry
            dkv_ref[krows, :] = jnp.where(lane < HEAD, dk, dv)
            dkpe_ref[krows, :] += jnp.where((lane >= HEAD) & (lane < MLA_QK), dk, 0.0)

        def k_pair(kp, _):
            ka, kb = 2 * kp, 2 * kp + 1
            rows_a, rows_b = _att_rows(ka), _att_rows(kb)
            kfa, vfa, kfb, vfb = kf_scr[rows_a, :], vf_scr[rows_a, :], kf_scr[rows_b, :], vf_scr[rows_b, :]
            zero = jnp.zeros((ATT_T, LANES), F32)
            ca, dq_a = back(kfa, ka, front(kfa, vfa, ka), (zero, zero), True)
            dq_ref[rows_a, :] += dq_a
            ca, dq_a = back(kfa, kb, front(kfa, vfa, kb), ca, False)
            cb, dq_b = back(kfb, kb, front(kfb, vfb, kb), (zero, zero), True)
            dq_ref[rows_b, :] += dq_a + dq_b

            def both(qi, c):
                ca, cb, fa, fb = c
                nxt = jnp.minimum(qi + 1, nb - 1)
                fa_next, fb_next = front(kfa, vfa, nxt), front(kfb, vfb, nxt)
                ca, dq_a = back(kfa, qi, fa, ca, False)
                cb, dq_b = back(kfb, qi, fb, cb, False)
                dq_ref[_att_rows(qi), :] += dq_a + dq_b
                return ca, cb, fa_next, fb_next

            first = jnp.minimum(kb + 1, nb - 1)
            ca, cb, _, _ = lax.fori_loop(kb + 1, nb, both, (ca, cb, front(kfa, vfa, first), front(kfb, vfb, first)))
            store(rows_a, ca)
            store(rows_b, cb)
            return 0

        lax.fori_loop(0, nb // 2, k_pair, 0)

    head = pl.BlockSpec((seq, LANES), lambda h: (0, h))
    shared = pl.BlockSpec((seq, LANES), lambda h: (0, 0))
    return pl.pallas_call(
        body, name="attn_bwd", grid=(HEADS,),
        in_specs=[head, head, shared, head, head, head],
        out_specs=[head, head, shared],
        out_shape=[jax.ShapeDtypeStruct((seq, HEADS * LANES), F32)] * 2
        + [jax.ShapeDtypeStruct((seq, LANES), F32)],
        scratch_shapes=[pltpu.VMEM((seq, LANES), BF16)] * 4 + [pltpu.VMEM((seq, LANES), F32)],
        compiler_params=_params(),
    )(q, kv, kpe, o, lse, do)


WKV_TB = 128
WKV_GROUP = SUBLANES
WKV_HALF = WIDTH // 2


def _wkv_consts():
    row = lax.broadcasted_iota(jnp.int32, (HEAD, WKV_HALF), 0)
    lane = lax.broadcasted_iota(jnp.int32, (HEAD, WKV_HALF), 1)
    diag = row == jnp.bitwise_and(lane, HEAD - 1)
    sub = lax.broadcasted_iota(jnp.int32, (WKV_GROUP, WKV_HALF), 0)
    return diag, sub


def _halves(x):
    return [x[:, :WKV_HALF], x[:, WKV_HALF:]]


def _diag_rows(row, diag):
    return _bf(jnp.where(diag, jnp.broadcast_to(row, diag.shape), 0.0))


def _put_row(tile, row, i, sub):
    return jnp.where(sub == i, jnp.broadcast_to(row, tile.shape), tile)


def _col_sum(x):
    return jnp.sum(x, axis=0, keepdims=True)


def _step(x, i):
    return x[i * HEAD:(i + 1) * HEAD]


def _head_dots(prods, ones_b, sub):
    tile = jnp.zeros((WKV_GROUP, WKV_HALF), F32)
    for i, p in enumerate(prods):
        tile = _put_row(tile, p, i, sub)
    res = _exact_dot(tile, ones_b, 0)
    return [res[i:i + 1] for i in range(len(prods))]


def wkv_fwd(r, w, k, v, a, b, ones_half):
    seq = r.shape[0]

    def body(r_ref, w_ref, k_ref, v_ref, a_ref, b_ref, ones_ref, y_ref, st_ref, s_scr):
        @pl.when(pl.program_id(0) == 0)
        def _():
            s_scr[...] = jnp.zeros_like(s_scr)

        ones_b = ones_ref[...]
        diag, sub = _wkv_consts()

        ng = WKV_TB // WKV_GROUP
        last = WKV_GROUP - 2

        def rows_of(g):
            return pl.ds(pl.multiple_of(g * WKV_GROUP, WKV_GROUP), WKV_GROUP)

        def pair_rows(x8, t):
            return jnp.concatenate([_diag_rows(x8[t:t + 1], diag), _diag_rows(x8[t + 1:t + 2], diag)], axis=0)

        def put_y(g, pairs_y):
            tile = _halves(y_ref[rows_of(g), :])
            for hf in range(2):
                tile[hf] = _put_row(_put_row(tile[hf], pairs_y[hf][0], last, sub), pairs_y[hf][1], last + 1, sub)
            y_ref[rows_of(g), :] = jnp.concatenate(tile, axis=1)

        def read_out(yexp):
            return _col_sum(jnp.where(diag, yexp[:HEAD], 0.0)), _col_sum(jnp.where(diag, yexp[HEAD:], 0.0))

        def group(g, carry):
            state, v_cur, read = (list(c) for c in carry)
            base = pl.multiple_of(g * WKV_GROUP, WKV_GROUP)
            rows = rows_of(g)
            r8, w8, k8, v8, a8, b8 = (_halves(ref[rows, :]) for ref in (r_ref, w_ref, k_ref, v_ref, a_ref, b_ref))
            v_after = _halves(v_ref[rows_of(jnp.minimum(g + 1, ng - 1)), :])
            evens = range(0, WKV_GROUP, 2)
            dots = [_head_dots([b8[hf][t:t + 1] * a8[hf][t + 1:t + 2] for t in evens]
                               + [k8[hf][t:t + 1] * a8[hf][t + 1:t + 2] for t in evens], ones_b, sub) for hf in range(2)]
            y8 = [jnp.zeros((WKV_GROUP, WKV_HALF), F32)] * 2
            y_before = [None, None]
            for t in evens:
                s0, s1 = slice(t, t + 1), slice(t + 1, t + 2)
                both = []
                for hf in range(2):
                    s_in = state[hf]
                    v_next = pair_rows(v8[hf], t + 2) if t < last else pair_rows(v_after[hf], 0)
                    res = _dot(jnp.concatenate([_bf(s_in * a8[hf][s0]), _bf(s_in * (w8[hf][s0] * a8[hf][s1])),
                                                v_next, read[hf]], axis=0), ones_b, 1, 0)
                    sa0, v0, v1 = res[:HEAD], v_cur[hf][:HEAD], v_cur[hf][HEAD:]
                    st0 = s_in * w8[hf][s0] + sa0 * b8[hf][s0] + v0 * k8[hf][s0]
                    sa1 = res[HEAD:2 * HEAD] + sa0 * dots[hf][t // 2] + v0 * dots[hf][WKV_GROUP // 2 + t // 2]
                    st1 = st0 * w8[hf][s1] + sa1 * b8[hf][s1] + v1 * k8[hf][s1]
                    both.append((st0, st1))
                    state[hf], v_cur[hf] = st1, res[2 * HEAD:4 * HEAD]
                    read[hf] = jnp.concatenate([_bf(st0 * r8[hf][s0]), _bf(st1 * r8[hf][s1])], axis=0)
                    ya, yb = read_out(res[4 * HEAD:])
                    if t == 0:
                        y_before[hf] = (ya, yb)
                    else:
                        y8[hf] = _put_row(_put_row(y8[hf], ya, t - 2, sub), yb, t - 1, sub)
                for j in range(2):
                    st_ref[base + t + j] = jnp.concatenate([both[0][j], both[1][j]], axis=1)
            y_ref[rows, :] = jnp.concatenate(y8, axis=1)
            put_y(jnp.maximum(g - 1, 0), y_before)
            return tuple(state), tuple(v_cur), tuple(read)

        v_first = _halves(v_ref[rows_of(0), :])
        init = (tuple(_halves(s_scr[...])),
                tuple(_dot(pair_rows(v_first[hf], 0), ones_b, 1, 0) for hf in range(2)),
                tuple(jnp.zeros((2 * HEAD, WKV_HALF), BF16) for _ in range(2)))
        fin, _, read = lax.fori_loop(0, ng, group, init)
        put_y(ng - 1, [read_out(_dot(read[hf], ones_b, 1, 0)) for hf in range(2)])
        s_scr[...] = jnp.concatenate(fin, axis=1)

    vec = pl.BlockSpec((WKV_TB, WIDTH), lambda i: (i, 0))
    return pl.pallas_call(
        body, name="wkv_fwd", grid=(seq // WKV_TB,),
        in_specs=[vec] * 6 + [_full_spec(ones_half)],
        out_specs=[vec, pl.BlockSpec((WKV_TB, HEAD, WIDTH), lambda i: (i, 0, 0))],
        out_shape=[jax.ShapeDtypeStruct((seq, WIDTH), F32), jax.ShapeDtypeStruct((seq, HEAD, WIDTH), F32)],
        scratch_shapes=[pltpu.VMEM((HEAD, WIDTH), F32)],
        compiler_params=_params(),
    )(r, w, k, v, a, b, ones_half)


def wkv_bwd(r, w, k, v, a, b, dy, states, ones_half):
    seq = r.shape[0]
    nb = seq // WKV_TB
    ng = WKV_TB // WKV_GROUP

    def body(r_ref, w_ref, k_ref, v_ref, a_ref, b_ref, dy_ref, st_ref, halo_ref, ones_ref,
             dr_ref, dw_ref, dk_ref, dv_ref, da_ref, db_ref, ds_scr):
        blk = nb - 1 - pl.program_id(0)

        @pl.when(pl.program_id(0) == 0)
        def _():
            ds_scr[...] = jnp.zeros_like(ds_scr)

        ones_b = ones_ref[...]
        diag, sub = _wkv_consts()
        before_block = jnp.where(blk == 0, 0.0, halo_ref[0])

        def rows_of(g):
            return pl.ds(pl.multiple_of(g * WKV_GROUP, WKV_GROUP), WKV_GROUP)

        def expand_rows(hf, dy8, v8, a8, t, s_t, s_u):
            s1, s0 = slice(t, t + 1), slice(t - 1, t)
            return jnp.concatenate([_diag_rows(dy8[hf][s1], diag), _diag_rows(dy8[hf][s0], diag),
                                    _diag_rows(v8[hf][s1], diag), _diag_rows(v8[hf][s0], diag),
                                    _bf(s_t[hf] * a8[hf][s1]), _bf(s_u[hf] * a8[hf][s0])], axis=0)

        def read_out(x):
            return _col_sum(jnp.where(diag, x[:HEAD], 0.0)), _col_sum(jnp.where(diag, x[HEAD:], 0.0))

        def put_dv(g, pair_dv):
            tile = _halves(dv_ref[rows_of(g), :])
            for hf in range(2):
                tile[hf] = _put_row(_put_row(tile[hf], pair_dv[hf][0], 1, sub), pair_dv[hf][1], 0, sub)
            dv_ref[rows_of(g), :] = jnp.concatenate(tile, axis=1)

        def group(gg, carry):
            dstate, e_cur, dv_pend = (list(c) for c in carry)
            g = ng - 1 - gg
            base = pl.multiple_of(g * WKV_GROUP, WKV_GROUP)
            rows = rows_of(g)
            r8, w8, k8, v8, a8, b8, dy8 = (
                _halves(ref[rows, :]) for ref in (r_ref, w_ref, k_ref, v_ref, a_ref, b_ref, dy_ref))
            g_next = jnp.maximum(g - 1, 0)
            base_next = pl.multiple_of(g_next * WKV_GROUP, WKV_GROUP)
            dy8n, v8n, a8n = (_halves(ref[rows_of(g_next), :]) for ref in (dy_ref, v_ref, a_ref))
            zero8 = jnp.zeros((WKV_GROUP, WKV_HALF), F32)
            out = {n: [zero8, zero8] for n in ("dr", "dw", "dk", "dv", "da", "db")}
            before_group = jnp.where(g == 0, before_block, st_ref[jnp.maximum(base - 1, 0)])
            states = [_halves(before_group)] + [_halves(st_ref[base + i]) for i in range(WKV_GROUP)]
            odds = range(1, WKV_GROUP, 2)
            dots = [_head_dots([a8[hf][t:t + 1] * b8[hf][t - 1:t] for t in odds]
                               + [r8[hf][t - 1:t] * b8[hf][t - 1:t] for t in odds], ones_b, sub) for hf in range(2)]
            dv_after = [None, None]

            def emit(hf, i, d_i, dsa_i, dy_i, v_i, sa_i):
                s_p, s_t = states[i][hf], states[i + 1][hf]
                for n, val in (("dr", _col_sum(s_t * dy_i)), ("dw", _col_sum(d_i * s_p)), ("db", _col_sum(d_i * sa_i)),
                               ("da", _col_sum(s_p * dsa_i)), ("dk", _col_sum(d_i * v_i))):
                    out[n][hf] = _put_row(out[n][hf], val, i, sub)

            for t in reversed(odds):
                s1, s0 = slice(t, t + 1), slice(t - 1, t)
                for hf in range(2):
                    dy1, dy0, v1, v0, sa1, sa0 = (_step(e_cur[hf], j) for j in range(6))
                    d1 = dstate[hf] + dy1 * r8[hf][s1]
                    if t > 1:
                        nxt = expand_rows(hf, dy8, v8, a8, t - 2, states[t - 2], states[t - 3])
                    else:
                        nxt = expand_rows(hf, dy8n, v8n, a8n, WKV_GROUP - 1, _halves(st_ref[base_next + WKV_GROUP - 2]),
                                          _halves(st_ref[base_next + WKV_GROUP - 3]))
                    res = _dot(jnp.concatenate([_bf(d1 * b8[hf][s1]), _bf(d1 * (w8[hf][s1] * b8[hf][s0])),
                                                nxt, dv_pend[hf]], axis=0), ones_b, 1, 0)
                    dsa1 = res[:HEAD]
                    d0 = d1 * w8[hf][s1] + dsa1 * a8[hf][s1] + dy0 * r8[hf][s0]
                    dsa0 = res[HEAD:2 * HEAD] + dsa1 * dots[hf][t // 2] + dy0 * dots[hf][WKV_GROUP // 2 + t // 2]
                    dstate[hf] = d0 * w8[hf][s0] + dsa0 * a8[hf][s0]
                    e_cur[hf] = res[2 * HEAD:8 * HEAD]
                    dv_pend[hf] = jnp.concatenate([_bf(d1 * k8[hf][s1]), _bf(d0 * k8[hf][s0])], axis=0)
                    emit(hf, t, d1, dsa1, dy1, v1, sa1)
                    emit(hf, t - 1, d0, dsa0, dy0, v0, sa0)
                    dv_a, dv_b = read_out(res[8 * HEAD:])
                    if t == WKV_GROUP - 1:
                        dv_after[hf] = (dv_a, dv_b)
                    else:
                        out["dv"][hf] = _put_row(_put_row(out["dv"][hf], dv_a, t + 2, sub), dv_b, t + 1, sub)
            for ref, n in ((dr_ref, "dr"), (dw_ref, "dw"), (dk_ref, "dk"), (dv_ref, "dv"), (da_ref, "da"), (db_ref, "db")):
                ref[rows, :] = jnp.concatenate(out[n], axis=1)
            put_dv(jnp.minimum(g + 1, ng - 1), dv_after)
            return tuple(dstate), tuple(e_cur), tuple(dv_pend)

        top = rows_of(ng - 1)
        dy8t, v8t, a8t = (_halves(ref[top, :]) for ref in (dy_ref, v_ref, a_ref))
        s_t, s_u = _halves(st_ref[WKV_TB - 2]), _halves(st_ref[WKV_TB - 3])
        init = (tuple(_halves(ds_scr[...])),
                tuple(_dot(expand_rows(hf, dy8t, v8t, a8t, WKV_GROUP - 1, s_t, s_u), ones_b, 1, 0) for hf in range(2)),
                tuple(jnp.zeros((2 * HEAD, WKV_HALF), BF16) for _ in range(2)))
        fin, _, dv_pend = lax.fori_loop(0, ng, group, init)
        put_dv(0, [read_out(_dot(dv_pend[hf], ones_b, 1, 0)) for hf in range(2)])
        ds_scr[...] = jnp.concatenate(fin, axis=1)

    vec = pl.BlockSpec((WKV_TB, WIDTH), lambda i: (nb - 1 - i, 0))
    return pl.pallas_call(
        body, name="wkv_bwd", grid=(nb,),
        in_specs=[vec] * 7 + [
            pl.BlockSpec((WKV_TB, HEAD, WIDTH), lambda i: (nb - 1 - i, 0, 0)),
            pl.BlockSpec((1, HEAD, WIDTH), lambda i: (jnp.maximum((nb - 1 - i) * WKV_TB - 1, 0), 0, 0)),
            _full_spec(ones_half)],
        out_specs=[vec] * 6,
        out_shape=[jax.ShapeDtypeStruct((seq, WIDTH), F32)] * 6,
        scratch_shapes=[pltpu.VMEM((HEAD, WIDTH), F32)],
        compiler_params=_params(),
    )(r, w, k, v, a, b, dy, states, states, ones_half)


def ada_fwd(c8, b_ada, gathered):
    cols = 3 * D_MODEL // 4

    def body(c_ref, b_ref, w_ref, o_ref):
        @pl.when(pl.program_id(1) == 0)
        def _():
            o_ref[...] = jnp.broadcast_to(b_ref[...], o_ref.shape)

        o_ref[...] += mm(_silu(c_ref[...]), w_ref[0])

    return pl.pallas_call(
        body, name="ada_fwd", grid=(4, D_MODEL // PACK_ROWS),
        in_specs=[pl.BlockSpec((SUBLANES, PACK_ROWS), lambda s, i: (0, i)),
                  pl.BlockSpec((1, cols), lambda s, i: (0, s)),
                  pl.BlockSpec((1, PACK_ROWS, cols), lambda s, i: (2 * s, 0, i))],
        out_specs=pl.BlockSpec((SUBLANES, cols), lambda s, i: (0, s)),
        out_shape=jax.ShapeDtypeStruct((SUBLANES, 3 * D_MODEL), F32),
        compiler_params=_params(("arbitrary", "arbitrary")),
    )(c8, b_ada, gathered)


def ada_grad_shard(sc_cols, dada_rows):
    n = len(sc_cols)

    def body(*refs):
        d_ref, o_ref = refs[n], refs[n + 1]
        acc = refs[0][...] * d_ref[0:1, :]
        for b in range(1, n):
            acc = acc + refs[b][...] * d_ref[b:b + 1, :]
        o_ref[...] = acc

    return pl.pallas_call(
        body, name="ada_grad_shard",
        out_shape=jax.ShapeDtypeStruct((sc_cols[0].shape[0], dada_rows.shape[1]), F32),
        compiler_params=pltpu.CompilerParams(vmem_limit_bytes=VMEM_LIMIT),
    )(*sc_cols, dada_rows)


def sum_slots(buf, tr):
    n, rows, cols = buf.shape

    def body(b_ref, o_ref):
        acc = b_ref[0].astype(F32)
        for s in range(1, n):
            acc = acc + b_ref[s].astype(F32)
        o_ref[...] = acc

    return pl.pallas_call(
        body, name="sum_slots", grid=(rows // tr,),
        in_specs=[pl.BlockSpec((n, tr, cols), lambda i: (0, i, 0))],
        out_specs=pl.BlockSpec((tr, cols), lambda i: (i, 0)),
        out_shape=jax.ShapeDtypeStruct((rows, cols), F32),
        compiler_params=_params(),
    )(buf)


def adamw_small(gathered, w, m, v):
    n = gathered.shape[0]

    def body(g_ref, w_ref, m_ref, v_ref, go_ref, d_ref, mo_ref, vo_ref):
        g = g_ref[0]
        for s in range(1, n):
            g = g + g_ref[s]
        go_ref[...] = g
        d_ref[...], mo_ref[...], vo_ref[...] = f_adamw(w_ref[...], g, m_ref[...], v_ref[...])

    return pl.pallas_call(
        body, name="adamw_small",
        out_shape=[jax.ShapeDtypeStruct(w.shape, F32)] * 4,
        compiler_params=pltpu.CompilerParams(vmem_limit_bytes=VMEM_LIMIT),
    )(gathered, w, m, v)


def _coords():
    return lax.axis_index("x"), lax.axis_index("y"), lax.axis_index("c")


def _flip(v, bit):
    return 1 - v if bit else v


def _hbm_call(body, name, out_shape, n_sems, *args):
    any_spec = pl.BlockSpec(memory_space=pl.ANY)
    return pl.pallas_call(
        body, name=name, out_shape=out_shape,
        in_specs=[any_spec] * len(args), out_specs=any_spec,
        scratch_shapes=[pltpu.SemaphoreType.DMA((n_sems,)), pltpu.SemaphoreType.DMA((n_sems,)),
                        pltpu.SemaphoreType.DMA],
    )(*args)


def all_gather8(name, block):
    def body(x_ref, out_ref, send_sems, recv_sems, local_sem):
        x, y, c = _coords()
        me, sibling = (x, y, c), (x, y, 1 - c)
        x_nbr, y_nbr, diagonal = (1 - x, y), (x, 1 - y), (1 - x, 1 - y)
        relay_from = (c * x + (1 - c) * (1 - x), c * (1 - y) + (1 - c) * y)
        relay_to = (c * (1 - x) + (1 - c) * x, c * y + (1 - c) * (1 - y))

        def slot(px, py, pc):
            return out_ref.at[4 * px + 2 * py + pc]

        def copy(k, blk, to, src=None):
            return pltpu.make_async_remote_copy(
                src_ref=slot(*blk) if src is None else src, dst_ref=slot(*blk),
                send_sem=send_sems.at[k], recv_sem=recv_sems.at[k], device_id=to, device_id_type=MESH)

        mine = pltpu.make_async_copy(x_ref, slot(*me), local_sem)
        mine.start()
        first = [copy(0, me, sibling, src=x_ref), copy(1, me, (*x_nbr, c), src=x_ref), copy(2, me, (*y_nbr, c), src=x_ref)]
        for cp in first:
            cp.start()
        copy(1, (*x_nbr, c), me).wait_recv()
        copy(2, (*y_nbr, c), me).wait_recv()
        later = [copy(3, (*relay_from, c), (*relay_to, c)), copy(4, (*x_nbr, c), sibling), copy(5, (*y_nbr, c), sibling)]
        for cp in later:
            cp.start()
        copy(3, (*diagonal, c), me).wait_recv()
        last = copy(6, (*diagonal, c), sibling)
        last.start()
        copy(0, sibling, me).wait_recv()
        for k, chip in ((4, x_nbr), (5, y_nbr), (6, diagonal)):
            copy(k, (*chip, 1 - c), me).wait_recv()
        for cp in first + later + [last]:
            cp.wait_send()
        mine.wait()

    return _hbm_call(body, name, jax.ShapeDtypeStruct((N_DEV,) + block.shape, block.dtype), 7, block)


def pair_swap(name, block):
    def body(x_ref, out_ref, send_sems, recv_sems, local_sem):
        x, y, c = _coords()
        cp = pltpu.make_async_remote_copy(
            src_ref=x_ref, dst_ref=out_ref, send_sem=send_sems.at[0], recv_sem=recv_sems.at[0],
            device_id=(x, y, 1 - c), device_id_type=MESH)
        cp.start()
        cp.wait_recv()
        cp.wait_send()

    return _hbm_call(body, name, jax.ShapeDtypeStruct(block.shape, block.dtype), 1, block)


def chip_all_to_all(name, buf):
    def body(x_ref, out_ref, send_sems, recv_sems, local_sem):
        x, y, c = _coords()
        me = 2 * x + y
        mine = pltpu.make_async_copy(x_ref.at[me], out_ref.at[me], local_sem)
        mine.start()
        copies = []
        for k in range(1, 4):
            px, py = _flip(x, k & 2), _flip(y, k & 1)
            copies.append(pltpu.make_async_remote_copy(
                src_ref=x_ref.at[2 * px + py], dst_ref=out_ref.at[me],
                send_sem=send_sems.at[k - 1], recv_sem=recv_sems.at[k - 1],
                device_id=(px, py, c), device_id_type=MESH))
        for cp in copies:
            cp.start()
        for cp in copies:
            cp.wait_recv()
        for cp in copies:
            cp.wait_send()
        mine.wait()

    return _hbm_call(body, name, jax.ShapeDtypeStruct(buf.shape, buf.dtype), 3, buf)


def _col_blocks(a, cols):
    a = jnp.pad(a, ((0, 0), (0, cols - a.shape[1])))
    return [a[i * PACK_ROWS:(i + 1) * PACK_ROWS] for i in range(a.shape[0] // PACK_ROWS)]


def _pack_shard(sh, dtype, with_ada=True):
    lora = jnp.concatenate([sh['w_decay_up'], sh['w_iclr_up']], axis=1)
    misc = jnp.concatenate([sh['w_ukv'], lora, jnp.zeros((LORA, 2 * LANES), lora.dtype)], axis=0)
    blocks = ((_col_blocks(sh['w_ada'], 768) if with_ada else [])
              + _col_blocks(sh['w_in'], 1408) + _col_blocks(sh['w_proj_a'], 256)
              + _col_blocks(sh['w_proj_b'], 256) + [sh['w_out']] + _col_blocks(sh['w_uq'], 256) + [misc])
    return jnp.concatenate([b.astype(dtype) for b in blocks], axis=1)


def _unpack_shard(p, with_ada=True):
    o = [0]

    def take(n_blocks, cols, used):
        blocks = [p[:, o[0] + i * cols:o[0] + (i + 1) * cols] for i in range(n_blocks)]
        o[0] += n_blocks * cols
        return jnp.concatenate(blocks, axis=0)[:, :used]

    out = {'w_ada': take(4, 768, 768)} if with_ada else {}
    out.update({'w_in': take(4, 1408, 1288), 'w_proj_a': take(2, 256, 256),
                'w_proj_b': take(2, 256, 256), 'w_out': take(1, 1024, 1024), 'w_uq': take(1, 256, 192)})
    misc = take(1, 256, 256)
    out['w_ukv'] = misc[:2 * LORA]
    out['w_decay_up'] = misc[2 * LORA:3 * LORA, :LANES]
    out['w_iclr_up'] = misc[2 * LORA:3 * LORA, LANES:]
    return out


def _pack_small(parts):
    flat = jnp.concatenate([p.reshape(-1) for p in parts])
    return jnp.pad(flat, (0, SMALL_ROWS * LANES - flat.shape[0])).reshape(SMALL_ROWS, LANES)


def _unpack_small(packed):
    flat, out, o = packed.reshape(-1), {}, 0
    for name, n in _SMALL:
        out[name] = flat[o:o + n]
        o += n
    return out


def _pad_heads_cols(w, used, left):
    k = w.shape[0]
    return jnp.pad(w.reshape(k, HEADS, used), ((0, 0), (0, 0), (left, LANES - used - left))).reshape(k, HEADS * LANES)


def _unpad_heads_cols(w, used, left):
    k = w.shape[0]
    return w.reshape(k, HEADS, LANES)[:, :, left:left + used].reshape(k, HEADS * used)


def kernel(x, c, positions, w_ada, b_ada, w_in, q_norm_g, w_uq, kv_norm_g, w_ukv, mu_rwkv, w0, w_decay_up, a0, w_iclr_up, k_k, k_a, r_k, gn_g, gn_b, w_proj_a, w_proj_b, w_out, post_g, post_b, loss_target, m_w_ada, m_b_ada, m_w_in, m_q_norm_g, m_w_uq, m_kv_norm_g, m_w_ukv, m_mu_rwkv, m_w0, m_w_decay_up, m_a0, m_w_iclr_up, m_k_k, m_k_a, m_r_k, m_gn_g, m_gn_b, m_w_proj_a, m_w_proj_b, m_w_out, m_post_g, m_post_b, v_w_ada, v_b_ada, v_w_in, v_q_norm_g, v_w_uq, v_kv_norm_g, v_w_ukv, v_mu_rwkv, v_w0, v_w_decay_up, v_a0, v_w_iclr_up, v_k_k, v_k_a, v_r_k, v_gn_g, v_gn_b, v_w_proj_a, v_w_proj_b, v_w_out, v_post_g, v_post_b):
    given = dict(locals())
    seq = x.shape[1]
    my_c = lax.axis_index("c")

    shard_names = [n for n, _, _ in _SHARDED]
    w_pack = _pack_shard({n: given[n][0] for n in shard_names}, BF16)
    my_half = lax.dynamic_slice_in_dim(w_pack, my_c * HALF_COLS, HALF_COLS, 1)
    gathered = all_gather8("gather_weights", my_half)
    shards = [_unpack_shard(jnp.concatenate([gathered[2 * s], gathered[2 * s + 1]], axis=1)) for s in range(4)]
    full = {n: jnp.concatenate([sh[n] for sh in shards], axis=ax) for n, _, ax in _SHARDED}

    wi = full['w_in']
    zcol = lambda n: jnp.zeros((D_MODEL, n), BF16)
    w_g1 = jnp.concatenate([wi[:, :384], zcol(HEAD), wi[:, 384:416], zcol(LANES - MLA_QK),
                            _pad_heads_cols(wi[:, 416:928], HEAD, HEAD)], axis=1)
    w_g2 = wi[:, 928:3104]
    w_g3 = wi[:, 3104:5152]
    w_uq_p = _pad_heads_cols(full['w_uq'], MLA_QK, 0)
    w_pa_p = jnp.pad(full['w_proj_a'].reshape(HEADS, HEAD, D_MODEL), ((0, 0), (HEAD, 0), (0, 0))).reshape(HEADS * LANES, D_MODEL)
    zl = jnp.zeros((LORA, WIDTH), BF16)
    w_lora = jnp.concatenate([jnp.concatenate([full['w_decay_up'], zl], 1),
                              jnp.concatenate([zl, full['w_iclr_up']], 1)], 0)

    hd = np.arange(WKV_HALF) // HEAD
    ones_half = jnp.asarray(hd[:, None] == hd[None, :], BF16)
    perm_np = np.zeros((LANES, LANES), np.float32)
    for d in range(MLA_ROPE // 2):
        perm_np[HEAD + 16 + d, HEAD + d] = -1.0
        perm_np[HEAD + d, HEAD + 16 + d] = 1.0
    perm = jnp.asarray(perm_np, BF16)
    inv = ROPE_THETA ** (-jnp.arange(0, MLA_ROPE, 2, dtype=F32) / MLA_ROPE)
    ang = positions[0].astype(F32)[:, None] * inv
    cos_a, sin_a = jnp.cos(ang), jnp.sin(ang)
    cs = jnp.concatenate([jnp.ones((seq, HEAD), F32), cos_a, cos_a, jnp.zeros((seq, LANES - MLA_QK), F32),
                          jnp.zeros((seq, HEAD), F32), sin_a, sin_a, jnp.zeros((seq, LANES - MLA_QK), F32)], axis=1)

    x2, tgt = x[0], loss_target[0]
    r_k2 = r_k.reshape(1, WIDTH)

    c8 = jnp.broadcast_to(c, (SUBLANES, D_MODEL))
    ada = ada_fwd(c8, b_ada, gathered)[:1]
    shift, scale, gate = ada[:, :D_MODEL], ada[:, D_MODEL:2 * D_MODEL], ada[:, 2 * D_MODEL:]

    f_in1, f_in2, f_in3 = _make_f_in((512, 1024)), _make_f_in((SHIFT_W, WIDTH)), _make_f_in((1024, 1024))
    tr = min(256, seq)
    p_mla, gpa, p_rwkv, gpb, ma, mb = row_fwd(
        "in_fwd", _make_f_in((512, 1024), (SHIFT_W, WIDTH), (1024, 1024)), [x2], [shift, scale, w_g1, w_g2, w_g3], [],
        [512, 1024, SHIFT_W, WIDTH, 1024, 1024], tr)

    mla_par = [q_norm_g, kv_norm_g, w_uq_p, full['w_ukv']]
    q_f, kv_f, kpe = row_fwd("mla_pre_fwd", f_mla_pre, [p_mla, cs], mla_par, [perm], [1024, 1024, LANES], tr)
    ya, lse = attn_fwd(q_f, kv_f, kpe)

    pre_par = [w0, a0, k_k, k_a, w_lora]
    u, rr, wd, k2, vv, an, bb = shift_stage_fwd("rwkv_pre_fwd", f_rwkv_pre, p_rwkv, mu_rwkv, pre_par, [ones_half],
                                                [WIDTH] * 6, tr)
    y_wkv, states = wkv_fwd(rr, wd, k2, vv, an, bb, ones_half)
    post_b_par = [r_k2, gn_g, gn_b]
    yb, = row_fwd("rwkv_post_fwd", f_rwkv_post, [y_wkv, rr, k2, vv], post_b_par, [ones_half], [WIDTH], tr)

    merge_rows, merge_par = [ya, gpa, yb, gpb, ma, mb], [w_pa_p, full['w_proj_b']]
    merged, = row_fwd("merge_fwd", f_merge, merge_rows, merge_par, [], [D_MODEL], tr)
    loss_rows, loss_par = [merged, x2, tgt], [gate, post_g, post_b, full['w_out']]
    lrows, = row_fwd("loss_fwd", f_loss, loss_rows, loss_par, [], [LANES], tr)
    loss = lax.psum(jnp.sum(lrows[:, 0]), ("x", "y", "c"))

    dl = jnp.broadcast_to((jnp.arange(LANES) == 0).astype(F32), (seq, LANES))
    (dmerged, dx_res), (dgate, dpost_g, dpost_b, dw_out) = row_bwd("loss_bwd", f_loss, loss_rows, 2, loss_par, [], [dl], tr)
    (dya, dgpa, dyb, dgpb, dma, dmb), (dw_pa_p, dw_pb) = row_bwd(
        "merge_bwd", f_merge, merge_rows, 6, merge_par, [], [dmerged], tr)

    (dy_wkv, dr1, dk1, dv1), (dr_k, dgn_g, dgn_b) = row_bwd(
        "rwkv_post_bwd", f_rwkv_post, [y_wkv, rr, k2, vv], 4, post_b_par, [ones_half], [dyb], tr)
    dr2, dwd, dk2, dv2, dan, dbb = wkv_bwd(rr, wd, k2, vv, an, bb, dy_wkv, states, ones_half)
    dp_rwkv, dmu, (dw0, da0, dk_k, dk_a, dw_lora) = shift_stage_bwd(
        "rwkv_pre_bwd", f_rwkv_pre, p_rwkv, u, mu_rwkv, pre_par, [ones_half],
        [(dr1, dr2), dwd, (dk1, dk2), (dv1, dv2), dan, dbb], tr)

    dq_f, dkv_f, dkpe = attn_bwd(q_f, kv_f, kpe, ya, lse, dya)
    (dp_mla,), (dqg, dkvg, dw_uq_p, dw_ukv) = row_bwd(
        "mla_pre_bwd", f_mla_pre, [p_mla, cs], 1, mla_par, [perm], [dq_f, dkv_f, dkpe], tr)

    (dx1,), (dsh1, dsc1, dw_g1) = row_bwd("in1_bwd", f_in1, [x2], 1, [shift, scale, w_g1], [], [dp_mla, dgpa], tr, [dx_res])
    (dx2,), (dsh2, dsc2, dw_g2) = row_bwd("in2_bwd", f_in2, [x2], 1, [shift, scale, w_g2], [], [dp_rwkv, dgpb], tr, [dx1])
    (dx3,), (dsh3, dsc3, dw_g3) = row_bwd("in3_bwd", f_in3, [x2], 1, [shift, scale, w_g3], [], [dma, dmb], tr, [dx2])
    grad_x = dx3[None]

    dada = jnp.concatenate([dsh1 + dsh2 + dsh3, dsc1 + dsc2 + dsc3, dgate], axis=1)
    local = {
        'w_in': jnp.concatenate([dw_g1[:, :384], dw_g1[:, 448:480], _unpad_heads_cols(dw_g1[:, 512:], HEAD, HEAD),
                                 dw_g2, dw_g3], axis=1),
        'w_uq': _unpad_heads_cols(dw_uq_p, MLA_QK, 0),
        'w_ukv': dw_ukv,
        'w_decay_up': dw_lora[:LORA, :WIDTH],
        'w_iclr_up': dw_lora[LORA:, WIDTH:],
        'w_proj_a': dw_pa_p.reshape(HEADS, LANES, D_MODEL)[:, HEAD:].reshape(WIDTH, D_MODEL),
        'w_proj_b': dw_pb,
        'w_out': dw_out,
    }
    small_local = {'b_ada': dada, 'q_norm_g': dqg, 'kv_norm_g': dkvg, 'mu_rwkv': dmu, 'w0': dw0, 'a0': da0,
                   'k_k': dk_k, 'k_a': dk_a, 'r_k': dr_k, 'gn_g': dgn_g, 'gn_b': dgn_b,
                   'post_g': dpost_g, 'post_b': dpost_b}

    def shard_of(g, axis, s):
        n = g.shape[axis] // 4
        return lax.slice_in_dim(g, s * n, (s + 1) * n, axis=axis)

    packed = jnp.stack([_pack_shard({n: shard_of(local[n], ax, s) for n, _, ax in _SHARDED if n != 'w_ada'}, F32, False)
                        for s in range(4)])
    keep = lax.dynamic_slice_in_dim(packed, my_c * GRAD_HALF, GRAD_HALF, 2).reshape(4 * PACK_ROWS, GRAD_HALF)
    give = lax.dynamic_slice_in_dim(packed, (1 - my_c) * GRAD_HALF, GRAD_HALF, 2).reshape(4 * PACK_ROWS, GRAD_HALF)
    pair_sum, = row_fwd("pair_sum", lambda p, q: (p + q,), [keep, pair_swap("swap_halves", give)], [], [],
                        [GRAD_HALF], PACK_ROWS // 2, BF16)
    received = chip_all_to_all("exchange_grads", pair_sum.reshape(4, PACK_ROWS, GRAD_HALF))
    my_sum = sum_slots(received, PACK_ROWS // 2)
    other_sum = pair_swap("swap_sums", my_sum)
    halves = [jnp.where(my_c == 0, my_sum, other_sum), jnp.where(my_c == 0, other_sum, my_sum)]
    g_shard = _unpack_shard(jnp.concatenate(halves, axis=1), False)

    small_pack = lambda d, extra=(): _pack_small([d[n] for n, _ in _SMALL] + list(extra))
    small_all = all_gather8("gather_small", small_pack(small_local, [c * jax.nn.sigmoid(c)]))
    sc_all = small_all[:, SMALL_USED:SMALL_USED + D_MODEL // LANES].reshape(N_DEV, D_MODEL)
    dada_all = small_all[:, :3 * D_MODEL // LANES].reshape(N_DEV, 3 * D_MODEL)
    my_cols = lax.dynamic_slice_in_dim(dada_all, (2 * lax.axis_index("x") + lax.axis_index("y")) * 768, 768, 1)
    g_shard['w_ada'] = ada_grad_shard([sc_all[b].reshape(D_MODEL, 1) for b in range(N_DEV)], my_cols)

    big = [{}, {}, {}, {}]
    for n in shard_names:
        w2, m2, v2 = given[n][0], given['m_' + n][0], given['v_' + n][0]
        cols = w2.shape[1]
        outs = row_fwd("adamw_" + n, f_adamw, [w2, g_shard[n], m2, v2], [], [], [cols] * 3, min(256, w2.shape[0]))
        for dst, val in zip(big, (g_shard[n], *outs)):
            dst[n] = val

    small_out = adamw_small(small_all, small_pack({n: given[n] for n, _ in _SMALL}),
                            small_pack({n: given['m_' + n] for n, _ in _SMALL}),
                            small_pack({n: given['v_' + n] for n, _ in _SMALL}))

    results = []
    for big_k, packed_small in zip(big, small_out):
        small = _unpack_small(packed_small)
        results.append([(big_k[n] if n in big_k else small[n]).reshape(given[n].shape) for n in _WEIGHTS])
    return (loss, grad_x, *results[0], *results[1], *results[2], *results[3])
```

```python
from typing import NamedTuple

import numpy as np
import jax
import jax.numpy as jnp
from jax import lax
from jax.experimental import pallas as pl
from jax.experimental.pallas import tpu as pltpu

F32 = jnp.float32
BF16 = jnp.bfloat16

D_MODEL = 1024
LN_EPS = 1e-5
RMS_EPS = 1e-6
GN_EPS = 64e-5
HEADS = 8
HEAD = 64
MLA_ROPE = 32
MLA_QK = HEAD + MLA_ROPE
ROPE_THETA = 10000.0
WIDTH = HEADS * HEAD
LORA = 64
SHIFT_W = 3 * WIDTH + 2 * LORA
CHUNK = 64
ALPHA = 2.0 ** 0.25

ADAM_LR, ADAM_B1, ADAM_B2, ADAM_EPS, ADAM_WD, ADAM_STEP = 0.001, 0.9, 0.999, 1e-08, 0.01, 10

LANES = 128
SUBLANES = 8
VMEM_LIMIT = 56 * 1024 * 1024
N_DEV = 8
MESH = pl.DeviceIdType.MESH
NEG = -1e30

_WEIGHTS = ['w_ada', 'b_ada', 'w_in', 'q_norm_g', 'w_uq', 'kv_norm_g', 'w_ukv', 'mu_rwkv', 'w0',
            'w_decay_up', 'a0', 'w_iclr_up', 'k_k', 'k_a', 'r_k', 'gn_g', 'gn_b', 'w_proj_a',
            'w_proj_b', 'w_out', 'post_g', 'post_b']
_SHARDED = [('w_ada', (1024, 3072), 1), ('w_in', (1024, 5152), 1), ('w_uq', (256, 768), 1),
            ('w_ukv', (128, 1024), 1), ('w_decay_up', (64, 512), 1), ('w_iclr_up', (64, 512), 1),
            ('w_proj_a', (512, 1024), 1), ('w_proj_b', (512, 1024), 1), ('w_out', (1024, 1024), 0)]
_SMALL = [('b_ada', 3072), ('q_norm_g', 256), ('kv_norm_g', 128), ('mu_rwkv', 1664), ('w0', 512),
          ('a0', 512), ('k_k', 512), ('k_a', 512), ('r_k', 512), ('gn_g', 512), ('gn_b', 512),
          ('post_g', 1024), ('post_b', 1024)]
PACK_ROWS = 256
PACK_COLS = 11264
HALF_COLS = PACK_COLS // 2
ADA_COLS = 4 * 768
GRAD_HALF = (PACK_COLS - ADA_COLS) // 2
SMALL_USED = 84
SMALL_ROWS = 96


def _bf(x):
    return x.astype(BF16)


def _dot(a, b, ca, cb):
    return lax.dot_general(a, b, (((ca,), (cb,)), ((), ())), preferred_element_type=F32)


class Weight(NamedTuple):
    value: jax.Array
    grad: jax.Array


@jax.custom_vjp
def _mm(a, w, w_grad):
    return _dot(_bf(a), _bf(w), 1, 0)


def _mm_fwd(a, w, w_grad):
    return _mm(a, w, w_grad), (a, w)


def _mm_bwd(res, g):
    a, w = res
    gb = _bf(g)
    return _dot(gb, _bf(w), 1, 1), jnp.zeros_like(w), _dot(_bf(a), gb, 0, 0)


_mm.defvjp(_mm_fwd, _mm_bwd)


def mm(a, w):
    if isinstance(w, Weight):
        return _mm(a, w.value, w.grad)
    return _dot(_bf(a), _bf(w), 1, 0)


def _split3(x):
    hi = _bf(x)
    r1 = x - hi.astype(F32)
    mid = _bf(r1)
    lo = _bf(r1 - mid.astype(F32))
    return hi, mid, lo


def _exact_dot(x, m, cm):
    hi, mid, lo = _split3(x)
    return _dot(hi, m, 1, cm) + _dot(mid, m, 1, cm) + _dot(lo, m, 1, cm)


def _head_sums(x, ones_blocks):
    n = ones_blocks.shape[0]
    parts = [_exact_dot(x[:, o:o + n], ones_blocks, 0) for o in range(0, x.shape[1], n)]
    return parts[0] if len(parts) == 1 else jnp.concatenate(parts, axis=1)


@jax.custom_vjp
def segsum(x, ones_blocks):
    return _head_sums(x, ones_blocks)


def _segsum_fwd(x, ones_blocks):
    return segsum(x, ones_blocks), ones_blocks


def _segsum_bwd(ones_blocks, g):
    return _head_sums(g, ones_blocks), jnp.zeros_like(ones_blocks)


segsum.defvjp(_segsum_fwd, _segsum_bwd)


@jax.custom_vjp
def lane_perm(x, perm):
    return _exact_dot(x, perm, 0)


def _lane_perm_fwd(x, perm):
    return lane_perm(x, perm), perm


def _lane_perm_bwd(perm, g):
    return _exact_dot(g, perm, 1), jnp.zeros_like(perm)


lane_perm.defvjp(_lane_perm_fwd, _lane_perm_bwd)


def _silu(z):
    return z * jax.nn.sigmoid(z)


def _softplus(z):
    return jnp.maximum(z, 0.0) + jnp.log(1.0 + jnp.exp(-jnp.abs(z)))


def _layer_norm(x):
    xc = x - jnp.mean(x, -1, keepdims=True)
    return xc * lax.rsqrt(jnp.mean(xc * xc, -1, keepdims=True) + LN_EPS)


def _rope(t, cos_t, sin_t, perm):
    outs = []
    for h in range(t.shape[1] // LANES):
        th = t[:, h * LANES:(h + 1) * LANES]
        outs.append(th * cos_t + lane_perm(th, perm) * sin_t)
    return outs[0] if len(outs) == 1 else jnp.concatenate(outs, axis=1)


def _make_f_in(*split_groups):
    def f_in(x, shift, scale, *weights):
        h = _layer_norm(x) * (1.0 + scale) + shift
        outs = []
        for w, splits in zip(weights, split_groups):
            p, o = mm(h, w), 0
            for s in splits:
                outs.append(p[:, o:o + s])
                o += s
        return tuple(outs)
    return f_in


def f_mla_pre(p, cs, qg, kvg, w_uq, w_ukv, perm):
    q_c, kv_c, k_r = p[:, :256], p[:, 256:384], p[:, 384:512]
    cos_t, sin_t = cs[:, :LANES], cs[:, LANES:]
    qn = q_c * lax.rsqrt(jnp.mean(q_c * q_c, -1, keepdims=True) + RMS_EPS) * qg
    kvn = kv_c * lax.rsqrt(jnp.mean(kv_c * kv_c, -1, keepdims=True) + RMS_EPS) * kvg
    q = _rope(mm(qn, w_uq), cos_t, sin_t, perm)
    kv = mm(kvn, w_ukv)
    return q, kv, _rope(k_r, cos_t, sin_t, perm)


def f_rwkv_pre(u, w0, a0, k_k, k_a, w_lora, ones_blocks):
    r, k, v, lo = u[:, :WIDTH], u[:, WIDTH:2 * WIDTH], u[:, 2 * WIDTH:3 * WIDTH], u[:, 3 * WIDTH:]
    lane = lax.broadcasted_iota(jnp.int32, lo.shape, 1)
    dl = mm(jnp.where(lane < LORA, jnp.tanh(lo), lo), w_lora)
    w_log = -_softplus(-(w0 + dl[:, :WIDTH])) - 0.5
    decay = jnp.exp(-jnp.exp(w_log))
    a = jax.nn.sigmoid(a0 + dl[:, WIDTH:])
    kk = k * k_k
    kk = kk / jnp.maximum(jnp.sqrt(segsum(kk * kk, ones_blocks)), 1e-12)
    k2 = k * (1.0 + (a - 1.0) * k_a)
    return r, decay, k2, v, -kk, kk * a


def f_rwkv_post(y, r, k2, v, r_k, gn_g, gn_b, ones_blocks):
    yc = y - segsum(y, ones_blocks) * (1.0 / HEAD)
    yn = yc * lax.rsqrt(segsum(yc * yc, ones_blocks) * (1.0 / HEAD) + GN_EPS)
    return (yn * gn_g + gn_b + segsum(r * k2 * r_k, ones_blocks) * v,)


def f_merge(ya, gpa, yb, gpb, ma, mb, w_pa, w_pb):
    pa = mm(ya * _silu(gpa), w_pa)
    pb = mm(yb * _silu(gpb), w_pb)
    return (jax.nn.sigmoid(ma) * pa + jax.nn.sigmoid(mb) * pb,)


def f_loss(merged, x, tgt, gate, post_g, post_b, w_out):
    z = ALPHA * x + (1.0 + gate) * mm(merged, w_out)
    err = _layer_norm(z) * post_g + post_b - tgt
    lrow = 0.5 * jnp.mean(err * err, -1, keepdims=True)
    return (jnp.broadcast_to(lrow, (lrow.shape[0], LANES)),)


def f_adamw(w, g, m, v):
    m2 = ADAM_B1 * m + (1.0 - ADAM_B1) * g
    v2 = ADAM_B2 * v + (1.0 - ADAM_B2) * jnp.square(g)
    m_hat = m2 / (1.0 - ADAM_B1 ** ADAM_STEP)
    v_hat = v2 / (1.0 - ADAM_B2 ** ADAM_STEP)
    return -ADAM_LR * (m_hat / (jnp.sqrt(v_hat) + ADAM_EPS) + ADAM_WD * w), m2, v2


def _params(sem=("arbitrary",)):
    return pltpu.CompilerParams(dimension_semantics=sem, vmem_limit_bytes=VMEM_LIMIT)


def _row_spec(tr, a):
    return pl.BlockSpec((tr, a.shape[1]), lambda i: (i, 0))


def _full_spec(a):
    return pl.BlockSpec(a.shape, lambda i: (0,) * a.ndim)


def row_fwd(name, f, rows, params, consts, out_widths, tr, out_dtype=F32):
    n_rows = rows[0].shape[0]
    nr, npar, ncon = len(rows), len(params), len(consts)

    def body(*refs):
        rv = [r[...] for r in refs[:nr]]
        pv = [r[...] for r in refs[nr:nr + npar]]
        cv = [r[...] for r in refs[nr + npar:nr + npar + ncon]]
        outs = f(*rv, *pv, *cv)
        for o_ref, o in zip(refs[nr + npar + ncon:], outs):
            o_ref[...] = o.astype(o_ref.dtype)

    return pl.pallas_call(
        body, name=name, grid=(n_rows // tr,),
        in_specs=[_row_spec(tr, a) for a in rows] + [_full_spec(a) for a in list(params) + list(consts)],
        out_specs=[pl.BlockSpec((tr, w), lambda i: (i, 0)) for w in out_widths],
        out_shape=[jax.ShapeDtypeStruct((n_rows, w), out_dtype) for w in out_widths],
        compiler_params=_params(),
    )(*rows, *params, *consts)


def row_bwd(name, f, rows, n_diff, params, consts, douts, tr, add_rows=None):
    n_rows = rows[0].shape[0]
    douts = [d if isinstance(d, (tuple, list)) else (d,) for d in douts]
    counts = [len(d) for d in douts]
    flat_d = [a for d in douts for a in d]
    add_rows = add_rows or [None] * n_diff
    adds = [a for a in add_rows if a is not None]
    nr, npar, ncon, nd, na = len(rows), len(params), len(consts), len(flat_d), len(adds)

    def body(*refs):
        o = 0
        rv = [r[...] for r in refs[o:o + nr]]; o += nr
        pv = [Weight(r[...], jnp.zeros(r.shape, F32)) if r.dtype == BF16 else r[...] for r in refs[o:o + npar]]
        o += npar
        cv = [r[...] for r in refs[o:o + ncon]]; o += ncon
        dv = []
        for cnt in counts:
            s = refs[o][...]
            for e in range(1, cnt):
                s = s + refs[o + e][...]
            dv.append(s)
            o += cnt
        add_v = [r[...] for r in refs[o:o + na]]; o += na
        drow_refs = refs[o:o + n_diff]; o += n_diff
        dpar_refs = refs[o:o + npar]

        def g(*args):
            return tuple(f(*args[:n_diff], *rv[n_diff:], *args[n_diff:], *cv))

        _, vjp = jax.vjp(g, *rv[:n_diff], *pv)
        grads = vjp(tuple(dv))
        ai = 0
        for j, (r, gr) in enumerate(zip(drow_refs, grads[:n_diff])):
            if add_rows[j] is not None:
                gr = gr + add_v[ai]
                ai += 1
            r[...] = gr

        @pl.when(pl.program_id(0) == 0)
        def _():
            for r in dpar_refs:
                r[...] = jnp.zeros_like(r)

        for r, gr in zip(dpar_refs, grads[n_diff:]):
            r[...] += gr.grad if isinstance(gr, Weight) else gr

    outs = pl.pallas_call(
        body, name=name, grid=(n_rows // tr,),
        in_specs=([_row_spec(tr, a) for a in rows] + [_full_spec(a) for a in list(params) + list(consts)]
                  + [_row_spec(tr, a) for a in flat_d + adds]),
        out_specs=[_row_spec(tr, a) for a in rows[:n_diff]] + [_full_spec(a) for a in params],
        out_shape=([jax.ShapeDtypeStruct(a.shape, F32) for a in rows[:n_diff]]
                   + [jax.ShapeDtypeStruct(a.shape, F32) for a in params]),
        compiler_params=_params(),
    )(*rows, *params, *consts, *flat_d, *adds)
    return outs[:n_diff], outs[n_diff:]


def shift_stage_fwd(name, f, p, mu, params, consts, out_widths, tr):
    n_rows, w = p.shape
    npar, ncon = len(params), len(consts)

    def body(*refs):
        p_ref, mu_ref = refs[:2]
        pv = [r[...] for r in refs[2:2 + npar]]
        cv = [r[...] for r in refs[2 + npar:2 + npar + ncon]]
        u_ref, out_refs, carry = refs[2 + npar + ncon], refs[3 + npar + ncon:-1], refs[-1]

        @pl.when(pl.program_id(0) == 0)
        def _():
            carry[...] = jnp.zeros_like(carry)

        x = p_ref[...]
        rolled = pltpu.roll(x, 1, 0)
        head = pltpu.roll(carry[...], 1, 0)
        fixed = jnp.concatenate([head, rolled[SUBLANES:]], axis=0)
        row = lax.broadcasted_iota(jnp.int32, x.shape, 0)
        prev = jnp.where(row == 0, fixed, rolled)
        u = x + (prev - x) * mu_ref[...]
        u_ref[...] = u
        carry[...] = x[tr - SUBLANES:]
        for o_ref, o in zip(out_refs, f(u, *pv, *cv)):
            o_ref[...] = o

    return pl.pallas_call(
        body, name=name, grid=(n_rows // tr,),
        in_specs=[_row_spec(tr, p), _full_spec(mu)] + [_full_spec(a) for a in list(params) + list(consts)],
        out_specs=[_row_spec(tr, p)] + [pl.BlockSpec((tr, ow), lambda i: (i, 0)) for ow in out_widths],
        out_shape=[jax.ShapeDtypeStruct(p.shape, F32)] + [jax.ShapeDtypeStruct((n_rows, ow), F32) for ow in out_widths],
        scratch_shapes=[pltpu.VMEM((SUBLANES, w), F32)],
        compiler_params=_params(),
    )(p, mu, *params, *consts)


def shift_stage_bwd(name, f, p, u, mu, params, consts, douts, tr):
    n_rows, w = p.shape
    nb = n_rows // tr
    douts = [d if isinstance(d, (tuple, list)) else (d,) for d in douts]
    counts = [len(d) for d in douts]
    flat_d = [a for d in douts for a in d]
    npar, ncon, nd = len(params), len(consts), len(flat_d)

    def body(*refs):
        p_ref, u_ref, mu_ref = refs[:3]
        o = 3
        pv = [Weight(r[...], jnp.zeros(r.shape, F32)) if r.dtype == BF16 else r[...] for r in refs[o:o + npar]]
        o += npar
        cv = [r[...] for r in refs[o:o + ncon]]; o += ncon
        dv = []
        for cnt in counts:
            s = refs[o][...]
            for e in range(1, cnt):
                s = s + refs[o + e][...]
            dv.append(s)
            o += cnt
        dp_ref, dmu_ref = refs[o], refs[o + 1]
        dpar_refs, carry = refs[o + 2:o + 2 + npar], refs[-1]

        @pl.when(pl.program_id(0) == 0)
        def _():
            carry[...] = jnp.zeros_like(carry)
            dmu_ref[...] = jnp.zeros_like(dmu_ref)
            for r in dpar_refs:
                r[...] = jnp.zeros_like(r)

        _, vjp = jax.vjp(lambda uu, *pp: tuple(f(uu, *pp, *cv)), u_ref[...], *pv)
        grads = vjp(tuple(dv))
        for r, gr in zip(dpar_refs, grads[1:]):
            r[...] += gr.grad if isinstance(gr, Weight) else gr

        d = grads[0]
        rolled = pltpu.roll(d, tr - 1, 0)
        tail = pltpu.roll(carry[...], SUBLANES - 1, 0)
        fixed = jnp.concatenate([rolled[:tr - SUBLANES], tail], axis=0)
        row = lax.broadcasted_iota(jnp.int32, d.shape, 0)
        nxt = jnp.where(row == tr - 1, fixed, rolled)
        mu_v = mu_ref[...]
        dp_ref[...] = d * (1.0 - mu_v) + nxt * mu_v
        dmu_ref[...] += jnp.sum(p_ref[...] * (nxt - d), axis=0, keepdims=True)
        carry[...] = d[:SUBLANES]

    rev = lambda i: (nb - 1 - i, 0)
    rows_rev = lambda a: pl.BlockSpec((tr, a.shape[1]), rev)
    outs = pl.pallas_call(
        body, name=name, grid=(nb,),
        in_specs=([rows_rev(p), rows_rev(u), _full_spec(mu)] + [_full_spec(a) for a in list(params) + list(consts)]
                  + [rows_rev(a) for a in flat_d]),
        out_specs=[rows_rev(p), _full_spec(mu)] + [_full_spec(a) for a in params],
        out_shape=([jax.ShapeDtypeStruct(p.shape, F32), jax.ShapeDtypeStruct(mu.shape, F32)]
                   + [jax.ShapeDtypeStruct(a.shape, F32) for a in params]),
        scratch_shapes=[pltpu.VMEM((SUBLANES, w), F32)],
        compiler_params=_params(),
    )(p, u, mu, *params, *consts, *flat_d)
    return outs[0], outs[1], outs[2:]


ATT_T = 256


def _att_rows(j):
    return pl.ds(pl.multiple_of(j * ATT_T, ATT_T), ATT_T)


def _att_prep(kv_ref, kpe_ref, kf_scr, vf_scr, n_blocks):
    lane = lax.broadcasted_iota(jnp.int32, (ATT_T, LANES), 1)

    def prep(j, _):
        rows = _att_rows(j)
        kv = kv_ref[rows, :]
        kf_scr[rows, :] = _bf(jnp.where(lane < HEAD, kv, kpe_ref[rows, :]))
        vf_scr[rows, :] = _bf(jnp.where(lane >= HEAD, kv, 0.0))
        return 0

    lax.fori_loop(0, n_blocks, prep, 0)


def _att_diag_mask():
    shift = CHUNK.bit_length() - 1
    qc = jnp.right_shift(lax.broadcasted_iota(jnp.int32, (ATT_T, ATT_T), 0), shift)
    kc = jnp.right_shift(lax.broadcasted_iota(jnp.int32, (ATT_T, ATT_T), 1), shift)
    return kc <= qc


def _wide(x):
    return jnp.concatenate([x] * (ATT_T // LANES), axis=1)


def attn_fwd(q, kv, kpe):
    seq = q.shape[0]
    nb = seq // ATT_T
    assert seq % (2 * ATT_T) == 0, "blocks are taken two per trip"
    scale = MLA_QK ** -0.5

    def body(q_ref, kv_ref, kpe_ref, o_ref, lse_ref, kf_scr, vf_scr):
        _att_prep(kv_ref, kpe_ref, kf_scr, vf_scr, nb)
        mask = _att_diag_mask()

        def scores(qb, kj):
            return _dot(qb, kf_scr[_att_rows(kj), :], 1, 1) * scale

        def update(s, kj, carry, masked):
            m, l, acc = carry
            if masked:
                s = jnp.where(mask, s, NEG)
            m_new = jnp.maximum(m, jnp.broadcast_to(jnp.max(s, -1, keepdims=True), m.shape))
            alpha = jnp.exp(m - m_new)
            p = jnp.exp(s - _wide(m_new))
            l = alpha * l + jnp.broadcast_to(jnp.sum(p, -1, keepdims=True), l.shape)
            acc = alpha * acc + _dot(_bf(p), vf_scr[_att_rows(kj), :], 1, 0)
            return m_new, l, acc

        def finish(rows, carry):
            m, l, acc = carry
            o_ref[rows, :] = acc / l
            lse_ref[rows, :] = m + jnp.log(l)

        def q_pair(qp, _):
            rows_a, rows_b = _att_rows(2 * qp), _att_rows(2 * qp + 1)
            qa, qb = _bf(q_ref[rows_a, :]), _bf(q_ref[rows_b, :])
            init = (jnp.full((ATT_T, LANES), NEG, F32), jnp.zeros((ATT_T, LANES), F32),
                    jnp.zeros((ATT_T, LANES), F32))

            def trip(kj, c):
                ca, cb, sa, sb = c
                sa_next, sb_next = scores(qa, kj + 1), scores(qb, kj + 1)
                return update(sa, kj, ca, False), update(sb, kj, cb, False), sa_next, sb_next

            ca, cb, sa, sb = lax.fori_loop(0, 2 * qp, trip, (init, init, scores(qa, 0), scores(qb, 0)))
            sb_last = scores(qb, 2 * qp + 1)
            ca = update(sa, 2 * qp, ca, True)
            cb = update(sb_last, 2 * qp + 1, update(sb, 2 * qp, cb, False), True)
            finish(rows_a, ca)
            finish(rows_b, cb)
            return 0

        lax.fori_loop(0, nb // 2, q_pair, 0)

    head = pl.BlockSpec((seq, LANES), lambda h: (0, h))
    return pl.pallas_call(
        body, name="attn_fwd", grid=(HEADS,),
        in_specs=[head, head, pl.BlockSpec((seq, LANES), lambda h: (0, 0))],
        out_specs=[head, head],
        out_shape=[jax.ShapeDtypeStruct((seq, HEADS * LANES), F32)] * 2,
        scratch_shapes=[pltpu.VMEM((seq, LANES), BF16)] * 2,
        compiler_params=_params(),
    )(q, kv, kpe)


def attn_bwd(q, kv, kpe, o, lse, do):
    seq = q.shape[0]
    nb = seq // ATT_T
    assert seq % (2 * ATT_T) == 0, "blocks are taken two per trip"
    scale = MLA_QK ** -0.5

    def body(q_ref, kv_ref, kpe_ref, o_ref, lse_ref, do_ref, dq_ref, dkv_ref, dkpe_ref,
             kf_scr, vf_scr, qb_scr, dob_scr, dsum):
        lane = lax.broadcasted_iota(jnp.int32, (ATT_T, LANES), 1)

        @pl.when(pl.program_id(0) == 0)
        def _():
            dkpe_ref[...] = jnp.zeros_like(dkpe_ref)

        dq_ref[...] = jnp.zeros_like(dq_ref)
        _att_prep(kv_ref, kpe_ref, kf_scr, vf_scr, nb)

        def pre(j, _):
            rows = _att_rows(j)
            d = do_ref[rows, :]
            qb_scr[rows, :] = _bf(q_ref[rows, :])
            dob_scr[rows, :] = _bf(d)
            dsum[rows, :] = jnp.broadcast_to(jnp.sum(d * o_ref[rows, :], -1, keepdims=True), (ATT_T, LANES))
            return 0

        lax.fori_loop(0, nb, pre, 0)
        mask = _att_diag_mask()

        def front(kf, vf, qi):
            rows = _att_rows(qi)
            return _dot(qb_scr[rows, :], kf, 1, 1), _dot(dob_scr[rows, :], vf, 1, 1)

        def back(kf, qi, fr, carry, masked):
            s, dp = fr
            dk, dv = carry
            rows = _att_rows(qi)
            qb, dob = qb_scr[rows, :], dob_scr[rows, :]
            p = jnp.exp(s * scale - _wide(lse_ref[rows, :]))
            if masked:
                p = jnp.where(mask, p, 0.0)
            ds = _bf(p * (dp - _wide(dsum[rows, :])) * scale)
            return (dk + _dot(ds, qb, 0, 0), dv + _dot(_bf(p), dob, 0, 0)), _dot(ds, kf, 1, 0)

        def store(krows, carry):
            dk, dv = carry
            dkv_ref[krows, :] = jnp.where(lane < HEAD, dk, dv)
            dkpe_ref[krows, :] += jnp.where((lane >= HEAD) & (lane < MLA_QK), dk, 0.0)

        def k_pair(kp, _):
            ka, kb = 2 * kp, 2 * kp + 1
            rows_a, rows_b = _att_rows(ka), _att_rows(kb)
            kfa, vfa, kfb, vfb = kf_scr[rows_a, :], vf_scr[rows_a, :], kf_scr[rows_b, :], vf_scr[rows_b, :]
            zero = jnp.zeros((ATT_T, LANES), F32)
            ca, dq_a = back(kfa, ka, front(kfa, vfa, ka), (zero, zero), True)
            dq_ref[rows_a, :] += dq_a
            ca, dq_a = back(kfa, kb, front(kfa, vfa, kb), ca, False)
            cb, dq_b = back(kfb, kb, front(kfb, vfb, kb), (zero, zero), True)
            dq_ref[rows_b, :] += dq_a + dq_b

            def both(qi, c):
                ca, cb, fa, fb = c
                nxt = jnp.minimum(qi + 1, nb - 1)
                fa_next, fb_next = front(kfa, vfa, nxt), front(kfb, vfb, nxt)
                ca, dq_a = back(kfa, qi, fa, ca, False)
                cb, dq_b = back(kfb, qi, fb, cb, False)
                dq_ref[_att_rows(qi), :] += dq_a + dq_b
                return ca, cb, fa_next, fb_next

            first = jnp.minimum(kb + 1, nb - 1)
            ca, cb, _, _ = lax.fori_loop(kb + 1, nb, both, (ca, cb, front(kfa, vfa, first), front(kfb, vfb, first)))
            store(rows_a, ca)
            store(rows_b, cb)
            return 0

        lax.fori_loop(0, nb // 2, k_pair, 0)

    head = pl.BlockSpec((seq, LANES), lambda h: (0, h))
    shared = pl.BlockSpec((seq, LANES), lambda h: (0, 0))
    return pl.pallas_call(
        body, name="attn_bwd", grid=(HEADS,),
        in_specs=[head, head, shared, head, head, head],
        out_specs=[head, head, shared],
        out_shape=[jax.ShapeDtypeStruct((seq, HEADS * LANES), F32)] * 2
        + [jax.ShapeDtypeStruct((seq, LANES), F32)],
        scratch_shapes=[pltpu.VMEM((seq, LANES), BF16)] * 4 + [pltpu.VMEM((seq, LANES), F32)],
        compiler_params=_params(),
    )(q, kv, kpe, o, lse, do)


WKV_TB = 128
WKV_GROUP = SUBLANES
WKV_HALF = WIDTH // 2


def _wkv_consts():
    row = lax.broadcasted_iota(jnp.int32, (HEAD, WKV_HALF), 0)
    lane = lax.broadcasted_iota(jnp.int32, (HEAD, WKV_HALF), 1)
    diag = row == jnp.bitwise_and(lane, HEAD - 1)
    sub = lax.broadcasted_iota(jnp.int32, (WKV_GROUP, WKV_HALF), 0)
    return diag, sub


def _halves(x):
    return [x[:, :WKV_HALF], x[:, WKV_HALF:]]


def _diag_rows(row, diag):
    return _bf(jnp.where(diag, jnp.broadcast_to(row, diag.shape), 0.0))


def _put_row(tile, row, i, sub):
    return jnp.where(sub == i, jnp.broadcast_to(row, tile.shape), tile)


def _col_sum(x):
    return jnp.sum(x, axis=0, keepdims=True)


def _step(x, i):
    return x[i * HEAD:(i + 1) * HEAD]


def _head_dots(prods, ones_b, sub):
    tile = jnp.zeros((WKV_GROUP, WKV_HALF), F32)
    for i, p in enumerate(prods):
        tile = _put_row(tile, p, i, sub)
    res = _exact_dot(tile, ones_b, 0)
    return [res[i:i + 1] for i in range(len(prods))]


def wkv_fwd(r, w, k, v, a, b, ones_half):
    seq = r.shape[0]

    def body(r_ref, w_ref, k_ref, v_ref, a_ref, b_ref, ones_ref, y_ref, st_ref, s_scr):
        @pl.when(pl.program_id(0) == 0)
        def _():
            s_scr[...] = jnp.zeros_like(s_scr)

        ones_b = ones_ref[...]
        diag, sub = _wkv_consts()

        ng = WKV_TB // WKV_GROUP
        last = WKV_GROUP - 2

        def rows_of(g):
            return pl.ds(pl.multiple_of(g * WKV_GROUP, WKV_GROUP), WKV_GROUP)

        def pair_rows(x8, t):
            return jnp.concatenate([_diag_rows(x8[t:t + 1], diag), _diag_rows(x8[t + 1:t + 2], diag)], axis=0)

        def put_y(g, pairs_y):
            tile = _halves(y_ref[rows_of(g), :])
            for hf in range(2):
                tile[hf] = _put_row(_put_row(tile[hf], pairs_y[hf][0], last, sub), pairs_y[hf][1], last + 1, sub)
            y_ref[rows_of(g), :] = jnp.concatenate(tile, axis=1)

        def read_out(yexp):
            return _col_sum(jnp.where(diag, yexp[:HEAD], 0.0)), _col_sum(jnp.where(diag, yexp[HEAD:], 0.0))

        def group(g, carry):
            state, v_cur, read = (list(c) for c in carry)
            base = pl.multiple_of(g * WKV_GROUP, WKV_GROUP)
            rows = rows_of(g)
            r8, w8, k8, v8, a8, b8 = (_halves(ref[rows, :]) for ref in (r_ref, w_ref, k_ref, v_ref, a_ref, b_ref))
            v_after = _halves(v_ref[rows_of(jnp.minimum(g + 1, ng - 1)), :])
            evens = range(0, WKV_GROUP, 2)
            dots = [_head_dots([b8[hf][t:t + 1] * a8[hf][t + 1:t + 2] for t in evens]
                               + [k8[hf][t:t + 1] * a8[hf][t + 1:t + 2] for t in evens], ones_b, sub) for hf in range(2)]
            y8 = [jnp.zeros((WKV_GROUP, WKV_HALF), F32)] * 2
            y_before = [None, None]
            for t in evens:
                s0, s1 = slice(t, t + 1), slice(t + 1, t + 2)
                both = []
                for hf in range(2):
                    s_in = state[hf]
                    v_next = pair_rows(v8[hf], t + 2) if t < last else pair_rows(v_after[hf], 0)
                    res = _dot(jnp.concatenate([_bf(s_in * a8[hf][s0]), _bf(s_in * (w8[hf][s0] * a8[hf][s1])),
                                                v_next, read[hf]], axis=0), ones_b, 1, 0)
                    sa0, v0, v1 = res[:HEAD], v_cur[hf][:HEAD], v_cur[hf][HEAD:]
                    st0 = s_in * w8[hf][s0] + sa0 * b8[hf][s0] + v0 * k8[hf][s0]
                    sa1 = res[HEAD:2 * HEAD] + sa0 * dots[hf][t // 2] + v0 * dots[hf][WKV_GROUP // 2 + t // 2]
                    st1 = st0 * w8[hf][s1] + sa1 * b8[hf][s1] + v1 * k8[hf][s1]
                    both.append((st0, st1))
                    state[hf], v_cur[hf] = st1, res[2 * HEAD:4 * HEAD]
                    read[hf] = jnp.concatenate([_bf(st0 * r8[hf][s0]), _bf(st1 * r8[hf][s1])], axis=0)
                    ya, yb = read_out(res[4 * HEAD:])
                    if t == 0:
                        y_before[hf] = (ya, yb)
                    else:
                        y8[hf] = _put_row(_put_row(y8[hf], ya, t - 2, sub), yb, t - 1, sub)
                for j in range(2):
                    st_ref[base + t + j] = jnp.concatenate([both[0][j], both[1][j]], axis=1)
            y_ref[rows, :] = jnp.concatenate(y8, axis=1)
            put_y(jnp.maximum(g - 1, 0), y_before)
            return tuple(state), tuple(v_cur), tuple(read)

        v_first = _halves(v_ref[rows_of(0), :])
        init = (tuple(_halves(s_scr[...])),
                tuple(_dot(pair_rows(v_first[hf], 0), ones_b, 1, 0) for hf in range(2)),
                tuple(jnp.zeros((2 * HEAD, WKV_HALF), BF16) for _ in range(2)))
        fin, _, read = lax.fori_loop(0, ng, group, init)
        put_y(ng - 1, [read_out(_dot(read[hf], ones_b, 1, 0)) for hf in range(2)])
        s_scr[...] = jnp.concatenate(fin, axis=1)

    vec = pl.BlockSpec((WKV_TB, WIDTH), lambda i: (i, 0))
    return pl.pallas_call(
        body, name="wkv_fwd", grid=(seq // WKV_TB,),
        in_specs=[vec] * 6 + [_full_spec(ones_half)],
        out_specs=[vec, pl.BlockSpec((WKV_TB, HEAD, WIDTH), lambda i: (i, 0, 0))],
        out_shape=[jax.ShapeDtypeStruct((seq, WIDTH), F32), jax.ShapeDtypeStruct((seq, HEAD, WIDTH), F32)],
        scratch_shapes=[pltpu.VMEM((HEAD, WIDTH), F32)],
        compiler_params=_params(),
    )(r, w, k, v, a, b, ones_half)


def wkv_bwd(r, w, k, v, a, b, dy, states, ones_half):
    seq = r.shape[0]
    nb = seq // WKV_TB
    ng = WKV_TB // WKV_GROUP

    def body(r_ref, w_ref, k_ref, v_ref, a_ref, b_ref, dy_ref, st_ref, halo_ref, ones_ref,
             dr_ref, dw_ref, dk_ref, dv_ref, da_ref, db_ref, ds_scr):
        blk = nb - 1 - pl.program_id(0)

        @pl.when(pl.program_id(0) == 0)
        def _():
            ds_scr[...] = jnp.zeros_like(ds_scr)

        ones_b = ones_ref[...]
        diag, sub = _wkv_consts()
        before_block = jnp.where(blk == 0, 0.0, halo_ref[0])

        def rows_of(g):
            return pl.ds(pl.multiple_of(g * WKV_GROUP, WKV_GROUP), WKV_GROUP)

        def expand_rows(hf, dy8, v8, a8, t, s_t, s_u):
            s1, s0 = slice(t, t + 1), slice(t - 1, t)
            return jnp.concatenate([_diag_rows(dy8[hf][s1], diag), _diag_rows(dy8[hf][s0], diag),
                                    _diag_rows(v8[hf][s1], diag), _diag_rows(v8[hf][s0], diag),
                                    _bf(s_t[hf] * a8[hf][s1]), _bf(s_u[hf] * a8[hf][s0])], axis=0)

        def read_out(x):
            return _col_sum(jnp.where(diag, x[:HEAD], 0.0)), _col_sum(jnp.where(diag, x[HEAD:], 0.0))

        def put_dv(g, pair_dv):
            tile = _halves(dv_ref[rows_of(g), :])
            for hf in range(2):
                tile[hf] = _put_row(_put_row(tile[hf], pair_dv[hf][0], 1, sub), pair_dv[hf][1], 0, sub)
            dv_ref[rows_of(g), :] = jnp.concatenate(tile, axis=1)

        def group(gg, carry):
            dstate, e_cur, dv_pend = (list(c) for c in carry)
            g = ng - 1 - gg
            base = pl.multiple_of(g * WKV_GROUP, WKV_GROUP)
            rows = rows_of(g)
            r8, w8, k8, v8, a8, b8, dy8 = (
                _halves(ref[rows, :]) for ref in (r_ref, w_ref, k_ref, v_ref, a_ref, b_ref, dy_ref))
            g_next = jnp.maximum(g - 1, 0)
            base_next = pl.multiple_of(g_next * WKV_GROUP, WKV_GROUP)
            dy8n, v8n, a8n = (_halves(ref[rows_of(g_next), :]) for ref in (dy_ref, v_ref, a_ref))
            zero8 = jnp.zeros((WKV_GROUP, WKV_HALF), F32)
            out = {n: [zero8, zero8] for n in ("dr", "dw", "dk", "dv", "da", "db")}
            before_group = jnp.where(g == 0, before_block, st_ref[jnp.maximum(base - 1, 0)])
            states = [_halves(before_group)] + [_halves(st_ref[base + i]) for i in range(WKV_GROUP)]
            odds = range(1, WKV_GROUP, 2)
            dots = [_head_dots([a8[hf][t:t + 1] * b8[hf][t - 1:t] for t in odds]
                               + [r8[hf][t - 1:t] * b8[hf][t - 1:t] for t in odds], ones_b, sub) for hf in range(2)]
            dv_after = [None, None]

            def emit(hf, i, d_i, dsa_i, dy_i, v_i, sa_i):
                s_p, s_t = states[i][hf], states[i + 1][hf]
                for n, val in (("dr", _col_sum(s_t * dy_i)), ("dw", _col_sum(d_i * s_p)), ("db", _col_sum(d_i * sa_i)),
                               ("da", _col_sum(s_p * dsa_i)), ("dk", _col_sum(d_i * v_i))):
                    out[n][hf] = _put_row(out[n][hf], val, i, sub)

            for t in reversed(odds):
                s1, s0 = slice(t, t + 1), slice(t - 1, t)
                for hf in range(2):
                    dy1, dy0, v1, v0, sa1, sa0 = (_step(e_cur[hf], j) for j in range(6))
                    d1 = dstate[hf] + dy1 * r8[hf][s1]
                    if t > 1:
                        nxt = expand_rows(hf, dy8, v8, a8, t - 2, states[t - 2], states[t - 3])
                    else:
                        nxt = expand_rows(hf, dy8n, v8n, a8n, WKV_GROUP - 1, _halves(st_ref[base_next + WKV_GROUP - 2]),
                                          _halves(st_ref[base_next + WKV_GROUP - 3]))
                    res = _dot(jnp.concatenate([_bf(d1 * b8[hf][s1]), _bf(d1 * (w8[hf][s1] * b8[hf][s0])),
                                                nxt, dv_pend[hf]], axis=0), ones_b, 1, 0)
                    dsa1 = res[:HEAD]
                    d0 = d1 * w8[hf][s1] + dsa1 * a8[hf][s1] + dy0 * r8[hf][s0]
                    dsa0 = res[HEAD:2 * HEAD] + dsa1 * dots[hf][t // 2] + dy0 * dots[hf][WKV_GROUP // 2 + t // 2]
                    dstate[hf] = d0 * w8[hf][s0] + dsa0 * a8[hf][s0]
                    e_cur[hf] = res[2 * HEAD:8 * HEAD]
                    dv_pend[hf] = jnp.concatenate([_bf(d1 * k8[hf][s1]), _bf(d0 * k8[hf][s0])], axis=0)
                    emit(hf, t, d1, dsa1, dy1, v1, sa1)
                    emit(hf, t - 1, d0, dsa0, dy0, v0, sa0)
                    dv_a, dv_b = read_out(res[8 * HEAD:])
                    if t == WKV_GROUP - 1:
                        dv_after[hf] = (dv_a, dv_b)
                    else:
                        out["dv"][hf] = _put_row(_put_row(out["dv"][hf], dv_a, t + 2, sub), dv_b, t + 1, sub)
            for ref, n in ((dr_ref, "dr"), (dw_ref, "dw"), (dk_ref, "dk"), (dv_ref, "dv"), (da_ref, "da"), (db_ref, "db")):
                ref[rows, :] = jnp.concatenate(out[n], axis=1)
            put_dv(jnp.minimum(g + 1, ng - 1), dv_after)
            return tuple(dstate), tuple(e_cur), tuple(dv_pend)

        top = rows_of(ng - 1)
        dy8t, v8t, a8t = (_halves(ref[top, :]) for ref in (dy_ref, v_ref, a_ref))
        s_t, s_u = _halves(st_ref[WKV_TB - 2]), _halves(st_ref[WKV_TB - 3])
        init = (tuple(_halves(ds_scr[...])),
                tuple(_dot(expand_rows(hf, dy8t, v8t, a8t, WKV_GROUP - 1, s_t, s_u), ones_b, 1, 0) for hf in range(2)),
                tuple(jnp.zeros((2 * HEAD, WKV_HALF), BF16) for _ in range(2)))
        fin, _, dv_pend = lax.fori_loop(0, ng, group, init)
        put_dv(0, [read_out(_dot(dv_pend[hf], ones_b, 1, 0)) for hf in range(2)])
        ds_scr[...] = jnp.concatenate(fin, axis=1)

    vec = pl.BlockSpec((WKV_TB, WIDTH), lambda i: (nb - 1 - i, 0))
    return pl.pallas_call(
        body, name="wkv_bwd", grid=(nb,),
        in_specs=[vec] * 7 + [
            pl.BlockSpec((WKV_TB, HEAD, WIDTH), lambda i: (nb - 1 - i, 0, 0)),
            pl.BlockSpec((1, HEAD, WIDTH), lambda i: (jnp.maximum((nb - 1 - i) * WKV_TB - 1, 0), 0, 0)),
            _full_spec(ones_half)],
        out_specs=[vec] * 6,
        out_shape=[jax.ShapeDtypeStruct((seq, WIDTH), F32)] * 6,
        scratch_shapes=[pltpu.VMEM((HEAD, WIDTH), F32)],
        compiler_params=_params(),
    )(r, w, k, v, a, b, dy, states, states, ones_half)


def ada_fwd(c8, b_ada, gathered):
    cols = 3 * D_MODEL // 4

    def body(c_ref, b_ref, w_ref, o_ref):
        @pl.when(pl.program_id(1) == 0)
        def _():
            o_ref[...] = jnp.broadcast_to(b_ref[...], o_ref.shape)

        o_ref[...] += mm(_silu(c_ref[...]), w_ref[0])

    return pl.pallas_call(
        body, name="ada_fwd", grid=(4, D_MODEL // PACK_ROWS),
        in_specs=[pl.BlockSpec((SUBLANES, PACK_ROWS), lambda s, i: (0, i)),
                  pl.BlockSpec((1, cols), lambda s, i: (0, s)),
                  pl.BlockSpec((1, PACK_ROWS, cols), lambda s, i: (2 * s, 0, i))],
        out_specs=pl.BlockSpec((SUBLANES, cols), lambda s, i: (0, s)),
        out_shape=jax.ShapeDtypeStruct((SUBLANES, 3 * D_MODEL), F32),
        compiler_params=_params(("arbitrary", "arbitrary")),
    )(c8, b_ada, gathered)


def ada_grad_shard(sc_cols, dada_rows):
    n = len(sc_cols)

    def body(*refs):
        d_ref, o_ref = refs[n], refs[n + 1]
        acc = refs[0][...] * d_ref[0:1, :]
        for b in range(1, n):
            acc = acc + refs[b][...] * d_ref[b:b + 1, :]
        o_ref[...] = acc

    return pl.pallas_call(
        body, name="ada_grad_shard",
        out_shape=jax.ShapeDtypeStruct((sc_cols[0].shape[0], dada_rows.shape[1]), F32),
        compiler_params=pltpu.CompilerParams(vmem_limit_bytes=VMEM_LIMIT),
    )(*sc_cols, dada_rows)


def sum_slots(buf, tr):
    n, rows, cols = buf.shape

    def body(b_ref, o_ref):
        acc = b_ref[0].astype(F32)
        for s in range(1, n):
            acc = acc + b_ref[s].astype(F32)
        o_ref[...] = acc

    return pl.pallas_call(
        body, name="sum_slots", grid=(rows // tr,),
        in_specs=[pl.BlockSpec((n, tr, cols), lambda i: (0, i, 0))],
        out_specs=pl.BlockSpec((tr, cols), lambda i: (i, 0)),
        out_shape=jax.ShapeDtypeStruct((rows, cols), F32),
        compiler_params=_params(),
    )(buf)


def adamw_small(gathered, w, m, v):
    n = gathered.shape[0]

    def body(g_ref, w_ref, m_ref, v_ref, go_ref, d_ref, mo_ref, vo_ref):
        g = g_ref[0]
        for s in range(1, n):
            g = g + g_ref[s]
        go_ref[...] = g
        d_ref[...], mo_ref[...], vo_ref[...] = f_adamw(w_ref[...], g, m_ref[...], v_ref[...])

    return pl.pallas_call(
        body, name="adamw_small",
        out_shape=[jax.ShapeDtypeStruct(w.shape, F32)] * 4,
        compiler_params=pltpu.CompilerParams(vmem_limit_bytes=VMEM_LIMIT),
    )(gathered, w, m, v)


def _coords():
    return lax.axis_index("x"), lax.axis_index("y"), lax.axis_index("c")


def _flip(v, bit):
    return 1 - v if bit else v


def _hbm_call(body, name, out_shape, n_sems, *args):
    any_spec = pl.BlockSpec(memory_space=pl.ANY)
    return pl.pallas_call(
        body, name=name, out_shape=out_shape,
        in_specs=[any_spec] * len(args), out_specs=any_spec,
        scratch_shapes=[pltpu.SemaphoreType.DMA((n_sems,)), pltpu.SemaphoreType.DMA((n_sems,)),
                        pltpu.SemaphoreType.DMA],
    )(*args)


def all_gather8(name, block):
    def body(x_ref, out_ref, send_sems, recv_sems, local_sem):
        x, y, c = _coords()
        me, sibling = (x, y, c), (x, y, 1 - c)
        x_nbr, y_nbr, diagonal = (1 - x, y), (x, 1 - y), (1 - x, 1 - y)
        relay_from = (c * x + (1 - c) * (1 - x), c * (1 - y) + (1 - c) * y)
        relay_to = (c * (1 - x) + (1 - c) * x, c * y + (1 - c) * (1 - y))

        def slot(px, py, pc):
            return out_ref.at[4 * px + 2 * py + pc]

        def copy(k, blk, to, src=None):
            return pltpu.make_async_remote_copy(
                src_ref=slot(*blk) if src is None else src, dst_ref=slot(*blk),
                send_sem=send_sems.at[k], recv_sem=recv_sems.at[k], device_id=to, device_id_type=MESH)

        mine = pltpu.make_async_copy(x_ref, slot(*me), local_sem)
        mine.start()
        first = [copy(0, me, sibling, src=x_ref), copy(1, me, (*x_nbr, c), src=x_ref), copy(2, me, (*y_nbr, c), src=x_ref)]
        for cp in first:
            cp.start()
        copy(1, (*x_nbr, c), me).wait_recv()
        copy(2, (*y_nbr, c), me).wait_recv()
        later = [copy(3, (*relay_from, c), (*relay_to, c)), copy(4, (*x_nbr, c), sibling), copy(5, (*y_nbr, c), sibling)]
        for cp in later:
            cp.start()
        copy(3, (*diagonal, c), me).wait_recv()
        last = copy(6, (*diagonal, c), sibling)
        last.start()
        copy(0, sibling, me).wait_recv()
        for k, chip in ((4, x_nbr), (5, y_nbr), (6, diagonal)):
            copy(k, (*chip, 1 - c), me).wait_recv()
        for cp in first + later + [last]:
            cp.wait_send()
        mine.wait()

    return _hbm_call(body, name, jax.ShapeDtypeStruct((N_DEV,) + block.shape, block.dtype), 7, block)


def pair_swap(name, block):
    def body(x_ref, out_ref, send_sems, recv_sems, local_sem):
        x, y, c = _coords()
        cp = pltpu.make_async_remote_copy(
            src_ref=x_ref, dst_ref=out_ref, send_sem=send_sems.at[0], recv_sem=recv_sems.at[0],
            device_id=(x, y, 1 - c), device_id_type=MESH)
        cp.start()
        cp.wait_recv()
        cp.wait_send()

    return _hbm_call(body, name, jax.ShapeDtypeStruct(block.shape, block.dtype), 1, block)


def chip_all_to_all(name, buf):
    def body(x_ref, out_ref, send_sems, recv_sems, local_sem):
        x, y, c = _coords()
        me = 2 * x + y
        mine = pltpu.make_async_copy(x_ref.at[me], out_ref.at[me], local_sem)
        mine.start()
        copies = []
        for k in range(1, 4):
            px, py = _flip(x, k & 2), _flip(y, k & 1)
            copies.append(pltpu.make_async_remote_copy(
                src_ref=x_ref.at[2 * px + py], dst_ref=out_ref.at[me],
                send_sem=send_sems.at[k - 1], recv_sem=recv_sems.at[k - 1],
                device_id=(px, py, c), device_id_type=MESH))
        for cp in copies:
            cp.start()
        for cp in copies:
            cp.wait_recv()
        for cp in copies:
            cp.wait_send()
        mine.wait()

    return _hbm_call(body, name, jax.ShapeDtypeStruct(buf.shape, buf.dtype), 3, buf)


def _col_blocks(a, cols):
    a = jnp.pad(a, ((0, 0), (0, cols - a.shape[1])))
    return [a[i * PACK_ROWS:(i + 1) * PACK_ROWS] for i in range(a.shape[0] // PACK_ROWS)]


def _pack_shard(sh, dtype, with_ada=True):
    lora = jnp.concatenate([sh['w_decay_up'], sh['w_iclr_up']], axis=1)
    misc = jnp.concatenate([sh['w_ukv'], lora, jnp.zeros((LORA, 2 * LANES), lora.dtype)], axis=0)
    blocks = ((_col_blocks(sh['w_ada'], 768) if with_ada else [])
              + _col_blocks(sh['w_in'], 1408) + _col_blocks(sh['w_proj_a'], 256)
              + _col_blocks(sh['w_proj_b'], 256) + [sh['w_out']] + _col_blocks(sh['w_uq'], 256) + [misc])
    return jnp.concatenate([b.astype(dtype) for b in blocks], axis=1)


def _unpack_shard(p, with_ada=True):
    o = [0]

    def take(n_blocks, cols, used):
        blocks = [p[:, o[0] + i * cols:o[0] + (i + 1) * cols] for i in range(n_blocks)]
        o[0] += n_blocks * cols
        return jnp.concatenate(blocks, axis=0)[:, :used]

    out = {'w_ada': take(4, 768, 768)} if with_ada else {}
    out.update({'w_in': take(4, 1408, 1288), 'w_proj_a': take(2, 256, 256),
                'w_proj_b': take(2, 256, 256), 'w_out': take(1, 1024, 1024), 'w_uq': take(1, 256, 192)})
    misc = take(1, 256, 256)
    out['w_ukv'] = misc[:2 * LORA]
    out['w_decay_up'] = misc[2 * LORA:3 * LORA, :LANES]
    out['w_iclr_up'] = misc[2 * LORA:3 * LORA, LANES:]
    return out


def _pack_small(parts):
    flat = jnp.concatenate([p.reshape(-1) for p in parts])
    return jnp.pad(flat, (0, SMALL_ROWS * LANES - flat.shape[0])).reshape(SMALL_ROWS, LANES)


def _unpack_small(packed):
    flat, out, o = packed.reshape(-1), {}, 0
    for name, n in _SMALL:
        out[name] = flat[o:o + n]
        o += n
    return out


def _pad_heads_cols(w, used, left):
    k = w.shape[0]
    return jnp.pad(w.reshape(k, HEADS, used), ((0, 0), (0, 0), (left, LANES - used - left))).reshape(k, HEADS * LANES)


def _unpad_heads_cols(w, used, left):
    k = w.shape[0]
    return w.reshape(k, HEADS, LANES)[:, :, left:left + used].reshape(k, HEADS * used)


def kernel(x, c, positions, w_ada, b_ada, w_in, q_norm_g, w_uq, kv_norm_g, w_ukv, mu_rwkv, w0, w_decay_up, a0, w_iclr_up, k_k, k_a, r_k, gn_g, gn_b, w_proj_a, w_proj_b, w_out, post_g, post_b, loss_target, m_w_ada, m_b_ada, m_w_in, m_q_norm_g, m_w_uq, m_kv_norm_g, m_w_ukv, m_mu_rwkv, m_w0, m_w_decay_up, m_a0, m_w_iclr_up, m_k_k, m_k_a, m_r_k, m_gn_g, m_gn_b, m_w_proj_a, m_w_proj_b, m_w_out, m_post_g, m_post_b, v_w_ada, v_b_ada, v_w_in, v_q_norm_g, v_w_uq, v_kv_norm_g, v_w_ukv, v_mu_rwkv, v_w0, v_w_decay_up, v_a0, v_w_iclr_up, v_k_k, v_k_a, v_r_k, v_gn_g, v_gn_b, v_w_proj_a, v_w_proj_b, v_w_out, v_post_g, v_post_b):
    given = dict(locals())
    seq = x.shape[1]
    my_c = lax.axis_index("c")

    shard_names = [n for n, _, _ in _SHARDED]
    w_pack = _pack_shard({n: given[n][0] for n in shard_names}, BF16)
    my_half = lax.dynamic_slice_in_dim(w_pack, my_c * HALF_COLS, HALF_COLS, 1)
    gathered = all_gather8("gather_weights", my_half)
    shards = [_unpack_shard(jnp.concatenate([gathered[2 * s], gathered[2 * s + 1]], axis=1)) for s in range(4)]
    full = {n: jnp.concatenate([sh[n] for sh in shards], axis=ax) for n, _, ax in _SHARDED}

    wi = full['w_in']
    zcol = lambda n: jnp.zeros((D_MODEL, n), BF16)
    w_g1 = jnp.concatenate([wi[:, :384], zcol(HEAD), wi[:, 384:416], zcol(LANES - MLA_QK),
                            _pad_heads_cols(wi[:, 416:928], HEAD, HEAD)], axis=1)
    w_g2 = wi[:, 928:3104]
    w_g3 = wi[:, 3104:5152]
    w_uq_p = _pad_heads_cols(full['w_uq'], MLA_QK, 0)
    w_pa_p = jnp.pad(full['w_proj_a'].reshape(HEADS, HEAD, D_MODEL), ((0, 0), (HEAD, 0), (0, 0))).reshape(HEADS * LANES, D_MODEL)
    zl = jnp.zeros((LORA, WIDTH), BF16)
    w_lora = jnp.concatenate([jnp.concatenate([full['w_decay_up'], zl], 1),
                              jnp.concatenate([zl, full['w_iclr_up']], 1)], 0)

    hd = np.arange(WKV_HALF) // HEAD
    ones_half = jnp.asarray(hd[:, None] == hd[None, :], BF16)
    perm_np = np.zeros((LANES, LANES), np.float32)
    for d in range(MLA_ROPE // 2):
        perm_np[HEAD + 16 + d, HEAD + d] = -1.0
        perm_np[HEAD + d, HEAD + 16 + d] = 1.0
    perm = jnp.asarray(perm_np, BF16)
    inv = ROPE_THETA ** (-jnp.arange(0, MLA_ROPE, 2, dtype=F32) / MLA_ROPE)
    ang = positions[0].astype(F32)[:, None] * inv
    cos_a, sin_a = jnp.cos(ang), jnp.sin(ang)
    cs = jnp.concatenate([jnp.ones((seq, HEAD), F32), cos_a, cos_a, jnp.zeros((seq, LANES - MLA_QK), F32),
                          jnp.zeros((seq, HEAD), F32), sin_a, sin_a, jnp.zeros((seq, LANES - MLA_QK), F32)], axis=1)

    x2, tgt = x[0], loss_target[0]
    r_k2 = r_k.reshape(1, WIDTH)

    c8 = jnp.broadcast_to(c, (SUBLANES, D_MODEL))
    ada = ada_fwd(c8, b_ada, gathered)[:1]
    shift, scale, gate = ada[:, :D_MODEL], ada[:, D_MODEL:2 * D_MODEL], ada[:, 2 * D_MODEL:]

    f_in1, f_in2, f_in3 = _make_f_in((512, 1024)), _make_f_in((SHIFT_W, WIDTH)), _make_f_in((1024, 1024))
    tr = min(256, seq)
    p_mla, gpa, p_rwkv, gpb, ma, mb = row_fwd(
        "in_fwd", _make_f_in((512, 1024), (SHIFT_W, WIDTH), (1024, 1024)), [x2], [shift, scale, w_g1, w_g2, w_g3], [],
        [512, 1024, SHIFT_W, WIDTH, 1024, 1024], tr)

    mla_par = [q_norm_g, kv_norm_g, w_uq_p, full['w_ukv']]
    q_f, kv_f, kpe = row_fwd("mla_pre_fwd", f_mla_pre, [p_mla, cs], mla_par, [perm], [1024, 1024, LANES], tr)
    ya, lse = attn_fwd(q_f, kv_f, kpe)

    pre_par = [w0, a0, k_k, k_a, w_lora]
    u, rr, wd, k2, vv, an, bb = shift_stage_fwd("rwkv_pre_fwd", f_rwkv_pre, p_rwkv, mu_rwkv, pre_par, [ones_half],
                                                [WIDTH] * 6, tr)
    y_wkv, states = wkv_fwd(rr, wd, k2, vv, an, bb, ones_half)
    post_b_par = [r_k2, gn_g, gn_b]
    merge_par, loss_par = [w_pa_p, full['w_proj_b']], [gate, post_g, post_b, full['w_out']]

    def f_tail(ya_, gpa_, y_, r_, k_, v_, gpb_, ma_, mb_, x_, tgt_, r_k_, gn_g_, gn_b_, w_pa_, w_pb_,
               gate_, post_g_, post_b_, w_out_, ones_):
        yb_, = f_rwkv_post(y_, r_, k_, v_, r_k_, gn_g_, gn_b_, ones_)
        merged_, = f_merge(ya_, gpa_, yb_, gpb_, ma_, mb_, w_pa_, w_pb_)
        return (yb_, merged_) + f_loss(merged_, x_, tgt_, gate_, post_g_, post_b_, w_out_)

    yb, merged, lrows = row_fwd("tail_fwd", f_tail, [ya, gpa, y_wkv, rr, k2, vv, gpb, ma, mb, x2, tgt],
                                post_b_par + merge_par + loss_par, [ones_half], [WIDTH, D_MODEL, LANES], tr)
    merge_rows, loss_rows = [ya, gpa, yb, gpb, ma, mb], [merged, x2, tgt]
    loss = lax.psum(jnp.sum(lrows[:, 0]), ("x", "y", "c"))

    dl = jnp.broadcast_to((jnp.arange(LANES) == 0).astype(F32), (seq, LANES))
    (dmerged, dx_res), (dgate, dpost_g, dpost_b, dw_out) = row_bwd("loss_bwd", f_loss, loss_rows, 2, loss_par, [], [dl], tr)
    (dya, dgpa, dyb, dgpb, dma, dmb), (dw_pa_p, dw_pb) = row_bwd(
        "merge_bwd", f_merge, merge_rows, 6, merge_par, [], [dmerged], tr)

    (dy_wkv, dr1, dk1, dv1), (dr_k, dgn_g, dgn_b) = row_bwd(
        "rwkv_post_bwd", f_rwkv_post, [y_wkv, rr, k2, vv], 4, post_b_par, [ones_half], [dyb], tr)
    dr2, dwd, dk2, dv2, dan, dbb = wkv_bwd(rr, wd, k2, vv, an, bb, dy_wkv, states, ones_half)
    dp_rwkv, dmu, (dw0, da0, dk_k, dk_a, dw_lora) = shift_stage_bwd(
        "rwkv_pre_bwd", f_rwkv_pre, p_rwkv, u, mu_rwkv, pre_par, [ones_half],
        [(dr1, dr2), dwd, (dk1, dk2), (dv1, dv2), dan, dbb], tr)

    dq_f, dkv_f, dkpe = attn_bwd(q_f, kv_f, kpe, ya, lse, dya)
    (dp_mla,), (dqg, dkvg, dw_uq_p, dw_ukv) = row_bwd(
        "mla_pre_bwd", f_mla_pre, [p_mla, cs], 1, mla_par, [perm], [dq_f, dkv_f, dkpe], tr)

    (dx1,), (dsh1, dsc1, dw_g1) = row_bwd("in1_bwd", f_in1, [x2], 1, [shift, scale, w_g1], [], [dp_mla, dgpa], tr, [dx_res])
    (dx2,), (dsh2, dsc2, dw_g2) = row_bwd("in2_bwd", f_in2, [x2], 1, [shift, scale, w_g2], [], [dp_rwkv, dgpb], tr, [dx1])
    (dx3,), (dsh3, dsc3, dw_g3) = row_bwd("in3_bwd", f_in3, [x2], 1, [shift, scale, w_g3], [], [dma, dmb], tr, [dx2])
    grad_x = dx3[None]

    dada = jnp.concatenate([dsh1 + dsh2 + dsh3, dsc1 + dsc2 + dsc3, dgate], axis=1)
    local = {
        'w_in': jnp.concatenate([dw_g1[:, :384], dw_g1[:, 448:480], _unpad_heads_cols(dw_g1[:, 512:], HEAD, HEAD),
                                 dw_g2, dw_g3], axis=1),
        'w_uq': _unpad_heads_cols(dw_uq_p, MLA_QK, 0),
        'w_ukv': dw_ukv,
        'w_decay_up': dw_lora[:LORA, :WIDTH],
        'w_iclr_up': dw_lora[LORA:, WIDTH:],
        'w_proj_a': dw_pa_p.reshape(HEADS, LANES, D_MODEL)[:, HEAD:].reshape(WIDTH, D_MODEL),
        'w_proj_b': dw_pb,
        'w_out': dw_out,
    }
    small_local = {'b_ada': dada, 'q_norm_g': dqg, 'kv_norm_g': dkvg, 'mu_rwkv': dmu, 'w0': dw0, 'a0': da0,
                   'k_k': dk_k, 'k_a': dk_a, 'r_k': dr_k, 'gn_g': dgn_g, 'gn_b': dgn_b,
                   'post_g': dpost_g, 'post_b': dpost_b}

    def shard_of(g, axis, s):
        n = g.shape[axis] // 4
        return lax.slice_in_dim(g, s * n, (s + 1) * n, axis=axis)

    packed = jnp.stack([_pack_shard({n: shard_of(local[n], ax, s) for n, _, ax in _SHARDED if n != 'w_ada'}, F32, False)
                        for s in range(4)])
    keep = lax.dynamic_slice_in_dim(packed, my_c * GRAD_HALF, GRAD_HALF, 2).reshape(4 * PACK_ROWS, GRAD_HALF)
    give = lax.dynamic_slice_in_dim(packed, (1 - my_c) * GRAD_HALF, GRAD_HALF, 2).reshape(4 * PACK_ROWS, GRAD_HALF)
    pair_sum, = row_fwd("pair_sum", lambda p, q: (p + q,), [keep, pair_swap("swap_halves", give)], [], [],
                        [GRAD_HALF], PACK_ROWS // 2, BF16)
    received = chip_all_to_all("exchange_grads", pair_sum.reshape(4, PACK_ROWS, GRAD_HALF))
    my_sum = sum_slots(received, PACK_ROWS // 2)
    other_sum = pair_swap("swap_sums", my_sum)
    halves = [jnp.where(my_c == 0, my_sum, other_sum), jnp.where(my_c == 0, other_sum, my_sum)]
    g_shard = _unpack_shard(jnp.concatenate(halves, axis=1), False)

    small_pack = lambda d, extra=(): _pack_small([d[n] for n, _ in _SMALL] + list(extra))
    small_all = all_gather8("gather_small", small_pack(small_local, [c * jax.nn.sigmoid(c)]))
    sc_all = small_all[:, SMALL_USED:SMALL_USED + D_MODEL // LANES].reshape(N_DEV, D_MODEL)
    dada_all = small_all[:, :3 * D_MODEL // LANES].reshape(N_DEV, 3 * D_MODEL)
    my_cols = lax.dynamic_slice_in_dim(dada_all, (2 * lax.axis_index("x") + lax.axis_index("y")) * 768, 768, 1)
    g_shard['w_ada'] = ada_grad_shard([sc_all[b].reshape(D_MODEL, 1) for b in range(N_DEV)], my_cols)

    big = [{}, {}, {}, {}]
    for n in shard_names:
        w2, m2, v2 = given[n][0], given['m_' + n][0], given['v_' + n][0]
        cols = w2.shape[1]
        outs = row_fwd("adamw_" + n, f_adamw, [w2, g_shard[n], m2, v2], [], [], [cols] * 3, min(256, w2.shape[0]))
        for dst, val in zip(big, (g_shard[n], *outs)):
            dst[n] = val

    small_out = adamw_small(small_all, small_pack({n: given[n] for n, _ in _SMALL}),
                            small_pack({n: given['m_' + n] for n, _ in _SMALL}),
                            small_pack({n: given['v_' + n] for n, _ in _SMALL}))

    results = []
    for big_k, packed_small in zip(big, small_out):
        small = _unpack_small(packed_small)
        results.append([(big_k[n] if n in big_k else small[n]).reshape(given[n].shape) for n in _WEIGHTS])
    return (loss, grad_x, *results[0], *results[1], *results[2], *results[3])
```

```python
from typing import NamedTuple

import numpy as np
import jax
import jax.numpy as jnp
from jax import lax
from jax.experimental import pallas as pl
from jax.experimental.pallas import tpu as pltpu

F32 = jnp.float32
BF16 = jnp.bfloat16

D_MODEL = 1024
LN_EPS = 1e-5
RMS_EPS = 1e-6
GN_EPS = 64e-5
HEADS = 8
HEAD = 64
MLA_ROPE = 32
MLA_QK = HEAD + MLA_ROPE
ROPE_THETA = 10000.0
WIDTH = HEADS * HEAD
LORA = 64
SHIFT_W = 3 * WIDTH + 2 * LORA
CHUNK = 64
ALPHA = 2.0 ** 0.25

ADAM_LR, ADAM_B1, ADAM_B2, ADAM_EPS, ADAM_WD, ADAM_STEP = 0.001, 0.9, 0.999, 1e-08, 0.01, 10

LANES = 128
SUBLANES = 8
VMEM_LIMIT = 56 * 1024 * 1024
N_DEV = 8
MESH = pl.DeviceIdType.MESH
NEG = -1e30

_WEIGHTS = ['w_ada', 'b_ada', 'w_in', 'q_norm_g', 'w_uq', 'kv_norm_g', 'w_ukv', 'mu_rwkv', 'w0',
            'w_decay_up', 'a0', 'w_iclr_up', 'k_k', 'k_a', 'r_k', 'gn_g', 'gn_b', 'w_proj_a',
            'w_proj_b', 'w_out', 'post_g', 'post_b']
_SHARDED = [('w_ada', (1024, 3072), 1), ('w_in', (1024, 5152), 1), ('w_uq', (256, 768), 1),
            ('w_ukv', (128, 1024), 1), ('w_decay_up', (64, 512), 1), ('w_iclr_up', (64, 512), 1),
            ('w_proj_a', (512, 1024), 1), ('w_proj_b', (512, 1024), 1), ('w_out', (1024, 1024), 0)]
_SMALL = [('b_ada', 3072), ('q_norm_g', 256), ('kv_norm_g', 128), ('mu_rwkv', 1664), ('w0', 512),
          ('a0', 512), ('k_k', 512), ('k_a', 512), ('r_k', 512), ('gn_g', 512), ('gn_b', 512),
          ('post_g', 1024), ('post_b', 1024)]
PACK_ROWS = 256
PACK_COLS = 11264
HALF_COLS = PACK_COLS // 2
ADA_COLS = 4 * 768
GRAD_HALF = (PACK_COLS - ADA_COLS) // 2
SMALL_USED = 84
SMALL_ROWS = 96


def _bf(x):
    return x.astype(BF16)


def _dot(a, b, ca, cb):
    return lax.dot_general(a, b, (((ca,), (cb,)), ((), ())), preferred_element_type=F32)


class Weight(NamedTuple):
    value: jax.Array
    grad: jax.Array


@jax.custom_vjp
def _mm(a, w, w_grad):
    return _dot(_bf(a), _bf(w), 1, 0)


def _mm_fwd(a, w, w_grad):
    return _mm(a, w, w_grad), (a, w)


def _mm_bwd(res, g):
    a, w = res
    gb = _bf(g)
    return _dot(gb, _bf(w), 1, 1), jnp.zeros_like(w), _dot(_bf(a), gb, 0, 0)


_mm.defvjp(_mm_fwd, _mm_bwd)


def mm(a, w):
    if isinstance(w, Weight):
        return _mm(a, w.value, w.grad)
    return _dot(_bf(a), _bf(w), 1, 0)


def _split3(x):
    hi = _bf(x)
    r1 = x - hi.astype(F32)
    mid = _bf(r1)
    lo = _bf(r1 - mid.astype(F32))
    return hi, mid, lo


def _exact_dot(x, m, cm):
    hi, mid, lo = _split3(x)
    return _dot(hi, m, 1, cm) + _dot(mid, m, 1, cm) + _dot(lo, m, 1, cm)


def _head_sums(x, ones_blocks):
    n = ones_blocks.shape[0]
    parts = [_exact_dot(x[:, o:o + n], ones_blocks, 0) for o in range(0, x.shape[1], n)]
    return parts[0] if len(parts) == 1 else jnp.concatenate(parts, axis=1)


@jax.custom_vjp
def segsum(x, ones_blocks):
    return _head_sums(x, ones_blocks)


def _segsum_fwd(x, ones_blocks):
    return segsum(x, ones_blocks), ones_blocks


def _segsum_bwd(ones_blocks, g):
    return _head_sums(g, ones_blocks), jnp.zeros_like(ones_blocks)


segsum.defvjp(_segsum_fwd, _segsum_bwd)


@jax.custom_vjp
def lane_perm(x, perm):
    return _exact_dot(x, perm, 0)


def _lane_perm_fwd(x, perm):
    return lane_perm(x, perm), perm


def _lane_perm_bwd(perm, g):
    return _exact_dot(g, perm, 1), jnp.zeros_like(perm)


lane_perm.defvjp(_lane_perm_fwd, _lane_perm_bwd)


def _silu(z):
    return z * jax.nn.sigmoid(z)


def _softplus(z):
    return jnp.maximum(z, 0.0) + jnp.log(1.0 + jnp.exp(-jnp.abs(z)))


def _layer_norm(x):
    xc = x - jnp.mean(x, -1, keepdims=True)
    return xc * lax.rsqrt(jnp.mean(xc * xc, -1, keepdims=True) + LN_EPS)


def _rope(t, cos_t, sin_t, perm):
    outs = []
    for h in range(t.shape[1] // LANES):
        th = t[:, h * LANES:(h + 1) * LANES]
        outs.append(th * cos_t + lane_perm(th, perm) * sin_t)
    return outs[0] if len(outs) == 1 else jnp.concatenate(outs, axis=1)


def _make_f_in(*split_groups):
    def f_in(x, shift, scale, *weights):
        h = _layer_norm(x) * (1.0 + scale) + shift
        outs = []
        for w, splits in zip(weights, split_groups):
            p, o = mm(h, w), 0
            for s in splits:
                outs.append(p[:, o:o + s])
                o += s
        return tuple(outs)
    return f_in


def f_mla_pre(p, cs, qg, kvg, w_uq, w_ukv, perm):
    q_c, kv_c, k_r = p[:, :256], p[:, 256:384], p[:, 384:512]
    cos_t, sin_t = cs[:, :LANES], cs[:, LANES:]
    qn = q_c * lax.rsqrt(jnp.mean(q_c * q_c, -1, keepdims=True) + RMS_EPS) * qg
    kvn = kv_c * lax.rsqrt(jnp.mean(kv_c * kv_c, -1, keepdims=True) + RMS_EPS) * kvg
    q = _rope(mm(qn, w_uq), cos_t, sin_t, perm)
    kv = mm(kvn, w_ukv)
    return q, kv, _rope(k_r, cos_t, sin_t, perm)


def f_rwkv_pre(u, w0, a0, k_k, k_a, w_lora, ones_blocks):
    r, k, v, lo = u[:, :WIDTH], u[:, WIDTH:2 * WIDTH], u[:, 2 * WIDTH:3 * WIDTH], u[:, 3 * WIDTH:]
    lane = lax.broadcasted_iota(jnp.int32, lo.shape, 1)
    dl = mm(jnp.where(lane < LORA, jnp.tanh(lo), lo), w_lora)
    w_log = -_softplus(-(w0 + dl[:, :WIDTH])) - 0.5
    decay = jnp.exp(-jnp.exp(w_log))
    a = jax.nn.sigmoid(a0 + dl[:, WIDTH:])
    kk = k * k_k
    kk = kk / jnp.maximum(jnp.sqrt(segsum(kk * kk, ones_blocks)), 1e-12)
    k2 = k * (1.0 + (a - 1.0) * k_a)
    return r, decay, k2, v, -kk, kk * a


def f_rwkv_post(y, r, k2, v, r_k, gn_g, gn_b, ones_blocks):
    yc = y - segsum(y, ones_blocks) * (1.0 / HEAD)
    yn = yc * lax.rsqrt(segsum(yc * yc, ones_blocks) * (1.0 / HEAD) + GN_EPS)
    return (yn * gn_g + gn_b + segsum(r * k2 * r_k, ones_blocks) * v,)


def f_merge(ya, gpa, yb, gpb, ma, mb, w_pa, w_pb):
    pa = mm(ya * _silu(gpa), w_pa)
    pb = mm(yb * _silu(gpb), w_pb)
    return (jax.nn.sigmoid(ma) * pa + jax.nn.sigmoid(mb) * pb,)


def f_loss(merged, x, tgt, gate, post_g, post_b, w_out):
    z = ALPHA * x + (1.0 + gate) * mm(merged, w_out)
    err = _layer_norm(z) * post_g + post_b - tgt
    lrow = 0.5 * jnp.mean(err * err, -1, keepdims=True)
    return (jnp.broadcast_to(lrow, (lrow.shape[0], LANES)),)


def f_adamw(w, g, m, v):
    m2 = ADAM_B1 * m + (1.0 - ADAM_B1) * g
    v2 = ADAM_B2 * v + (1.0 - ADAM_B2) * jnp.square(g)
    m_hat = m2 / (1.0 - ADAM_B1 ** ADAM_STEP)
    v_hat = v2 / (1.0 - ADAM_B2 ** ADAM_STEP)
    return -ADAM_LR * (m_hat / (jnp.sqrt(v_hat) + ADAM_EPS) + ADAM_WD * w), m2, v2


def _params(sem=("arbitrary",)):
    return pltpu.CompilerParams(dimension_semantics=sem, vmem_limit_bytes=VMEM_LIMIT)


def _row_spec(tr, a):
    return pl.BlockSpec((tr, a.shape[1]), lambda i: (i, 0))


def _full_spec(a):
    return pl.BlockSpec(a.shape, lambda i: (0,) * a.ndim)


def row_fwd(name, f, rows, params, consts, out_widths, tr, out_dtype=F32):
    n_rows = rows[0].shape[0]
    nr, npar, ncon = len(rows), len(params), len(consts)

    def body(*refs):
        rv = [r[...] for r in refs[:nr]]
        pv = [r[...] for r in refs[nr:nr + npar]]
        cv = [r[...] for r in refs[nr + npar:nr + npar + ncon]]
        outs = f(*rv, *pv, *cv)
        for o_ref, o in zip(refs[nr + npar + ncon:], outs):
            o_ref[...] = o.astype(o_ref.dtype)

    return pl.pallas_call(
        body, name=name, grid=(n_rows // tr,),
        in_specs=[_row_spec(tr, a) for a in rows] + [_full_spec(a) for a in list(params) + list(consts)],
        out_specs=[pl.BlockSpec((tr, w), lambda i: (i, 0)) for w in out_widths],
        out_shape=[jax.ShapeDtypeStruct((n_rows, w), out_dtype) for w in out_widths],
        compiler_params=_params(),
    )(*rows, *params, *consts)


def row_bwd(name, f, rows, n_diff, params, consts, douts, tr, add_rows=None):
    n_rows = rows[0].shape[0]
    douts = [d if isinstance(d, (tuple, list)) else (d,) for d in douts]
    counts = [len(d) for d in douts]
    flat_d = [a for d in douts for a in d]
    add_rows = add_rows or [None] * n_diff
    adds = [a for a in add_rows if a is not None]
    nr, npar, ncon, nd, na = len(rows), len(params), len(consts), len(flat_d), len(adds)

    def body(*refs):
        o = 0
        rv = [r[...] for r in refs[o:o + nr]]; o += nr
        pv = [Weight(r[...], jnp.zeros(r.shape, F32)) if r.dtype == BF16 else r[...] for r in refs[o:o + npar]]
        o += npar
        cv = [r[...] for r in refs[o:o + ncon]]; o += ncon
        dv = []
        for cnt in counts:
            s = refs[o][...]
            for e in range(1, cnt):
                s = s + refs[o + e][...]
            dv.append(s)
            o += cnt
        add_v = [r[...] for r in refs[o:o + na]]; o += na
        drow_refs = refs[o:o + n_diff]; o += n_diff
        dpar_refs = refs[o:o + npar]

        def g(*args):
            return tuple(f(*args[:n_diff], *rv[n_diff:], *args[n_diff:], *cv))

        _, vjp = jax.vjp(g, *rv[:n_diff], *pv)
        grads = vjp(tuple(dv))
        ai = 0
        for j, (r, gr) in enumerate(zip(drow_refs, grads[:n_diff])):
            if add_rows[j] is not None:
                gr = gr + add_v[ai]
                ai += 1
            r[...] = gr

        @pl.when(pl.program_id(0) == 0)
        def _():
            for r in dpar_refs:
                r[...] = jnp.zeros_like(r)

        for r, gr in zip(dpar_refs, grads[n_diff:]):
            r[...] += gr.grad if isinstance(gr, Weight) else gr

    outs = pl.pallas_call(
        body, name=name, grid=(n_rows // tr,),
        in_specs=([_row_spec(tr, a) for a in rows] + [_full_spec(a) for a in list(params) + list(consts)]
                  + [_row_spec(tr, a) for a in flat_d + adds]),
        out_specs=[_row_spec(tr, a) for a in rows[:n_diff]] + [_full_spec(a) for a in params],
        out_shape=([jax.ShapeDtypeStruct(a.shape, F32) for a in rows[:n_diff]]
                   + [jax.ShapeDtypeStruct(a.shape, F32) for a in params]),
        compiler_params=_params(),
    )(*rows, *params, *consts, *flat_d, *adds)
    return outs[:n_diff], outs[n_diff:]


def shift_stage_fwd(name, f, p, mu, params, consts, out_widths, tr):
    n_rows, w = p.shape
    npar, ncon = len(params), len(consts)

    def body(*refs):
        p_ref, mu_ref = refs[:2]
        pv = [r[...] for r in refs[2:2 + npar]]
        cv = [r[...] for r in refs[2 + npar:2 + npar + ncon]]
        u_ref, out_refs, carry = refs[2 + npar + ncon], refs[3 + npar + ncon:-1], refs[-1]

        @pl.when(pl.program_id(0) == 0)
        def _():
            carry[...] = jnp.zeros_like(carry)

        x = p_ref[...]
        rolled = pltpu.roll(x, 1, 0)
        head = pltpu.roll(carry[...], 1, 0)
        fixed = jnp.concatenate([head, rolled[SUBLANES:]], axis=0)
        row = lax.broadcasted_iota(jnp.int32, x.shape, 0)
        prev = jnp.where(row == 0, fixed, rolled)
        u = x + (prev - x) * mu_ref[...]
        u_ref[...] = u
        carry[...] = x[tr - SUBLANES:]
        for o_ref, o in zip(out_refs, f(u, *pv, *cv)):
            o_ref[...] = o

    return pl.pallas_call(
        body, name=name, grid=(n_rows // tr,),
        in_specs=[_row_spec(tr, p), _full_spec(mu)] + [_full_spec(a) for a in list(params) + list(consts)],
        out_specs=[_row_spec(tr, p)] + [pl.BlockSpec((tr, ow), lambda i: (i, 0)) for ow in out_widths],
        out_shape=[jax.ShapeDtypeStruct(p.shape, F32)] + [jax.ShapeDtypeStruct((n_rows, ow), F32) for ow in out_widths],
        scratch_shapes=[pltpu.VMEM((SUBLANES, w), F32)],
        compiler_params=_params(),
    )(p, mu, *params, *consts)


def shift_stage_bwd(name, f, p, u, mu, params, consts, douts, tr):
    n_rows, w = p.shape
    nb = n_rows // tr
    douts = [d if isinstance(d, (tuple, list)) else (d,) for d in douts]
    counts = [len(d) for d in douts]
    flat_d = [a for d in douts for a in d]
    npar, ncon, nd = len(params), len(consts), len(flat_d)

    def body(*refs):
        p_ref, u_ref, mu_ref = refs[:3]
        o = 3
        pv = [Weight(r[...], jnp.zeros(r.shape, F32)) if r.dtype == BF16 else r[...] for r in refs[o:o + npar]]
        o += npar
        cv = [r[...] for r in refs[o:o + ncon]]; o += ncon
        dv = []
        for cnt in counts:
            s = refs[o][...]
            for e in range(1, cnt):
                s = s + refs[o + e][...]
            dv.append(s)
            o += cnt
        dp_ref, dmu_ref = refs[o], refs[o + 1]
        dpar_refs, carry = refs[o + 2:o + 2 + npar], refs[-1]

        @pl.when(pl.program_id(0) == 0)
        def _():
            carry[...] = jnp.zeros_like(carry)
            dmu_ref[...] = jnp.zeros_like(dmu_ref)
            for r in dpar_refs:
                r[...] = jnp.zeros_like(r)

        _, vjp = jax.vjp(lambda uu, *pp: tuple(f(uu, *pp, *cv)), u_ref[...], *pv)
        grads = vjp(tuple(dv))
        for r, gr in zip(dpar_refs, grads[1:]):
            r[...] += gr.grad if isinstance(gr, Weight) else gr

        d = grads[0]
        rolled = pltpu.roll(d, tr - 1, 0)
        tail = pltpu.roll(carry[...], SUBLANES - 1, 0)
        fixed = jnp.concatenate([rolled[:tr - SUBLANES], tail], axis=0)
        row = lax.broadcasted_iota(jnp.int32, d.shape, 0)
        nxt = jnp.where(row == tr - 1, fixed, rolled)
        mu_v = mu_ref[...]
        dp_ref[...] = d * (1.0 - mu_v) + nxt * mu_v
        dmu_ref[...] += jnp.sum(p_ref[...] * (nxt - d), axis=0, keepdims=True)
        carry[...] = d[:SUBLANES]

    rev = lambda i: (nb - 1 - i, 0)
    rows_rev = lambda a: pl.BlockSpec((tr, a.shape[1]), rev)
    outs = pl.pallas_call(
        body, name=name, grid=(nb,),
        in_specs=([rows_rev(p), rows_rev(u), _full_spec(mu)] + [_full_spec(a) for a in list(params) + list(consts)]
                  + [rows_rev(a) for a in flat_d]),
        out_specs=[rows_rev(p), _full_spec(mu)] + [_full_spec(a) for a in params],
        out_shape=([jax.ShapeDtypeStruct(p.shape, F32), jax.ShapeDtypeStruct(mu.shape, F32)]
                   + [jax.ShapeDtypeStruct(a.shape, F32) for a in params]),
        scratch_shapes=[pltpu.VMEM((SUBLANES, w), F32)],
        compiler_params=_params(),
    )(p, u, mu, *params, *consts, *flat_d)
    return outs[0], outs[1], outs[2:]


ATT_T = 256


def _att_rows(j):
    return pl.ds(pl.multiple_of(j * ATT_T, ATT_T), ATT_T)


def _att_prep(kv_ref, kpe_ref, kf_scr, vf_scr, n_blocks):
    lane = lax.broadcasted_iota(jnp.int32, (ATT_T, LANES), 1)

    def prep(j, _):
        rows = _att_rows(j)
        kv = kv_ref[rows, :]
        kf_scr[rows, :] = _bf(jnp.where(lane < HEAD, kv, kpe_ref[rows, :]))
        vf_scr[rows, :] = _bf(jnp.where(lane >= HEAD, kv, 0.0))
        return 0

    lax.fori_loop(0, n_blocks, prep, 0)


def _att_diag_mask():
    shift = CHUNK.bit_length() - 1
    qc = jnp.right_shift(lax.broadcasted_iota(jnp.int32, (ATT_T, ATT_T), 0), shift)
    kc = jnp.right_shift(lax.broadcasted_iota(jnp.int32, (ATT_T, ATT_T), 1), shift)
    return kc <= qc


def _wide(x):
    return jnp.concatenate([x] * (ATT_T // LANES), axis=1)


def attn_fwd(q, kv, kpe):
    seq = q.shape[0]
    nb = seq // ATT_T
    assert seq % (2 * ATT_T) == 0, "blocks are taken two per trip"
    scale = MLA_QK ** -0.5

    def body(q_ref, kv_ref, kpe_ref, o_ref, lse_ref, kf_scr, vf_scr):
        _att_prep(kv_ref, kpe_ref, kf_scr, vf_scr, nb)
        mask = _att_diag_mask()

        def scores(qb, kj):
            return _dot(qb, kf_scr[_att_rows(kj), :], 1, 1) * scale

        def update(s, kj, carry, masked):
            m, l, acc = carry
            if masked:
                s = jnp.where(mask, s, NEG)
            m_new = jnp.maximum(m, jnp.broadcast_to(jnp.max(s, -1, keepdims=True), m.shape))
            alpha = jnp.exp(m - m_new)
            p = jnp.exp(s - _wide(m_new))
            l = alpha * l + jnp.broadcast_to(jnp.sum(p, -1, keepdims=True), l.shape)
            acc = alpha * acc + _dot(_bf(p), vf_scr[_att_rows(kj), :], 1, 0)
            return m_new, l, acc

        def finish(rows, carry):
            m, l, acc = carry
            o_ref[rows, :] = acc / l
            lse_ref[rows, :] = m + jnp.log(l)

        def q_pair(qp, _):
            rows_a, rows_b = _att_rows(2 * qp), _att_rows(2 * qp + 1)
            qa, qb = _bf(q_ref[rows_a, :]), _bf(q_ref[rows_b, :])
            init = (jnp.full((ATT_T, LANES), NEG, F32), jnp.zeros((ATT_T, LANES), F32),
                    jnp.zeros((ATT_T, LANES), F32))

            def trip(kj, c):
                ca, cb, sa, sb = c
                sa_next, sb_next = scores(qa, kj + 1), scores(qb, kj + 1)
                return update(sa, kj, ca, False), update(sb, kj, cb, False), sa_next, sb_next

            ca, cb, sa, sb = lax.fori_loop(0, 2 * qp, trip, (init, init, scores(qa, 0), scores(qb, 0)))
            sb_last = scores(qb, 2 * qp + 1)
            ca = update(sa, 2 * qp, ca, True)
            cb = update(sb_last, 2 * qp + 1, update(sb, 2 * qp, cb, False), True)
            finish(rows_a, ca)
            finish(rows_b, cb)
            return 0

        lax.fori_loop(0, nb // 2, q_pair, 0)

    head = pl.BlockSpec((seq, LANES), lambda h: (0, h))
    return pl.pallas_call(
        body, name="attn_fwd", grid=(HEADS,),
        in_specs=[head, head, pl.BlockSpec((seq, LANES), lambda h: (0, 0))],
        out_specs=[head, head],
        out_shape=[jax.ShapeDtypeStruct((seq, HEADS * LANES), F32)] * 2,
        scratch_shapes=[pltpu.VMEM((seq, LANES), BF16)] * 2,
        compiler_params=_params(),
    )(q, kv, kpe)


def attn_bwd(q, kv, kpe, o, lse, do):
    seq = q.shape[0]
    nb = seq // ATT_T
    assert seq % (2 * ATT_T) == 0, "blocks are taken two per trip"
    scale = MLA_QK ** -0.5

    def body(q_ref, kv_ref, kpe_ref, o_ref, lse_ref, do_ref, dq_ref, dkv_ref, dkpe_ref,
             kf_scr, vf_scr, qb_scr, dob_scr, dsum):
        lane = lax.broadcasted_iota(jnp.int32, (ATT_T, LANES), 1)

        @pl.when(pl.program_id(0) == 0)
        def _():
            dkpe_ref[...] = jnp.zeros_like(dkpe_ref)

        dq_ref[...] = jnp.zeros_like(dq_ref)
        _att_prep(kv_ref, kpe_ref, kf_scr, vf_scr, nb)

        def pre(j, _):
            rows = _att_rows(j)
            d = do_ref[rows, :]
            qb_scr[rows, :] = _bf(q_ref[rows, :])
            dob_scr[rows, :] = _bf(d)
            dsum[rows, :] = jnp.broadcast_to(jnp.sum(d * o_ref[rows, :], -1, keepdims=True), (ATT_T, LANES))
            return 0

        lax.fori_loop(0, nb, pre, 0)
        mask = _att_diag_mask()

        def front(kf, vf, qi):
            rows = _att_rows(qi)
            return _dot(qb_scr[rows, :], kf, 1, 1), _dot(dob_scr[rows, :], vf, 1, 1)

        def back(kf, qi, fr, carry, masked):
            s, dp = fr
            dk, dv = carry
            rows = _att_rows(qi)
            qb, dob = qb_scr[rows, :], dob_scr[rows, :]
            p = jnp.exp(s * scale - _wide(lse_ref[rows, :]))
            if masked:
                p = jnp.where(mask, p, 0.0)
            ds = _bf(p * (dp - _wide(dsum[rows, :])) * scale)
            return (dk + _dot(ds, qb, 0, 0), dv + _dot(_bf(p), dob, 0, 0)), _dot(ds, kf, 1, 0)

        def store(krows, carry):
            dk, dv = carry
            dkv_ref[krows, :] = jnp.where(lane < HEAD, dk, dv)
            dkpe_ref[krows, :] += jnp.where((lane >= HEAD) & (lane < MLA_QK), dk, 0.0)

        def k_pair(kp, _):
            ka, kb = 2 * kp, 2 * kp + 1
            rows_a, rows_b = _att_rows(ka), _att_rows(kb)
            kfa, vfa, kfb, vfb = kf_scr[rows_a, :], vf_scr[rows_a, :], kf_scr[rows_b, :], vf_scr[rows_b, :]
            zero = jnp.zeros((ATT_T, LANES), F32)
            ca, dq_a = back(kfa, ka, front(kfa, vfa, ka), (zero, zero), True)
            dq_ref[rows_a, :] += dq_a
            ca, dq_a = back(kfa, kb, front(kfa, vfa, kb), ca, False)
            cb, dq_b = back(kfb, kb, front(kfb, vfb, kb), (zero, zero), True)
            dq_ref[rows_b, :] += dq_a + dq_b

            def both(qi, c):
                ca, cb, fa, fb = c
                nxt = jnp.minimum(qi + 1, nb - 1)
                fa_next, fb_next = front(kfa, vfa, nxt), front(kfb, vfb, nxt)
                ca, dq_a = back(kfa, qi, fa, ca, False)
                cb, dq_b = back(kfb, qi, fb, cb, False)
                dq_ref[_att_rows(qi), :] += dq_a + dq_b
                return ca, cb, fa_next, fb_next

            first = jnp.minimum(kb + 1, nb - 1)
            ca, cb, _, _ = lax.fori_loop(kb + 1, nb, both, (ca, cb, front(kfa, vfa, first), front(kfb, vfb, first)))
            store(rows_a, ca)
            store(rows_b, cb)
            return 0

        lax.fori_loop(0, nb // 2, k_pair, 0)

    head = pl.BlockSpec((seq, LANES), lambda h: (0, h))
    shared = pl.BlockSpec((seq, LANES), lambda h: (0, 0))
    return pl.pallas_call(
        body, name="attn_bwd", grid=(HEADS,),
        in_specs=[head, head, shared, head, head, head],
        out_specs=[head, head, shared],
        out_shape=[jax.ShapeDtypeStruct((seq, HEADS * LANES), F32)] * 2
        + [jax.ShapeDtypeStruct((seq, LANES), F32)],
        scratch_shapes=[pltpu.VMEM((seq, LANES), BF16)] * 4 + [pltpu.VMEM((seq, LANES), F32)],
        compiler_params=_params(),
    )(q, kv, kpe, o, lse, do)


WKV_TB = 128
WKV_GROUP = SUBLANES
WKV_HALF = WIDTH // 2


def _wkv_consts():
    row = lax.broadcasted_iota(jnp.int32, (HEAD, WKV_HALF), 0)
    lane = lax.broadcasted_iota(jnp.int32, (HEAD, WKV_HALF), 1)
    diag = row == jnp.bitwise_and(lane, HEAD - 1)
    sub = lax.broadcasted_iota(jnp.int32, (WKV_GROUP, WKV_HALF), 0)
    return diag, sub


def _halves(x):
    return [x[:, :WKV_HALF], x[:, WKV_HALF:]]


def _diag_rows(row, diag):
    return _bf(jnp.where(diag, jnp.broadcast_to(row, diag.shape), 0.0))


def _put_row(tile, row, i, sub):
    return jnp.where(sub == i, jnp.broadcast_to(row, tile.shape), tile)


def _col_sum(x):
    return jnp.sum(x, axis=0, keepdims=True)


def _step(x, i):
    return x[i * HEAD:(i + 1) * HEAD]


def _head_dots(prods, ones_b, sub):
    tile = jnp.zeros((WKV_GROUP, WKV_HALF), F32)
    for i, p in enumerate(prods):
        tile = _put_row(tile, p, i, sub)
    res = _exact_dot(tile, ones_b, 0)
    return [res[i:i + 1] for i in range(len(prods))]


def wkv_fwd(r, w, k, v, a, b, ones_half):
    seq = r.shape[0]

    def body(r_ref, w_ref, k_ref, v_ref, a_ref, b_ref, ones_ref, y_ref, st_ref, s_scr):
        @pl.when(pl.program_id(0) == 0)
        def _():
            s_scr[...] = jnp.zeros_like(s_scr)

        ones_b = ones_ref[...]
        diag, sub = _wkv_consts()

        ng = WKV_TB // WKV_GROUP
        last = WKV_GROUP - 2

        def rows_of(g):
            return pl.ds(pl.multiple_of(g * WKV_GROUP, WKV_GROUP), WKV_GROUP)

        def pair_rows(x8, t):
            return jnp.concatenate([_diag_rows(x8[t:t + 1], diag), _diag_rows(x8[t + 1:t + 2], diag)], axis=0)

        def put_y(g, pairs_y):
            tile = _halves(y_ref[rows_of(g), :])
            for hf in range(2):
                tile[hf] = _put_row(_put_row(tile[hf], pairs_y[hf][0], last, sub), pairs_y[hf][1], last + 1, sub)
            y_ref[rows_of(g), :] = jnp.concatenate(tile, axis=1)

        def read_out(yexp):
            return _col_sum(jnp.where(diag, yexp[:HEAD], 0.0)), _col_sum(jnp.where(diag, yexp[HEAD:], 0.0))

        def group(g, carry):
            state, v_cur, read = (list(c) for c in carry)
            base = pl.multiple_of(g * WKV_GROUP, WKV_GROUP)
            rows = rows_of(g)
            r8, w8, k8, v8, a8, b8 = (_halves(ref[rows, :]) for ref in (r_ref, w_ref, k_ref, v_ref, a_ref, b_ref))
            v_after = _halves(v_ref[rows_of(jnp.minimum(g + 1, ng - 1)), :])
            evens = range(0, WKV_GROUP, 2)
            dots = [_head_dots([b8[hf][t:t + 1] * a8[hf][t + 1:t + 2] for t in evens]
                               + [k8[hf][t:t + 1] * a8[hf][t + 1:t + 2] for t in evens], ones_b, sub) for hf in range(2)]
            y8 = [jnp.zeros((WKV_GROUP, WKV_HALF), F32)] * 2
            y_before = [None, None]
            for t in evens:
                s0, s1 = slice(t, t + 1), slice(t + 1, t + 2)
                both = []
                for hf in range(2):
                    s_in = state[hf]
                    v_next = pair_rows(v8[hf], t + 2) if t < last else pair_rows(v_after[hf], 0)
                    res = _dot(jnp.concatenate([_bf(s_in * a8[hf][s0]), _bf(s_in * (w8[hf][s0] * a8[hf][s1])),
                                                v_next, read[hf]], axis=0), ones_b, 1, 0)
                    sa0, v0, v1 = res[:HEAD], v_cur[hf][:HEAD], v_cur[hf][HEAD:]
                    st0 = s_in * w8[hf][s0] + sa0 * b8[hf][s0] + v0 * k8[hf][s0]
                    sa1 = res[HEAD:2 * HEAD] + sa0 * dots[hf][t // 2] + v0 * dots[hf][WKV_GROUP // 2 + t // 2]
                    st1 = st0 * w8[hf][s1] + sa1 * b8[hf][s1] + v1 * k8[hf][s1]
                    both.append((st0, st1))
                    state[hf], v_cur[hf] = st1, res[2 * HEAD:4 * HEAD]
                    read[hf] = jnp.concatenate([_bf(st0 * r8[hf][s0]), _bf(st1 * r8[hf][s1])], axis=0)
                    ya, yb = read_out(res[4 * HEAD:])
                    if t == 0:
                        y_before[hf] = (ya, yb)
                    else:
                        y8[hf] = _put_row(_put_row(y8[hf], ya, t - 2, sub), yb, t - 1, sub)
                for j in range(2):
                    st_ref[base + t + j] = jnp.concatenate([both[0][j], both[1][j]], axis=1)
            y_ref[rows, :] = jnp.concatenate(y8, axis=1)
            put_y(jnp.maximum(g - 1, 0), y_before)
            return tuple(state), tuple(v_cur), tuple(read)

        v_first = _halves(v_ref[rows_of(0), :])
        init = (tuple(_halves(s_scr[...])),
                tuple(_dot(pair_rows(v_first[hf], 0), ones_b, 1, 0) for hf in range(2)),
                tuple(jnp.zeros((2 * HEAD, WKV_HALF), BF16) for _ in range(2)))
        fin, _, read = lax.fori_loop(0, ng, group, init)
        put_y(ng - 1, [read_out(_dot(read[hf], ones_b, 1, 0)) for hf in range(2)])
        s_scr[...] = jnp.concatenate(fin, axis=1)

    vec = pl.BlockSpec((WKV_TB, WIDTH), lambda i: (i, 0))
    return pl.pallas_call(
        body, name="wkv_fwd", grid=(seq // WKV_TB,),
        in_specs=[vec] * 6 + [_full_spec(ones_half)],
        out_specs=[vec, pl.BlockSpec((WKV_TB, HEAD, WIDTH), lambda i: (i, 0, 0))],
        out_shape=[jax.ShapeDtypeStruct((seq, WIDTH), F32), jax.ShapeDtypeStruct((seq, HEAD, WIDTH), F32)],
        scratch_shapes=[pltpu.VMEM((HEAD, WIDTH), F32)],
        compiler_params=_params(),
    )(r, w, k, v, a, b, ones_half)


def wkv_bwd(r, w, k, v, a, b, dy, states, ones_half):
    seq = r.shape[0]
    nb = seq // WKV_TB
    ng = WKV_TB // WKV_GROUP

    def body(r_ref, w_ref, k_ref, v_ref, a_ref, b_ref, dy_ref, st_ref, halo_ref, ones_ref,
             dr_ref, dw_ref, dk_ref, dv_ref, da_ref, db_ref, ds_scr):
        blk = nb - 1 - pl.program_id(0)

        @pl.when(pl.program_id(0) == 0)
        def _():
            ds_scr[...] = jnp.zeros_like(ds_scr)

        ones_b = ones_ref[...]
        diag, sub = _wkv_consts()
        before_block = jnp.where(blk == 0, 0.0, halo_ref[0])

        def rows_of(g):
            return pl.ds(pl.multiple_of(g * WKV_GROUP, WKV_GROUP), WKV_GROUP)

        def expand_rows(hf, dy8, v8, a8, t, s_t, s_u):
            s1, s0 = slice(t, t + 1), slice(t - 1, t)
            return jnp.concatenate([_diag_rows(dy8[hf][s1], diag), _diag_rows(dy8[hf][s0], diag),
                                    _diag_rows(v8[hf][s1], diag), _diag_rows(v8[hf][s0], diag),
                                    _bf(s_t[hf] * a8[hf][s1]), _bf(s_u[hf] * a8[hf][s0])], axis=0)

        def read_out(x):
            return _col_sum(jnp.where(diag, x[:HEAD], 0.0)), _col_sum(jnp.where(diag, x[HEAD:], 0.0))

        def put_dv(g, pair_dv):
            tile = _halves(dv_ref[rows_of(g), :])
            for hf in range(2):
                tile[hf] = _put_row(_put_row(tile[hf], pair_dv[hf][0], 1, sub), pair_dv[hf][1], 0, sub)
            dv_ref[rows_of(g), :] = jnp.concatenate(tile, axis=1)

        def group(gg, carry):
            dstate, e_cur, dv_pend = (list(c) for c in carry)
            g = ng - 1 - gg
            base = pl.multiple_of(g * WKV_GROUP, WKV_GROUP)
            rows = rows_of(g)
            r8, w8, k8, v8, a8, b8, dy8 = (
                _halves(ref[rows, :]) for ref in (r_ref, w_ref, k_ref, v_ref, a_ref, b_ref, dy_ref))
            g_next = jnp.maximum(g - 1, 0)
            base_next = pl.multiple_of(g_next * WKV_GROUP, WKV_GROUP)
            dy8n, v8n, a8n = (_halves(ref[rows_of(g_next), :]) for ref in (dy_ref, v_ref, a_ref))
            zero8 = jnp.zeros((WKV_GROUP, WKV_HALF), F32)
            out = {n: [zero8, zero8] for n in ("dr", "dw", "dk", "dv", "da", "db")}
            before_group = jnp.where(g == 0, before_block, st_ref[jnp.maximum(base - 1, 0)])
            states = [_halves(before_group)] + [_halves(st_ref[base + i]) for i in range(WKV_GROUP)]
            odds = range(1, WKV_GROUP, 2)
            dots = [_head_dots([a8[hf][t:t + 1] * b8[hf][t - 1:t] for t in odds]
                               + [r8[hf][t - 1:t] * b8[hf][t - 1:t] for t in odds], ones_b, sub) for hf in range(2)]
            dv_after = [None, None]

            def emit(hf, i, d_i, dsa_i, dy_i, v_i, sa_i):
                s_p, s_t = states[i][hf], states[i + 1][hf]
                for n, val in (("dr", _col_sum(s_t * dy_i)), ("dw", _col_sum(d_i * s_p)), ("db", _col_sum(d_i * sa_i)),
                               ("da", _col_sum(s_p * dsa_i)), ("dk", _col_sum(d_i * v_i))):
                    out[n][hf] = _put_row(out[n][hf], val, i, sub)

            for t in reversed(odds):
                s1, s0 = slice(t, t + 1), slice(t - 1, t)
                for hf in range(2):
                    dy1, dy0, v1, v0, sa1, sa0 = (_step(e_cur[hf], j) for j in range(6))
                    d1 = dstate[hf] + dy1 * r8[hf][s1]
                    if t > 1:
                        nxt = expand_rows(hf, dy8, v8, a8, t - 2, states[t - 2], states[t - 3])
                    else:
                        nxt = expand_rows(hf, dy8n, v8n, a8n, WKV_GROUP - 1, _halves(st_ref[base_next + WKV_GROUP - 2]),
                                          _halves(st_ref[base_next + WKV_GROUP - 3]))
                    res = _dot(jnp.concatenate([_bf(d1 * b8[hf][s1]), _bf(d1 * (w8[hf][s1] * b8[hf][s0])),
                                                nxt, dv_pend[hf]], axis=0), ones_b, 1, 0)
                    dsa1 = res[:HEAD]
                    d0 = d1 * w8[hf][s1] + dsa1 * a8[hf][s1] + dy0 * r8[hf][s0]
                    dsa0 = res[HEAD:2 * HEAD] + dsa1 * dots[hf][t // 2] + dy0 * dots[hf][WKV_GROUP // 2 + t // 2]
                    dstate[hf] = d0 * w8[hf][s0] + dsa0 * a8[hf][s0]
                    e_cur[hf] = res[2 * HEAD:8 * HEAD]
                    dv_pend[hf] = jnp.concatenate([_bf(d1 * k8[hf][s1]), _bf(d0 * k8[hf][s0])], axis=0)
                    emit(hf, t, d1, dsa1, dy1, v1, sa1)
                    emit(hf, t - 1, d0, dsa0, dy0, v0, sa0)
                    dv_a, dv_b = read_out(res[8 * HEAD:])
                    if t == WKV_GROUP - 1:
                        dv_after[hf] = (dv_a, dv_b)
                    else:
                        out["dv"][hf] = _put_row(_put_row(out["dv"][hf], dv_a, t + 2, sub), dv_b, t + 1, sub)
            for ref, n in ((dr_ref, "dr"), (dw_ref, "dw"), (dk_ref, "dk"), (dv_ref, "dv"), (da_ref, "da"), (db_ref, "db")):
                ref[rows, :] = jnp.concatenate(out[n], axis=1)
            put_dv(jnp.minimum(g + 1, ng - 1), dv_after)
            return tuple(dstate), tuple(e_cur), tuple(dv_pend)

        top = rows_of(ng - 1)
        dy8t, v8t, a8t = (_halves(ref[top, :]) for ref in (dy_ref, v_ref, a_ref))
        s_t, s_u = _halves(st_ref[WKV_TB - 2]), _halves(st_ref[WKV_TB - 3])
        init = (tuple(_halves(ds_scr[...])),
                tuple(_dot(expand_rows(hf, dy8t, v8t, a8t, WKV_GROUP - 1, s_t, s_u), ones_b, 1, 0) for hf in range(2)),
                tuple(jnp.zeros((2 * HEAD, WKV_HALF), BF16) for _ in range(2)))
        fin, _, dv_pend = lax.fori_loop(0, ng, group, init)
        put_dv(0, [read_out(_dot(dv_pend[hf], ones_b, 1, 0)) for hf in range(2)])
        ds_scr[...] = jnp.concatenate(fin, axis=1)

    vec = pl.BlockSpec((WKV_TB, WIDTH), lambda i: (nb - 1 - i, 0))
    return pl.pallas_call(
        body, name="wkv_bwd", grid=(nb,),
        in_specs=[vec] * 7 + [
            pl.BlockSpec((WKV_TB, HEAD, WIDTH), lambda i: (nb - 1 - i, 0, 0)),
            pl.BlockSpec((1, HEAD, WIDTH), lambda i: (jnp.maximum((nb - 1 - i) * WKV_TB - 1, 0), 0, 0)),
            _full_spec(ones_half)],
        out_specs=[vec] * 6,
        out_shape=[jax.ShapeDtypeStruct((seq, WIDTH), F32)] * 6,
        scratch_shapes=[pltpu.VMEM((HEAD, WIDTH), F32)],
        compiler_params=_params(),
    )(r, w, k, v, a, b, dy, states, states, ones_half)


def ada_fwd(c8, b_ada, gathered):
    cols = 3 * D_MODEL // 4

    def body(c_ref, b_ref, w_ref, o_ref):
        @pl.when(pl.program_id(1) == 0)
        def _():
            o_ref[...] = jnp.broadcast_to(b_ref[...], o_ref.shape)

        o_ref[...] += mm(_silu(c_ref[...]), w_ref[0])

    return pl.pallas_call(
        body, name="ada_fwd", grid=(4, D_MODEL // PACK_ROWS),
        in_specs=[pl.BlockSpec((SUBLANES, PACK_ROWS), lambda s, i: (0, i)),
                  pl.BlockSpec((1, cols), lambda s, i: (0, s)),
                  pl.BlockSpec((1, PACK_ROWS, cols), lambda s, i: (2 * s, 0, i))],
        out_specs=pl.BlockSpec((SUBLANES, cols), lambda s, i: (0, s)),
        out_shape=jax.ShapeDtypeStruct((SUBLANES, 3 * D_MODEL), F32),
        compiler_params=_params(("arbitrary", "arbitrary")),
    )(c8, b_ada, gathered)


def ada_grad_shard(sc_cols, dada_rows):
    n = len(sc_cols)

    def body(*refs):
        d_ref, o_ref = refs[n], refs[n + 1]
        acc = refs[0][...] * d_ref[0:1, :]
        for b in range(1, n):
            acc = acc + refs[b][...] * d_ref[b:b + 1, :]
        o_ref[...] = acc

    return pl.pallas_call(
        body, name="ada_grad_shard",
        out_shape=jax.ShapeDtypeStruct((sc_cols[0].shape[0], dada_rows.shape[1]), F32),
        compiler_params=pltpu.CompilerParams(vmem_limit_bytes=VMEM_LIMIT),
    )(*sc_cols, dada_rows)


def sum_slots(buf, tr):
    n, rows, cols = buf.shape

    def body(b_ref, o_ref):
        acc = b_ref[0].astype(F32)
        for s in range(1, n):
            acc = acc + b_ref[s].astype(F32)
        o_ref[...] = acc

    return pl.pallas_call(
        body, name="sum_slots", grid=(rows // tr,),
        in_specs=[pl.BlockSpec((n, tr, cols), lambda i: (0, i, 0))],
        out_specs=pl.BlockSpec((tr, cols), lambda i: (i, 0)),
        out_shape=jax.ShapeDtypeStruct((rows, cols), F32),
        compiler_params=_params(),
    )(buf)


def adamw_small(gathered, w, m, v):
    n = gathered.shape[0]

    def body(g_ref, w_ref, m_ref, v_ref, go_ref, d_ref, mo_ref, vo_ref):
        g = g_ref[0]
        for s in range(1, n):
            g = g + g_ref[s]
        go_ref[...] = g
        d_ref[...], mo_ref[...], vo_ref[...] = f_adamw(w_ref[...], g, m_ref[...], v_ref[...])

    return pl.pallas_call(
        body, name="adamw_small",
        out_shape=[jax.ShapeDtypeStruct(w.shape, F32)] * 4,
        compiler_params=pltpu.CompilerParams(vmem_limit_bytes=VMEM_LIMIT),
    )(gathered, w, m, v)


def _coords():
    return lax.axis_index("x"), lax.axis_index("y"), lax.axis_index("c")


def _flip(v, bit):
    return 1 - v if bit else v


def _hbm_call(body, name, out_shape, n_sems, *args):
    any_spec = pl.BlockSpec(memory_space=pl.ANY)
    return pl.pallas_call(
        body, name=name, out_shape=out_shape,
        in_specs=[any_spec] * len(args), out_specs=any_spec,
        scratch_shapes=[pltpu.SemaphoreType.DMA((n_sems,)), pltpu.SemaphoreType.DMA((n_sems,)),
                        pltpu.SemaphoreType.DMA],
    )(*args)


def all_gather8(name, block):
    def body(x_ref, out_ref, send_sems, recv_sems, local_sem):
        x, y, c = _coords()
        me, sibling = (x, y, c), (x, y, 1 - c)
        x_nbr, y_nbr, diagonal = (1 - x, y), (x, 1 - y), (1 - x, 1 - y)
        relay_from = (c * x + (1 - c) * (1 - x), c * (1 - y) + (1 - c) * y)
        relay_to = (c * (1 - x) + (1 - c) * x, c * y + (1 - c) * (1 - y))

        def slot(px, py, pc):
            return out_ref.at[4 * px + 2 * py + pc]

        def copy(k, blk, to, src=None):
            return pltpu.make_async_remote_copy(
                src_ref=slot(*blk) if src is None else src, dst_ref=slot(*blk),
                send_sem=send_sems.at[k], recv_sem=recv_sems.at[k], device_id=to, device_id_type=MESH)

        mine = pltpu.make_async_copy(x_ref, slot(*me), local_sem)
        mine.start()
        first = [copy(0, me, sibling, src=x_ref), copy(1, me, (*x_nbr, c), src=x_ref), copy(2, me, (*y_nbr, c), src=x_ref)]
        for cp in first:
            cp.start()
        copy(1, (*x_nbr, c), me).wait_recv()
        copy(2, (*y_nbr, c), me).wait_recv()
        later = [copy(3, (*relay_from, c), (*relay_to, c)), copy(4, (*x_nbr, c), sibling), copy(5, (*y_nbr, c), sibling)]
        for cp in later:
            cp.start()
        copy(3, (*diagonal, c), me).wait_recv()
        last = copy(6, (*diagonal, c), sibling)
        last.start()
        copy(0, sibling, me).wait_recv()
        for k, chip in ((4, x_nbr), (5, y_nbr), (6, diagonal)):
            copy(k, (*chip, 1 - c), me).wait_recv()
        for cp in first + later + [last]:
            cp.wait_send()
        mine.wait()

    return _hbm_call(body, name, jax.ShapeDtypeStruct((N_DEV,) + block.shape, block.dtype), 7, block)


def pair_swap(name, block):
    def body(x_ref, out_ref, send_sems, recv_sems, local_sem):
        x, y, c = _coords()
        cp = pltpu.make_async_remote_copy(
            src_ref=x_ref, dst_ref=out_ref, send_sem=send_sems.at[0], recv_sem=recv_sems.at[0],
            device_id=(x, y, 1 - c), device_id_type=MESH)
        cp.start()
        cp.wait_recv()
        cp.wait_send()

    return _hbm_call(body, name, jax.ShapeDtypeStruct(block.shape, block.dtype), 1, block)


def chip_all_to_all(name, buf):
    def body(x_ref, out_ref, send_sems, recv_sems, local_sem):
        x, y, c = _coords()
        me = 2 * x + y
        mine = pltpu.make_async_copy(x_ref.at[me], out_ref.at[me], local_sem)
        mine.start()
        copies = []
        for k in range(1, 4):
            px, py = _flip(x, k & 2), _flip(y, k & 1)
            copies.append(pltpu.make_async_remote_copy(
                src_ref=x_ref.at[2 * px + py], dst_ref=out_ref.at[me],
                send_sem=send_sems.at[k - 1], recv_sem=recv_sems.at[k - 1],
                device_id=(px, py, c), device_id_type=MESH))
        for cp in copies:
            cp.start()
        for cp in copies:
            cp.wait_recv()
        for cp in copies:
            cp.wait_send()
        mine.wait()

    return _hbm_call(body, name, jax.ShapeDtypeStruct(buf.shape, buf.dtype), 3, buf)


def _col_blocks(a, cols):
    a = jnp.pad(a, ((0, 0), (0, cols - a.shape[1])))
    return [a[i * PACK_ROWS:(i + 1) * PACK_ROWS] for i in range(a.shape[0] // PACK_ROWS)]


def _pack_shard(sh, dtype, with_ada=True):
    lora = jnp.concatenate([sh['w_decay_up'], sh['w_iclr_up']], axis=1)
    misc = jnp.concatenate([sh['w_ukv'], lora, jnp.zeros((LORA, 2 * LANES), lora.dtype)], axis=0)
    blocks = ((_col_blocks(sh['w_ada'], 768) if with_ada else [])
              + _col_blocks(sh['w_in'], 1408) + _col_blocks(sh['w_proj_a'], 256)
              + _col_blocks(sh['w_proj_b'], 256) + [sh['w_out']] + _col_blocks(sh['w_uq'], 256) + [misc])
    return jnp.concatenate([b.astype(dtype) for b in blocks], axis=1)


def _unpack_shard(p, with_ada=True):
    o = [0]

    def take(n_blocks, cols, used):
        blocks = [p[:, o[0] + i * cols:o[0] + (i + 1) * cols] for i in range(n_blocks)]
        o[0] += n_blocks * cols
        return jnp.concatenate(blocks, axis=0)[:, :used]

    out = {'w_ada': take(4, 768, 768)} if with_ada else {}
    out.update({'w_in': take(4, 1408, 1288), 'w_proj_a': take(2, 256, 256),
                'w_proj_b': take(2, 256, 256), 'w_out': take(1, 1024, 1024), 'w_uq': take(1, 256, 192)})
    misc = take(1, 256, 256)
    out['w_ukv'] = misc[:2 * LORA]
    out['w_decay_up'] = misc[2 * LORA:3 * LORA, :LANES]
    out['w_iclr_up'] = misc[2 * LORA:3 * LORA, LANES:]
    return out


def _pack_small(parts):
    flat = jnp.concatenate([p.reshape(-1) for p in parts])
    return jnp.pad(flat, (0, SMALL_ROWS * LANES - flat.shape[0])).reshape(SMALL_ROWS, LANES)


def _unpack_small(packed):
    flat, out, o = packed.reshape(-1), {}, 0
    for name, n in _SMALL:
        out[name] = flat[o:o + n]
        o += n
    return out


def _pad_heads_cols(w, used, left):
    k = w.shape[0]
    return jnp.pad(w.reshape(k, HEADS, used), ((0, 0), (0, 0), (left, LANES - used - left))).reshape(k, HEADS * LANES)


def _unpad_heads_cols(w, used, left):
    k = w.shape[0]
    return w.reshape(k, HEADS, LANES)[:, :, left:left + used].reshape(k, HEADS * used)


def kernel(x, c, positions, w_ada, b_ada, w_in, q_norm_g, w_uq, kv_norm_g, w_ukv, mu_rwkv, w0, w_decay_up, a0, w_iclr_up, k_k, k_a, r_k, gn_g, gn_b, w_proj_a, w_proj_b, w_out, post_g, post_b, loss_target, m_w_ada, m_b_ada, m_w_in, m_q_norm_g, m_w_uq, m_kv_norm_g, m_w_ukv, m_mu_rwkv, m_w0, m_w_decay_up, m_a0, m_w_iclr_up, m_k_k, m_k_a, m_r_k, m_gn_g, m_gn_b, m_w_proj_a, m_w_proj_b, m_w_out, m_post_g, m_post_b, v_w_ada, v_b_ada, v_w_in, v_q_norm_g, v_w_uq, v_kv_norm_g, v_w_ukv, v_mu_rwkv, v_w0, v_w_decay_up, v_a0, v_w_iclr_up, v_k_k, v_k_a, v_r_k, v_gn_g, v_gn_b, v_w_proj_a, v_w_proj_b, v_w_out, v_post_g, v_post_b):
    given = dict(locals())
    seq = x.shape[1]
    my_c = lax.axis_index("c")

    shard_names = [n for n, _, _ in _SHARDED]
    w_pack = _pack_shard({n: given[n][0] for n in shard_names}, BF16)
    my_half = lax.dynamic_slice_in_dim(w_pack, my_c * HALF_COLS, HALF_COLS, 1)
    gathered = all_gather8("gather_weights", my_half)
    shards = [_unpack_shard(jnp.concatenate([gathered[2 * s], gathered[2 * s + 1]], axis=1)) for s in range(4)]
    full = {n: jnp.concatenate([sh[n] for sh in shards], axis=ax) for n, _, ax in _SHARDED if n not in ('w_in', 'w_ada')}

    s0, s1, s2, s3 = (sh['w_in'] for sh in shards)
    zcol = lambda n: jnp.zeros((D_MODEL, n), BF16)
    w_g1 = jnp.concatenate([s0[:, :384], zcol(HEAD), s0[:, 384:416], zcol(LANES - MLA_QK),
                            _pad_heads_cols(s0[:, 416:928], HEAD, HEAD)], axis=1)
    w_g2 = jnp.concatenate([s0[:, 928:], s1, s2[:, :528]], axis=1)
    w_g3 = jnp.concatenate([s2[:, 528:], s3], axis=1)
    w_uq_p = _pad_heads_cols(full['w_uq'], MLA_QK, 0)
    w_pa_p = jnp.pad(full['w_proj_a'].reshape(HEADS, HEAD, D_MODEL), ((0, 0), (HEAD, 0), (0, 0))).reshape(HEADS * LANES, D_MODEL)
    zl = jnp.zeros((LORA, WIDTH), BF16)
    w_lora = jnp.concatenate([jnp.concatenate([full['w_decay_up'], zl], 1),
                              jnp.concatenate([zl, full['w_iclr_up']], 1)], 0)

    hd = np.arange(WKV_HALF) // HEAD
    ones_half = jnp.asarray(hd[:, None] == hd[None, :], BF16)
    perm_np = np.zeros((LANES, LANES), np.float32)
    for d in range(MLA_ROPE // 2):
        perm_np[HEAD + 16 + d, HEAD + d] = -1.0
        perm_np[HEAD + d, HEAD + 16 + d] = 1.0
    perm = jnp.asarray(perm_np, BF16)
    inv = ROPE_THETA ** (-jnp.arange(0, MLA_ROPE, 2, dtype=F32) / MLA_ROPE)
    ang = positions[0].astype(F32)[:, None] * inv
    cos_a, sin_a = jnp.cos(ang), jnp.sin(ang)
    cs = jnp.concatenate([jnp.ones((seq, HEAD), F32), cos_a, cos_a, jnp.zeros((seq, LANES - MLA_QK), F32),
                          jnp.zeros((seq, HEAD), F32), sin_a, sin_a, jnp.zeros((seq, LANES - MLA_QK), F32)], axis=1)

    x2, tgt = x[0], loss_target[0]
    r_k2 = r_k.reshape(1, WIDTH)

    c8 = jnp.broadcast_to(c, (SUBLANES, D_MODEL))
    ada = ada_fwd(c8, b_ada, gathered)[:1]
    shift, scale, gate = ada[:, :D_MODEL], ada[:, D_MODEL:2 * D_MODEL], ada[:, 2 * D_MODEL:]

    f_in1, f_in2, f_in3 = _make_f_in((512, 1024)), _make_f_in((SHIFT_W, WIDTH)), _make_f_in((1024, 1024))
    tr = min(256, seq)
    p_mla, gpa, p_rwkv, gpb, ma, mb = row_fwd(
        "in_fwd", _make_f_in((512, 1024), (SHIFT_W, WIDTH), (1024, 1024)), [x2], [shift, scale, w_g1, w_g2, w_g3], [],
        [512, 1024, SHIFT_W, WIDTH, 1024, 1024], tr)

    mla_par = [q_norm_g, kv_norm_g, w_uq_p, full['w_ukv']]
    q_f, kv_f, kpe = row_fwd("mla_pre_fwd", f_mla_pre, [p_mla, cs], mla_par, [perm], [1024, 1024, LANES], tr)
    ya, lse = attn_fwd(q_f, kv_f, kpe)

    pre_par = [w0, a0, k_k, k_a, w_lora]
    u, rr, wd, k2, vv, an, bb = shift_stage_fwd("rwkv_pre_fwd", f_rwkv_pre, p_rwkv, mu_rwkv, pre_par, [ones_half],
                                                [WIDTH] * 6, tr)
    y_wkv, states = wkv_fwd(rr, wd, k2, vv, an, bb, ones_half)
    post_b_par = [r_k2, gn_g, gn_b]
    merge_par, loss_par = [w_pa_p, full['w_proj_b']], [gate, post_g, post_b, full['w_out']]

    def f_tail(ya_, gpa_, y_, r_, k_, v_, gpb_, ma_, mb_, x_, tgt_, r_k_, gn_g_, gn_b_, w_pa_, w_pb_,
               gate_, post_g_, post_b_, w_out_, ones_):
        yb_, = f_rwkv_post(y_, r_, k_, v_, r_k_, gn_g_, gn_b_, ones_)
        merged_, = f_merge(ya_, gpa_, yb_, gpb_, ma_, mb_, w_pa_, w_pb_)
        return (yb_, merged_) + f_loss(merged_, x_, tgt_, gate_, post_g_, post_b_, w_out_)

    yb, merged, lrows = row_fwd("tail_fwd", f_tail, [ya, gpa, y_wkv, rr, k2, vv, gpb, ma, mb, x2, tgt],
                                post_b_par + merge_par + loss_par, [ones_half], [WIDTH, D_MODEL, LANES], tr)
    merge_rows, loss_rows = [ya, gpa, yb, gpb, ma, mb], [merged, x2, tgt]
    loss = lax.psum(jnp.sum(lrows[:, 0]), ("x", "y", "c"))

    dl = jnp.broadcast_to((jnp.arange(LANES) == 0).astype(F32), (seq, LANES))
    (dmerged, dx_res), (dgate, dpost_g, dpost_b, dw_out) = row_bwd("loss_bwd", f_loss, loss_rows, 2, loss_par, [], [dl], tr)
    (dya, dgpa, dyb, dgpb, dma, dmb), (dw_pa_p, dw_pb) = row_bwd(
        "merge_bwd", f_merge, merge_rows, 6, merge_par, [], [dmerged], tr)

    (dy_wkv, dr1, dk1, dv1), (dr_k, dgn_g, dgn_b) = row_bwd(
        "rwkv_post_bwd", f_rwkv_post, [y_wkv, rr, k2, vv], 4, post_b_par, [ones_half], [dyb], tr)
    dr2, dwd, dk2, dv2, dan, dbb = wkv_bwd(rr, wd, k2, vv, an, bb, dy_wkv, states, ones_half)
    dp_rwkv, dmu, (dw0, da0, dk_k, dk_a, dw_lora) = shift_stage_bwd(
        "rwkv_pre_bwd", f_rwkv_pre, p_rwkv, u, mu_rwkv, pre_par, [ones_half],
        [(dr1, dr2), dwd, (dk1, dk2), (dv1, dv2), dan, dbb], tr)

    dq_f, dkv_f, dkpe = attn_bwd(q_f, kv_f, kpe, ya, lse, dya)
    (dp_mla,), (dqg, dkvg, dw_uq_p, dw_ukv) = row_bwd(
        "mla_pre_bwd", f_mla_pre, [p_mla, cs], 1, mla_par, [perm], [dq_f, dkv_f, dkpe], tr)

    (dx1,), (dsh1, dsc1, dw_g1) = row_bwd("in1_bwd", f_in1, [x2], 1, [shift, scale, w_g1], [], [dp_mla, dgpa], tr, [dx_res])
    (dx2,), (dsh2, dsc2, dw_g2) = row_bwd("in2_bwd", f_in2, [x2], 1, [shift, scale, w_g2], [], [dp_rwkv, dgpb], tr, [dx1])
    (dx3,), (dsh3, dsc3, dw_g3) = row_bwd("in3_bwd", f_in3, [x2], 1, [shift, scale, w_g3], [], [dma, dmb], tr, [dx2])
    grad_x = dx3[None]

    dada = jnp.concatenate([dsh1 + dsh2 + dsh3, dsc1 + dsc2 + dsc3, dgate], axis=1)
    local = {
        'w_uq': _unpad_heads_cols(dw_uq_p, MLA_QK, 0),
        'w_ukv': dw_ukv,
        'w_decay_up': dw_lora[:LORA, :WIDTH],
        'w_iclr_up': dw_lora[LORA:, WIDTH:],
        'w_proj_a': dw_pa_p.reshape(HEADS, LANES, D_MODEL)[:, HEAD:].reshape(WIDTH, D_MODEL),
        'w_proj_b': dw_pb,
        'w_out': dw_out,
    }
    small_local = {'b_ada': dada, 'q_norm_g': dqg, 'kv_norm_g': dkvg, 'mu_rwkv': dmu, 'w0': dw0, 'a0': da0,
                   'k_k': dk_k, 'k_a': dk_a, 'r_k': dr_k, 'gn_g': dgn_g, 'gn_b': dgn_b,
                   'post_g': dpost_g, 'post_b': dpost_b}

    def shard_of(g, axis, s):
        n = g.shape[axis] // 4
        return lax.slice_in_dim(g, s * n, (s + 1) * n, axis=axis)

    g_in = [jnp.concatenate([dw_g1[:, :384], dw_g1[:, 448:480], _unpad_heads_cols(dw_g1[:, 512:], HEAD, HEAD),
                             dw_g2[:, :360]], axis=1),
            dw_g2[:, 360:1648], jnp.concatenate([dw_g2[:, 1648:], dw_g3[:, :760]], axis=1), dw_g3[:, 760:]]
    packed = jnp.stack([_pack_shard({n: g_in[s] if n == 'w_in' else shard_of(local[n], ax, s)
                                     for n, _, ax in _SHARDED if n != 'w_ada'}, F32, False) for s in range(4)])
    keep = lax.dynamic_slice_in_dim(packed, my_c * GRAD_HALF, GRAD_HALF, 2).reshape(4 * PACK_ROWS, GRAD_HALF)
    give = lax.dynamic_slice_in_dim(packed, (1 - my_c) * GRAD_HALF, GRAD_HALF, 2).reshape(4 * PACK_ROWS, GRAD_HALF)
    pair_sum, = row_fwd("pair_sum", lambda p, q: (p + q,), [keep, pair_swap("swap_halves", give)], [], [],
                        [GRAD_HALF], PACK_ROWS // 2, BF16)
    received = chip_all_to_all("exchange_grads", pair_sum.reshape(4, PACK_ROWS, GRAD_HALF))
    my_sum = sum_slots(received, PACK_ROWS // 2)
    other_sum = pair_swap("swap_sums", my_sum)
    halves = [jnp.where(my_c == 0, my_sum, other_sum), jnp.where(my_c == 0, other_sum, my_sum)]
    g_shard = _unpack_shard(jnp.concatenate(halves, axis=1), False)

    small_pack = lambda d, extra=(): _pack_small([d[n] for n, _ in _SMALL] + list(extra))
    small_all = all_gather8("gather_small", small_pack(small_local, [c * jax.nn.sigmoid(c)]))
    sc_all = small_all[:, SMALL_USED:SMALL_USED + D_MODEL // LANES].reshape(N_DEV, D_MODEL)
    dada_all = small_all[:, :3 * D_MODEL // LANES].reshape(N_DEV, 3 * D_MODEL)
    my_cols = lax.dynamic_slice_in_dim(dada_all, (2 * lax.axis_index("x") + lax.axis_index("y")) * 768, 768, 1)
    g_shard['w_ada'] = ada_grad_shard([sc_all[b].reshape(D_MODEL, 1) for b in range(N_DEV)], my_cols)

    big = [{}, {}, {}, {}]
    for n in shard_names:
        w2, m2, v2 = given[n][0], given['m_' + n][0], given['v_' + n][0]
        cols = w2.shape[1]
        outs = row_fwd("adamw_" + n, f_adamw, [w2, g_shard[n], m2, v2], [], [], [cols] * 3, min(256, w2.shape[0]))
        for dst, val in zip(big, (g_shard[n], *outs)):
            dst[n] = val

    small_out = adamw_small(small_all, small_pack({n: given[n] for n, _ in _SMALL}),
                            small_pack({n: given['m_' + n] for n, _ in _SMALL}),
                            small_pack({n: given['v_' + n] for n, _ in _SMALL}))

    results = []
    for big_k, packed_small in zip(big, small_out):
        small = _unpack_small(packed_small)
        results.append([(big_k[n] if n in big_k else small[n]).reshape(given[n].shape) for n in _WEIGHTS])
    return (loss, grad_x, *results[0], *results[1], *results[2], *results[3])
```

```python
from typing import NamedTuple

import numpy as np
import jax
import jax.numpy as jnp
from jax import lax
from jax.experimental import pallas as pl
from jax.experimental.pallas import tpu as pltpu

F32 = jnp.float32
BF16 = jnp.bfloat16

D_MODEL = 1024
LN_EPS = 1e-5
RMS_EPS = 1e-6
GN_EPS = 64e-5
HEADS = 8
HEAD = 64
MLA_ROPE = 32
MLA_QK = HEAD + MLA_ROPE
ROPE_THETA = 10000.0
WIDTH = HEADS * HEAD
LORA = 64
SHIFT_W = 3 * WIDTH + 2 * LORA
CHUNK = 64
ALPHA = 2.0 ** 0.25

ADAM_LR, ADAM_B1, ADAM_B2, ADAM_EPS, ADAM_WD, ADAM_STEP = 0.001, 0.9, 0.999, 1e-08, 0.01, 10

LANES = 128
SUBLANES = 8
VMEM_LIMIT = 56 * 1024 * 1024
N_DEV = 8
MESH = pl.DeviceIdType.MESH
NEG = -1e30

_WEIGHTS = ['w_ada', 'b_ada', 'w_in', 'q_norm_g', 'w_uq', 'kv_norm_g', 'w_ukv', 'mu_rwkv', 'w0',
            'w_decay_up', 'a0', 'w_iclr_up', 'k_k', 'k_a', 'r_k', 'gn_g', 'gn_b', 'w_proj_a',
            'w_proj_b', 'w_out', 'post_g', 'post_b']
_SHARDED = [('w_ada', (1024, 3072), 1), ('w_in', (1024, 5152), 1), ('w_uq', (256, 768), 1),
            ('w_ukv', (128, 1024), 1), ('w_decay_up', (64, 512), 1), ('w_iclr_up', (64, 512), 1),
            ('w_proj_a', (512, 1024), 1), ('w_proj_b', (512, 1024), 1), ('w_out', (1024, 1024), 0)]
_SMALL = [('b_ada', 3072), ('q_norm_g', 256), ('kv_norm_g', 128), ('mu_rwkv', 1664), ('w0', 512),
          ('a0', 512), ('k_k', 512), ('k_a', 512), ('r_k', 512), ('gn_g', 512), ('gn_b', 512),
          ('post_g', 1024), ('post_b', 1024)]
PACK_ROWS = 256
PACK_COLS = 11264
HALF_COLS = PACK_COLS // 2
ADA_COLS = 4 * 768
GRAD_HALF = (PACK_COLS - ADA_COLS) // 2
SMALL_USED = 84
SMALL_ROWS = 96


def _bf(x):
    return x.astype(BF16)


def _dot(a, b, ca, cb):
    return lax.dot_general(a, b, (((ca,), (cb,)), ((), ())), preferred_element_type=F32)


class Weight(NamedTuple):
    value: jax.Array
    grad: jax.Array


@jax.custom_vjp
def _mm(a, w, w_grad):
    return _dot(_bf(a), _bf(w), 1, 0)


def _mm_fwd(a, w, w_grad):
    return _mm(a, w, w_grad), (a, w)


def _mm_bwd(res, g):
    a, w = res
    gb = _bf(g)
    return _dot(gb, _bf(w), 1, 1), jnp.zeros_like(w), _dot(_bf(a), gb, 0, 0)


_mm.defvjp(_mm_fwd, _mm_bwd)


def mm(a, w):
    if isinstance(w, Weight):
        return _mm(a, w.value, w.grad)
    return _dot(_bf(a), _bf(w), 1, 0)


def _split3(x):
    hi = _bf(x)
    r1 = x - hi.astype(F32)
    mid = _bf(r1)
    lo = _bf(r1 - mid.astype(F32))
    return hi, mid, lo


def _exact_dot(x, m, cm):
    hi, mid, lo = _split3(x)
    return _dot(hi, m, 1, cm) + _dot(mid, m, 1, cm) + _dot(lo, m, 1, cm)


def _head_sums(x, ones_blocks):
    n = ones_blocks.shape[0]
    parts = [_exact_dot(x[:, o:o + n], ones_blocks, 0) for o in range(0, x.shape[1], n)]
    return parts[0] if len(parts) == 1 else jnp.concatenate(parts, axis=1)


@jax.custom_vjp
def segsum(x, ones_blocks):
    return _head_sums(x, ones_blocks)


def _segsum_fwd(x, ones_blocks):
    return segsum(x, ones_blocks), ones_blocks


def _segsum_bwd(ones_blocks, g):
    return _head_sums(g, ones_blocks), jnp.zeros_like(ones_blocks)


segsum.defvjp(_segsum_fwd, _segsum_bwd)


@jax.custom_vjp
def lane_perm(x, perm):
    return _exact_dot(x, perm, 0)


def _lane_perm_fwd(x, perm):
    return lane_perm(x, perm), perm


def _lane_perm_bwd(perm, g):
    return _exact_dot(g, perm, 1), jnp.zeros_like(perm)


lane_perm.defvjp(_lane_perm_fwd, _lane_perm_bwd)


def _silu(z):
    return z * jax.nn.sigmoid(z)


def _softplus(z):
    return jnp.maximum(z, 0.0) + jnp.log(1.0 + jnp.exp(-jnp.abs(z)))


def _layer_norm(x):
    xc = x - jnp.mean(x, -1, keepdims=True)
    return xc * lax.rsqrt(jnp.mean(xc * xc, -1, keepdims=True) + LN_EPS)


def _rope(t, cos_t, sin_t, perm):
    outs = []
    for h in range(t.shape[1] // LANES):
        th = t[:, h * LANES:(h + 1) * LANES]
        outs.append(th * cos_t + lane_perm(th, perm) * sin_t)
    return outs[0] if len(outs) == 1 else jnp.concatenate(outs, axis=1)


def _make_f_in(*split_groups):
    def f_in(x, shift, scale, *weights):
        h = _layer_norm(x) * (1.0 + scale) + shift
        outs = []
        for w, splits in zip(weights, split_groups):
            p, o = mm(h, w), 0
            for s in splits:
                outs.append(p[:, o:o + s])
                o += s
        return tuple(outs)
    return f_in


def f_mla_pre(p, cs, qg, kvg, w_uq, w_ukv, perm):
    q_c, kv_c, k_r = p[:, :256], p[:, 256:384], p[:, 384:512]
    cos_t, sin_t = cs[:, :LANES], cs[:, LANES:]
    qn = q_c * lax.rsqrt(jnp.mean(q_c * q_c, -1, keepdims=True) + RMS_EPS) * qg
    kvn = kv_c * lax.rsqrt(jnp.mean(kv_c * kv_c, -1, keepdims=True) + RMS_EPS) * kvg
    q = _rope(mm(qn, w_uq), cos_t, sin_t, perm)
    kv = mm(kvn, w_ukv)
    return q, kv, _rope(k_r, cos_t, sin_t, perm)


def f_rwkv_pre(u, w0, a0, k_k, k_a, w_lora, ones_blocks):
    r, k, v, lo = u[:, :WIDTH], u[:, WIDTH:2 * WIDTH], u[:, 2 * WIDTH:3 * WIDTH], u[:, 3 * WIDTH:]
    lane = lax.broadcasted_iota(jnp.int32, lo.shape, 1)
    dl = mm(jnp.where(lane < LORA, jnp.tanh(lo), lo), w_lora)
    w_log = -_softplus(-(w0 + dl[:, :WIDTH])) - 0.5
    decay = jnp.exp(-jnp.exp(w_log))
    a = jax.nn.sigmoid(a0 + dl[:, WIDTH:])
    kk = k * k_k
    kk = kk / jnp.maximum(jnp.sqrt(segsum(kk * kk, ones_blocks)), 1e-12)
    k2 = k * (1.0 + (a - 1.0) * k_a)
    return r, decay, k2, v, -kk, kk * a


def f_rwkv_post(y, r, k2, v, r_k, gn_g, gn_b, ones_blocks):
    yc = y - segsum(y, ones_blocks) * (1.0 / HEAD)
    yn = yc * lax.rsqrt(segsum(yc * yc, ones_blocks) * (1.0 / HEAD) + GN_EPS)
    return (yn * gn_g + gn_b + segsum(r * k2 * r_k, ones_blocks) * v,)


def f_merge(ya, gpa, yb, gpb, ma, mb, w_pa, w_pb):
    pa = mm(ya * _silu(gpa), w_pa)
    pb = mm(yb * _silu(gpb), w_pb)
    return (jax.nn.sigmoid(ma) * pa + jax.nn.sigmoid(mb) * pb,)


def f_loss(merged, x, tgt, gate, post_g, post_b, w_out):
    z = ALPHA * x + (1.0 + gate) * mm(merged, w_out)
    err = _layer_norm(z) * post_g + post_b - tgt
    lrow = 0.5 * jnp.mean(err * err, -1, keepdims=True)
    return (jnp.broadcast_to(lrow, (lrow.shape[0], LANES)),)


def f_adamw(w, g, m, v):
    m2 = ADAM_B1 * m + (1.0 - ADAM_B1) * g
    v2 = ADAM_B2 * v + (1.0 - ADAM_B2) * jnp.square(g)
    m_hat = m2 / (1.0 - ADAM_B1 ** ADAM_STEP)
    v_hat = v2 / (1.0 - ADAM_B2 ** ADAM_STEP)
    return -ADAM_LR * (m_hat / (jnp.sqrt(v_hat) + ADAM_EPS) + ADAM_WD * w), m2, v2


def _params(sem=("arbitrary",)):
    return pltpu.CompilerParams(dimension_semantics=sem, vmem_limit_bytes=VMEM_LIMIT)


def _row_spec(tr, a):
    return pl.BlockSpec((tr, a.shape[1]), lambda i: (i, 0))


def _full_spec(a):
    return pl.BlockSpec(a.shape, lambda i: (0,) * a.ndim)


def row_fwd(name, f, rows, params, consts, out_widths, tr, out_dtype=F32):
    n_rows = rows[0].shape[0]
    nr, npar, ncon = len(rows), len(params), len(consts)

    def body(*refs):
        rv = [r[...] for r in refs[:nr]]
        pv = [r[...] for r in refs[nr:nr + npar]]
        cv = [r[...] for r in refs[nr + npar:nr + npar + ncon]]
        outs = f(*rv, *pv, *cv)
        for o_ref, o in zip(refs[nr + npar + ncon:], outs):
            o_ref[...] = o.astype(o_ref.dtype)

    return pl.pallas_call(
        body, name=name, grid=(n_rows // tr,),
        in_specs=[_row_spec(tr, a) for a in rows] + [_full_spec(a) for a in list(params) + list(consts)],
        out_specs=[pl.BlockSpec((tr, w), lambda i: (i, 0)) for w in out_widths],
        out_shape=[jax.ShapeDtypeStruct((n_rows, w), out_dtype) for w in out_widths],
        compiler_params=_params(),
    )(*rows, *params, *consts)


def row_bwd(name, f, rows, n_diff, params, consts, douts, tr, add_rows=None):
    n_rows = rows[0].shape[0]
    douts = [d if isinstance(d, (tuple, list)) else (d,) for d in douts]
    counts = [len(d) for d in douts]
    flat_d = [a for d in douts for a in d]
    add_rows = add_rows or [None] * n_diff
    adds = [a for a in add_rows if a is not None]
    nr, npar, ncon, nd, na = len(rows), len(params), len(consts), len(flat_d), len(adds)

    def body(*refs):
        o = 0
        rv = [r[...] for r in refs[o:o + nr]]; o += nr
        pv = [Weight(r[...], jnp.zeros(r.shape, F32)) if r.dtype == BF16 else r[...] for r in refs[o:o + npar]]
        o += npar
        cv = [r[...] for r in refs[o:o + ncon]]; o += ncon
        dv = []
        for cnt in counts:
            s = refs[o][...]
            for e in range(1, cnt):
                s = s + refs[o + e][...]
            dv.append(s)
            o += cnt
        add_v = [r[...] for r in refs[o:o + na]]; o += na
        drow_refs = refs[o:o + n_diff]; o += n_diff
        dpar_refs = refs[o:o + npar]

        def g(*args):
            return tuple(f(*args[:n_diff], *rv[n_diff:], *args[n_diff:], *cv))

        _, vjp = jax.vjp(g, *rv[:n_diff], *pv)
        grads = vjp(tuple(dv))
        ai = 0
        for j, (r, gr) in enumerate(zip(drow_refs, grads[:n_diff])):
            if add_rows[j] is not None:
                gr = gr + add_v[ai]
                ai += 1
            r[...] = gr

        @pl.when(pl.program_id(0) == 0)
        def _():
            for r in dpar_refs:
                r[...] = jnp.zeros_like(r)

        for r, gr in zip(dpar_refs, grads[n_diff:]):
            r[...] += gr.grad if isinstance(gr, Weight) else gr

    outs = pl.pallas_call(
        body, name=name, grid=(n_rows // tr,),
        in_specs=([_row_spec(tr, a) for a in rows] + [_full_spec(a) for a in list(params) + list(consts)]
                  + [_row_spec(tr, a) for a in flat_d + adds]),
        out_specs=[_row_spec(tr, a) for a in rows[:n_diff]] + [_full_spec(a) for a in params],
        out_shape=([jax.ShapeDtypeStruct(a.shape, F32) for a in rows[:n_diff]]
                   + [jax.ShapeDtypeStruct(a.shape, F32) for a in params]),
        compiler_params=_params(),
    )(*rows, *params, *consts, *flat_d, *adds)
    return outs[:n_diff], outs[n_diff:]


def shift_stage_fwd(name, f, p, mu, params, consts, out_widths, tr):
    n_rows, w = p.shape
    npar, ncon = len(params), len(consts)

    def body(*refs):
        p_ref, mu_ref = refs[:2]
        pv = [r[...] for r in refs[2:2 + npar]]
        cv = [r[...] for r in refs[2 + npar:2 + npar + ncon]]
        u_ref, out_refs, carry = refs[2 + npar + ncon], refs[3 + npar + ncon:-1], refs[-1]

        @pl.when(pl.program_id(0) == 0)
        def _():
            carry[...] = jnp.zeros_like(carry)

        x = p_ref[...]
        rolled = pltpu.roll(x, 1, 0)
        head = pltpu.roll(carry[...], 1, 0)
        fixed = jnp.concatenate([head, rolled[SUBLANES:]], axis=0)
        row = lax.broadcasted_iota(jnp.int32, x.shape, 0)
        prev = jnp.where(row == 0, fixed, rolled)
        u = x + (prev - x) * mu_ref[...]
        u_ref[...] = u
        carry[...] = x[tr - SUBLANES:]
        for o_ref, o in zip(out_refs, f(u, *pv, *cv)):
            o_ref[...] = o

    return pl.pallas_call(
        body, name=name, grid=(n_rows // tr,),
        in_specs=[_row_spec(tr, p), _full_spec(mu)] + [_full_spec(a) for a in list(params) + list(consts)],
        out_specs=[_row_spec(tr, p)] + [pl.BlockSpec((tr, ow), lambda i: (i, 0)) for ow in out_widths],
        out_shape=[jax.ShapeDtypeStruct(p.shape, F32)] + [jax.ShapeDtypeStruct((n_rows, ow), F32) for ow in out_widths],
        scratch_shapes=[pltpu.VMEM((SUBLANES, w), F32)],
        compiler_params=_params(),
    )(p, mu, *params, *consts)


def shift_stage_bwd(name, f, p, u, mu, params, consts, douts, tr):
    n_rows, w = p.shape
    nb = n_rows // tr
    douts = [d if isinstance(d, (tuple, list)) else (d,) for d in douts]
    counts = [len(d) for d in douts]
    flat_d = [a for d in douts for a in d]
    npar, ncon, nd = len(params), len(consts), len(flat_d)

    def body(*refs):
        p_ref, u_ref, mu_ref = refs[:3]
        o = 3
        pv = [Weight(r[...], jnp.zeros(r.shape, F32)) if r.dtype == BF16 else r[...] for r in refs[o:o + npar]]
        o += npar
        cv = [r[...] for r in refs[o:o + ncon]]; o += ncon
        dv = []
        for cnt in counts:
            s = refs[o][...]
            for e in range(1, cnt):
                s = s + refs[o + e][...]
            dv.append(s)
            o += cnt
        dp_ref, dmu_ref = refs[o], refs[o + 1]
        dpar_refs, carry = refs[o + 2:o + 2 + npar], refs[-1]

        @pl.when(pl.program_id(0) == 0)
        def _():
            carry[...] = jnp.zeros_like(carry)
            dmu_ref[...] = jnp.zeros_like(dmu_ref)
            for r in dpar_refs:
                r[...] = jnp.zeros_like(r)

        _, vjp = jax.vjp(lambda uu, *pp: tuple(f(uu, *pp, *cv)), u_ref[...], *pv)
        grads = vjp(tuple(dv))
        for r, gr in zip(dpar_refs, grads[1:]):
            r[...] += gr.grad if isinstance(gr, Weight) else gr

        d = grads[0]
        rolled = pltpu.roll(d, tr - 1, 0)
        tail = pltpu.roll(carry[...], SUBLANES - 1, 0)
        fixed = jnp.concatenate([rolled[:tr - SUBLANES], tail], axis=0)
        row = lax.broadcasted_iota(jnp.int32, d.shape, 0)
        nxt = jnp.where(row == tr - 1, fixed, rolled)
        mu_v = mu_ref[...]
        dp_ref[...] = d * (1.0 - mu_v) + nxt * mu_v
        dmu_ref[...] += jnp.sum(p_ref[...] * (nxt - d), axis=0, keepdims=True)
        carry[...] = d[:SUBLANES]

    rev = lambda i: (nb - 1 - i, 0)
    rows_rev = lambda a: pl.BlockSpec((tr, a.shape[1]), rev)
    outs = pl.pallas_call(
        body, name=name, grid=(nb,),
        in_specs=([rows_rev(p), rows_rev(u), _full_spec(mu)] + [_full_spec(a) for a in list(params) + list(consts)]
                  + [rows_rev(a) for a in flat_d]),
        out_specs=[rows_rev(p), _full_spec(mu)] + [_full_spec(a) for a in params],
        out_shape=([jax.ShapeDtypeStruct(p.shape, F32), jax.ShapeDtypeStruct(mu.shape, F32)]
                   + [jax.ShapeDtypeStruct(a.shape, F32) for a in params]),
        scratch_shapes=[pltpu.VMEM((SUBLANES, w), F32)],
        compiler_params=_params(),
    )(p, u, mu, *params, *consts, *flat_d)
    return outs[0], outs[1], outs[2:]


ATT_T = 256


def _att_rows(j):
    return pl.ds(pl.multiple_of(j * ATT_T, ATT_T), ATT_T)


def _att_prep(kv_ref, kpe_ref, kf_scr, vf_scr, n_blocks):
    lane = lax.broadcasted_iota(jnp.int32, (ATT_T, LANES), 1)

    def prep(j, _):
        rows = _att_rows(j)
        kv = kv_ref[rows, :]
        kf_scr[rows, :] = _bf(jnp.where(lane < HEAD, kv, kpe_ref[rows, :]))
        vf_scr[rows, :] = _bf(jnp.where(lane >= HEAD, kv, 0.0))
        return 0

    lax.fori_loop(0, n_blocks, prep, 0)


def _att_diag_mask():
    shift = CHUNK.bit_length() - 1
    qc = jnp.right_shift(lax.broadcasted_iota(jnp.int32, (ATT_T, ATT_T), 0), shift)
    kc = jnp.right_shift(lax.broadcasted_iota(jnp.int32, (ATT_T, ATT_T), 1), shift)
    return kc <= qc


def _wide(x):
    return jnp.concatenate([x] * (ATT_T // LANES), axis=1)


def attn_fwd(q, kv, kpe):
    seq = q.shape[0]
    nb = seq // ATT_T
    assert seq % (2 * ATT_T) == 0, "blocks are taken two per trip"
    scale = MLA_QK ** -0.5

    def body(q_ref, kv_ref, kpe_ref, o_ref, lse_ref, kf_scr, vf_scr):
        _att_prep(kv_ref, kpe_ref, kf_scr, vf_scr, nb)
        mask = _att_diag_mask()

        def scores(qb, kj):
            return _dot(qb, kf_scr[_att_rows(kj), :], 1, 1) * scale

        def update(s, kj, carry, masked):
            m, l, acc = carry
            if masked:
                s = jnp.where(mask, s, NEG)
            m_new = jnp.maximum(m, jnp.broadcast_to(jnp.max(s, -1, keepdims=True), m.shape))
            alpha = jnp.exp(m - m_new)
            p = jnp.exp(s - _wide(m_new))
            l = alpha * l + jnp.broadcast_to(jnp.sum(p, -1, keepdims=True), l.shape)
            acc = alpha * acc + _dot(_bf(p), vf_scr[_att_rows(kj), :], 1, 0)
            return m_new, l, acc

        def finish(rows, carry):
            m, l, acc = carry
            o_ref[rows, :] = acc / l
            lse_ref[rows, :] = m + jnp.log(l)

        def q_pair(qp, _):
            rows_a, rows_b = _att_rows(2 * qp), _att_rows(2 * qp + 1)
            qa, qb = _bf(q_ref[rows_a, :]), _bf(q_ref[rows_b, :])
            init = (jnp.full((ATT_T, LANES), NEG, F32), jnp.zeros((ATT_T, LANES), F32),
                    jnp.zeros((ATT_T, LANES), F32))

            def trip(kj, c):
                ca, cb, sa, sb = c
                sa_next, sb_next = scores(qa, kj + 1), scores(qb, kj + 1)
                return update(sa, kj, ca, False), update(sb, kj, cb, False), sa_next, sb_next

            ca, cb, sa, sb = lax.fori_loop(0, 2 * qp, trip, (init, init, scores(qa, 0), scores(qb, 0)))
            sb_last = scores(qb, 2 * qp + 1)
            ca = update(sa, 2 * qp, ca, True)
            cb = update(sb_last, 2 * qp + 1, update(sb, 2 * qp, cb, False), True)
            finish(rows_a, ca)
            finish(rows_b, cb)
            return 0

        lax.fori_loop(0, nb // 2, q_pair, 0)

    head = pl.BlockSpec((seq, LANES), lambda h: (0, h))
    return pl.pallas_call(
        body, name="attn_fwd", grid=(HEADS,),
        in_specs=[head, head, pl.BlockSpec((seq, LANES), lambda h: (0, 0))],
        out_specs=[head, head],
        out_shape=[jax.ShapeDtypeStruct((seq, HEADS * LANES), F32)] * 2,
        scratch_shapes=[pltpu.VMEM((seq, LANES), BF16)] * 2,
        compiler_params=_params(),
    )(q, kv, kpe)


def attn_bwd(q, kv, kpe, o, lse, do):
    seq = q.shape[0]
    nb = seq // ATT_T
    assert seq % (2 * ATT_T) == 0, "blocks are taken two per trip"
    scale = MLA_QK ** -0.5

    def body(q_ref, kv_ref, kpe_ref, o_ref, lse_ref, do_ref, dq_ref, dkv_ref, dkpe_ref,
             kf_scr, vf_scr, qb_scr, dob_scr, dsum):
        lane = lax.broadcasted_iota(jnp.int32, (ATT_T, LANES), 1)

        @pl.when(pl.program_id(0) == 0)
        def _():
            dkpe_ref[...] = jnp.zeros_like(dkpe_ref)

        dq_ref[...] = jnp.zeros_like(dq_ref)
        _att_prep(kv_ref, kpe_ref, kf_scr, vf_scr, nb)

        def pre(j, _):
            rows = _att_rows(j)
            d = do_ref[rows, :]
            qb_scr[rows, :] = _bf(q_ref[rows, :])
            dob_scr[rows, :] = _bf(d)
            dsum[rows, :] = jnp.broadcast_to(jnp.sum(d * o_ref[rows, :], -1, keepdims=True), (ATT_T, LANES))
            return 0

        lax.fori_loop(0, nb, pre, 0)
        mask = _att_diag_mask()

        def front(kf, vf, qi):
            rows = _att_rows(qi)
            return _dot(qb_scr[rows, :], kf, 1, 1), _dot(dob_scr[rows, :], vf, 1, 1)

        def back(kf, qi, fr, carry, masked):
            s, dp = fr
            dk, dv = carry
            rows = _att_rows(qi)
            qb, dob = qb_scr[rows, :], dob_scr[rows, :]
            p = jnp.exp(s * scale - _wide(lse_ref[rows, :]))
            if masked:
                p = jnp.where(mask, p, 0.0)
            ds = _bf(p * (dp - _wide(dsum[rows, :])) * scale)
            return (dk + _dot(ds, qb, 0, 0), dv + _dot(_bf(p), dob, 0, 0)), _dot(ds, kf, 1, 0)

        def store(krows, carry):
            dk, dv = carry
            dkv_ref[krows, :] = jnp.where(lane < HEAD, dk, dv)
            dkpe_ref[krows, :] += jnp.where((lane >= HEAD) & (lane < MLA_QK), dk, 0.0)

        def k_pair(kp, _):
            ka, kb = 2 * kp, 2 * kp + 1
            rows_a, rows_b = _att_rows(ka), _att_rows(kb)
            kfa, vfa, kfb, vfb = kf_scr[rows_a, :], vf_scr[rows_a, :], kf_scr[rows_b, :], vf_scr[rows_b, :]
            zero = jnp.zeros((ATT_T, LANES), F32)
            ca, dq_a = back(kfa, ka, front(kfa, vfa, ka), (zero, zero), True)
            dq_ref[rows_a, :] += dq_a
            ca, dq_a = back(kfa, kb, front(kfa, vfa, kb), ca, False)
            cb, dq_b = back(kfb, kb, front(kfb, vfb, kb), (zero, zero), True)
            dq_ref[rows_b, :] += dq_a + dq_b

            def both(qi, c):
                ca, cb, fa, fb = c
                nxt = jnp.minimum(qi + 1, nb - 1)
                fa_next, fb_next = front(kfa, vfa, nxt), front(kfb, vfb, nxt)
                ca, dq_a = back(kfa, qi, fa, ca, False)
                cb, dq_b = back(kfb, qi, fb, cb, False)
                dq_ref[_att_rows(qi), :] += dq_a + dq_b
                return ca, cb, fa_next, fb_next

            first = jnp.minimum(kb + 1, nb - 1)
            ca, cb, _, _ = lax.fori_loop(kb + 1, nb, both, (ca, cb, front(kfa, vfa, first), front(kfb, vfb, first)))
            store(rows_a, ca)
            store(rows_b, cb)
            return 0

        lax.fori_loop(0, nb // 2, k_pair, 0)

    head = pl.BlockSpec((seq, LANES), lambda h: (0, h))
    shared = pl.BlockSpec((seq, LANES), lambda h: (0, 0))
    return pl.pallas_call(
        body, name="attn_bwd", grid=(HEADS,),
        in_specs=[head, head, shared, head, head, head],
        out_specs=[head, head, shared],
        out_shape=[jax.ShapeDtypeStruct((seq, HEADS * LANES), F32)] * 2
        + [jax.ShapeDtypeStruct((seq, LANES), F32)],
        scratch_shapes=[pltpu.VMEM((seq, LANES), BF16)] * 4 + [pltpu.VMEM((seq, LANES), F32)],
        compiler_params=_params(),
    )(q, kv, kpe, o, lse, do)


WKV_TB = 128
WKV_GROUP = SUBLANES
WKV_HALF = WIDTH // 2


def _wkv_consts():
    row = lax.broadcasted_iota(jnp.int32, (HEAD, WKV_HALF), 0)
    lane = lax.broadcasted_iota(jnp.int32, (HEAD, WKV_HALF), 1)
    diag = row == jnp.bitwise_and(lane, HEAD - 1)
    sub = lax.broadcasted_iota(jnp.int32, (WKV_GROUP, WKV_HALF), 0)
    return diag, sub


def _halves(x):
    return [x[:, :WKV_HALF], x[:, WKV_HALF:]]


def _diag_rows(row, diag):
    return _bf(jnp.where(diag, jnp.broadcast_to(row, diag.shape), 0.0))


def _put_row(tile, row, i, sub):
    return jnp.where(sub == i, jnp.broadcast_to(row, tile.shape), tile)


def _col_sum(x):
    return jnp.sum(x, axis=0, keepdims=True)


def _step(x, i):
    return x[i * HEAD:(i + 1) * HEAD]


def _head_dots(prods, ones_b, sub):
    tile = jnp.zeros((WKV_GROUP, WKV_HALF), F32)
    for i, p in enumerate(prods):
        tile = _put_row(tile, p, i, sub)
    res = _exact_dot(tile, ones_b, 0)
    return [res[i:i + 1] for i in range(len(prods))]


def wkv_fwd(r, w, k, v, a, b, ones_half):
    seq = r.shape[0]

    def body(r_ref, w_ref, k_ref, v_ref, a_ref, b_ref, ones_ref, y_ref, st_ref, s_scr):
        @pl.when(pl.program_id(0) == 0)
        def _():
            s_scr[...] = jnp.zeros_like(s_scr)

        ones_b = ones_ref[...]
        diag, sub = _wkv_consts()

        ng = WKV_TB // WKV_GROUP
        last = WKV_GROUP - 2

        def rows_of(g):
            return pl.ds(pl.multiple_of(g * WKV_GROUP, WKV_GROUP), WKV_GROUP)

        def pair_rows(x8, t):
            return jnp.concatenate([_diag_rows(x8[t:t + 1], diag), _diag_rows(x8[t + 1:t + 2], diag)], axis=0)

        def put_y(g, pairs_y):
            tile = _halves(y_ref[rows_of(g), :])
            for hf in range(2):
                tile[hf] = _put_row(_put_row(tile[hf], pairs_y[hf][0], last, sub), pairs_y[hf][1], last + 1, sub)
            y_ref[rows_of(g), :] = jnp.concatenate(tile, axis=1)

        def read_out(yexp):
            return _col_sum(jnp.where(diag, yexp[:HEAD], 0.0)), _col_sum(jnp.where(diag, yexp[HEAD:], 0.0))

        def group(g, carry):
            state, v_cur, read = (list(c) for c in carry)
            base = pl.multiple_of(g * WKV_GROUP, WKV_GROUP)
            rows = rows_of(g)
            r8, w8, k8, v8, a8, b8 = (_halves(ref[rows, :]) for ref in (r_ref, w_ref, k_ref, v_ref, a_ref, b_ref))
            v_after = _halves(v_ref[rows_of(jnp.minimum(g + 1, ng - 1)), :])
            evens = range(0, WKV_GROUP, 2)
            dots = [_head_dots([b8[hf][t:t + 1] * a8[hf][t + 1:t + 2] for t in evens]
                               + [k8[hf][t:t + 1] * a8[hf][t + 1:t + 2] for t in evens], ones_b, sub) for hf in range(2)]
            y8 = [jnp.zeros((WKV_GROUP, WKV_HALF), F32)] * 2
            y_before = [None, None]
            for t in evens:
                s0, s1 = slice(t, t + 1), slice(t + 1, t + 2)
                both = []
                for hf in range(2):
                    s_in = state[hf]
                    v_next = pair_rows(v8[hf], t + 2) if t < last else pair_rows(v_after[hf], 0)
                    res = _dot(jnp.concatenate([_bf(s_in * a8[hf][s0]), _bf(s_in * (w8[hf][s0] * a8[hf][s1])),
                                                v_next, read[hf]], axis=0), ones_b, 1, 0)
                    sa0, v0, v1 = res[:HEAD], v_cur[hf][:HEAD], v_cur[hf][HEAD:]
                    st0 = s_in * w8[hf][s0] + sa0 * b8[hf][s0] + v0 * k8[hf][s0]
                    sa1 = res[HEAD:2 * HEAD] + sa0 * dots[hf][t // 2] + v0 * dots[hf][WKV_GROUP // 2 + t // 2]
                    st1 = st0 * w8[hf][s1] + sa1 * b8[hf][s1] + v1 * k8[hf][s1]
                    both.append((st0, st1))
                    state[hf], v_cur[hf] = st1, res[2 * HEAD:4 * HEAD]
                    read[hf] = jnp.concatenate([_bf(st0 * r8[hf][s0]), _bf(st1 * r8[hf][s1])], axis=0)
                    ya, yb = read_out(res[4 * HEAD:])
                    if t == 0:
                        y_before[hf] = (ya, yb)
                    else:
                        y8[hf] = _put_row(_put_row(y8[hf], ya, t - 2, sub), yb, t - 1, sub)
                for j in range(2):
                    st_ref[base + t + j] = jnp.concatenate([both[0][j], both[1][j]], axis=1)
            y_ref[rows, :] = jnp.concatenate(y8, axis=1)
            put_y(jnp.maximum(g - 1, 0), y_before)
            return tuple(state), tuple(v_cur), tuple(read)

        v_first = _halves(v_ref[rows_of(0), :])
        init = (tuple(_halves(s_scr[...])),
                tuple(_dot(pair_rows(v_first[hf], 0), ones_b, 1, 0) for hf in range(2)),
                tuple(jnp.zeros((2 * HEAD, WKV_HALF), BF16) for _ in range(2)))
        fin, _, read = lax.fori_loop(0, ng, group, init)
        put_y(ng - 1, [read_out(_dot(read[hf], ones_b, 1, 0)) for hf in range(2)])
        s_scr[...] = jnp.concatenate(fin, axis=1)

    vec = pl.BlockSpec((WKV_TB, WIDTH), lambda i: (i, 0))
    return pl.pallas_call(
        body, name="wkv_fwd", grid=(seq // WKV_TB,),
        in_specs=[vec] * 6 + [_full_spec(ones_half)],
        out_specs=[vec, pl.BlockSpec((WKV_TB, HEAD, WIDTH), lambda i: (i, 0, 0))],
        out_shape=[jax.ShapeDtypeStruct((seq, WIDTH), F32), jax.ShapeDtypeStruct((seq, HEAD, WIDTH), F32)],
        scratch_shapes=[pltpu.VMEM((HEAD, WIDTH), F32)],
        compiler_params=_params(),
    )(r, w, k, v, a, b, ones_half)


def wkv_bwd(r, w, k, v, a, b, dy, states, ones_half):
    seq = r.shape[0]
    nb = seq // WKV_TB
    ng = WKV_TB // WKV_GROUP

    def body(r_ref, w_ref, k_ref, v_ref, a_ref, b_ref, dy_ref, st_ref, halo_ref, ones_ref,
             dr_ref, dw_ref, dk_ref, dv_ref, da_ref, db_ref, ds_scr):
        blk = nb - 1 - pl.program_id(0)

        @pl.when(pl.program_id(0) == 0)
        def _():
            ds_scr[...] = jnp.zeros_like(ds_scr)

        ones_b = ones_ref[...]
        diag, sub = _wkv_consts()
        before_block = jnp.where(blk == 0, 0.0, halo_ref[0])

        def rows_of(g):
            return pl.ds(pl.multiple_of(g * WKV_GROUP, WKV_GROUP), WKV_GROUP)

        def expand_rows(hf, dy8, v8, a8, t, s_t, s_u):
            s1, s0 = slice(t, t + 1), slice(t - 1, t)
            return jnp.concatenate([_diag_rows(dy8[hf][s1], diag), _diag_rows(dy8[hf][s0], diag),
                                    _diag_rows(v8[hf][s1], diag), _diag_rows(v8[hf][s0], diag),
                                    _bf(s_t * a8[hf][s1]), _bf(s_u * a8[hf][s0])], axis=0)

        def read_out(x):
            return _col_sum(jnp.where(diag, x[:HEAD], 0.0)), _col_sum(jnp.where(diag, x[HEAD:], 0.0))

        def put_dv(g, pair_dv):
            tile = _halves(dv_ref[rows_of(g), :])
            for hf in range(2):
                tile[hf] = _put_row(_put_row(tile[hf], pair_dv[hf][0], 1, sub), pair_dv[hf][1], 0, sub)
            dv_ref[rows_of(g), :] = jnp.concatenate(tile, axis=1)

        def group(gg, carry):
            dstate, e_cur, dv_pend = (list(c) for c in carry)
            g = ng - 1 - gg
            base = pl.multiple_of(g * WKV_GROUP, WKV_GROUP)
            rows = rows_of(g)
            r8, w8, k8, v8, a8, b8, dy8 = (
                _halves(ref[rows, :]) for ref in (r_ref, w_ref, k_ref, v_ref, a_ref, b_ref, dy_ref))
            g_next = jnp.maximum(g - 1, 0)
            base_next = pl.multiple_of(g_next * WKV_GROUP, WKV_GROUP)
            dy8n, v8n, a8n = (_halves(ref[rows_of(g_next), :]) for ref in (dy_ref, v_ref, a_ref))
            zero8 = jnp.zeros((WKV_GROUP, WKV_HALF), F32)
            out = {n: [zero8, zero8] for n in ("dr", "dw", "dk", "dv", "da", "db")}
            before_group = jnp.where(g == 0, before_block, st_ref[jnp.maximum(base - 1, 0)])
            before_h = _halves(before_group)

            def state_before(i, hf):
                if i == 0:
                    return before_h[hf]
                return st_ref[base + i - 1, :, hf * WKV_HALF:(hf + 1) * WKV_HALF]
            odds = range(1, WKV_GROUP, 2)
            dots = [_head_dots([a8[hf][t:t + 1] * b8[hf][t - 1:t] for t in odds]
                               + [r8[hf][t - 1:t] * b8[hf][t - 1:t] for t in odds], ones_b, sub) for hf in range(2)]
            dv_after = [None, None]

            def emit(hf, i, d_i, dsa_i, dy_i, v_i, sa_i):
                s_p, s_t = state_before(i, hf), state_before(i + 1, hf)
                for n, val in (("dr", _col_sum(s_t * dy_i)), ("dw", _col_sum(d_i * s_p)), ("db", _col_sum(d_i * sa_i)),
                               ("da", _col_sum(s_p * dsa_i)), ("dk", _col_sum(d_i * v_i))):
                    out[n][hf] = _put_row(out[n][hf], val, i, sub)

            for t in reversed(odds):
                s1, s0 = slice(t, t + 1), slice(t - 1, t)
                for hf in range(2):
                    dy1, dy0, v1, v0, sa1, sa0 = (_step(e_cur[hf], j) for j in range(6))
                    d1 = dstate[hf] + dy1 * r8[hf][s1]
                    if t > 1:
                        nxt = expand_rows(hf, dy8, v8, a8, t - 2, state_before(t - 2, hf), state_before(t - 3, hf))
                    else:
                        lanes = slice(hf * WKV_HALF, (hf + 1) * WKV_HALF)
                        nxt = expand_rows(hf, dy8n, v8n, a8n, WKV_GROUP - 1, st_ref[base_next + WKV_GROUP - 2, :, lanes],
                                          st_ref[base_next + WKV_GROUP - 3, :, lanes])
                    res = _dot(jnp.concatenate([_bf(d1 * b8[hf][s1]), _bf(d1 * (w8[hf][s1] * b8[hf][s0])),
                                                nxt, dv_pend[hf]], axis=0), ones_b, 1, 0)
                    dsa1 = res[:HEAD]
                    d0 = d1 * w8[hf][s1] + dsa1 * a8[hf][s1] + dy0 * r8[hf][s0]
                    dsa0 = res[HEAD:2 * HEAD] + dsa1 * dots[hf][t // 2] + dy0 * dots[hf][WKV_GROUP // 2 + t // 2]
                    dstate[hf] = d0 * w8[hf][s0] + dsa0 * a8[hf][s0]
                    e_cur[hf] = res[2 * HEAD:8 * HEAD]
                    dv_pend[hf] = jnp.concatenate([_bf(d1 * k8[hf][s1]), _bf(d0 * k8[hf][s0])], axis=0)
                    emit(hf, t, d1, dsa1, dy1, v1, sa1)
                    emit(hf, t - 1, d0, dsa0, dy0, v0, sa0)
                    dv_a, dv_b = read_out(res[8 * HEAD:])
                    if t == WKV_GROUP - 1:
                        dv_after[hf] = (dv_a, dv_b)
                    else:
                        out["dv"][hf] = _put_row(_put_row(out["dv"][hf], dv_a, t + 2, sub), dv_b, t + 1, sub)
            for ref, n in ((dr_ref, "dr"), (dw_ref, "dw"), (dk_ref, "dk"), (dv_ref, "dv"), (da_ref, "da"), (db_ref, "db")):
                ref[rows, :] = jnp.concatenate(out[n], axis=1)
            put_dv(jnp.minimum(g + 1, ng - 1), dv_after)
            return tuple(dstate), tuple(e_cur), tuple(dv_pend)

        top = rows_of(ng - 1)
        dy8t, v8t, a8t = (_halves(ref[top, :]) for ref in (dy_ref, v_ref, a_ref))
        s_t, s_u = _halves(st_ref[WKV_TB - 2]), _halves(st_ref[WKV_TB - 3])
        init = (tuple(_halves(ds_scr[...])),
                tuple(_dot(expand_rows(hf, dy8t, v8t, a8t, WKV_GROUP - 1, s_t[hf], s_u[hf]), ones_b, 1, 0) for hf in range(2)),
                tuple(jnp.zeros((2 * HEAD, WKV_HALF), BF16) for _ in range(2)))
        fin, _, dv_pend = lax.fori_loop(0, ng, group, init)
        put_dv(0, [read_out(_dot(dv_pend[hf], ones_b, 1, 0)) for hf in range(2)])
        ds_scr[...] = jnp.concatenate(fin, axis=1)

    vec = pl.BlockSpec((WKV_TB, WIDTH), lambda i: (nb - 1 - i, 0))
    return pl.pallas_call(
        body, name="wkv_bwd", grid=(nb,),
        in_specs=[vec] * 7 + [
            pl.BlockSpec((WKV_TB, HEAD, WIDTH), lambda i: (nb - 1 - i, 0, 0)),
            pl.BlockSpec((1, HEAD, WIDTH), lambda i: (jnp.maximum((nb - 1 - i) * WKV_TB - 1, 0), 0, 0)),
            _full_spec(ones_half)],
        out_specs=[vec] * 6,
        out_shape=[jax.ShapeDtypeStruct((seq, WIDTH), F32)] * 6,
        scratch_shapes=[pltpu.VMEM((HEAD, WIDTH), F32)],
        compiler_params=_params(),
    )(r, w, k, v, a, b, dy, states, states, ones_half)


def ada_fwd(c8, b_ada, gathered):
    cols = 3 * D_MODEL // 4

    def body(c_ref, b_ref, w_ref, o_ref):
        @pl.when(pl.program_id(1) == 0)
        def _():
            o_ref[...] = jnp.broadcast_to(b_ref[...], o_ref.shape)

        o_ref[...] += mm(_silu(c_ref[...]), w_ref[0])

    return pl.pallas_call(
        body, name="ada_fwd", grid=(4, D_MODEL // PACK_ROWS),
        in_specs=[pl.BlockSpec((SUBLANES, PACK_ROWS), lambda s, i: (0, i)),
                  pl.BlockSpec((1, cols), lambda s, i: (0, s)),
                  pl.BlockSpec((1, PACK_ROWS, cols), lambda s, i: (2 * s, 0, i))],
        out_specs=pl.BlockSpec((SUBLANES, cols), lambda s, i: (0, s)),
        out_shape=jax.ShapeDtypeStruct((SUBLANES, 3 * D_MODEL), F32),
        compiler_params=_params(("arbitrary", "arbitrary")),
    )(c8, b_ada, gathered)


def ada_grad_shard(sc_cols, dada_rows):
    n = len(sc_cols)

    def body(*refs):
        d_ref, o_ref = refs[n], refs[n + 1]
        acc = refs[0][...] * d_ref[0:1, :]
        for b in range(1, n):
            acc = acc + refs[b][...] * d_ref[b:b + 1, :]
        o_ref[...] = acc

    return pl.pallas_call(
        body, name="ada_grad_shard",
        out_shape=jax.ShapeDtypeStruct((sc_cols[0].shape[0], dada_rows.shape[1]), F32),
        compiler_params=pltpu.CompilerParams(vmem_limit_bytes=VMEM_LIMIT),
    )(*sc_cols, dada_rows)


def sum_slots(buf, tr):
    n, rows, cols = buf.shape

    def body(b_ref, o_ref):
        acc = b_ref[0].astype(F32)
        for s in range(1, n):
            acc = acc + b_ref[s].astype(F32)
        o_ref[...] = acc

    return pl.pallas_call(
        body, name="sum_slots", grid=(rows // tr,),
        in_specs=[pl.BlockSpec((n, tr, cols), lambda i: (0, i, 0))],
        out_specs=pl.BlockSpec((tr, cols), lambda i: (i, 0)),
        out_shape=jax.ShapeDtypeStruct((rows, cols), F32),
        compiler_params=_params(),
    )(buf)


def adamw_small(gathered, w, m, v):
    n = gathered.shape[0]

    def body(g_ref, w_ref, m_ref, v_ref, go_ref, d_ref, mo_ref, vo_ref):
        g = g_ref[0]
        for s in range(1, n):
            g = g + g_ref[s]
        go_ref[...] = g
        d_ref[...], mo_ref[...], vo_ref[...] = f_adamw(w_ref[...], g, m_ref[...], v_ref[...])

    return pl.pallas_call(
        body, name="adamw_small",
        out_shape=[jax.ShapeDtypeStruct(w.shape, F32)] * 4,
        compiler_params=pltpu.CompilerParams(vmem_limit_bytes=VMEM_LIMIT),
    )(gathered, w, m, v)


def _coords():
    return lax.axis_index("x"), lax.axis_index("y"), lax.axis_index("c")


def _flip(v, bit):
    return 1 - v if bit else v


def _hbm_call(body, name, out_shape, n_sems, *args):
    any_spec = pl.BlockSpec(memory_space=pl.ANY)
    return pl.pallas_call(
        body, name=name, out_shape=out_shape,
        in_specs=[any_spec] * len(args), out_specs=any_spec,
        scratch_shapes=[pltpu.SemaphoreType.DMA((n_sems,)), pltpu.SemaphoreType.DMA((n_sems,)),
                        pltpu.SemaphoreType.DMA],
    )(*args)


def all_gather8(name, block):
    def body(x_ref, out_ref, send_sems, recv_sems, local_sem):
        x, y, c = _coords()
        me, sibling = (x, y, c), (x, y, 1 - c)
        x_nbr, y_nbr, diagonal = (1 - x, y), (x, 1 - y), (1 - x, 1 - y)
        relay_from = (c * x + (1 - c) * (1 - x), c * (1 - y) + (1 - c) * y)
        relay_to = (c * (1 - x) + (1 - c) * x, c * y + (1 - c) * (1 - y))

        def slot(px, py, pc):
            return out_ref.at[4 * px + 2 * py + pc]

        def copy(k, blk, to, src=None):
            return pltpu.make_async_remote_copy(
                src_ref=slot(*blk) if src is None else src, dst_ref=slot(*blk),
                send_sem=send_sems.at[k], recv_sem=recv_sems.at[k], device_id=to, device_id_type=MESH)

        mine = pltpu.make_async_copy(x_ref, slot(*me), local_sem)
        mine.start()
        first = [copy(0, me, sibling, src=x_ref), copy(1, me, (*x_nbr, c), src=x_ref), copy(2, me, (*y_nbr, c), src=x_ref)]
        for cp in first:
            cp.start()
        copy(1, (*x_nbr, c), me).wait_recv()
        copy(2, (*y_nbr, c), me).wait_recv()
        later = [copy(3, (*relay_from, c), (*relay_to, c)), copy(4, (*x_nbr, c), sibling), copy(5, (*y_nbr, c), sibling)]
        for cp in later:
            cp.start()
        copy(3, (*diagonal, c), me).wait_recv()
        last = copy(6, (*diagonal, c), sibling)
        last.start()
        copy(0, sibling, me).wait_recv()
        for k, chip in ((4, x_nbr), (5, y_nbr), (6, diagonal)):
            copy(k, (*chip, 1 - c), me).wait_recv()
        for cp in first + later + [last]:
            cp.wait_send()
        mine.wait()

    return _hbm_call(body, name, jax.ShapeDtypeStruct((N_DEV,) + block.shape, block.dtype), 7, block)


def pair_swap(name, block):
    def body(x_ref, out_ref, send_sems, recv_sems, local_sem):
        x, y, c = _coords()
        cp = pltpu.make_async_remote_copy(
            src_ref=x_ref, dst_ref=out_ref, send_sem=send_sems.at[0], recv_sem=recv_sems.at[0],
            device_id=(x, y, 1 - c), device_id_type=MESH)
        cp.start()
        cp.wait_recv()
        cp.wait_send()

    return _hbm_call(body, name, jax.ShapeDtypeStruct(block.shape, block.dtype), 1, block)


def chip_all_to_all(name, buf):
    def body(x_ref, out_ref, send_sems, recv_sems, local_sem):
        x, y, c = _coords()
        me = 2 * x + y
        mine = pltpu.make_async_copy(x_ref.at[me], out_ref.at[me], local_sem)
        mine.start()
        copies = []
        for k in range(1, 4):
            px, py = _flip(x, k & 2), _flip(y, k & 1)
            copies.append(pltpu.make_async_remote_copy(
                src_ref=x_ref.at[2 * px + py], dst_ref=out_ref.at[me],
                send_sem=send_sems.at[k - 1], recv_sem=recv_sems.at[k - 1],
                device_id=(px, py, c), device_id_type=MESH))
        for cp in copies:
            cp.start()
        for cp in copies:
            cp.wait_recv()
        for cp in copies:
            cp.wait_send()
        mine.wait()

    return _hbm_call(body, name, jax.ShapeDtypeStruct(buf.shape, buf.dtype), 3, buf)


def _col_blocks(a, cols):
    a = jnp.pad(a, ((0, 0), (0, cols - a.shape[1])))
    return [a[i * PACK_ROWS:(i + 1) * PACK_ROWS] for i in range(a.shape[0] // PACK_ROWS)]


def _pack_shard(sh, dtype, with_ada=True):
    lora = jnp.concatenate([sh['w_decay_up'], sh['w_iclr_up']], axis=1)
    misc = jnp.concatenate([sh['w_ukv'], lora, jnp.zeros((LORA, 2 * LANES), lora.dtype)], axis=0)
    blocks = ((_col_blocks(sh['w_ada'], 768) if with_ada else [])
              + _col_blocks(sh['w_in'], 1408) + _col_blocks(sh['w_proj_a'], 256)
              + _col_blocks(sh['w_proj_b'], 256) + [sh['w_out']] + _col_blocks(sh['w_uq'], 256) + [misc])
    return jnp.concatenate([b.astype(dtype) for b in blocks], axis=1)


def _unpack_shard(p, with_ada=True):
    o = [0]

    def take(n_blocks, cols, used):
        blocks = [p[:, o[0] + i * cols:o[0] + (i + 1) * cols] for i in range(n_blocks)]
        o[0] += n_blocks * cols
        return jnp.concatenate(blocks, axis=0)[:, :used]

    out = {'w_ada': take(4, 768, 768)} if with_ada else {}
    out.update({'w_in': take(4, 1408, 1288), 'w_proj_a': take(2, 256, 256),
                'w_proj_b': take(2, 256, 256), 'w_out': take(1, 1024, 1024), 'w_uq': take(1, 256, 192)})
    misc = take(1, 256, 256)
    out['w_ukv'] = misc[:2 * LORA]
    out['w_decay_up'] = misc[2 * LORA:3 * LORA, :LANES]
    out['w_iclr_up'] = misc[2 * LORA:3 * LORA, LANES:]
    return out


def _pack_small(parts):
    flat = jnp.concatenate([p.reshape(-1) for p in parts])
    return jnp.pad(flat, (0, SMALL_ROWS * LANES - flat.shape[0])).reshape(SMALL_ROWS, LANES)


def _unpack_small(packed):
    flat, out, o = packed.reshape(-1), {}, 0
    for name, n in _SMALL:
        out[name] = flat[o:o + n]
        o += n
    return out


def _pad_heads_cols(w, used, left):
    k = w.shape[0]
    return jnp.pad(w.reshape(k, HEADS, used), ((0, 0), (0, 0), (left, LANES - used - left))).reshape(k, HEADS * LANES)


def _unpad_heads_cols(w, used, left):
    k = w.shape[0]
    return w.reshape(k, HEADS, LANES)[:, :, left:left + used].reshape(k, HEADS * used)


def kernel(x, c, positions, w_ada, b_ada, w_in, q_norm_g, w_uq, kv_norm_g, w_ukv, mu_rwkv, w0, w_decay_up, a0, w_iclr_up, k_k, k_a, r_k, gn_g, gn_b, w_proj_a, w_proj_b, w_out, post_g, post_b, loss_target, m_w_ada, m_b_ada, m_w_in, m_q_norm_g, m_w_uq, m_kv_norm_g, m_w_ukv, m_mu_rwkv, m_w0, m_w_decay_up, m_a0, m_w_iclr_up, m_k_k, m_k_a, m_r_k, m_gn_g, m_gn_b, m_w_proj_a, m_w_proj_b, m_w_out, m_post_g, m_post_b, v_w_ada, v_b_ada, v_w_in, v_q_norm_g, v_w_uq, v_kv_norm_g, v_w_ukv, v_mu_rwkv, v_w0, v_w_decay_up, v_a0, v_w_iclr_up, v_k_k, v_k_a, v_r_k, v_gn_g, v_gn_b, v_w_proj_a, v_w_proj_b, v_w_out, v_post_g, v_post_b):
    given = dict(locals())
    seq = x.shape[1]
    my_c = lax.axis_index("c")

    shard_names = [n for n, _, _ in _SHARDED]
    w_pack = _pack_shard({n: given[n][0] for n in shard_names}, BF16)
    my_half = lax.dynamic_slice_in_dim(w_pack, my_c * HALF_COLS, HALF_COLS, 1)
    gathered = all_gather8("gather_weights", my_half)
    shards = [_unpack_shard(jnp.concatenate([gathered[2 * s], gathered[2 * s + 1]], axis=1)) for s in range(4)]
    full = {n: jnp.concatenate([sh[n] for sh in shards], axis=ax) for n, _, ax in _SHARDED if n not in ('w_in', 'w_ada')}

    s0, s1, s2, s3 = (sh['w_in'] for sh in shards)
    zcol = lambda n: jnp.zeros((D_MODEL, n), BF16)
    w_g1 = jnp.concatenate([s0[:, :384], zcol(HEAD), s0[:, 384:416], zcol(LANES - MLA_QK),
                            _pad_heads_cols(s0[:, 416:928], HEAD, HEAD)], axis=1)
    w_g2 = jnp.concatenate([s0[:, 928:], s1, s2[:, :528]], axis=1)
    w_g3 = jnp.concatenate([s2[:, 528:], s3], axis=1)
    w_uq_p = _pad_heads_cols(full['w_uq'], MLA_QK, 0)
    w_pa_p = jnp.pad(full['w_proj_a'].reshape(HEADS, HEAD, D_MODEL), ((0, 0), (HEAD, 0), (0, 0))).reshape(HEADS * LANES, D_MODEL)
    zl = jnp.zeros((LORA, WIDTH), BF16)
    w_lora = jnp.concatenate([jnp.concatenate([full['w_decay_up'], zl], 1),
                              jnp.concatenate([zl, full['w_iclr_up']], 1)], 0)

    hd = np.arange(WKV_HALF) // HEAD
    ones_half = jnp.asarray(hd[:, None] == hd[None, :], BF16)
    perm_np = np.zeros((LANES, LANES), np.float32)
    for d in range(MLA_ROPE // 2):
        perm_np[HEAD + 16 + d, HEAD + d] = -1.0
        perm_np[HEAD + d, HEAD + 16 + d] = 1.0
    perm = jnp.asarray(perm_np, BF16)
    inv = ROPE_THETA ** (-jnp.arange(0, MLA_ROPE, 2, dtype=F32) / MLA_ROPE)
    ang = positions[0].astype(F32)[:, None] * inv
    cos_a, sin_a = jnp.cos(ang), jnp.sin(ang)
    cs = jnp.concatenate([jnp.ones((seq, HEAD), F32), cos_a, cos_a, jnp.zeros((seq, LANES - MLA_QK), F32),
                          jnp.zeros((seq, HEAD), F32), sin_a, sin_a, jnp.zeros((seq, LANES - MLA_QK), F32)], axis=1)

    x2, tgt = x[0], loss_target[0]
    r_k2 = r_k.reshape(1, WIDTH)

    c8 = jnp.broadcast_to(c, (SUBLANES, D_MODEL))
    ada = ada_fwd(c8, b_ada, gathered)[:1]
    shift, scale, gate = ada[:, :D_MODEL], ada[:, D_MODEL:2 * D_MODEL], ada[:, 2 * D_MODEL:]

    f_in1, f_in2, f_in3 = _make_f_in((512, 1024)), _make_f_in((SHIFT_W, WIDTH)), _make_f_in((1024, 1024))
    tr = min(256, seq)
    p_mla, gpa, p_rwkv, gpb, ma, mb = row_fwd(
        "in_fwd", _make_f_in((512, 1024), (SHIFT_W, WIDTH), (1024, 1024)), [x2], [shift, scale, w_g1, w_g2, w_g3], [],
        [512, 1024, SHIFT_W, WIDTH, 1024, 1024], tr)

    mla_par = [q_norm_g, kv_norm_g, w_uq_p, full['w_ukv']]
    q_f, kv_f, kpe = row_fwd("mla_pre_fwd", f_mla_pre, [p_mla, cs], mla_par, [perm], [1024, 1024, LANES], tr)
    ya, lse = attn_fwd(q_f, kv_f, kpe)

    pre_par = [w0, a0, k_k, k_a, w_lora]
    u, rr, wd, k2, vv, an, bb = shift_stage_fwd("rwkv_pre_fwd", f_rwkv_pre, p_rwkv, mu_rwkv, pre_par, [ones_half],
                                                [WIDTH] * 6, tr)
    y_wkv, states = wkv_fwd(rr, wd, k2, vv, an, bb, ones_half)
    post_b_par = [r_k2, gn_g, gn_b]
    merge_par, loss_par = [w_pa_p, full['w_proj_b']], [gate, post_g, post_b, full['w_out']]

    def f_tail(ya_, gpa_, y_, r_, k_, v_, gpb_, ma_, mb_, x_, tgt_, r_k_, gn_g_, gn_b_, w_pa_, w_pb_,
               gate_, post_g_, post_b_, w_out_, ones_):
        yb_, = f_rwkv_post(y_, r_, k_, v_, r_k_, gn_g_, gn_b_, ones_)
        merged_, = f_merge(ya_, gpa_, yb_, gpb_, ma_, mb_, w_pa_, w_pb_)
        return (yb_, merged_) + f_loss(merged_, x_, tgt_, gate_, post_g_, post_b_, w_out_)

    yb, merged, lrows = row_fwd("tail_fwd", f_tail, [ya, gpa, y_wkv, rr, k2, vv, gpb, ma, mb, x2, tgt],
                                post_b_par + merge_par + loss_par, [ones_half], [WIDTH, D_MODEL, LANES], tr)
    merge_rows, loss_rows = [ya, gpa, yb, gpb, ma, mb], [merged, x2, tgt]
    loss = lax.psum(jnp.sum(lrows[:, 0]), ("x", "y", "c"))

    dl = jnp.broadcast_to((jnp.arange(LANES) == 0).astype(F32), (seq, LANES))
    (dmerged, dx_res), (dgate, dpost_g, dpost_b, dw_out) = row_bwd("loss_bwd", f_loss, loss_rows, 2, loss_par, [], [dl], tr)
    (dya, dgpa, dyb, dgpb, dma, dmb), (dw_pa_p, dw_pb) = row_bwd(
        "merge_bwd", f_merge, merge_rows, 6, merge_par, [], [dmerged], tr)

    (dy_wkv, dr1, dk1, dv1), (dr_k, dgn_g, dgn_b) = row_bwd(
        "rwkv_post_bwd", f_rwkv_post, [y_wkv, rr, k2, vv], 4, post_b_par, [ones_half], [dyb], tr)
    dr2, dwd, dk2, dv2, dan, dbb = wkv_bwd(rr, wd, k2, vv, an, bb, dy_wkv, states, ones_half)
    dp_rwkv, dmu, (dw0, da0, dk_k, dk_a, dw_lora) = shift_stage_bwd(
        "rwkv_pre_bwd", f_rwkv_pre, p_rwkv, u, mu_rwkv, pre_par, [ones_half],
        [(dr1, dr2), dwd, (dk1, dk2), (dv1, dv2), dan, dbb], tr)

    dq_f, dkv_f, dkpe = attn_bwd(q_f, kv_f, kpe, ya, lse, dya)
    (dp_mla,), (dqg, dkvg, dw_uq_p, dw_ukv) = row_bwd(
        "mla_pre_bwd", f_mla_pre, [p_mla, cs], 1, mla_par, [perm], [dq_f, dkv_f, dkpe], tr)

    (dx1,), (dsh1, dsc1, dw_g1) = row_bwd("in1_bwd", f_in1, [x2], 1, [shift, scale, w_g1], [], [dp_mla, dgpa], tr, [dx_res])
    (dx2,), (dsh2, dsc2, dw_g2) = row_bwd("in2_bwd", f_in2, [x2], 1, [shift, scale, w_g2], [], [dp_rwkv, dgpb], tr, [dx1])
    (dx3,), (dsh3, dsc3, dw_g3) = row_bwd("in3_bwd", f_in3, [x2], 1, [shift, scale, w_g3], [], [dma, dmb], tr, [dx2])
    grad_x = dx3[None]

    dada = jnp.concatenate([dsh1 + dsh2 + dsh3, dsc1 + dsc2 + dsc3, dgate], axis=1)
    local = {
        'w_uq': _unpad_heads_cols(dw_uq_p, MLA_QK, 0),
        'w_ukv': dw_ukv,
        'w_decay_up': dw_lora[:LORA, :WIDTH],
        'w_iclr_up': dw_lora[LORA:, WIDTH:],
        'w_proj_a': dw_pa_p.reshape(HEADS, LANES, D_MODEL)[:, HEAD:].reshape(WIDTH, D_MODEL),
        'w_proj_b': dw_pb,
        'w_out': dw_out,
    }
    small_local = {'b_ada': dada, 'q_norm_g': dqg, 'kv_norm_g': dkvg, 'mu_rwkv': dmu, 'w0': dw0, 'a0': da0,
                   'k_k': dk_k, 'k_a': dk_a, 'r_k': dr_k, 'gn_g': dgn_g, 'gn_b': dgn_b,
                   'post_g': dpost_g, 'post_b': dpost_b}

    def shard_of(g, axis, s):
        n = g.shape[axis] // 4
        return lax.slice_in_dim(g, s * n, (s + 1) * n, axis=axis)

    g_in = [jnp.concatenate([dw_g1[:, :384], dw_g1[:, 448:480], _unpad_heads_cols(dw_g1[:, 512:], HEAD, HEAD),
                             dw_g2[:, :360]], axis=1),
            dw_g2[:, 360:1648], jnp.concatenate([dw_g2[:, 1648:], dw_g3[:, :760]], axis=1), dw_g3[:, 760:]]
    packed = jnp.stack([_pack_shard({n: g_in[s] if n == 'w_in' else shard_of(local[n], ax, s)
                                     for n, _, ax in _SHARDED if n != 'w_ada'}, F32, False) for s in range(4)])
    keep = lax.dynamic_slice_in_dim(packed, my_c * GRAD_HALF, GRAD_HALF, 2).reshape(4 * PACK_ROWS, GRAD_HALF)
    give = lax.dynamic_slice_in_dim(packed, (1 - my_c) * GRAD_HALF, GRAD_HALF, 2).reshape(4 * PACK_ROWS, GRAD_HALF)
    pair_sum, = row_fwd("pair_sum", lambda p, q: (p + q,), [keep, pair_swap("swap_halves", give)], [], [],
                        [GRAD_HALF], PACK_ROWS // 2, BF16)
    received = chip_all_to_all("exchange_grads", pair_sum.reshape(4, PACK_ROWS, GRAD_HALF))
    my_sum = sum_slots(received, PACK_ROWS // 2)
    other_sum = pair_swap("swap_sums", my_sum)
    halves = [jnp.where(my_c == 0, my_sum, other_sum), jnp.where(my_c == 0, other_sum, my_sum)]
    g_shard = _unpack_shard(jnp.concatenate(halves, axis=1), False)

    small_pack = lambda d, extra=(): _pack_small([d[n] for n, _ in _SMALL] + list(extra))
    small_all = all_gather8("gather_small", small_pack(small_local, [c * jax.nn.sigmoid(c)]))
    sc_all = small_all[:, SMALL_USED:SMALL_USED + D_MODEL // LANES].reshape(N_DEV, D_MODEL)
    dada_all = small_all[:, :3 * D_MODEL // LANES].reshape(N_DEV, 3 * D_MODEL)
    my_cols = lax.dynamic_slice_in_dim(dada_all, (2 * lax.axis_index("x") + lax.axis_index("y")) * 768, 768, 1)
    g_shard['w_ada'] = ada_grad_shard([sc_all[b].reshape(D_MODEL, 1) for b in range(N_DEV)], my_cols)

    big = [{}, {}, {}, {}]
    for n in shard_names:
        w2, m2, v2 = given[n][0], given['m_' + n][0], given['v_' + n][0]
        cols = w2.shape[1]
        outs = row_fwd("adamw_" + n, f_adamw, [w2, g_shard[n], m2, v2], [], [], [cols] * 3, min(256, w2.shape[0]))
        for dst, val in zip(big, (g_shard[n], *outs)):
            dst[n] = val

    small_out = adamw_small(small_all, small_pack({n: given[n] for n, _ in _SMALL}),
                            small_pack({n: given['m_' + n] for n, _ in _SMALL}),
                            small_pack({n: given['v_' + n] for n, _ in _SMALL}))

    results = []
    for big_k, packed_small in zip(big, small_out):
        small = _unpack_small(packed_small)
        results.append([(big_k[n] if n in big_k else small[n]).reshape(given[n].shape) for n in _WEIGHTS])
    return (loss, grad_x, *results[0], *results[1], *results[2], *results[3])
```
